```python
import jax, jax.numpy as jnp
from jax import lax
import numpy as np

D_MODEL = 1024
BATCH = 8
SEQ = 8192
DEPTH = 2

CHUNK = 64
CONV_W = 4
LRU_WIDTH = D_MODEL
LRU_BLOCKS = 16
LRU_BLOCK_DIM = LRU_WIDTH // LRU_BLOCKS
LRU_C = 8.0
SSD_EXPAND = 2
SSD_INNER = SSD_EXPAND * D_MODEL
SSD_HEAD_DIM = 64
SSD_HEADS = SSD_INNER // SSD_HEAD_DIM
SSD_GROUPS = 4
SSD_HEADS_PER_GROUP = SSD_HEADS // SSD_GROUPS
SSD_STATE = 128
SSD_CONV_DIM = SSD_INNER + 2 * SSD_GROUPS * SSD_STATE
N_BRANCH = 2
D_FF = ((8 * D_MODEL + 3 * 256 - 1) // (3 * 256)) * 256
EPS = 1e-6
IN_WIDTHS = (LRU_WIDTH, LRU_WIDTH, SSD_INNER, SSD_CONV_DIM, SSD_HEADS, N_BRANCH * D_MODEL)
IN_DIM = sum(IN_WIDTHS)

kernel_name = "hybrid_rglru_ssd_parallel_gated_block"


def _split(t, widths):
    offs = np.cumsum(widths)[:-1].tolist()
    return jnp.split(t, offs, axis=-1)


def rms_norm(x, g):
    xf = x.astype(jnp.float32)
    y = xf * lax.rsqrt(jnp.mean(xf * xf, axis=-1, keepdims=True) + EPS)
    return (y * g.astype(jnp.float32)).astype(x.dtype)


def causal_dw_conv(x, w, b):
    y = lax.conv_general_dilated(
        x, w[:, None, :].astype(x.dtype), window_strides=(1,), padding=[(CONV_W - 1, 0)],
        dimension_numbers=('NWC', 'WIO', 'NWC'), feature_group_count=x.shape[-1])
    return y + b.astype(x.dtype)


def rg_lru(x, w_a, b_a, w_x, b_x, lam):
    bsz, s, w = x.shape
    f32 = jnp.float32
    xf = x.astype(f32)
    xb = xf.reshape(bsz, s, LRU_BLOCKS, LRU_BLOCK_DIM)
    r = jax.nn.sigmoid(jnp.einsum('bshi,hij->bshj', xb, w_a.astype(f32)).reshape(bsz, s, w) + b_a.astype(f32))
    i = jax.nn.sigmoid(jnp.einsum('bshi,hij->bshj', xb, w_x.astype(f32)).reshape(bsz, s, w) + b_x.astype(f32))
    log_a = -LRU_C * r * jax.nn.softplus(-lam.astype(f32))
    a = jnp.exp(log_a)
    u = jnp.sqrt(-jnp.expm1(2.0 * log_a)) * (i * xf)

    def combine(lhs, rhs):
        a1, b1 = lhs
        a2, b2 = rhs
        return a1 * a2, a2 * b1 + b2

    _, h = lax.associative_scan(combine, (a, u), axis=1)
    return h.astype(x.dtype)


def ssd_scan(x, dt, A, Bm, Cm):
    b, s = x.shape[:2]
    c = s // CHUNK
    G, K, P, N = SSD_GROUPS, SSD_HEADS_PER_GROUP, SSD_HEAD_DIM, SSD_STATE
    xdt = (x * dt[..., None]).reshape(b, c, CHUNK, G, K, P)
    a = (dt * A).reshape(b, c, CHUNK, G, K)
    Bc = Bm.reshape(b, c, CHUNK, G, N)
    Cc = Cm.reshape(b, c, CHUNK, G, N)
    a_cs = jnp.cumsum(a, axis=2)
    seg = a_cs[:, :, :, None] - a_cs[:, :, None, :]
    causal = jnp.tril(jnp.ones((CHUNK, CHUNK), dtype=bool))[None, None, :, :, None, None]
    decay = jnp.exp(jnp.where(causal, seg, -jnp.inf))
    scores = jnp.einsum('bclgn,bcsgn->bclsg', Cc, Bc)
    y_diag = jnp.einsum('bclsg,bclsgk,bcsgkp->bclgkp', scores, decay, xdt)
    decay_to_end = jnp.exp(a_cs[:, :, -1:] - a_cs)
    states = jnp.einsum('bclgn,bclgk,bclgkp->bcgkpn', Bc, decay_to_end, xdt)
    chunk_decay = jnp.exp(a_cs[:, :, -1])

    def step(h, inp):
        st, dc = inp
        return h * dc[..., None, None] + st, h

    h0 = jnp.zeros((b, G, K, P, N), x.dtype)
    _, prev = lax.scan(step, h0, (jnp.moveaxis(states, 1, 0), jnp.moveaxis(chunk_decay, 1, 0)))
    prev = jnp.moveaxis(prev, 0, 1)
    y_off = jnp.einsum('bclgn,bcgkpn,bclgk->bclgkp', Cc, prev, jnp.exp(a_cs))
    return (y_diag + y_off).reshape(b, s, G * K, P)


def hybrid_mixer(xn, w_in, b_gate, lru_conv_w, lru_conv_b, lru_w_a, lru_b_a, lru_w_x, lru_b_x,
                 lru_lambda, ssd_conv_w, ssd_conv_b, ssd_dt_bias, ssd_A_log, ssd_D, ssd_norm_g,
                 w_branch, w_out):
    bsz, s, _ = xn.shape
    f32 = jnp.float32
    proj = xn @ w_in
    lru_x, lru_gate, z, xbc, dt_raw, gates = _split(proj, IN_WIDTHS)
    u = causal_dw_conv(lru_x, lru_conv_w, lru_conv_b)
    h = rg_lru(u, lru_w_a, lru_b_a, lru_w_x, lru_b_x, lru_lambda)
    y_a = jax.nn.gelu(lru_gate) * h
    xbc = jax.nn.silu(causal_dw_conv(xbc, ssd_conv_w, ssd_conv_b))
    xs, Bm, Cm = _split(xbc, (SSD_INNER, SSD_GROUPS * SSD_STATE, SSD_GROUPS * SSD_STATE))
    dt = jax.nn.softplus(dt_raw.astype(f32) + ssd_dt_bias.astype(f32))
    A = -jnp.exp(ssd_A_log.astype(f32))
    xh = xs.astype(f32).reshape(bsz, s, SSD_HEADS, SSD_HEAD_DIM)
    y = ssd_scan(xh, dt, A,
                 Bm.astype(f32).reshape(bsz, s, SSD_GROUPS, SSD_STATE),
                 Cm.astype(f32).reshape(bsz, s, SSD_GROUPS, SSD_STATE))
    y = y + ssd_D.astype(f32)[:, None] * xh
    y = y.reshape(bsz, s, SSD_INNER) * jax.nn.silu(z.astype(f32))
    yg = y.reshape(bsz, s, SSD_GROUPS, SSD_INNER // SSD_GROUPS)
    yg = yg * lax.rsqrt(jnp.mean(yg * yg, axis=-1, keepdims=True) + EPS)
    y_b = (yg.reshape(bsz, s, SSD_INNER) * ssd_norm_g.astype(f32)).astype(xn.dtype)
    g = jax.nn.sigmoid(gates + b_gate)
    g_a, g_b = _split(g, (D_MODEL, D_MODEL))
    merged = g_a * (y_a @ w_branch[:LRU_WIDTH]) + g_b * (y_b @ w_branch[LRU_WIDTH:])
    return merged @ w_out


def swiglu(xn, w_ffn_in, w_ffn_out):
    gate, up = _split(xn @ w_ffn_in, (D_FF, D_FF))
    return (jax.nn.silu(gate) * up) @ w_ffn_out


def _fwd_setup_inputs(seed: int = 0) -> dict:
    key = jax.random.key(seed)
    ks = jax.random.split(key, 24)
    nrm = lambda k, shape, scale: jax.random.normal(k, shape, jnp.float32) * scale
    L = DEPTH
    a_c = jax.random.uniform(ks[9], (L, LRU_WIDTH), jnp.float32, 0.9, 0.999)
    sig = a_c ** (1.0 / LRU_C)
    lru_lambda = jnp.log(sig) - jnp.log1p(-sig)
    dt0 = jnp.exp(jax.random.uniform(ks[12], (L, SSD_HEADS), jnp.float32, np.log(1e-3), np.log(1e-1)))
    ssd_dt_bias = dt0 + jnp.log(-jnp.expm1(-dt0))
    ssd_A_log = jnp.log(jax.random.uniform(ks[13], (L, SSD_HEADS), jnp.float32, 1.0, 16.0))
    w_branch = jnp.concatenate([
        nrm(ks[16], (L, LRU_WIDTH, D_MODEL), LRU_WIDTH ** -0.5),
        nrm(ks[17], (L, SSD_INNER, D_MODEL), SSD_INNER ** -0.5)], axis=1)
    return {
        "x": nrm(ks[0], (BATCH, SEQ, D_MODEL), 1.0),
        "norm1_g": 1.0 + nrm(ks[1], (L, D_MODEL), 0.02),
        "w_in": nrm(ks[2], (L, D_MODEL, IN_DIM), D_MODEL ** -0.5),
        "b_gate": nrm(ks[3], (L, N_BRANCH * D_MODEL), 0.02),
        "lru_conv_w": nrm(ks[4], (L, CONV_W, LRU_WIDTH), CONV_W ** -0.5),
        "lru_conv_b": nrm(ks[5], (L, LRU_WIDTH), 0.02),
        "lru_w_a": nrm(ks[6], (L, LRU_BLOCKS, LRU_BLOCK_DIM, LRU_BLOCK_DIM), LRU_BLOCK_DIM ** -0.5),
        "lru_b_a": nrm(ks[7], (L, LRU_WIDTH), 0.02),
        "lru_w_x": nrm(ks[8], (L, LRU_BLOCKS, LRU_BLOCK_DIM, LRU_BLOCK_DIM), LRU_BLOCK_DIM ** -0.5),
        "lru_b_x": nrm(ks[18], (L, LRU_WIDTH), 0.02),
        "lru_lambda": lru_lambda,
        "ssd_conv_w": nrm(ks[10], (L, CONV_W, SSD_CONV_DIM), CONV_W ** -0.5),
        "ssd_conv_b": nrm(ks[11], (L, SSD_CONV_DIM), 0.02),
        "ssd_dt_bias": ssd_dt_bias,
        "ssd_A_log": ssd_A_log,
        "ssd_D": 1.0 + nrm(ks[14], (L, SSD_HEADS), 0.02),
        "ssd_norm_g": 1.0 + nrm(ks[15], (L, SSD_INNER), 0.02),
        "w_branch": w_branch,
        "w_out": nrm(ks[19], (L, D_MODEL, D_MODEL), D_MODEL ** -0.5),
        "norm2_g": 1.0 + nrm(ks[20], (L, D_MODEL), 0.02),
        "w_ffn_in": nrm(ks[21], (L, D_MODEL, 2 * D_FF), D_MODEL ** -0.5),
        "w_ffn_out": nrm(ks[22], (L, D_FF, D_MODEL), D_FF ** -0.5),
        "norm_f": 1.0 + nrm(ks[23], (D_MODEL,), 0.02),
    }


def _fwd_reference(x, norm1_g, w_in, b_gate, lru_conv_w, lru_conv_b, lru_w_a, lru_b_a, lru_w_x, lru_b_x,
              lru_lambda, ssd_conv_w, ssd_conv_b, ssd_dt_bias, ssd_A_log, ssd_D, ssd_norm_g,
              w_branch, w_out, norm2_g, w_ffn_in, w_ffn_out, norm_f):
    h = x
    for l in range(DEPTH):
        h = h + hybrid_mixer(rms_norm(h, norm1_g[l]), w_in[l], b_gate[l], lru_conv_w[l], lru_conv_b[l],
                             lru_w_a[l], lru_b_a[l], lru_w_x[l], lru_b_x[l], lru_lambda[l],
                             ssd_conv_w[l], ssd_conv_b[l], ssd_dt_bias[l], ssd_A_log[l], ssd_D[l],
                             ssd_norm_g[l], w_branch[l], w_out[l])
        h = h + swiglu(rms_norm(h, norm2_g[l]), w_ffn_in[l], w_ffn_out[l])
    return rms_norm(h, norm_f)


import jax as _jax
import jax.numpy as _jnp

TWIN_FORMAT = 'train_step'
FWD_PARAMS = ['x', 'norm1_g', 'w_in', 'b_gate', 'lru_conv_w', 'lru_conv_b', 'lru_w_a', 'lru_b_a', 'lru_w_x', 'lru_b_x', 'lru_lambda', 'ssd_conv_w', 'ssd_conv_b', 'ssd_dt_bias', 'ssd_A_log', 'ssd_D', 'ssd_norm_g', 'w_branch', 'w_out', 'norm2_g', 'w_ffn_in', 'w_ffn_out', 'norm_f']
TWIN_WEIGHTS = ['norm1_g', 'w_in', 'b_gate', 'lru_conv_w', 'lru_conv_b', 'lru_w_a', 'lru_b_a', 'lru_w_x', 'lru_b_x', 'lru_lambda', 'ssd_conv_w', 'ssd_conv_b', 'ssd_dt_bias', 'ssd_A_log', 'ssd_D', 'ssd_norm_g', 'w_branch', 'w_out', 'norm2_g', 'w_ffn_in', 'w_ffn_out', 'norm_f']
TWIN_DIFF_INPUT = 'x'
TWIN_INPUTS = ['x', 'norm1_g', 'w_in', 'b_gate', 'lru_conv_w', 'lru_conv_b', 'lru_w_a', 'lru_b_a', 'lru_w_x', 'lru_b_x', 'lru_lambda', 'ssd_conv_w', 'ssd_conv_b', 'ssd_dt_bias', 'ssd_A_log', 'ssd_D', 'ssd_norm_g', 'w_branch', 'w_out', 'norm2_g', 'w_ffn_in', 'w_ffn_out', 'norm_f', 'loss_target', 'm_norm1_g', 'm_w_in', 'm_b_gate', 'm_lru_conv_w', 'm_lru_conv_b', 'm_lru_w_a', 'm_lru_b_a', 'm_lru_w_x', 'm_lru_b_x', 'm_lru_lambda', 'm_ssd_conv_w', 'm_ssd_conv_b', 'm_ssd_dt_bias', 'm_ssd_A_log', 'm_ssd_D', 'm_ssd_norm_g', 'm_w_branch', 'm_w_out', 'm_norm2_g', 'm_w_ffn_in', 'm_w_ffn_out', 'm_norm_f', 'v_norm1_g', 'v_w_in', 'v_b_gate', 'v_lru_conv_w', 'v_lru_conv_b', 'v_lru_w_a', 'v_lru_b_a', 'v_lru_w_x', 'v_lru_b_x', 'v_lru_lambda', 'v_ssd_conv_w', 'v_ssd_conv_b', 'v_ssd_dt_bias', 'v_ssd_A_log', 'v_ssd_D', 'v_ssd_norm_g', 'v_w_branch', 'v_w_out', 'v_norm2_g', 'v_w_ffn_in', 'v_w_ffn_out', 'v_norm_f']
TWIN_OUTPUTS = ['loss', 'grad_x', 'grad_norm1_g', 'grad_w_in', 'grad_b_gate', 'grad_lru_conv_w', 'grad_lru_conv_b', 'grad_lru_w_a', 'grad_lru_b_a', 'grad_lru_w_x', 'grad_lru_b_x', 'grad_lru_lambda', 'grad_ssd_conv_w', 'grad_ssd_conv_b', 'grad_ssd_dt_bias', 'grad_ssd_A_log', 'grad_ssd_D', 'grad_ssd_norm_g', 'grad_w_branch', 'grad_w_out', 'grad_norm2_g', 'grad_w_ffn_in', 'grad_w_ffn_out', 'grad_norm_f', 'delta_norm1_g', 'delta_w_in', 'delta_b_gate', 'delta_lru_conv_w', 'delta_lru_conv_b', 'delta_lru_w_a', 'delta_lru_b_a', 'delta_lru_w_x', 'delta_lru_b_x', 'delta_lru_lambda', 'delta_ssd_conv_w', 'delta_ssd_conv_b', 'delta_ssd_dt_bias', 'delta_ssd_A_log', 'delta_ssd_D', 'delta_ssd_norm_g', 'delta_w_branch', 'delta_w_out', 'delta_norm2_g', 'delta_w_ffn_in', 'delta_w_ffn_out', 'delta_norm_f', 'new_m_norm1_g', 'new_m_w_in', 'new_m_b_gate', 'new_m_lru_conv_w', 'new_m_lru_conv_b', 'new_m_lru_w_a', 'new_m_lru_b_a', 'new_m_lru_w_x', 'new_m_lru_b_x', 'new_m_lru_lambda', 'new_m_ssd_conv_w', 'new_m_ssd_conv_b', 'new_m_ssd_dt_bias', 'new_m_ssd_A_log', 'new_m_ssd_D', 'new_m_ssd_norm_g', 'new_m_w_branch', 'new_m_w_out', 'new_m_norm2_g', 'new_m_w_ffn_in', 'new_m_w_ffn_out', 'new_m_norm_f', 'new_v_norm1_g', 'new_v_w_in', 'new_v_b_gate', 'new_v_lru_conv_w', 'new_v_lru_conv_b', 'new_v_lru_w_a', 'new_v_lru_b_a', 'new_v_lru_w_x', 'new_v_lru_b_x', 'new_v_lru_lambda', 'new_v_ssd_conv_w', 'new_v_ssd_conv_b', 'new_v_ssd_dt_bias', 'new_v_ssd_A_log', 'new_v_ssd_D', 'new_v_ssd_norm_g', 'new_v_w_branch', 'new_v_w_out', 'new_v_norm2_g', 'new_v_w_ffn_in', 'new_v_w_ffn_out', 'new_v_norm_f']
TWIN_LEAF_KINDS = {'loss': 'loss', 'grad_x': 'grad_x', 'grad_norm1_g': 'grad_w', 'grad_w_in': 'grad_w', 'grad_b_gate': 'grad_w', 'grad_lru_conv_w': 'grad_w', 'grad_lru_conv_b': 'grad_w', 'grad_lru_w_a': 'grad_w', 'grad_lru_b_a': 'grad_w', 'grad_lru_w_x': 'grad_w', 'grad_lru_b_x': 'grad_w', 'grad_lru_lambda': 'grad_w', 'grad_ssd_conv_w': 'grad_w', 'grad_ssd_conv_b': 'grad_w', 'grad_ssd_dt_bias': 'grad_w', 'grad_ssd_A_log': 'grad_w', 'grad_ssd_D': 'grad_w', 'grad_ssd_norm_g': 'grad_w', 'grad_w_branch': 'grad_w', 'grad_w_out': 'grad_w', 'grad_norm2_g': 'grad_w', 'grad_w_ffn_in': 'grad_w', 'grad_w_ffn_out': 'grad_w', 'grad_norm_f': 'grad_w', 'delta_norm1_g': 'delta_w', 'delta_w_in': 'delta_w', 'delta_b_gate': 'delta_w', 'delta_lru_conv_w': 'delta_w', 'delta_lru_conv_b': 'delta_w', 'delta_lru_w_a': 'delta_w', 'delta_lru_b_a': 'delta_w', 'delta_lru_w_x': 'delta_w', 'delta_lru_b_x': 'delta_w', 'delta_lru_lambda': 'delta_w', 'delta_ssd_conv_w': 'delta_w', 'delta_ssd_conv_b': 'delta_w', 'delta_ssd_dt_bias': 'delta_w', 'delta_ssd_A_log': 'delta_w', 'delta_ssd_D': 'delta_w', 'delta_ssd_norm_g': 'delta_w', 'delta_w_branch': 'delta_w', 'delta_w_out': 'delta_w', 'delta_norm2_g': 'delta_w', 'delta_w_ffn_in': 'delta_w', 'delta_w_ffn_out': 'delta_w', 'delta_norm_f': 'delta_w', 'new_m_norm1_g': 'new_m', 'new_m_w_in': 'new_m', 'new_m_b_gate': 'new_m', 'new_m_lru_conv_w': 'new_m', 'new_m_lru_conv_b': 'new_m', 'new_m_lru_w_a': 'new_m', 'new_m_lru_b_a': 'new_m', 'new_m_lru_w_x': 'new_m', 'new_m_lru_b_x': 'new_m', 'new_m_lru_lambda': 'new_m', 'new_m_ssd_conv_w': 'new_m', 'new_m_ssd_conv_b': 'new_m', 'new_m_ssd_dt_bias': 'new_m', 'new_m_ssd_A_log': 'new_m', 'new_m_ssd_D': 'new_m', 'new_m_ssd_norm_g': 'new_m', 'new_m_w_branch': 'new_m', 'new_m_w_out': 'new_m', 'new_m_norm2_g': 'new_m', 'new_m_w_ffn_in': 'new_m', 'new_m_w_ffn_out': 'new_m', 'new_m_norm_f': 'new_m', 'new_v_norm1_g': 'new_v', 'new_v_w_in': 'new_v', 'new_v_b_gate': 'new_v', 'new_v_lru_conv_w': 'new_v', 'new_v_lru_conv_b': 'new_v', 'new_v_lru_w_a': 'new_v', 'new_v_lru_b_a': 'new_v', 'new_v_lru_w_x': 'new_v', 'new_v_lru_b_x': 'new_v', 'new_v_lru_lambda': 'new_v', 'new_v_ssd_conv_w': 'new_v', 'new_v_ssd_conv_b': 'new_v', 'new_v_ssd_dt_bias': 'new_v', 'new_v_ssd_A_log': 'new_v', 'new_v_ssd_D': 'new_v', 'new_v_ssd_norm_g': 'new_v', 'new_v_w_branch': 'new_v', 'new_v_w_out': 'new_v', 'new_v_norm2_g': 'new_v', 'new_v_w_ffn_in': 'new_v', 'new_v_w_ffn_out': 'new_v', 'new_v_norm_f': 'new_v'}


def _forward(args):
    return _fwd_reference(*[args[k] for k in FWD_PARAMS])


def _output_shape():
    def fwd():
        inp = _fwd_setup_inputs(0)
        return _fwd_reference(*[inp[k] for k in FWD_PARAMS])
    out = _jax.eval_shape(fwd)
    return out.shape, out.dtype

N_MICROBATCH = 1
ADAM_LR = 0.001
ADAM_B1 = 0.9
ADAM_B2 = 0.999
ADAM_EPS = 1e-08
ADAM_WD = 0.01
ADAM_STEP = 10
PER_EXAMPLE_BATCH_AXIS = {'x': 0, 'loss_target': 0}
SHARED_INPUTS = []
_WEIGHT_DTYPES = {'norm1_g': _jnp.float32, 'w_in': _jnp.float32, 'b_gate': _jnp.float32, 'lru_conv_w': _jnp.float32, 'lru_conv_b': _jnp.float32, 'lru_w_a': _jnp.float32, 'lru_b_a': _jnp.float32, 'lru_w_x': _jnp.float32, 'lru_b_x': _jnp.float32, 'lru_lambda': _jnp.float32, 'ssd_conv_w': _jnp.float32, 'ssd_conv_b': _jnp.float32, 'ssd_dt_bias': _jnp.float32, 'ssd_A_log': _jnp.float32, 'ssd_D': _jnp.float32, 'ssd_norm_g': _jnp.float32, 'w_branch': _jnp.float32, 'w_out': _jnp.float32, 'norm2_g': _jnp.float32, 'w_ffn_in': _jnp.float32, 'w_ffn_out': _jnp.float32, 'norm_f': _jnp.float32}
MOMENT_SCALE = {'norm1_g': 2.232564e-01, 'w_in': 7.418026e-02, 'b_gate': 4.135553e-02, 'lru_conv_w': 7.827744e-02, 'lru_conv_b': 7.076712e-01, 'lru_w_a': 2.639637e-02, 'lru_b_a': 2.090949e-02, 'lru_w_x': 4.736202e-02, 'lru_b_x': 2.697792e-02, 'lru_lambda': 3.971400e-02, 'ssd_conv_w': 8.169894e-02, 'ssd_conv_b': 1.088752e-01, 'ssd_dt_bias': 2.622109e-01, 'ssd_A_log': 2.319835e-01, 'ssd_D': 4.895225e-01, 'ssd_norm_g': 9.477800e-02, 'w_branch': 1.149693e-01, 'w_out': 1.500944e-01, 'norm2_g': 1.724759e-01, 'w_ffn_in': 7.308928e-02, 'w_ffn_out': 1.192472e-01, 'norm_f': 6.393208e+01}


def _to_microbatches(a, axis):
    t = _jnp.moveaxis(a, axis, 0)
    t = t.reshape((N_MICROBATCH, t.shape[0] // N_MICROBATCH) + t.shape[1:])
    return _jnp.moveaxis(t, 1, axis + 1)


def setup_inputs(seed: int = 0) -> dict:
    inp = _fwd_setup_inputs(seed)
    key = _jax.random.fold_in(_jax.random.key(seed), 7919)
    shape, _ = _output_shape()
    out = dict(inp)
    out["loss_target"] = _jax.random.normal(_jax.random.fold_in(key, 0), shape, _jnp.float32)
    for i, name in enumerate(TWIN_WEIGHTS):
        w = inp[name].astype(_jnp.float32)
        if MOMENT_SCALE is None:
            s = _jnp.sqrt(_jnp.mean(_jnp.square(w)) + 1e-30)
        else:
            s = MOMENT_SCALE[name]
        km, kv = _jax.random.split(_jax.random.fold_in(key, i + 1))
        out[name] = w
        out["m_" + name] = s * _jax.random.normal(km, w.shape, _jnp.float32)
        out["v_" + name] = (s * s) * _jax.random.uniform(kv, w.shape, _jnp.float32, 0.5, 1.5)
    if N_MICROBATCH > 1:
        for name, axis in PER_EXAMPLE_BATCH_AXIS.items():
            out[name] = _to_microbatches(out[name], axis)
    return {'x': out['x'], 'norm1_g': out['norm1_g'], 'w_in': out['w_in'], 'b_gate': out['b_gate'], 'lru_conv_w': out['lru_conv_w'], 'lru_conv_b': out['lru_conv_b'], 'lru_w_a': out['lru_w_a'], 'lru_b_a': out['lru_b_a'], 'lru_w_x': out['lru_w_x'], 'lru_b_x': out['lru_b_x'], 'lru_lambda': out['lru_lambda'], 'ssd_conv_w': out['ssd_conv_w'], 'ssd_conv_b': out['ssd_conv_b'], 'ssd_dt_bias': out['ssd_dt_bias'], 'ssd_A_log': out['ssd_A_log'], 'ssd_D': out['ssd_D'], 'ssd_norm_g': out['ssd_norm_g'], 'w_branch': out['w_branch'], 'w_out': out['w_out'], 'norm2_g': out['norm2_g'], 'w_ffn_in': out['w_ffn_in'], 'w_ffn_out': out['w_ffn_out'], 'norm_f': out['norm_f'], 'loss_target': out['loss_target'], 'm_norm1_g': out['m_norm1_g'], 'm_w_in': out['m_w_in'], 'm_b_gate': out['m_b_gate'], 'm_lru_conv_w': out['m_lru_conv_w'], 'm_lru_conv_b': out['m_lru_conv_b'], 'm_lru_w_a': out['m_lru_w_a'], 'm_lru_b_a': out['m_lru_b_a'], 'm_lru_w_x': out['m_lru_w_x'], 'm_lru_b_x': out['m_lru_b_x'], 'm_lru_lambda': out['m_lru_lambda'], 'm_ssd_conv_w': out['m_ssd_conv_w'], 'm_ssd_conv_b': out['m_ssd_conv_b'], 'm_ssd_dt_bias': out['m_ssd_dt_bias'], 'm_ssd_A_log': out['m_ssd_A_log'], 'm_ssd_D': out['m_ssd_D'], 'm_ssd_norm_g': out['m_ssd_norm_g'], 'm_w_branch': out['m_w_branch'], 'm_w_out': out['m_w_out'], 'm_norm2_g': out['m_norm2_g'], 'm_w_ffn_in': out['m_w_ffn_in'], 'm_w_ffn_out': out['m_w_ffn_out'], 'm_norm_f': out['m_norm_f'], 'v_norm1_g': out['v_norm1_g'], 'v_w_in': out['v_w_in'], 'v_b_gate': out['v_b_gate'], 'v_lru_conv_w': out['v_lru_conv_w'], 'v_lru_conv_b': out['v_lru_conv_b'], 'v_lru_w_a': out['v_lru_w_a'], 'v_lru_b_a': out['v_lru_b_a'], 'v_lru_w_x': out['v_lru_w_x'], 'v_lru_b_x': out['v_lru_b_x'], 'v_lru_lambda': out['v_lru_lambda'], 'v_ssd_conv_w': out['v_ssd_conv_w'], 'v_ssd_conv_b': out['v_ssd_conv_b'], 'v_ssd_dt_bias': out['v_ssd_dt_bias'], 'v_ssd_A_log': out['v_ssd_A_log'], 'v_ssd_D': out['v_ssd_D'], 'v_ssd_norm_g': out['v_ssd_norm_g'], 'v_w_branch': out['v_w_branch'], 'v_w_out': out['v_w_out'], 'v_norm2_g': out['v_norm2_g'], 'v_w_ffn_in': out['v_w_ffn_in'], 'v_w_ffn_out': out['v_w_ffn_out'], 'v_norm_f': out['v_norm_f']}


def _loss(weights, diff, rest, loss_target):
    with _jax.named_scope("forward"):
        args = {**rest, TWIN_DIFF_INPUT: diff, **{k: w.astype(_WEIGHT_DTYPES[k]) for k, w in weights.items()}}
        y = _forward(args)
    with _jax.named_scope("loss_head"):
        err = _jnp.square(y.astype(_jnp.float32) - loss_target)
        return 0.5 * _jnp.sum(_jnp.mean(err, axis=-1)) if err.ndim else 0.5 * err


def _adamw(w, g, m, v):
    m = ADAM_B1 * m + (1.0 - ADAM_B1) * g
    v = ADAM_B2 * v + (1.0 - ADAM_B2) * _jnp.square(g)
    m_hat = m / (1.0 - ADAM_B1 ** ADAM_STEP)
    v_hat = v / (1.0 - ADAM_B2 ** ADAM_STEP)
    delta = -ADAM_LR * (m_hat / (_jnp.sqrt(v_hat) + ADAM_EPS) + ADAM_WD * w)
    return delta, m, v


def reference(x, norm1_g, w_in, b_gate, lru_conv_w, lru_conv_b, lru_w_a, lru_b_a, lru_w_x, lru_b_x, lru_lambda, ssd_conv_w, ssd_conv_b, ssd_dt_bias, ssd_A_log, ssd_D, ssd_norm_g, w_branch, w_out, norm2_g, w_ffn_in, w_ffn_out, norm_f, loss_target, m_norm1_g, m_w_in, m_b_gate, m_lru_conv_w, m_lru_conv_b, m_lru_w_a, m_lru_b_a, m_lru_w_x, m_lru_b_x, m_lru_lambda, m_ssd_conv_w, m_ssd_conv_b, m_ssd_dt_bias, m_ssd_A_log, m_ssd_D, m_ssd_norm_g, m_w_branch, m_w_out, m_norm2_g, m_w_ffn_in, m_w_ffn_out, m_norm_f, v_norm1_g, v_w_in, v_b_gate, v_lru_conv_w, v_lru_conv_b, v_lru_w_a, v_lru_b_a, v_lru_w_x, v_lru_b_x, v_lru_lambda, v_ssd_conv_w, v_ssd_conv_b, v_ssd_dt_bias, v_ssd_A_log, v_ssd_D, v_ssd_norm_g, v_w_branch, v_w_out, v_norm2_g, v_w_ffn_in, v_w_ffn_out, v_norm_f):
    given = dict(x=x, norm1_g=norm1_g, w_in=w_in, b_gate=b_gate, lru_conv_w=lru_conv_w, lru_conv_b=lru_conv_b, lru_w_a=lru_w_a, lru_b_a=lru_b_a, lru_w_x=lru_w_x, lru_b_x=lru_b_x, lru_lambda=lru_lambda, ssd_conv_w=ssd_conv_w, ssd_conv_b=ssd_conv_b, ssd_dt_bias=ssd_dt_bias, ssd_A_log=ssd_A_log, ssd_D=ssd_D, ssd_norm_g=ssd_norm_g, w_branch=w_branch, w_out=w_out, norm2_g=norm2_g, w_ffn_in=w_ffn_in, w_ffn_out=w_ffn_out, norm_f=norm_f, loss_target=loss_target, m_norm1_g=m_norm1_g, m_w_in=m_w_in, m_b_gate=m_b_gate, m_lru_conv_w=m_lru_conv_w, m_lru_conv_b=m_lru_conv_b, m_lru_w_a=m_lru_w_a, m_lru_b_a=m_lru_b_a, m_lru_w_x=m_lru_w_x, m_lru_b_x=m_lru_b_x, m_lru_lambda=m_lru_lambda, m_ssd_conv_w=m_ssd_conv_w, m_ssd_conv_b=m_ssd_conv_b, m_ssd_dt_bias=m_ssd_dt_bias, m_ssd_A_log=m_ssd_A_log, m_ssd_D=m_ssd_D, m_ssd_norm_g=m_ssd_norm_g, m_w_branch=m_w_branch, m_w_out=m_w_out, m_norm2_g=m_norm2_g, m_w_ffn_in=m_w_ffn_in, m_w_ffn_out=m_w_ffn_out, m_norm_f=m_norm_f, v_norm1_g=v_norm1_g, v_w_in=v_w_in, v_b_gate=v_b_gate, v_lru_conv_w=v_lru_conv_w, v_lru_conv_b=v_lru_conv_b, v_lru_w_a=v_lru_w_a, v_lru_b_a=v_lru_b_a, v_lru_w_x=v_lru_w_x, v_lru_b_x=v_lru_b_x, v_lru_lambda=v_lru_lambda, v_ssd_conv_w=v_ssd_conv_w, v_ssd_conv_b=v_ssd_conv_b, v_ssd_dt_bias=v_ssd_dt_bias, v_ssd_A_log=v_ssd_A_log, v_ssd_D=v_ssd_D, v_ssd_norm_g=v_ssd_norm_g, v_w_branch=v_w_branch, v_w_out=v_w_out, v_norm2_g=v_norm2_g, v_w_ffn_in=v_w_ffn_in, v_w_ffn_out=v_w_ffn_out, v_norm_f=v_norm_f)
    weights = {n: given[n] for n in TWIN_WEIGHTS}
    shared = {n: given[n] for n in SHARED_INPUTS}
    per_example = {n: given[n] for n in ['x']}
    grad_fn = _jax.value_and_grad(_loss, argnums=(0, 1))

    def one_microbatch(ex, loss_target):
        ex = dict(ex)
        diff = ex.pop(TWIN_DIFF_INPUT)
        return grad_fn(weights, diff, {**shared, **ex}, loss_target)

    if N_MICROBATCH == 1:
        loss, (grad_w, grad_x) = one_microbatch(per_example, given["loss_target"])
    else:
        def body(carry, xs):
            loss_sum, grad_sum = carry
            l_k, (gw_k, gx_k) = one_microbatch(xs[0], xs[1])
            with _jax.named_scope("update"):
                return (loss_sum + l_k, _jax.tree.map(_jnp.add, grad_sum, gw_k)), gx_k

        init = (_jnp.zeros((), _jnp.float32), _jax.tree.map(_jnp.zeros_like, weights))
        (loss, grad_w), grad_x = _jax.lax.scan(body, init, (per_example, given["loss_target"]))
    with _jax.named_scope("update"):
        delta_w, new_m, new_v = {}, {}, {}
        for n in TWIN_WEIGHTS:
            delta_w[n], new_m[n], new_v[n] = _adamw(weights[n], grad_w[n], given["m_" + n], given["v_" + n])
    return (loss, grad_x, *[grad_w[n] for n in TWIN_WEIGHTS], *[delta_w[n] for n in TWIN_WEIGHTS],
            *[new_m[n] for n in TWIN_WEIGHTS], *[new_v[n] for n in TWIN_WEIGHTS])
```

```python
import functools

import numpy as np
import jax
import jax.numpy as jnp
from jax import lax
from jax.experimental import pallas as pl
from jax.experimental.pallas import tpu as pltpu

F32 = jnp.float32
BF16 = jnp.bfloat16
HIGHEST = lax.Precision.HIGHEST

D_MODEL = 1024
DEPTH = 2
CHUNK = 64
LRU_C = 8.0
SSD_INNER = 2048
SSD_HEADS = 32
SSD_HEAD_DIM = 64
SSD_GROUPS = 4
SSD_STATE = 128
SSD_CONV_DIM = 3072
D_FF = 2816
EPS = 1e-6
N_DEV = 8
LANES = 128
DT_PAD = LANES
IN_WIDTHS = (1024, 1024, 2048, 3072, 32, 2048)
IN_OFFS = tuple(int(v) for v in np.cumsum((0,) + IN_WIDTHS))

ADAM_LR = 0.001
ADAM_B1 = 0.9
ADAM_B2 = 0.999
ADAM_EPS = 1e-08
ADAM_WD = 0.01
ADAM_STEP = 10

VMEM_LIMIT = 56 * 1024 * 1024
MESH = pl.DeviceIdType.MESH


def _cparams(sem=None):
    return pltpu.CompilerParams(dimension_semantics=sem, vmem_limit_bytes=VMEM_LIMIT)


def _sds(shape, dtype):
    return jax.ShapeDtypeStruct(tuple(shape), dtype)


def _pick(n, cap):
    if n <= cap:
        return n
    best = LANES
    for t in range(LANES, cap + 1, LANES):
        if n % t == 0:
            best = t
    assert n % best == 0, (n, cap)
    return best


def _mm(name, pairs, mode, out_dtype=F32, res=None):
    M = pairs[0][0].shape[0]
    N = pairs[0][1].shape[1] if mode == "nn" else pairs[0][1].shape[0]
    tm = min(M, 1024 if len(pairs) == 1 else 512)
    tn = _pick(N, 1536)
    tks, nks, starts = [], [], []
    s = 0
    for a, _ in pairs:
        k = a.shape[1]
        tk = _pick(k, 512)
        tks.append(tk)
        nks.append(k // tk)
        starts.append(s)
        s += k // tk
    nk = s
    npair = len(pairs)

    def body(*refs):
        ab = refs[:2 * npair]
        pos = 2 * npair
        r_ref = None
        if res is not None:
            r_ref = refs[pos]
            pos += 1
        o_ref, acc = refs[pos], refs[pos + 1]
        k = pl.program_id(2)

        @pl.when(k == 0)
        def _():
            acc[...] = jnp.zeros_like(acc)

        for p in range(npair):
            a_ref, b_ref = ab[2 * p], ab[2 * p + 1]

            def step(a_ref=a_ref, b_ref=b_ref):
                a = a_ref[...].astype(BF16)
                b = b_ref[...].astype(BF16)
                dims = (((1,), (0,)), ((), ())) if mode == "nn" else (((1,), (1,)), ((), ()))
                acc[...] += lax.dot_general(a, b, dims, preferred_element_type=F32)

            if npair == 1:
                step()
            else:
                pl.when((k >= starts[p]) & (k < starts[p] + nks[p]))(step)

        @pl.when(k == nk - 1)
        def _():
            r = acc[...]
            if r_ref is not None:
                r = r + r_ref[...].astype(F32)
            o_ref[...] = r.astype(o_ref.dtype)

    in_specs, args = [], []
    for p, (a, b) in enumerate(pairs):
        def kk(k, p=p):
            return jnp.clip(k - starts[p], 0, nks[p] - 1)
        in_specs.append(pl.BlockSpec((tm, tks[p]), lambda i, j, k, kk=kk: (i, kk(k))))
        if mode == "nn":
            in_specs.append(pl.BlockSpec((tks[p], tn), lambda i, j, k, kk=kk: (kk(k), j)))
        else:
            in_specs.append(pl.BlockSpec((tn, tks[p]), lambda i, j, k, kk=kk: (j, kk(k))))
        args += [a, b]
    if res is not None:
        in_specs.append(pl.BlockSpec((tm, tn), lambda i, j, k: (i, j)))
        args.append(res)
    return pl.pallas_call(
        body, grid=(M // tm, N // tn, nk), in_specs=in_specs,
        out_specs=pl.BlockSpec((tm, tn), lambda i, j, k: (i, j)),
        out_shape=_sds((M, N), out_dtype),
        scratch_shapes=[pltpu.VMEM((tm, tn), F32)],
        compiler_params=_cparams(("parallel", "parallel", "arbitrary")), name=name,
    )(*args)


def _mm_tn(name, a, b):
    M, Ka = a.shape
    N = b.shape[1]
    tm = min(M, 512)
    tka = _pick(Ka, 1024)
    tn = _pick(N, 1024)
    nm = M // tm

    def body(a_ref, b_ref, o_ref, acc):
        k = pl.program_id(2)

        @pl.when(k == 0)
        def _():
            acc[...] = jnp.zeros_like(acc)

        acc[...] += lax.dot_general(a_ref[...].astype(BF16), b_ref[...].astype(BF16),
                                    (((0,), (0,)), ((), ())), preferred_element_type=F32)

        @pl.when(k == nm - 1)
        def _():
            o_ref[...] = acc[...]

    return pl.pallas_call(
        body, grid=(Ka // tka, N // tn, nm),
        in_specs=[pl.BlockSpec((tm, tka), lambda i, j, k: (k, i)),
                  pl.BlockSpec((tm, tn), lambda i, j, k: (k, j))],
        out_specs=pl.BlockSpec((tka, tn), lambda i, j, k: (i, j)),
        out_shape=_sds((Ka, N), F32),
        scratch_shapes=[pltpu.VMEM((tka, tn), F32)],
        compiler_params=_cparams(("parallel", "parallel", "arbitrary")), name=name,
    )(a, b)


def _rowwise_fwd(name, fn, rows, pars, outs, tb):
    rows = [r if isinstance(r, tuple) else (r, 0, r.shape[1]) for r in rows]
    T = rows[0][0].shape[0]
    tb = min(tb, T)
    nr, npar = len(rows), len(pars)

    def body(*refs):
        rv = [r[...].astype(F32) for r in refs[:nr]]
        pv = [p[...] for p in refs[nr:nr + npar]]
        res = fn(*rv, *pv)
        for o, r in zip(refs[nr + npar:], res):
            o[...] = r.astype(o.dtype)

    in_specs = [pl.BlockSpec((tb, w), lambda i, c=c: (i, c)) for _, c, w in rows]
    in_specs += [pl.BlockSpec(p.shape, lambda i, n=p.ndim: (0,) * n) for p in pars]
    return pl.pallas_call(
        body, grid=(T // tb,), in_specs=in_specs,
        out_specs=[pl.BlockSpec((tb, w), lambda i: (i, 0)) for w, _ in outs],
        out_shape=[_sds((T, w), dt) for w, dt in outs],
        compiler_params=_cparams(("parallel",)), name=name,
    )(*[r[0] for r in rows], *pars)


def _rowwise_bwd(name, fn, rows, pars, cot_rows, cot_fn, row_out, tb):
    rows = [r if isinstance(r, tuple) else (r, 0, r.shape[1]) for r in rows]
    cot_rows = [r if isinstance(r, tuple) else (r, 0, r.shape[1]) for r in cot_rows]
    T = rows[0][0].shape[0]
    tb = min(tb, T)
    nr, npar, nc = len(rows), len(pars), len(cot_rows)
    want = [k for k, dt in enumerate(row_out) if dt is not None]

    def body(*refs):
        i = pl.program_id(0)
        rv = [r[...].astype(F32) for r in refs[:nr]]
        pv = [p[...] for p in refs[nr:nr + npar]]
        cv = [c[...].astype(F32) for c in refs[nr + npar:nr + npar + nc]]
        o_refs = refs[nr + npar + nc:]
        _, vjp = jax.vjp(fn, *rv, *pv)
        grads = vjp(tuple(cot_fn(*cv)))
        for o, k in zip(o_refs[:len(want)], want):
            o[...] = grads[k].astype(o.dtype)
        p_refs = o_refs[len(want):]

        @pl.when(i == 0)
        def _():
            for o in p_refs:
                o[...] = jnp.zeros_like(o)

        for o, g in zip(p_refs, grads[nr:]):
            o[...] += g

    in_specs = [pl.BlockSpec((tb, w), lambda i, c=c: (i, c)) for _, c, w in rows]
    in_specs += [pl.BlockSpec(p.shape, lambda i, n=p.ndim: (0,) * n) for p in pars]
    in_specs += [pl.BlockSpec((tb, w), lambda i, c=c: (i, c)) for _, c, w in cot_rows]
    out_specs = [pl.BlockSpec((tb, rows[k][2]), lambda i: (i, 0)) for k in want]
    out_specs += [pl.BlockSpec(p.shape, lambda i, n=p.ndim: (0,) * n) for p in pars]
    out_shape = [_sds((T, rows[k][2]), row_out[k]) for k in want] + [_sds(p.shape, F32) for p in pars]
    return pl.pallas_call(
        body, grid=(T // tb,), in_specs=in_specs, out_specs=out_specs, out_shape=out_shape,
        compiler_params=_cparams(("arbitrary",)), name=name,
    )(*[r[0] for r in rows], *pars, *[r[0] for r in cot_rows])


def _f_rms(x, g):
    r = lax.rsqrt(jnp.mean(x * x, axis=-1, keepdims=True) + EPS)
    return (x * r * g,)


def _f_lru_gates(u, wa, ba, wx, bx, lam):
    ub = u.astype(BF16)
    ra, rx = [], []
    for k in range(D_MODEL // LANES):
        uk = ub[:, LANES * k:LANES * (k + 1)]
        ra.append(jnp.dot(uk, wa[k].astype(BF16), preferred_element_type=F32))
        rx.append(jnp.dot(uk, wx[k].astype(BF16), preferred_element_type=F32))
    r = jax.nn.sigmoid(jnp.concatenate(ra, axis=1) + ba)
    i = jax.nn.sigmoid(jnp.concatenate(rx, axis=1) + bx)
    log_a = -LRU_C * r * jax.nn.softplus(-lam)
    a = jnp.exp(log_a)
    t = jnp.tanh(log_a)
    b = jnp.sqrt(-2.0 * t / (1.0 - t)) * (i * u)
    return a, b


def _f_post(hl, lgate, yssd, xpre_s, z, d_e, ng):
    ya = jax.nn.gelu(lgate) * hl
    y = (yssd + d_e * jax.nn.silu(xpre_s)) * jax.nn.silu(z)
    gw = SSD_INNER // SSD_GROUPS
    parts = []
    for g in range(SSD_GROUPS):
        yg = y[:, gw * g:gw * (g + 1)]
        parts.append(yg * lax.rsqrt(jnp.mean(yg * yg, axis=-1, keepdims=True) + EPS))
    return ya, jnp.concatenate(parts, axis=1) * ng


def _f_merge(m_a, m_b, gates, bg):
    g = jax.nn.sigmoid(gates + bg)
    return (g[:, :D_MODEL] * m_a + g[:, D_MODEL:] * m_b,)


def _f_act(gu):
    return (jax.nn.silu(gu[:, :D_FF]) * gu[:, D_FF:],)


def _ident(*c):
    return c


def _loss_head(h, target, nf):
    T = h.shape[0]
    tb = min(256, T)

    def body(h_ref, t_ref, nf_ref, dh_ref, dnf_ref, loss_ref):
        i = pl.program_id(0)

        @pl.when(i == 0)
        def _():
            dnf_ref[...] = jnp.zeros_like(dnf_ref)
            loss_ref[...] = jnp.zeros_like(loss_ref)

        (y,), vjp = jax.vjp(_f_rms, h_ref[...], nf_ref[...])
        err = y - t_ref[...]
        dh, dnf = vjp((err * (1.0 / D_MODEL),))
        dh_ref[...] = dh
        dnf_ref[...] += dnf
        part = 0.5 * jnp.sum(jnp.mean(err * err, axis=-1, keepdims=True), axis=0, keepdims=True)
        loss_ref[...] += jnp.broadcast_to(part, loss_ref.shape)

    return pl.pallas_call(
        body, grid=(T // tb,),
        in_specs=[pl.BlockSpec((tb, D_MODEL), lambda i: (i, 0)), pl.BlockSpec((tb, D_MODEL), lambda i: (i, 0)),
                  pl.BlockSpec((1, D_MODEL), lambda i: (0, 0))],
        out_specs=[pl.BlockSpec((tb, D_MODEL), lambda i: (i, 0)), pl.BlockSpec((1, D_MODEL), lambda i: (0, 0)),
                   pl.BlockSpec((8, LANES), lambda i: (0, 0))],
        out_shape=[_sds((T, D_MODEL), F32), _sds((1, D_MODEL), F32), _sds((8, LANES), F32)],
        compiler_params=_cparams(("arbitrary",)), name="loss_head",
    )(h, target, nf)


CONV_TC = 1024
HALO = 8


def _conv_fwd(name, x, w8, b):
    T, C = x.shape
    tb = min(512, T)
    nb = tb // HALO

    def body(x_ref, xp_ref, w_ref, b_ref, y_ref, sc):
        i = pl.program_id(0)
        xv = x_ref[...]
        sc[pl.ds(HALO, tb), :] = xv
        sc[pl.ds(0, HALO), :] = jnp.where(i > 0, xp_ref[...], 0.0)
        acc = b_ref[...] + w_ref[3:4, :] * xv
        for k in range(3):
            acc = acc + w_ref[k:k + 1, :] * sc[pl.ds(HALO - 3 + k, tb), :]
        y_ref[...] = acc

    return pl.pallas_call(
        body, grid=(T // tb, C // CONV_TC),
        in_specs=[pl.BlockSpec((tb, CONV_TC), lambda i, j: (i, j)),
                  pl.BlockSpec((HALO, CONV_TC), lambda i, j: (jnp.maximum(i * nb - 1, 0), j)),
                  pl.BlockSpec((8, CONV_TC), lambda i, j: (0, j)),
                  pl.BlockSpec((1, CONV_TC), lambda i, j: (0, j))],
        out_specs=pl.BlockSpec((tb, CONV_TC), lambda i, j: (i, j)),
        out_shape=_sds((T, C), F32),
        scratch_shapes=[pltpu.VMEM((tb + HALO, CONV_TC), F32)],
        compiler_params=_cparams(("parallel", "parallel")), name=name,
    )(x, x, w8, b)


def _conv_bwd(name, x, dy, w8):
    T, C = x.shape
    tb = min(512, T)
    nb = tb // HALO
    nt = T // tb

    def body(x_ref, xp_ref, dy_ref, dyn_ref, w_ref, dx_ref, dwb_ref, scx, scd):
        i = pl.program_id(1)
        dyv = dy_ref[...]
        scx[pl.ds(HALO, tb), :] = x_ref[...]
        scx[pl.ds(0, HALO), :] = jnp.where(i > 0, xp_ref[...], 0.0)
        scd[pl.ds(0, tb), :] = dyv
        scd[pl.ds(tb, HALO), :] = jnp.where(i < nt - 1, dyn_ref[...], 0.0)
        dx = w_ref[3:4, :] * dyv
        rows = []
        for k in range(3):
            dx = dx + w_ref[k:k + 1, :] * scd[pl.ds(3 - k, tb), :]
            rows.append(jnp.sum(dyv * scx[pl.ds(HALO - 3 + k, tb), :], axis=0, keepdims=True))
        rows.append(jnp.sum(dyv * x_ref[...], axis=0, keepdims=True))
        rows.append(jnp.sum(dyv, axis=0, keepdims=True))
        rows.append(jnp.zeros((3, CONV_TC), F32))
        dx_ref[...] = dx

        @pl.when(i == 0)
        def _():
            dwb_ref[...] = jnp.zeros_like(dwb_ref)

        dwb_ref[...] += jnp.concatenate(rows, axis=0)

    return pl.pallas_call(
        body, grid=(C // CONV_TC, nt),
        in_specs=[pl.BlockSpec((tb, CONV_TC), lambda j, i: (i, j)),
                  pl.BlockSpec((HALO, CONV_TC), lambda j, i: (jnp.maximum(i * nb - 1, 0), j)),
                  pl.BlockSpec((tb, CONV_TC), lambda j, i: (i, j)),
                  pl.BlockSpec((HALO, CONV_TC), lambda j, i: (jnp.minimum((i + 1) * nb, T // HALO - 1), j)),
                  pl.BlockSpec((8, CONV_TC), lambda j, i: (0, j))],
        out_specs=[pl.BlockSpec((tb, CONV_TC), lambda j, i: (i, j)),
                   pl.BlockSpec((8, CONV_TC), lambda j, i: (0, j))],
        out_shape=[_sds((T, C), F32), _sds((8, C), F32)],
        scratch_shapes=[pltpu.VMEM((tb + HALO, CONV_TC), F32), pltpu.VMEM((tb + HALO, CONV_TC), F32)],
        compiler_params=_cparams(("parallel", "arbitrary")), name=name,
    )(x, x, dy, dy, w8)


SCAN_TB = 512


def _lru_scan_fwd(a, b):
    T, C = a.shape
    tb = min(SCAN_TB, T)

    def body(a_ref, b_ref, h_ref, hp_ref, carry):
        @pl.when(pl.program_id(0) == 0)
        def _():
            carry[...] = jnp.zeros_like(carry)

        def group(gi, h):
            r0 = pl.multiple_of(gi * 8, 8)
            at = a_ref[pl.ds(r0, 8), :]
            bt = b_ref[pl.ds(r0, 8), :]
            hs, hps = [], []
            for r in range(8):
                hps.append(h)
                h = at[r:r + 1, :] * h + bt[r:r + 1, :]
                hs.append(h)
            h_ref[pl.ds(r0, 8), :] = jnp.concatenate(hs, axis=0)
            hp_ref[pl.ds(r0, 8), :] = jnp.concatenate(hps, axis=0)
            return h

        carry[0:1, :] = lax.fori_loop(0, tb // 8, group, carry[0:1, :])

    spec = pl.BlockSpec((tb, C), lambda i: (i, 0))
    return pl.pallas_call(
        body, grid=(T // tb,), in_specs=[spec, spec], out_specs=[spec, spec],
        out_shape=[_sds((T, C), F32), _sds((T, C), F32)],
        scratch_shapes=[pltpu.VMEM((8, C), F32)],
        compiler_params=_cparams(("arbitrary",)), name="lru_scan_fwd",
    )(a, b)


def _lru_scan_bwd(a, dh):
    T, C = a.shape
    tb = min(SCAN_TB, T)
    nt = T // tb

    def body(a_ref, dh_ref, g_ref, carry):
        @pl.when(pl.program_id(0) == 0)
        def _():
            carry[...] = jnp.zeros_like(carry)

        def group(gi, c):
            r0 = pl.multiple_of((tb // 8 - 1 - gi) * 8, 8)
            at = a_ref[pl.ds(r0, 8), :]
            dt = dh_ref[pl.ds(r0, 8), :]
            gs = [None] * 8
            for r in range(7, -1, -1):
                g = dt[r:r + 1, :] + c
                c = at[r:r + 1, :] * g
                gs[r] = g
            g_ref[pl.ds(r0, 8), :] = jnp.concatenate(gs, axis=0)
            return c

        carry[0:1, :] = lax.fori_loop(0, tb // 8, group, carry[0:1, :])

    spec = pl.BlockSpec((tb, C), lambda i: (nt - 1 - i, 0))
    return pl.pallas_call(
        body, grid=(nt,), in_specs=[spec, spec], out_specs=spec,
        out_shape=_sds((T, C), F32),
        scratch_shapes=[pltpu.VMEM((8, C), F32)],
        compiler_params=_cparams(("arbitrary",)), name="lru_scan_bwd",
    )(a, dh)


def _ssd_chunk(xpre, dtraw, state, dtb, alog):
    xc = jax.nn.silu(xpre)
    xs = xc[:, :SSD_INNER]
    bm = xc[:, SSD_INNER:SSD_INNER + SSD_GROUPS * SSD_STATE]
    cm = xc[:, SSD_INNER + SSD_GROUPS * SSD_STATE:]
    dt = jax.nn.softplus(dtraw + dtb)
    a = dt * (-jnp.exp(alog))
    li = lax.broadcasted_iota(jnp.int32, (CHUNK, CHUNK), 0)
    si = lax.broadcasted_iota(jnp.int32, (CHUNK, CHUNK), 1)
    tril = li >= si
    ltri = tril.astype(F32)
    eye = (li == si).astype(F32)
    a_cs = jnp.dot(ltri, a, precision=HIGHEST, preferred_element_type=F32)
    a_cs_t = lax.dot_general(a, ltri, (((0,), (1,)), ((), ())), precision=HIGHEST,
                             preferred_element_type=F32)
    dt_t = lax.dot_general(dt, eye, (((0,), (0,)), ((), ())), precision=HIGHEST,
                           preferred_element_type=F32)
    a_last = a_cs[CHUNK - 1:CHUNK, :]
    ex = jnp.exp(a_cs)
    dte = jnp.exp(a_last - a_cs) * dt
    cd = jnp.exp(a_last)
    left = lax.broadcasted_iota(jnp.int32, (CHUNK, LANES), 1) < SSD_HEAD_DIM
    right = jnp.logical_not(left)
    left1 = left[0:1]
    gw = SSD_INNER // SSD_GROUPS
    ys, news = [], []
    for g in range(SSD_GROUPS):
        bg = bm[:, SSD_STATE * g:SSD_STATE * (g + 1)].astype(BF16)
        cg = cm[:, SSD_STATE * g:SSD_STATE * (g + 1)].astype(BF16)
        scores = lax.dot_general(cg, bg, (((1,), (1,)), ((), ())), preferred_element_type=F32)
        coff = jnp.dot(cg, state[:, gw * g:gw * (g + 1)].astype(BF16), preferred_element_type=F32)
        for j in range(gw // LANES):
            h0 = (gw * g + LANES * j) // SSD_HEAD_DIM
            lo = gw * g + LANES * j
            xp = xs[:, lo:lo + LANES]
            acc = None
            for hh, msk in ((h0, left), (h0 + 1, right)):
                seg = a_cs[:, hh:hh + 1] - a_cs_t[hh:hh + 1, :]
                dm = jnp.exp(jnp.where(tril, seg, -1e30)) * dt_t[hh:hh + 1, :]
                sd = (scores * dm).astype(BF16)
                xm = jnp.where(msk, xp, 0.0).astype(BF16)
                t = jnp.dot(sd, xm, preferred_element_type=F32)
                acc = t if acc is None else acc + t
            exe = jnp.where(left, ex[:, h0:h0 + 1], ex[:, h0 + 1:h0 + 2])
            ys.append(acc + coff[:, LANES * j:LANES * (j + 1)] * exe)
            dtee = jnp.where(left, dte[:, h0:h0 + 1], dte[:, h0 + 1:h0 + 2])
            xw = (xp * dtee).astype(BF16)
            cde = jnp.where(left1, cd[:, h0:h0 + 1], cd[:, h0 + 1:h0 + 2])
            news.append(state[:, lo:lo + LANES] * cde
                        + lax.dot_general(bg, xw, (((0,), (0,)), ((), ())), preferred_element_type=F32))
    return jnp.concatenate(ys, axis=1), jnp.concatenate(news, axis=1)


def _ssd_fwd(xpre, dtraw, dtb, alog):
    T = xpre.shape[0]
    n = T // CHUNK

    def body(xp, dr, dtb_ref, al_ref, y_ref, st_ref, state):
        @pl.when(pl.program_id(0) == 0)
        def _():
            state[...] = jnp.zeros_like(state)

        st_ref[0] = state[...]
        y, new = _ssd_chunk(xp[...], dr[...], state[...], dtb_ref[...], al_ref[...])
        y_ref[...] = y
        state[...] = new

    small = pl.BlockSpec((1, LANES), lambda c: (0, 0))
    return pl.pallas_call(
        body, grid=(n,),
        in_specs=[pl.BlockSpec((CHUNK, SSD_CONV_DIM), lambda c: (c, 0)),
                  pl.BlockSpec((CHUNK, DT_PAD), lambda c: (c, 0)), small, small],
        out_specs=[pl.BlockSpec((CHUNK, SSD_INNER), lambda c: (c, 0)),
                   pl.BlockSpec((1, SSD_STATE, SSD_INNER), lambda c: (c, 0, 0))],
        out_shape=[_sds((T, SSD_INNER), F32), _sds((n, SSD_STATE, SSD_INNER), F32)],
        scratch_shapes=[pltpu.VMEM((SSD_STATE, SSD_INNER), F32)],
        compiler_params=_cparams(("arbitrary",)), name="ssd_fwd",
    )(xpre, dtraw, dtb, alog)


def _ssd_bwd(xpre, dtraw, states, dy, dxs_extra, dtb, alog):
    T = xpre.shape[0]
    n = T // CHUNK

    def body(xp, dr, st, dy_ref, dx_ref, dtb_ref, al_ref, dxp_ref, ddr_ref, ddtb_ref, dal_ref, dstate):
        @pl.when(pl.program_id(0) == 0)
        def _():
            dstate[...] = jnp.zeros_like(dstate)
            ddtb_ref[...] = jnp.zeros_like(ddtb_ref)
            dal_ref[...] = jnp.zeros_like(dal_ref)

        _, vjp = jax.vjp(_ssd_chunk, xp[...], dr[...], st[0], dtb_ref[...], al_ref[...])
        dxp, ddr, ds, db, da = vjp((dy_ref[...], dstate[...]))
        dxp_ref[:, :SSD_INNER] = dxp[:, :SSD_INNER] + dx_ref[...]
        dxp_ref[:, SSD_INNER:] = dxp[:, SSD_INNER:]
        ddr_ref[...] = ddr
        dstate[...] = ds
        ddtb_ref[...] += db
        dal_ref[...] += da

    def rev(c):
        return (n - 1 - c, 0)

    small = pl.BlockSpec((1, LANES), lambda c: (0, 0))
    return pl.pallas_call(
        body, grid=(n,),
        in_specs=[pl.BlockSpec((CHUNK, SSD_CONV_DIM), rev), pl.BlockSpec((CHUNK, DT_PAD), rev),
                  pl.BlockSpec((1, SSD_STATE, SSD_INNER), lambda c: (n - 1 - c, 0, 0)),
                  pl.BlockSpec((CHUNK, SSD_INNER), rev), pl.BlockSpec((CHUNK, SSD_INNER), rev), small, small],
        out_specs=[pl.BlockSpec((CHUNK, SSD_CONV_DIM), rev), pl.BlockSpec((CHUNK, DT_PAD), rev), small, small],
        out_shape=[_sds((T, SSD_CONV_DIM), F32), _sds((T, DT_PAD), F32), _sds((1, LANES), F32), _sds((1, LANES), F32)],
        scratch_shapes=[pltpu.VMEM((SSD_STATE, SSD_INNER), F32)],
        compiler_params=_cparams(("arbitrary",)), name="ssd_bwd",
    )(xpre, dtraw, states, dy, dxs_extra, dtb, alog)


HBM_SPEC = pl.BlockSpec(memory_space=pltpu.HBM)
N_PEER = N_DEV - 1


def _position():
    return lax.axis_index("x"), lax.axis_index("y"), lax.axis_index("c")


def _all_gather(name, blk):
    R, W = blk.shape

    def body(x_ref, out_ref, send_sems, recv_sems, local_sem):
        x, y, c = _position()
        me, sibling = (x, y, c), (x, y, 1 - c)
        chips = [(1 - x, y), (x, 1 - y), (1 - x, 1 - y)]

        def rows(px, py, pc):
            return out_ref.at[pl.ds((4 * px + 2 * py + pc) * R, R), :]

        def copy(k, block, to, src=None):
            return pltpu.make_async_remote_copy(
                src_ref=rows(*block) if src is None else src, dst_ref=rows(*block),
                send_sem=send_sems.at[k], recv_sem=recv_sems.at[k], device_id=to, device_id_type=MESH)

        mine = pltpu.make_async_copy(x_ref, rows(*me), local_sem)
        mine.start()
        first = [copy(0, me, sibling, src=x_ref)]
        first += [copy(1 + j, me, (*chip, c), src=x_ref) for j, chip in enumerate(chips)]
        for cp in first:
            cp.start()
        passed = [copy(4 + j, (*chip, c), sibling) for j, chip in enumerate(chips)]
        for j, chip in enumerate(chips):
            copy(1 + j, (*chip, c), me).wait_recv()
            passed[j].start()
        copy(0, sibling, me).wait_recv()
        for j, chip in enumerate(chips):
            copy(4 + j, (*chip, 1 - c), me).wait_recv()
        for cp in first + passed:
            cp.wait_send()
        mine.wait()

    return pl.pallas_call(
        body, out_shape=_sds((N_DEV * R, W), blk.dtype), in_specs=[HBM_SPEC], out_specs=HBM_SPEC,
        scratch_shapes=[pltpu.SemaphoreType.DMA((N_PEER,)), pltpu.SemaphoreType.DMA((N_PEER,)),
                        pltpu.SemaphoreType.DMA(())],
        name=name,
    )(blk)


def _scatter_exchange(name, g):
    R = g.shape[0] // N_DEV

    def body(g_ref, q_ref, send_sems, recv_sems, local_sem):
        x, y, c = _position()
        me = 4 * x + 2 * y + c

        def blk(ref, idx):
            return ref.at[pl.ds(idx * R, R), :]

        mine = pltpu.make_async_copy(blk(g_ref, me), blk(q_ref, me), local_sem)
        mine.start()
        sends, recvs = [], []
        for k in range(1, N_DEV):
            px, py, pc = (x + (k >> 2)) % 2, (y + ((k >> 1) & 1)) % 2, (c + (k & 1)) % 2
            peer = 4 * px + 2 * py + pc
            sends.append(pltpu.make_async_remote_copy(
                src_ref=blk(g_ref, peer), dst_ref=blk(q_ref, me), send_sem=send_sems.at[k - 1],
                recv_sem=recv_sems.at[k - 1], device_id=(px, py, pc), device_id_type=MESH))
            recvs.append(pltpu.make_async_remote_copy(
                src_ref=blk(g_ref, me), dst_ref=blk(q_ref, peer), send_sem=send_sems.at[k - 1],
                recv_sem=recv_sems.at[k - 1], device_id=(px, py, pc), device_id_type=MESH))
        for cp in sends:
            cp.start()
        for cp in recvs:
            cp.wait_recv()
        for cp in sends:
            cp.wait_send()
        mine.wait()

    return pl.pallas_call(
        body, out_shape=_sds(g.shape, g.dtype), in_specs=[HBM_SPEC], out_specs=HBM_SPEC,
        scratch_shapes=[pltpu.SemaphoreType.DMA((N_PEER,)), pltpu.SemaphoreType.DMA((N_PEER,)),
                        pltpu.SemaphoreType.DMA(())],
        name=name,
    )(g)


def _sum_blocks(name, q, tr):
    R = q.shape[0] // N_DEV
    W = q.shape[1]
    tr = min(tr, R)
    assert R % tr == 0
    nb = R // tr

    def body(*refs):
        acc = refs[0][...].astype(F32)
        for r in refs[1:N_DEV]:
            acc = acc + r[...].astype(F32)
        refs[N_DEV][...] = acc

    return pl.pallas_call(
        body, grid=(nb,),
        in_specs=[pl.BlockSpec((tr, W), lambda i, s=s: (s * nb + i, 0)) for s in range(N_DEV)],
        out_specs=pl.BlockSpec((tr, W), lambda i: (i, 0)), out_shape=_sds((R, W), F32),
        compiler_params=_cparams(("parallel",)), name=name,
    )(*([q] * N_DEV))


def _adamw(name, w, g, m, v):
    R, C = w.shape
    tr = R
    if R > 512:
        tr = max(t for t in range(8, 513, 8) if R % t == 0)

    def body(w_ref, g_ref, m_ref, v_ref, d_ref, m2_ref, v2_ref):
        gv = g_ref[...]
        m2 = ADAM_B1 * m_ref[...] + (1.0 - ADAM_B1) * gv
        v2 = ADAM_B2 * v_ref[...] + (1.0 - ADAM_B2) * jnp.square(gv)
        m_hat = m2 / (1.0 - ADAM_B1 ** ADAM_STEP)
        v_hat = v2 / (1.0 - ADAM_B2 ** ADAM_STEP)
        d_ref[...] = -ADAM_LR * (m_hat / (jnp.sqrt(v_hat) + ADAM_EPS) + ADAM_WD * w_ref[...])
        m2_ref[...] = m2
        v2_ref[...] = v2

    spec = pl.BlockSpec((tr, C), lambda i: (i, 0))
    return pl.pallas_call(
        body, grid=(R // tr,), in_specs=[spec] * 4, out_specs=[spec] * 3,
        out_shape=[_sds((R, C), F32)] * 3, compiler_params=_cparams(("parallel",)), name=name,
    )(w, g, m, v)


BIG = ("w_in", "w_branch", "w_out", "w_ffn_in", "w_ffn_out")
BIG_ROW_SHARDED = {"w_in": False, "w_branch": True, "w_out": True, "w_ffn_in": False, "w_ffn_out": True}
SMALL = ("norm1_g", "b_gate", "lru_conv_w", "lru_conv_b", "lru_w_a", "lru_b_a", "lru_w_x", "lru_b_x", "lru_lambda",
         "ssd_conv_w", "ssd_conv_b", "ssd_dt_bias", "ssd_A_log", "ssd_D", "ssd_norm_g", "norm2_g", "norm_f")
PACK_ROWS = 1024
SMALL_ROWS = 256


def _pack(arrs, dtype, row_mult):
    parts = []
    for a in arrs:
        f = a.reshape(-1).astype(dtype)
        pad = (-f.shape[0]) % LANES
        if pad:
            f = jnp.concatenate([f, jnp.zeros((pad,), dtype)])
        parts.append(f)
    f = jnp.concatenate(parts)
    pad = (-f.shape[0]) % (LANES * row_mult)
    if pad:
        f = jnp.concatenate([f, jnp.zeros((pad,), dtype)])
    return f.reshape(-1, LANES)


def _unpack(flat, shapes, lead=()):
    f = flat.reshape(lead + (-1,))
    out, off = [], 0
    for s in shapes:
        n = int(np.prod(s))
        out.append(f[..., off:off + n].reshape(lead + tuple(s)))
        off += n + (-n) % LANES
    return out


def _full_from_shards(name, st):
    if BIG_ROW_SHARDED[name]:
        return st.reshape((-1,) + st.shape[2:])
    return jnp.transpose(st, (1, 0, 2)).reshape(st.shape[1], -1)


def _shards_from_full(name, full):
    if BIG_ROW_SHARDED[name]:
        return full.reshape((N_DEV, full.shape[0] // N_DEV) + full.shape[1:])
    return jnp.transpose(full.reshape(full.shape[0], N_DEV, -1), (1, 0, 2))


def _block_diag_tiles(w):
    z = jnp.zeros((8, 64, 64), w.dtype)
    w2 = w.reshape(8, 2, 64, 64)
    top = jnp.concatenate([w2[:, 0], z], axis=2)
    bot = jnp.concatenate([z, w2[:, 1]], axis=2)
    return jnp.concatenate([top, bot], axis=1)


def _block_diag_untile(t):
    return jnp.stack([t[:, :64, :64], t[:, 64:, 64:]], axis=1).reshape(16, 64, 64)


def _pad_lanes(a, width):
    return jnp.concatenate([a, jnp.zeros(a.shape[:-1] + (width - a.shape[-1],), a.dtype)], axis=-1)


def _pad_rows8(w):
    return jnp.concatenate([w, jnp.zeros((8 - w.shape[0],) + w.shape[1:], w.dtype)], axis=0)


def _layer_params(l, full, small):
    w_in = full["w_in"][l]
    seg = [w_in[:, IN_OFFS[k]:IN_OFFS[k + 1]] for k in range(6)]
    p = {
        "w_lx": seg[0], "w_lg": seg[1], "w_z": seg[2], "w_xbc": seg[3], "w_dt": _pad_lanes(seg[4], DT_PAD), "w_g": seg[5],
        "w_bra": full["w_branch"][l][:D_MODEL], "w_brb": full["w_branch"][l][D_MODEL:],
        "w_out": full["w_out"][l], "w_ffn_in": full["w_ffn_in"][l], "w_ffn_out": full["w_ffn_out"][l],
    }
    for n in ("norm1_g", "b_gate", "lru_conv_b", "lru_b_a", "lru_b_x", "lru_lambda", "ssd_conv_b", "ssd_norm_g", "norm2_g"):
        p[n] = small[n][l].reshape(1, -1)
    p["lru_conv_w8"] = _pad_rows8(small["lru_conv_w"][l])
    p["ssd_conv_w8"] = _pad_rows8(small["ssd_conv_w"][l])
    p["wa"] = _block_diag_tiles(small["lru_w_a"][l])
    p["wx"] = _block_diag_tiles(small["lru_w_x"][l])
    p["dtb"] = _pad_lanes(small["ssd_dt_bias"][l].reshape(1, -1), DT_PAD)
    p["alog"] = _pad_lanes(small["ssd_A_log"][l].reshape(1, -1), DT_PAD)
    p["d_e"] = jnp.repeat(small["ssd_D"][l], SSD_HEAD_DIM).reshape(1, -1)
    return p


def _layer_fwd(l, h, p):
    n = f"l{l}_"
    s = {"h_in": h}
    (xn,) = _rowwise_fwd(n + "rms1", _f_rms, [h], [p["norm1_g"]], [(D_MODEL, BF16)], 256)
    s["xn"] = xn
    for k in ("lx", "lg", "z", "xbc", "g", "dt"):
        s[k] = _mm(n + "in_" + k, [(xn, p["w_" + k])], "nn")
    s["u"] = _conv_fwd(n + "lru_conv", s["lx"], p["lru_conv_w8"], p["lru_conv_b"])
    lru_pars = [p["wa"], p["lru_b_a"], p["wx"], p["lru_b_x"], p["lru_lambda"]]
    s["a"], b = _rowwise_fwd(n + "lru_gates", _f_lru_gates, [s["u"]], lru_pars, [(D_MODEL, F32)] * 2, 256)
    s["hl"], s["hprev"] = _lru_scan_fwd(s["a"], b)
    s["xpre"] = _conv_fwd(n + "ssd_conv", s["xbc"], p["ssd_conv_w8"], p["ssd_conv_b"])
    s["yssd"], s["states"] = _ssd_fwd(s["xpre"], s["dt"], p["dtb"], p["alog"])
    post_rows = [s["hl"], s["lg"], s["yssd"], (s["xpre"], 0, SSD_INNER), s["z"]]
    s["ya"], s["yb"] = _rowwise_fwd(n + "post", _f_post, post_rows, [p["d_e"], p["ssd_norm_g"]],
                                    [(D_MODEL, BF16), (SSD_INNER, BF16)], 128)
    s["ma"] = _mm(n + "br_a", [(s["ya"], p["w_bra"])], "nn")
    s["mb"] = _mm(n + "br_b", [(s["yb"], p["w_brb"])], "nn")
    (s["merged"],) = _rowwise_fwd(n + "merge", _f_merge, [s["ma"], s["mb"], s["g"]], [p["b_gate"]],
                                  [(D_MODEL, BF16)], 256)
    s["h_mid"] = _mm(n + "out", [(s["merged"], p["w_out"])], "nn", res=h)
    (s["xn2"],) = _rowwise_fwd(n + "rms2", _f_rms, [s["h_mid"]], [p["norm2_g"]], [(D_MODEL, BF16)], 256)
    s["gu"] = _mm(n + "ffn_in", [(s["xn2"], p["w_ffn_in"])], "nn")
    (s["act"],) = _rowwise_fwd(n + "act", _f_act, [s["gu"]], [], [(D_FF, BF16)], 256)
    h_out = _mm(n + "ffn_out", [(s["act"], p["w_ffn_out"])], "nn", res=s["h_mid"])
    return h_out, s


def _add(name, a, b):
    return _rowwise_fwd(name, lambda x, y: (x + y,), [a, b], [], [(a.shape[1], F32)], 512)[0]


def _layer_bwd(l, dh, s, p):
    n = f"l{l}_b_"
    gw, gs = {}, {}
    d_act = _mm(n + "d_act", [(dh, p["w_ffn_out"])], "nt")
    gw["w_ffn_out"] = _mm_tn(n + "dw_ffn_out", s["act"], dh)
    (d_gu,) = _rowwise_bwd(n + "act", _f_act, [s["gu"]], [], [d_act], _ident, [F32], 128)
    d_xn2 = _mm(n + "d_xn2", [(d_gu, p["w_ffn_in"])], "nt")
    gw["w_ffn_in"] = _mm_tn(n + "dw_ffn_in", s["xn2"], d_gu)
    d_rms2, gs["norm2_g"] = _rowwise_bwd(n + "rms2", _f_rms, [s["h_mid"]], [p["norm2_g"]], [d_xn2], _ident, [F32], 256)
    dh_mid = _add(n + "add2", dh, d_rms2)
    d_merged = _mm(n + "d_merged", [(dh_mid, p["w_out"])], "nt")
    gw["w_out"] = _mm_tn(n + "dw_out", s["merged"], dh_mid)
    d_ma, d_mb, d_g, gs["b_gate"] = _rowwise_bwd(n + "merge", _f_merge, [s["ma"], s["mb"], s["g"]], [p["b_gate"]],
                                                 [d_merged], _ident, [F32, F32, F32], 256)
    d_ya = _mm(n + "d_ya", [(d_ma, p["w_bra"])], "nt")
    d_yb = _mm(n + "d_yb", [(d_mb, p["w_brb"])], "nt")
    gw["w_branch"] = jnp.concatenate([_mm_tn(n + "dw_bra", s["ya"], d_ma), _mm_tn(n + "dw_brb", s["yb"], d_mb)], axis=0)
    post_rows = [s["hl"], s["lg"], s["yssd"], (s["xpre"], 0, SSD_INNER), s["z"]]
    d_hl, d_lg, d_yssd, d_xs, d_z, d_de, gs["ssd_norm_g"] = _rowwise_bwd(
        n + "post", _f_post, post_rows, [p["d_e"], p["ssd_norm_g"]], [d_ya, d_yb], _ident, [F32] * 5, 64)
    gs["ssd_D"] = d_de.reshape(SSD_HEADS, SSD_HEAD_DIM).sum(axis=1)
    d_xpre, d_dt, d_dtb, d_alog = _ssd_bwd(s["xpre"], s["dt"], s["states"], d_yssd, d_xs, p["dtb"], p["alog"])
    gs["ssd_dt_bias"] = d_dtb[0, :SSD_HEADS]
    gs["ssd_A_log"] = d_alog[0, :SSD_HEADS]
    d_xbc, dwb = _conv_bwd(n + "ssd_conv", s["xbc"], d_xpre, p["ssd_conv_w8"])
    gs["ssd_conv_w"], gs["ssd_conv_b"] = dwb[:4], dwb[4]
    g_scan = _lru_scan_bwd(s["a"], d_hl)
    lru_pars = [p["wa"], p["lru_b_a"], p["wx"], p["lru_b_x"], p["lru_lambda"]]
    d_u, d_wa, gs["lru_b_a"], d_wx, gs["lru_b_x"], gs["lru_lambda"] = _rowwise_bwd(
        n + "lru_gates", _f_lru_gates, [s["u"]], lru_pars, [g_scan, s["hprev"]],
        lambda g, hp: (g * hp, g), [F32], 128)
    gs["lru_w_a"], gs["lru_w_x"] = _block_diag_untile(d_wa), _block_diag_untile(d_wx)
    d_lx, dwb = _conv_bwd(n + "lru_conv", s["lx"], d_u, p["lru_conv_w8"])
    gs["lru_conv_w"], gs["lru_conv_b"] = dwb[:4], dwb[4]
    segs = [("lx", d_lx), ("lg", d_lg), ("z", d_z), ("xbc", d_xbc), ("dt", d_dt), ("g", d_g)]
    d_xn = _mm(n + "d_xn", [(d, p["w_" + k]) for k, d in segs], "nt")
    dws = [_mm_tn(n + "dw_in_" + k, s["xn"], d) for k, d in segs]
    dws[4] = dws[4][:, :IN_WIDTHS[4]]
    gw["w_in"] = jnp.concatenate(dws, axis=1)
    d_rms1, gs["norm1_g"] = _rowwise_bwd(n + "rms1", _f_rms, [s["h_in"]], [p["norm1_g"]], [d_xn], _ident, [F32], 256)
    dh_in = _add(n + "add1", dh_mid, d_rms1)
    for k in ("norm1_g", "norm2_g", "b_gate", "ssd_norm_g", "lru_b_a", "lru_b_x", "lru_lambda"):
        gs[k] = gs[k].reshape(-1)
    return dh_in, gw, gs


def _step(inp):
    x = inp["x"][0]
    target = inp["loss_target"][0]
    dev = 4 * lax.axis_index("x") + 2 * lax.axis_index("y") + lax.axis_index("c")

    big_shapes = [inp[n].shape for n in BIG]
    mine = _pack([inp[n] for n in BIG], BF16, PACK_ROWS)
    rows = mine.shape[0]
    gathered = _all_gather("gather_weights", mine).reshape(N_DEV, rows, LANES)
    full = {}
    for n, st in zip(BIG, _unpack(gathered, big_shapes, lead=(N_DEV,))):
        full[n] = jnp.stack([_full_from_shards(n, st[:, l]) for l in range(DEPTH)])
    small = {n: inp[n] for n in SMALL}
    conv_shapes = [inp["lru_conv_w"].shape, inp["ssd_conv_w"].shape]
    conv_mine = _pack([inp["lru_conv_w"], inp["ssd_conv_w"]], F32, 8)
    conv_all = _all_gather("gather_conv", conv_mine).reshape(N_DEV, conv_mine.shape[0], LANES)
    lc, sc = _unpack(conv_all, conv_shapes, lead=(N_DEV,))
    small["lru_conv_w"] = jnp.moveaxis(lc, 0, 2).reshape(DEPTH, 4, -1)
    small["ssd_conv_w"] = jnp.moveaxis(sc, 0, 2).reshape(DEPTH, 4, -1)

    params = [_layer_params(l, full, small) for l in range(DEPTH)]
    h, saved = x, []
    for l in range(DEPTH):
        h, s = _layer_fwd(l, h, params[l])
        saved.append(s)
    dh, d_nf, loss_acc = _loss_head(h, target, small["norm_f"].reshape(1, -1))
    loss = lax.psum(loss_acc[0, 0], ("x", "y", "c"))

    gws, gss = [None] * DEPTH, [None] * DEPTH
    for l in reversed(range(DEPTH)):
        dh, gws[l], gss[l] = _layer_bwd(l, dh, saved[l], params[l])
    grad_x = dh[None]

    contrib = [jnp.stack([_shards_from_full(n, gws[l][n]) for l in range(DEPTH)], axis=1) for n in BIG]
    packed = jnp.stack([_pack([c[j] for c in contrib], BF16, PACK_ROWS) for j in range(N_DEV)])
    received = _scatter_exchange("scatter_grads", packed.reshape(N_DEV * rows, LANES))
    g_flat = _sum_blocks("sum_grads", received, PACK_ROWS)
    g_big = dict(zip(BIG, _unpack(g_flat, big_shapes)))

    small_full = {n: jnp.stack([gss[l][n] for l in range(DEPTH)]) for n in SMALL if n != "norm_f"}
    small_full["norm_f"] = d_nf.reshape(-1)
    part = _pack([small_full[n] for n in SMALL], F32, SMALL_ROWS)
    everyone = _all_gather("gather_small_grads", part)
    g_small_flat = _sum_blocks("sum_small_grads", everyone, SMALL_ROWS)
    g_small = dict(zip(SMALL, _unpack(g_small_flat, [small_full[n].shape for n in SMALL])))
    for n in ("lru_conv_w", "ssd_conv_w"):
        w = inp[n].shape[-1]
        g_small[n] = lax.dynamic_slice_in_dim(g_small[n], dev * w, w, axis=2)

    out = {"loss": loss, "grad_x": grad_x}
    for n in BIG:
        shp = inp[n].shape
        views = [a.reshape(-1, shp[-1]) for a in (inp[n], g_big[n], inp["m_" + n], inp["v_" + n])]
        d, m2, v2 = _adamw("adamw_" + n, *views)
        out["grad_" + n] = g_big[n]
        out["delta_" + n], out["new_m_" + n], out["new_v_" + n] = d.reshape(shp), m2.reshape(shp), v2.reshape(shp)
    shapes = [inp[n].shape for n in SMALL]
    packs = [_pack([src[pre + n] for n in SMALL], F32, SMALL_ROWS)
             for src, pre in ((inp, ""), (g_small, ""), (inp, "m_"), (inp, "v_"))]
    d, m2, v2 = _adamw("adamw_small", *packs)
    for n, dd, mm, vv in zip(SMALL, _unpack(d, shapes), _unpack(m2, shapes), _unpack(v2, shapes)):
        out["grad_" + n] = g_small[n]
        out["delta_" + n], out["new_m_" + n], out["new_v_" + n] = dd, mm, vv
    return out


WEIGHTS = ("norm1_g", "w_in", "b_gate", "lru_conv_w", "lru_conv_b", "lru_w_a", "lru_b_a", "lru_w_x", "lru_b_x",
           "lru_lambda", "ssd_conv_w", "ssd_conv_b", "ssd_dt_bias", "ssd_A_log", "ssd_D", "ssd_norm_g", "w_branch",
           "w_out", "norm2_g", "w_ffn_in", "w_ffn_out", "norm_f")


def kernel(x, norm1_g, w_in, b_gate, lru_conv_w, lru_conv_b, lru_w_a, lru_b_a, lru_w_x, lru_b_x, lru_lambda, ssd_conv_w, ssd_conv_b, ssd_dt_bias, ssd_A_log, ssd_D, ssd_norm_g, w_branch, w_out, norm2_g, w_ffn_in, w_ffn_out, norm_f, loss_target, m_norm1_g, m_w_in, m_b_gate, m_lru_conv_w, m_lru_conv_b, m_lru_w_a, m_lru_b_a, m_lru_w_x, m_lru_b_x, m_lru_lambda, m_ssd_conv_w, m_ssd_conv_b, m_ssd_dt_bias, m_ssd_A_log, m_ssd_D, m_ssd_norm_g, m_w_branch, m_w_out, m_norm2_g, m_w_ffn_in, m_w_ffn_out, m_norm_f, v_norm1_g, v_w_in, v_b_gate, v_lru_conv_w, v_lru_conv_b, v_lru_w_a, v_lru_b_a, v_lru_w_x, v_lru_b_x, v_lru_lambda, v_ssd_conv_w, v_ssd_conv_b, v_ssd_dt_bias, v_ssd_A_log, v_ssd_D, v_ssd_norm_g, v_w_branch, v_w_out, v_norm2_g, v_w_ffn_in, v_w_ffn_out, v_norm_f):
    out = _step(dict(locals()))
    res = [out["loss"], out["grad_x"]]
    for pre in ("grad_", "delta_", "new_m_", "new_v_"):
        res += [out[pre + n] for n in WEIGHTS]
    return tuple(res)
```

```python
import functools

import numpy as np
import jax
import jax.numpy as jnp
from jax import lax
from jax.experimental import pallas as pl
from jax.experimental.pallas import tpu as pltpu

F32 = jnp.float32
BF16 = jnp.bfloat16
HIGHEST = lax.Precision.HIGHEST

D_MODEL = 1024
DEPTH = 2
CHUNK = 64
LRU_C = 8.0
SSD_INNER = 2048
SSD_HEADS = 32
SSD_HEAD_DIM = 64
SSD_GROUPS = 4
SSD_STATE = 128
SSD_CONV_DIM = 3072
D_FF = 2816
EPS = 1e-6
N_DEV = 8
LANES = 128
DT_PAD = LANES
IN_WIDTHS = (1024, 1024, 2048, 3072, 32, 2048)
IN_OFFS = tuple(int(v) for v in np.cumsum((0,) + IN_WIDTHS))

ADAM_LR = 0.001
ADAM_B1 = 0.9
ADAM_B2 = 0.999
ADAM_EPS = 1e-08
ADAM_WD = 0.01
ADAM_STEP = 10

VMEM_LIMIT = 56 * 1024 * 1024
MESH = pl.DeviceIdType.MESH


def _cparams(sem=None):
    return pltpu.CompilerParams(dimension_semantics=sem, vmem_limit_bytes=VMEM_LIMIT)


def _sds(shape, dtype):
    return jax.ShapeDtypeStruct(tuple(shape), dtype)


def _pick(n, cap):
    if n <= cap:
        return n
    best = LANES
    for t in range(LANES, cap + 1, LANES):
        if n % t == 0:
            best = t
    assert n % best == 0, (n, cap)
    return best


def _mm(name, pairs, mode, out_dtype=F32, res=None):
    M = pairs[0][0].shape[0]
    N = pairs[0][1].shape[1] if mode == "nn" else pairs[0][1].shape[0]
    npair = len(pairs)
    tm = min(M, 1024 if npair == 1 else 512)
    tn = _pick(N, 1536 if npair == 1 else 1024)
    tks, nks, starts = [], [], []
    s = 0
    for a, _ in pairs:
        k = a.shape[1]
        tk = _pick(k, 1536)
        tks.append(tk)
        nks.append(k // tk)
        starts.append(s)
        s += k // tk
    nk = s
    dims = (((1,), (0,)), ((), ())) if mode == "nn" else (((1,), (1,)), ((), ()))

    def body(*refs):
        ab = refs[:2 * npair]
        pos = 2 * npair
        r_ref = None
        if res is not None:
            r_ref = refs[pos]
            pos += 1
        o_ref = refs[pos]
        acc = refs[pos + 1] if nk > 1 else None
        k = pl.program_id(2)

        def finish(r):
            if r_ref is not None:
                r = r + r_ref[...].astype(F32)
            o_ref[...] = r.astype(o_ref.dtype)

        for p in range(npair):
            a_ref, b_ref = ab[2 * p], ab[2 * p + 1]
            lo, hi = starts[p], starts[p] + nks[p]

            def step(a_ref=a_ref, b_ref=b_ref, lo=lo, hi=hi):
                d = lax.dot_general(a_ref[...].astype(BF16), b_ref[...].astype(BF16), dims,
                                    preferred_element_type=F32)
                if nk == 1:
                    finish(d)
                    return
                if lo == 0:
                    @pl.when(k == 0)
                    def _():
                        acc[...] = d
                if hi == nk:
                    @pl.when(k == nk - 1)
                    def _():
                        finish(acc[...] + d)
                if max(lo, 1) < min(hi, nk - 1):
                    @pl.when((k > 0) & (k < nk - 1))
                    def _():
                        acc[...] += d

            if npair == 1:
                step()
            else:
                pl.when((k >= lo) & (k < hi))(step)

    in_specs, args = [], []
    for p, (a, b) in enumerate(pairs):
        def kk(k, p=p):
            return jnp.clip(k - starts[p], 0, nks[p] - 1)
        in_specs.append(pl.BlockSpec((tm, tks[p]), lambda i, j, k, kk=kk: (i, kk(k))))
        if mode == "nn":
            in_specs.append(pl.BlockSpec((tks[p], tn), lambda i, j, k, kk=kk: (kk(k), j)))
        else:
            in_specs.append(pl.BlockSpec((tn, tks[p]), lambda i, j, k, kk=kk: (j, kk(k))))
        args += [a, b]
    if res is not None:
        in_specs.append(pl.BlockSpec((tm, tn), lambda i, j, k: (i, j)))
        args.append(res)
    return pl.pallas_call(
        body, grid=(M // tm, N // tn, nk), in_specs=in_specs,
        out_specs=pl.BlockSpec((tm, tn), lambda i, j, k: (i, j)),
        out_shape=_sds((M, N), out_dtype),
        scratch_shapes=[pltpu.VMEM((tm, tn), F32)] if nk > 1 else [],
        compiler_params=_cparams(("parallel", "parallel", "arbitrary")), name=name,
    )(*args)


def _mm_tn(name, a, b, out_dtype=BF16):
    M, Ka = a.shape
    N = b.shape[1]
    tm = min(M, 2048)
    tka = _pick(Ka, 1024)
    tn = _pick(N, 1024)
    nm = M // tm

    def body(a_ref, b_ref, o_ref, *scratch):
        k = pl.program_id(2)
        d = lax.dot_general(a_ref[...].astype(BF16), b_ref[...].astype(BF16),
                            (((0,), (0,)), ((), ())), preferred_element_type=F32)
        if nm == 1:
            o_ref[...] = d.astype(o_ref.dtype)
            return
        acc = scratch[0]

        @pl.when(k == 0)
        def _():
            acc[...] = d

        @pl.when((k > 0) & (k < nm - 1))
        def _():
            acc[...] += d

        @pl.when(k == nm - 1)
        def _():
            o_ref[...] = (acc[...] + d).astype(o_ref.dtype)

    return pl.pallas_call(
        body, grid=(Ka // tka, N // tn, nm),
        in_specs=[pl.BlockSpec((tm, tka), lambda i, j, k: (k, i)),
                  pl.BlockSpec((tm, tn), lambda i, j, k: (k, j))],
        out_specs=pl.BlockSpec((tka, tn), lambda i, j, k: (i, j)),
        out_shape=_sds((Ka, N), out_dtype),
        scratch_shapes=[pltpu.VMEM((tka, tn), F32)] if nm > 1 else [],
        compiler_params=_cparams(("parallel", "parallel", "arbitrary")), name=name,
    )(a, b)


def _rowwise_fwd(name, fn, rows, pars, outs, tb):
    rows = [r if isinstance(r, tuple) else (r, 0, r.shape[1]) for r in rows]
    T = rows[0][0].shape[0]
    tb = min(tb, T)
    nr, npar = len(rows), len(pars)

    def body(*refs):
        rv = [r[...].astype(F32) for r in refs[:nr]]
        pv = [p[...] for p in refs[nr:nr + npar]]
        res = fn(*rv, *pv)
        for o, r in zip(refs[nr + npar:], res):
            o[...] = r.astype(o.dtype)

    in_specs = [pl.BlockSpec((tb, w), lambda i, c=c: (i, c)) for _, c, w in rows]
    in_specs += [pl.BlockSpec(p.shape, lambda i, n=p.ndim: (0,) * n) for p in pars]
    return pl.pallas_call(
        body, grid=(T // tb,), in_specs=in_specs,
        out_specs=[pl.BlockSpec((tb, w), lambda i: (i, 0)) for w, _ in outs],
        out_shape=[_sds((T, w), dt) for w, dt in outs],
        compiler_params=_cparams(("parallel",)), name=name,
    )(*[r[0] for r in rows], *pars)


def _rowwise_bwd(name, fn, rows, pars, cot_rows, cot_fn, row_out, tb):
    rows = [r if isinstance(r, tuple) else (r, 0, r.shape[1]) for r in rows]
    cot_rows = [r if isinstance(r, tuple) else (r, 0, r.shape[1]) for r in cot_rows]
    T = rows[0][0].shape[0]
    tb = min(tb, T)
    nr, npar, nc = len(rows), len(pars), len(cot_rows)
    want = [k for k, dt in enumerate(row_out) if dt is not None]

    def body(*refs):
        i = pl.program_id(0)
        rv = [r[...].astype(F32) for r in refs[:nr]]
        pv = [p[...] for p in refs[nr:nr + npar]]
        cv = [c[...].astype(F32) for c in refs[nr + npar:nr + npar + nc]]
        o_refs = refs[nr + npar + nc:]
        _, vjp = jax.vjp(fn, *rv, *pv)
        grads = vjp(tuple(cot_fn(*cv)))
        for o, k in zip(o_refs[:len(want)], want):
            o[...] = grads[k].astype(o.dtype)
        p_refs = o_refs[len(want):]

        @pl.when(i == 0)
        def _():
            for o in p_refs:
                o[...] = jnp.zeros_like(o)

        for o, g in zip(p_refs, grads[nr:]):
            o[...] += g

    in_specs = [pl.BlockSpec((tb, w), lambda i, c=c: (i, c)) for _, c, w in rows]
    in_specs += [pl.BlockSpec(p.shape, lambda i, n=p.ndim: (0,) * n) for p in pars]
    in_specs += [pl.BlockSpec((tb, w), lambda i, c=c: (i, c)) for _, c, w in cot_rows]
    out_specs = [pl.BlockSpec((tb, rows[k][2]), lambda i: (i, 0)) for k in want]
    out_specs += [pl.BlockSpec(p.shape, lambda i, n=p.ndim: (0,) * n) for p in pars]
    out_shape = [_sds((T, rows[k][2]), row_out[k]) for k in want] + [_sds(p.shape, F32) for p in pars]
    return pl.pallas_call(
        body, grid=(T // tb,), in_specs=in_specs, out_specs=out_specs, out_shape=out_shape,
        compiler_params=_cparams(("arbitrary",)), name=name,
    )(*[r[0] for r in rows], *pars, *[r[0] for r in cot_rows])


def _f_rms(x, g):
    r = lax.rsqrt(jnp.mean(x * x, axis=-1, keepdims=True) + EPS)
    return (x * r * g,)


def _f_lru_gates(u, wa, ba, wx, bx, lam):
    ub = u.astype(BF16)
    ra, rx = [], []
    for k in range(D_MODEL // LANES):
        uk = ub[:, LANES * k:LANES * (k + 1)]
        ra.append(jnp.dot(uk, wa[k].astype(BF16), preferred_element_type=F32))
        rx.append(jnp.dot(uk, wx[k].astype(BF16), preferred_element_type=F32))
    r = jax.nn.sigmoid(jnp.concatenate(ra, axis=1) + ba)
    i = jax.nn.sigmoid(jnp.concatenate(rx, axis=1) + bx)
    log_a = -LRU_C * r * jax.nn.softplus(-lam)
    a = jnp.exp(log_a)
    t = jnp.tanh(log_a)
    b = jnp.sqrt(-2.0 * t / (1.0 - t)) * (i * u)
    return a, b


def _f_post(hl, lgate, yssd, xpre_s, z, d_e, ng):
    ya = jax.nn.gelu(lgate) * hl
    y = (yssd + d_e * jax.nn.silu(xpre_s)) * jax.nn.silu(z)
    gw = SSD_INNER // SSD_GROUPS
    parts = []
    for g in range(SSD_GROUPS):
        yg = y[:, gw * g:gw * (g + 1)]
        parts.append(yg * lax.rsqrt(jnp.mean(yg * yg, axis=-1, keepdims=True) + EPS))
    return ya, jnp.concatenate(parts, axis=1) * ng


def _f_merge(m_a, m_b, gates, bg):
    g = jax.nn.sigmoid(gates + bg)
    return (g[:, :D_MODEL] * m_a + g[:, D_MODEL:] * m_b,)


def _f_act(gu):
    return (jax.nn.silu(gu[:, :D_FF]) * gu[:, D_FF:],)


def _ident(*c):
    return c


def _loss_head(h, target, nf):
    T = h.shape[0]
    tb = min(256, T)

    def body(h_ref, t_ref, nf_ref, dh_ref, dnf_ref, loss_ref, dhb_ref):
        i = pl.program_id(0)

        @pl.when(i == 0)
        def _():
            dnf_ref[...] = jnp.zeros_like(dnf_ref)
            loss_ref[...] = jnp.zeros_like(loss_ref)

        (y,), vjp = jax.vjp(_f_rms, h_ref[...], nf_ref[...])
        err = y - t_ref[...]
        dh, dnf = vjp((err * (1.0 / D_MODEL),))
        dh_ref[...] = dh
        dhb_ref[...] = dh.astype(BF16)
        dnf_ref[...] += dnf
        part = 0.5 * jnp.sum(jnp.mean(err * err, axis=-1, keepdims=True), axis=0, keepdims=True)
        loss_ref[...] += jnp.broadcast_to(part, loss_ref.shape)

    return pl.pallas_call(
        body, grid=(T // tb,),
        in_specs=[pl.BlockSpec((tb, D_MODEL), lambda i: (i, 0)), pl.BlockSpec((tb, D_MODEL), lambda i: (i, 0)),
                  pl.BlockSpec((1, D_MODEL), lambda i: (0, 0))],
        out_specs=[pl.BlockSpec((tb, D_MODEL), lambda i: (i, 0)), pl.BlockSpec((1, D_MODEL), lambda i: (0, 0)),
                   pl.BlockSpec((8, LANES), lambda i: (0, 0)), pl.BlockSpec((tb, D_MODEL), lambda i: (i, 0))],
        out_shape=[_sds((T, D_MODEL), F32), _sds((1, D_MODEL), F32), _sds((8, LANES), F32),
                   _sds((T, D_MODEL), BF16)],
        compiler_params=_cparams(("arbitrary",)), name="loss_head",
    )(h, target, nf)


CONV_TC = 1024
HALO = 8


def _conv_fwd(name, x, w8, b):
    T, C = x.shape
    tb = min(512, T)
    nb = tb // HALO

    def body(x_ref, xp_ref, w_ref, b_ref, y_ref, sc):
        i = pl.program_id(0)
        xv = x_ref[...]
        sc[pl.ds(HALO, tb), :] = xv
        sc[pl.ds(0, HALO), :] = jnp.where(i > 0, xp_ref[...], 0.0)
        acc = b_ref[...] + w_ref[3:4, :] * xv
        for k in range(3):
            acc = acc + w_ref[k:k + 1, :] * sc[pl.ds(HALO - 3 + k, tb), :]
        y_ref[...] = acc

    return pl.pallas_call(
        body, grid=(T // tb, C // CONV_TC),
        in_specs=[pl.BlockSpec((tb, CONV_TC), lambda i, j: (i, j)),
                  pl.BlockSpec((HALO, CONV_TC), lambda i, j: (jnp.maximum(i * nb - 1, 0), j)),
                  pl.BlockSpec((8, CONV_TC), lambda i, j: (0, j)),
                  pl.BlockSpec((1, CONV_TC), lambda i, j: (0, j))],
        out_specs=pl.BlockSpec((tb, CONV_TC), lambda i, j: (i, j)),
        out_shape=_sds((T, C), F32),
        scratch_shapes=[pltpu.VMEM((tb + HALO, CONV_TC), F32)],
        compiler_params=_cparams(("parallel", "parallel")), name=name,
    )(x, x, w8, b)


def _conv_bwd(name, x, dy, w8):
    T, C = x.shape
    tb = min(512, T)
    nb = tb // HALO
    nt = T // tb

    def body(x_ref, xp_ref, dy_ref, dyn_ref, w_ref, dx_ref, dwb_ref, scx, scd):
        i = pl.program_id(1)
        dyv = dy_ref[...]
        scx[pl.ds(HALO, tb), :] = x_ref[...]
        scx[pl.ds(0, HALO), :] = jnp.where(i > 0, xp_ref[...], 0.0)
        scd[pl.ds(0, tb), :] = dyv
        scd[pl.ds(tb, HALO), :] = jnp.where(i < nt - 1, dyn_ref[...], 0.0)
        dx = w_ref[3:4, :] * dyv
        rows = []
        for k in range(3):
            dx = dx + w_ref[k:k + 1, :] * scd[pl.ds(3 - k, tb), :]
            rows.append(jnp.sum(dyv * scx[pl.ds(HALO - 3 + k, tb), :], axis=0, keepdims=True))
        rows.append(jnp.sum(dyv * x_ref[...], axis=0, keepdims=True))
        rows.append(jnp.sum(dyv, axis=0, keepdims=True))
        rows.append(jnp.zeros((3, CONV_TC), F32))
        dx_ref[...] = dx.astype(dx_ref.dtype)

        @pl.when(i == 0)
        def _():
            dwb_ref[...] = jnp.zeros_like(dwb_ref)

        dwb_ref[...] += jnp.concatenate(rows, axis=0)

    return pl.pallas_call(
        body, grid=(C // CONV_TC, nt),
        in_specs=[pl.BlockSpec((tb, CONV_TC), lambda j, i: (i, j)),
                  pl.BlockSpec((HALO, CONV_TC), lambda j, i: (jnp.maximum(i * nb - 1, 0), j)),
                  pl.BlockSpec((tb, CONV_TC), lambda j, i: (i, j)),
                  pl.BlockSpec((HALO, CONV_TC), lambda j, i: (jnp.minimum((i + 1) * nb, T // HALO - 1), j)),
                  pl.BlockSpec((8, CONV_TC), lambda j, i: (0, j))],
        out_specs=[pl.BlockSpec((tb, CONV_TC), lambda j, i: (i, j)),
                   pl.BlockSpec((8, CONV_TC), lambda j, i: (0, j))],
        out_shape=[_sds((T, C), BF16), _sds((8, C), F32)],
        scratch_shapes=[pltpu.VMEM((tb + HALO, CONV_TC), F32), pltpu.VMEM((tb + HALO, CONV_TC), F32)],
        compiler_params=_cparams(("parallel", "arbitrary")), name=name,
    )(x, x, dy, dy, w8)


SCAN_TB = 512


def _lru_scan_fwd(a, b):
    T, C = a.shape
    tb = min(SCAN_TB, T)

    def body(a_ref, b_ref, h_ref, hp_ref, carry):
        @pl.when(pl.program_id(0) == 0)
        def _():
            carry[...] = jnp.zeros_like(carry)

        def group(gi, h):
            r0 = pl.multiple_of(gi * 8, 8)
            at = a_ref[pl.ds(r0, 8), :]
            bt = b_ref[pl.ds(r0, 8), :]
            hs, hps = [], []
            for r in range(8):
                hps.append(h)
                h = at[r:r + 1, :] * h + bt[r:r + 1, :]
                hs.append(h)
            h_ref[pl.ds(r0, 8), :] = jnp.concatenate(hs, axis=0)
            hp_ref[pl.ds(r0, 8), :] = jnp.concatenate(hps, axis=0)
            return h

        carry[0:1, :] = lax.fori_loop(0, tb // 8, group, carry[0:1, :])

    spec = pl.BlockSpec((tb, C), lambda i: (i, 0))
    return pl.pallas_call(
        body, grid=(T // tb,), in_specs=[spec, spec], out_specs=[spec, spec],
        out_shape=[_sds((T, C), F32), _sds((T, C), F32)],
        scratch_shapes=[pltpu.VMEM((8, C), F32)],
        compiler_params=_cparams(("arbitrary",)), name="lru_scan_fwd",
    )(a, b)


def _lru_scan_bwd(a, dh):
    T, C = a.shape
    tb = min(SCAN_TB, T)
    nt = T // tb

    def body(a_ref, dh_ref, g_ref, carry):
        @pl.when(pl.program_id(0) == 0)
        def _():
            carry[...] = jnp.zeros_like(carry)

        def group(gi, c):
            r0 = pl.multiple_of((tb // 8 - 1 - gi) * 8, 8)
            at = a_ref[pl.ds(r0, 8), :]
            dt = dh_ref[pl.ds(r0, 8), :]
            gs = [None] * 8
            for r in range(7, -1, -1):
                g = dt[r:r + 1, :] + c
                c = at[r:r + 1, :] * g
                gs[r] = g
            g_ref[pl.ds(r0, 8), :] = jnp.concatenate(gs, axis=0)
            return c

        carry[0:1, :] = lax.fori_loop(0, tb // 8, group, carry[0:1, :])

    spec = pl.BlockSpec((tb, C), lambda i: (nt - 1 - i, 0))
    return pl.pallas_call(
        body, grid=(nt,), in_specs=[spec, spec], out_specs=spec,
        out_shape=_sds((T, C), F32),
        scratch_shapes=[pltpu.VMEM((8, C), F32)],
        compiler_params=_cparams(("arbitrary",)), name="lru_scan_bwd",
    )(a, dh)


def _ssd_chunk(xpre, dtraw, state, dtb, alog):
    xc = jax.nn.silu(xpre)
    xs = xc[:, :SSD_INNER]
    bm = xc[:, SSD_INNER:SSD_INNER + SSD_GROUPS * SSD_STATE]
    cm = xc[:, SSD_INNER + SSD_GROUPS * SSD_STATE:]
    dt = jax.nn.softplus(dtraw + dtb)
    a = dt * (-jnp.exp(alog))
    li = lax.broadcasted_iota(jnp.int32, (CHUNK, CHUNK), 0)
    si = lax.broadcasted_iota(jnp.int32, (CHUNK, CHUNK), 1)
    tril = li >= si
    ltri = tril.astype(F32)
    eye = (li == si).astype(F32)
    a_cs = jnp.dot(ltri, a, precision=HIGHEST, preferred_element_type=F32)
    a_cs_t = lax.dot_general(a, ltri, (((0,), (1,)), ((), ())), precision=HIGHEST,
                             preferred_element_type=F32)
    dt_t = lax.dot_general(dt, eye, (((0,), (0,)), ((), ())), precision=HIGHEST,
                           preferred_element_type=F32)
    a_last = a_cs[CHUNK - 1:CHUNK, :]
    ex = jnp.exp(a_cs)
    dte = jnp.exp(a_last - a_cs) * dt
    cd = jnp.exp(a_last)
    left = lax.broadcasted_iota(jnp.int32, (CHUNK, LANES), 1) < SSD_HEAD_DIM
    right = jnp.logical_not(left)
    left1 = left[0:1]
    gw = SSD_INNER // SSD_GROUPS
    ys, news = [], []
    for g in range(SSD_GROUPS):
        bg = bm[:, SSD_STATE * g:SSD_STATE * (g + 1)].astype(BF16)
        cg = cm[:, SSD_STATE * g:SSD_STATE * (g + 1)].astype(BF16)
        scores = lax.dot_general(cg, bg, (((1,), (1,)), ((), ())), preferred_element_type=F32)
        coff = jnp.dot(cg, state[:, gw * g:gw * (g + 1)].astype(BF16), preferred_element_type=F32)
        for j in range(gw // LANES):
            h0 = (gw * g + LANES * j) // SSD_HEAD_DIM
            lo = gw * g + LANES * j
            xp = xs[:, lo:lo + LANES]
            acc = None
            for hh, msk in ((h0, left), (h0 + 1, right)):
                seg = a_cs[:, hh:hh + 1] - a_cs_t[hh:hh + 1, :]
                dm = jnp.exp(jnp.where(tril, seg, -1e30)) * dt_t[hh:hh + 1, :]
                sd = (scores * dm).astype(BF16)
                xm = jnp.where(msk, xp, 0.0).astype(BF16)
                t = jnp.dot(sd, xm, preferred_element_type=F32)
                acc = t if acc is None else acc + t
            exe = jnp.where(left, ex[:, h0:h0 + 1], ex[:, h0 + 1:h0 + 2])
            ys.append(acc + coff[:, LANES * j:LANES * (j + 1)] * exe)
            dtee = jnp.where(left, dte[:, h0:h0 + 1], dte[:, h0 + 1:h0 + 2])
            xw = (xp * dtee).astype(BF16)
            cde = jnp.where(left1, cd[:, h0:h0 + 1], cd[:, h0 + 1:h0 + 2])
            news.append(state[:, lo:lo + LANES] * cde
                        + lax.dot_general(bg, xw, (((0,), (0,)), ((), ())), preferred_element_type=F32))
    return jnp.concatenate(ys, axis=1), jnp.concatenate(news, axis=1)


def _ssd_fwd(xpre, dtraw, dtb, alog):
    T = xpre.shape[0]
    n = T // CHUNK

    def body(xp, dr, dtb_ref, al_ref, y_ref, st_ref, state):
        @pl.when(pl.program_id(0) == 0)
        def _():
            state[...] = jnp.zeros_like(state)

        st_ref[0] = state[...]
        y, new = _ssd_chunk(xp[...], dr[...], state[...], dtb_ref[...], al_ref[...])
        y_ref[...] = y
        state[...] = new

    small = pl.BlockSpec((1, LANES), lambda c: (0, 0))
    return pl.pallas_call(
        body, grid=(n,),
        in_specs=[pl.BlockSpec((CHUNK, SSD_CONV_DIM), lambda c: (c, 0)),
                  pl.BlockSpec((CHUNK, DT_PAD), lambda c: (c, 0)), small, small],
        out_specs=[pl.BlockSpec((CHUNK, SSD_INNER), lambda c: (c, 0)),
                   pl.BlockSpec((1, SSD_STATE, SSD_INNER), lambda c: (c, 0, 0))],
        out_shape=[_sds((T, SSD_INNER), F32), _sds((n, SSD_STATE, SSD_INNER), F32)],
        scratch_shapes=[pltpu.VMEM((SSD_STATE, SSD_INNER), F32)],
        compiler_params=_cparams(("arbitrary",)), name="ssd_fwd",
    )(xpre, dtraw, dtb, alog)


def _ssd_bwd(xpre, dtraw, states, dy, dxs_extra, dtb, alog):
    T = xpre.shape[0]
    n = T // CHUNK

    def body(xp, dr, st, dy_ref, dx_ref, dtb_ref, al_ref, dxp_ref, ddr_ref, ddtb_ref, dal_ref, dstate):
        @pl.when(pl.program_id(0) == 0)
        def _():
            dstate[...] = jnp.zeros_like(dstate)
            ddtb_ref[...] = jnp.zeros_like(ddtb_ref)
            dal_ref[...] = jnp.zeros_like(dal_ref)

        _, vjp = jax.vjp(_ssd_chunk, xp[...], dr[...], st[0], dtb_ref[...], al_ref[...])
        dxp, ddr, ds, db, da = vjp((dy_ref[...], dstate[...]))
        dxp_ref[:, :SSD_INNER] = dxp[:, :SSD_INNER] + dx_ref[...]
        dxp_ref[:, SSD_INNER:] = dxp[:, SSD_INNER:]
        ddr_ref[...] = ddr.astype(ddr_ref.dtype)
        dstate[...] = ds
        ddtb_ref[...] += db
        dal_ref[...] += da

    def rev(c):
        return (n - 1 - c, 0)

    small = pl.BlockSpec((1, LANES), lambda c: (0, 0))
    return pl.pallas_call(
        body, grid=(n,),
        in_specs=[pl.BlockSpec((CHUNK, SSD_CONV_DIM), rev), pl.BlockSpec((CHUNK, DT_PAD), rev),
                  pl.BlockSpec((1, SSD_STATE, SSD_INNER), lambda c: (n - 1 - c, 0, 0)),
                  pl.BlockSpec((CHUNK, SSD_INNER), rev), pl.BlockSpec((CHUNK, SSD_INNER), rev), small, small],
        out_specs=[pl.BlockSpec((CHUNK, SSD_CONV_DIM), rev), pl.BlockSpec((CHUNK, DT_PAD), rev), small, small],
        out_shape=[_sds((T, SSD_CONV_DIM), F32), _sds((T, DT_PAD), BF16), _sds((1, LANES), F32), _sds((1, LANES), F32)],
        scratch_shapes=[pltpu.VMEM((SSD_STATE, SSD_INNER), F32)],
        compiler_params=_cparams(("arbitrary",)), name="ssd_bwd",
    )(xpre, dtraw, states, dy, dxs_extra, dtb, alog)


HBM_SPEC = pl.BlockSpec(memory_space=pltpu.HBM)
N_PEER = N_DEV - 1


def _position():
    return lax.axis_index("x"), lax.axis_index("y"), lax.axis_index("c")


def _all_gather(name, blks):
    na = len(blks)

    def body(*refs):
        x_refs, out_refs = refs[:na], refs[na:2 * na]
        send_sems, recv_sems, local_sems = refs[2 * na:]
        x, y, c = _position()
        me, sibling = (x, y, c), (x, y, 1 - c)
        chips = [(1 - x, y), (x, 1 - y), (1 - x, 1 - y)]

        def slot(a, px, py, pc):
            return out_refs[a].at[4 * px + 2 * py + pc]

        def copy(a, k, block, to, src=None):
            return pltpu.make_async_remote_copy(
                src_ref=slot(a, *block) if src is None else src, dst_ref=slot(a, *block),
                send_sem=send_sems.at[a * N_PEER + k], recv_sem=recv_sems.at[a * N_PEER + k],
                device_id=to, device_id_type=MESH)

        mine = [pltpu.make_async_copy(x_refs[a], slot(a, *me), local_sems.at[a]) for a in range(na)]
        first = []
        for a in range(na):
            mine[a].start()
            first.append(copy(a, 0, me, sibling, src=x_refs[a]))
            first += [copy(a, 1 + j, me, (*chip, c), src=x_refs[a]) for j, chip in enumerate(chips)]
        for cp in first:
            cp.start()
        passed = []
        for j, chip in enumerate(chips):
            for a in range(na):
                copy(a, 1 + j, (*chip, c), me).wait_recv()
                passed.append(copy(a, 4 + j, (*chip, c), sibling))
                passed[-1].start()
        for a in range(na):
            copy(a, 0, sibling, me).wait_recv()
            for j, chip in enumerate(chips):
                copy(a, 4 + j, (*chip, 1 - c), me).wait_recv()
        for cp in first + passed:
            cp.wait_send()
        for cp in mine:
            cp.wait()

    return pl.pallas_call(
        body, out_shape=[_sds((N_DEV,) + b.shape, b.dtype) for b in blks],
        in_specs=[HBM_SPEC] * na, out_specs=[HBM_SPEC] * na,
        scratch_shapes=[pltpu.SemaphoreType.DMA((na * N_PEER,)), pltpu.SemaphoreType.DMA((na * N_PEER,)),
                        pltpu.SemaphoreType.DMA((na,))],
        name=name,
    )(*blks)


def _scatter_exchange(name, gs):
    na = len(gs)

    def body(*refs):
        g_refs, q_refs = refs[:na], refs[na:2 * na]
        send_sems, recv_sems, local_sems = refs[2 * na:]
        x, y, c = _position()
        me = 4 * x + 2 * y + c
        mine, sends, recvs = [], [], []
        for a in range(na):
            mine.append(pltpu.make_async_copy(g_refs[a].at[me], q_refs[a].at[me], local_sems.at[a]))
            for k in range(1, N_DEV):
                px, py, pc = (x + (k >> 2)) % 2, (y + ((k >> 1) & 1)) % 2, (c + (k & 1)) % 2
                peer = 4 * px + 2 * py + pc
                sem = a * N_PEER + k - 1
                sends.append(pltpu.make_async_remote_copy(
                    src_ref=g_refs[a].at[peer], dst_ref=q_refs[a].at[me], send_sem=send_sems.at[sem],
                    recv_sem=recv_sems.at[sem], device_id=(px, py, pc), device_id_type=MESH))
                recvs.append(pltpu.make_async_remote_copy(
                    src_ref=g_refs[a].at[me], dst_ref=q_refs[a].at[peer], send_sem=send_sems.at[sem],
                    recv_sem=recv_sems.at[sem], device_id=(px, py, pc), device_id_type=MESH))
        for cp in mine + sends:
            cp.start()
        for cp in recvs:
            cp.wait_recv()
        for cp in sends:
            cp.wait_send()
        for cp in mine:
            cp.wait()

    return pl.pallas_call(
        body, out_shape=[_sds(g.shape, g.dtype) for g in gs],
        in_specs=[HBM_SPEC] * na, out_specs=[HBM_SPEC] * na,
        scratch_shapes=[pltpu.SemaphoreType.DMA((na * N_PEER,)), pltpu.SemaphoreType.DMA((na * N_PEER,)),
                        pltpu.SemaphoreType.DMA((na,))],
        name=name,
    )(*gs)


def _adamw_sum(name, w, m, v, recv):
    L, r, c = w.shape
    tr = r if r <= 512 else 256
    assert r % tr == 0

    def body(w_ref, m_ref, v_ref, q_ref, g_ref, d_ref, m2_ref, v2_ref):
        gv = q_ref[0, 0].astype(F32)
        for s in range(1, N_DEV):
            gv = gv + q_ref[s, 0].astype(F32)
        m2 = ADAM_B1 * m_ref[0] + (1.0 - ADAM_B1) * gv
        v2 = ADAM_B2 * v_ref[0] + (1.0 - ADAM_B2) * jnp.square(gv)
        m_hat = m2 / (1.0 - ADAM_B1 ** ADAM_STEP)
        v_hat = v2 / (1.0 - ADAM_B2 ** ADAM_STEP)
        g_ref[0] = gv
        d_ref[0] = -ADAM_LR * (m_hat / (jnp.sqrt(v_hat) + ADAM_EPS) + ADAM_WD * w_ref[0])
        m2_ref[0] = m2
        v2_ref[0] = v2

    spec = pl.BlockSpec((1, tr, c), lambda l, i: (l, i, 0))
    return pl.pallas_call(
        body, grid=(L, r // tr),
        in_specs=[spec] * 3 + [pl.BlockSpec((N_DEV, 1, tr, c), lambda l, i: (0, l, i, 0))],
        out_specs=[spec] * 4, out_shape=[_sds((L, r, c), F32)] * 4,
        compiler_params=_cparams(("parallel", "parallel")), name=name,
    )(w, m, v, recv)


def _sum_blocks(name, q, tr):
    R = q.shape[0] // N_DEV
    W = q.shape[1]
    tr = min(tr, R)
    assert R % tr == 0
    nb = R // tr

    def body(*refs):
        acc = refs[0][...].astype(F32)
        for r in refs[1:N_DEV]:
            acc = acc + r[...].astype(F32)
        refs[N_DEV][...] = acc

    return pl.pallas_call(
        body, grid=(nb,),
        in_specs=[pl.BlockSpec((tr, W), lambda i, s=s: (s * nb + i, 0)) for s in range(N_DEV)],
        out_specs=pl.BlockSpec((tr, W), lambda i: (i, 0)), out_shape=_sds((R, W), F32),
        compiler_params=_cparams(("parallel",)), name=name,
    )(*([q] * N_DEV))


def _adamw(name, w, g, m, v):
    R, C = w.shape
    tr = R
    if R > 512:
        tr = max(t for t in range(8, 513, 8) if R % t == 0)

    def body(w_ref, g_ref, m_ref, v_ref, d_ref, m2_ref, v2_ref):
        gv = g_ref[...]
        m2 = ADAM_B1 * m_ref[...] + (1.0 - ADAM_B1) * gv
        v2 = ADAM_B2 * v_ref[...] + (1.0 - ADAM_B2) * jnp.square(gv)
        m_hat = m2 / (1.0 - ADAM_B1 ** ADAM_STEP)
        v_hat = v2 / (1.0 - ADAM_B2 ** ADAM_STEP)
        d_ref[...] = -ADAM_LR * (m_hat / (jnp.sqrt(v_hat) + ADAM_EPS) + ADAM_WD * w_ref[...])
        m2_ref[...] = m2
        v2_ref[...] = v2

    spec = pl.BlockSpec((tr, C), lambda i: (i, 0))
    return pl.pallas_call(
        body, grid=(R // tr,), in_specs=[spec] * 4, out_specs=[spec] * 3,
        out_shape=[_sds((R, C), F32)] * 3, compiler_params=_cparams(("parallel",)), name=name,
    )(w, g, m, v)


BIG = ("w_in", "w_branch", "w_out", "w_ffn_in", "w_ffn_out")
BIG_ROW_SHARDED = {"w_in": False, "w_branch": True, "w_out": True, "w_ffn_in": False, "w_ffn_out": True}
SMALL = ("norm1_g", "b_gate", "lru_conv_w", "lru_conv_b", "lru_w_a", "lru_b_a", "lru_w_x", "lru_b_x", "lru_lambda",
         "ssd_conv_w", "ssd_conv_b", "ssd_dt_bias", "ssd_A_log", "ssd_D", "ssd_norm_g", "norm2_g", "norm_f")
SMALL_ROWS = 256


def _pack(arrs, dtype, row_mult):
    parts = []
    for a in arrs:
        f = a.reshape(-1).astype(dtype)
        pad = (-f.shape[0]) % LANES
        if pad:
            f = jnp.concatenate([f, jnp.zeros((pad,), dtype)])
        parts.append(f)
    f = jnp.concatenate(parts)
    pad = (-f.shape[0]) % (LANES * row_mult)
    if pad:
        f = jnp.concatenate([f, jnp.zeros((pad,), dtype)])
    return f.reshape(-1, LANES)


def _unpack(flat, shapes, lead=()):
    f = flat.reshape(lead + (-1,))
    out, off = [], 0
    for s in shapes:
        n = int(np.prod(s))
        out.append(f[..., off:off + n].reshape(lead + tuple(s)))
        off += n + (-n) % LANES
    return out


def _full_from_shards(name, st):
    if BIG_ROW_SHARDED[name]:
        return st.reshape((-1,) + st.shape[2:])
    return jnp.transpose(st, (1, 0, 2)).reshape(st.shape[1], -1)


def _shards_from_full(name, full):
    if BIG_ROW_SHARDED[name]:
        return full.reshape((N_DEV, full.shape[0] // N_DEV) + full.shape[1:])
    return jnp.transpose(full.reshape(full.shape[0], N_DEV, -1), (1, 0, 2))


def _block_diag_tiles(w):
    z = jnp.zeros((8, 64, 64), w.dtype)
    w2 = w.reshape(8, 2, 64, 64)
    top = jnp.concatenate([w2[:, 0], z], axis=2)
    bot = jnp.concatenate([z, w2[:, 1]], axis=2)
    return jnp.concatenate([top, bot], axis=1)


def _block_diag_untile(t):
    return jnp.stack([t[:, :64, :64], t[:, 64:, 64:]], axis=1).reshape(16, 64, 64)


def _pad_lanes(a, width):
    return jnp.concatenate([a, jnp.zeros(a.shape[:-1] + (width - a.shape[-1],), a.dtype)], axis=-1)


def _pad_rows8(w):
    return jnp.concatenate([w, jnp.zeros((8 - w.shape[0],) + w.shape[1:], w.dtype)], axis=0)


def _layer_params(l, shards, small):
    full = {n: _full_from_shards(n, shards[n]) for n in BIG}
    w_in = full["w_in"]
    seg = [w_in[:, IN_OFFS[k]:IN_OFFS[k + 1]] for k in range(6)]
    p = {
        "w_lx": seg[0], "w_lg": seg[1], "w_z": seg[2], "w_xbc": seg[3], "w_dt": _pad_lanes(seg[4], DT_PAD), "w_g": seg[5],
        "w_bra": full["w_branch"][:D_MODEL], "w_brb": full["w_branch"][D_MODEL:],
        "w_out": full["w_out"], "w_ffn_in": full["w_ffn_in"], "w_ffn_out": full["w_ffn_out"],
    }
    for n in ("norm1_g", "b_gate", "lru_conv_b", "lru_b_a", "lru_b_x", "lru_lambda", "ssd_conv_b", "ssd_norm_g", "norm2_g"):
        p[n] = small[n][l].reshape(1, -1)
    p["lru_conv_w8"] = _pad_rows8(small["lru_conv_w"][l])
    p["ssd_conv_w8"] = _pad_rows8(small["ssd_conv_w"][l])
    p["wa"] = _block_diag_tiles(small["lru_w_a"][l])
    p["wx"] = _block_diag_tiles(small["lru_w_x"][l])
    p["dtb"] = _pad_lanes(small["ssd_dt_bias"][l].reshape(1, -1), DT_PAD)
    p["alog"] = _pad_lanes(small["ssd_A_log"][l].reshape(1, -1), DT_PAD)
    p["d_e"] = jnp.repeat(small["ssd_D"][l], SSD_HEAD_DIM).reshape(1, -1)
    return p


def _layer_fwd(l, h, p):
    n = f"l{l}_"
    s = {"h_in": h}
    (xn,) = _rowwise_fwd(n + "rms1", _f_rms, [h], [p["norm1_g"]], [(D_MODEL, BF16)], 256)
    s["xn"] = xn
    for k in ("lx", "lg", "z", "xbc", "g", "dt"):
        s[k] = _mm(n + "in_" + k, [(xn, p["w_" + k])], "nn")
    s["u"] = _conv_fwd(n + "lru_conv", s["lx"], p["lru_conv_w8"], p["lru_conv_b"])
    lru_pars = [p["wa"], p["lru_b_a"], p["wx"], p["lru_b_x"], p["lru_lambda"]]
    s["a"], b = _rowwise_fwd(n + "lru_gates", _f_lru_gates, [s["u"]], lru_pars, [(D_MODEL, F32)] * 2, 256)
    s["hl"], s["hprev"] = _lru_scan_fwd(s["a"], b)
    s["xpre"] = _conv_fwd(n + "ssd_conv", s["xbc"], p["ssd_conv_w8"], p["ssd_conv_b"])
    s["yssd"], s["states"] = _ssd_fwd(s["xpre"], s["dt"], p["dtb"], p["alog"])
    post_rows = [s["hl"], s["lg"], s["yssd"], (s["xpre"], 0, SSD_INNER), s["z"]]
    s["ya"], s["yb"] = _rowwise_fwd(n + "post", _f_post, post_rows, [p["d_e"], p["ssd_norm_g"]],
                                    [(D_MODEL, BF16), (SSD_INNER, BF16)], 128)
    s["ma"] = _mm(n + "br_a", [(s["ya"], p["w_bra"])], "nn")
    s["mb"] = _mm(n + "br_b", [(s["yb"], p["w_brb"])], "nn")
    (s["merged"],) = _rowwise_fwd(n + "merge", _f_merge, [s["ma"], s["mb"], s["g"]], [p["b_gate"]],
                                  [(D_MODEL, BF16)], 256)
    s["h_mid"] = _mm(n + "out", [(s["merged"], p["w_out"])], "nn", res=h)
    (s["xn2"],) = _rowwise_fwd(n + "rms2", _f_rms, [s["h_mid"]], [p["norm2_g"]], [(D_MODEL, BF16)], 256)
    s["gu"] = _mm(n + "ffn_in", [(s["xn2"], p["w_ffn_in"])], "nn")
    (s["act"],) = _rowwise_fwd(n + "act", _f_act, [s["gu"]], [], [(D_FF, BF16)], 256)
    h_out = _mm(n + "ffn_out", [(s["act"], p["w_ffn_out"])], "nn", res=s["h_mid"])
    return h_out, s


def _add(name, a, b):
    w = a.shape[1]
    return _rowwise_fwd(name, lambda x, y: (x + y, x + y), [a, b], [], [(w, F32), (w, BF16)], 512)


def _layer_bwd(l, dh, dh_b, s, p):
    n = f"l{l}_b_"
    gw, gs = {}, {}
    d_act = _mm(n + "d_act", [(dh_b, p["w_ffn_out"])], "nt")
    gw["w_ffn_out"] = _mm_tn(n + "dw_ffn_out", s["act"], dh_b)
    (d_gu,) = _rowwise_bwd(n + "act", _f_act, [s["gu"]], [], [d_act], _ident, [BF16], 128)
    d_xn2 = _mm(n + "d_xn2", [(d_gu, p["w_ffn_in"])], "nt")
    gw["w_ffn_in"] = _mm_tn(n + "dw_ffn_in", s["xn2"], d_gu)
    d_rms2, gs["norm2_g"] = _rowwise_bwd(n + "rms2", _f_rms, [s["h_mid"]], [p["norm2_g"]], [d_xn2], _ident, [F32], 256)
    dh_mid, dh_mid_b = _add(n + "add2", dh, d_rms2)
    d_merged = _mm(n + "d_merged", [(dh_mid_b, p["w_out"])], "nt")
    gw["w_out"] = _mm_tn(n + "dw_out", s["merged"], dh_mid_b)
    d_ma, d_mb, d_g, gs["b_gate"] = _rowwise_bwd(n + "merge", _f_merge, [s["ma"], s["mb"], s["g"]], [p["b_gate"]],
                                                 [d_merged], _ident, [BF16, BF16, BF16], 256)
    d_ya = _mm(n + "d_ya", [(d_ma, p["w_bra"])], "nt")
    d_yb = _mm(n + "d_yb", [(d_mb, p["w_brb"])], "nt")
    gw["w_branch"] = jnp.concatenate([_mm_tn(n + "dw_bra", s["ya"], d_ma), _mm_tn(n + "dw_brb", s["yb"], d_mb)], axis=0)
    post_rows = [s["hl"], s["lg"], s["yssd"], (s["xpre"], 0, SSD_INNER), s["z"]]
    d_hl, d_lg, d_yssd, d_xs, d_z, d_de, gs["ssd_norm_g"] = _rowwise_bwd(
        n + "post", _f_post, post_rows, [p["d_e"], p["ssd_norm_g"]], [d_ya, d_yb], _ident,
        [F32, BF16, F32, F32, BF16], 64)
    gs["ssd_D"] = d_de.reshape(SSD_HEADS, SSD_HEAD_DIM).sum(axis=1)
    d_xpre, d_dt, d_dtb, d_alog = _ssd_bwd(s["xpre"], s["dt"], s["states"], d_yssd, d_xs, p["dtb"], p["alog"])
    gs["ssd_dt_bias"] = d_dtb[0, :SSD_HEADS]
    gs["ssd_A_log"] = d_alog[0, :SSD_HEADS]
    d_xbc, dwb = _conv_bwd(n + "ssd_conv", s["xbc"], d_xpre, p["ssd_conv_w8"])
    gs["ssd_conv_w"], gs["ssd_conv_b"] = dwb[:4], dwb[4]
    g_scan = _lru_scan_bwd(s["a"], d_hl)
    lru_pars = [p["wa"], p["lru_b_a"], p["wx"], p["lru_b_x"], p["lru_lambda"]]
    d_u, d_wa, gs["lru_b_a"], d_wx, gs["lru_b_x"], gs["lru_lambda"] = _rowwise_bwd(
        n + "lru_gates", _f_lru_gates, [s["u"]], lru_pars, [g_scan, s["hprev"]],
        lambda g, hp: (g * hp, g), [F32], 128)
    gs["lru_w_a"], gs["lru_w_x"] = _block_diag_untile(d_wa), _block_diag_untile(d_wx)
    d_lx, dwb = _conv_bwd(n + "lru_conv", s["lx"], d_u, p["lru_conv_w8"])
    gs["lru_conv_w"], gs["lru_conv_b"] = dwb[:4], dwb[4]
    segs = [("lx", d_lx), ("lg", d_lg), ("z", d_z), ("xbc", d_xbc), ("dt", d_dt), ("g", d_g)]
    d_xn = _mm(n + "d_xn", [(d, p["w_" + k]) for k, d in segs], "nt")
    dws = [_mm_tn(n + "dw_in_" + k, s["xn"], d) for k, d in segs]
    dws[4] = dws[4][:, :IN_WIDTHS[4]]
    gw["w_in"] = jnp.concatenate(dws, axis=1)
    d_rms1, gs["norm1_g"] = _rowwise_bwd(n + "rms1", _f_rms, [s["h_in"]], [p["norm1_g"]], [d_xn], _ident, [F32], 256)
    dh_in, dh_in_b = _add(n + "add1", dh_mid, d_rms1)
    for k in ("norm1_g", "norm2_g", "b_gate", "ssd_norm_g", "lru_b_a", "lru_b_x", "lru_lambda"):
        gs[k] = gs[k].reshape(-1)
    return dh_in, dh_in_b, gw, gs


def _step(inp):
    x = inp["x"][0]
    target = inp["loss_target"][0]
    dev = 4 * lax.axis_index("x") + 2 * lax.axis_index("y") + lax.axis_index("c")

    mine = [inp[n][l].astype(BF16) for l in range(DEPTH) for n in BIG]
    gathered = _all_gather("gather_weights", mine + [inp["lru_conv_w"], inp["ssd_conv_w"]])
    shards = [dict(zip(BIG, gathered[len(BIG) * l:len(BIG) * (l + 1)])) for l in range(DEPTH)]
    small = {n: inp[n] for n in SMALL}
    small["lru_conv_w"] = jnp.moveaxis(gathered[-2], 0, 2).reshape(DEPTH, 4, -1)
    small["ssd_conv_w"] = jnp.moveaxis(gathered[-1], 0, 2).reshape(DEPTH, 4, -1)

    params = [_layer_params(l, shards[l], small) for l in range(DEPTH)]
    h, saved = x, []
    for l in range(DEPTH):
        h, s = _layer_fwd(l, h, params[l])
        saved.append(s)
    dh, d_nf, loss_acc, dh_b = _loss_head(h, target, small["norm_f"].reshape(1, -1))
    loss = lax.psum(loss_acc[0, 0], ("x", "y", "c"))

    gws, gss = [None] * DEPTH, [None] * DEPTH
    for l in reversed(range(DEPTH)):
        dh, dh_b, gws[l], gss[l] = _layer_bwd(l, dh, dh_b, saved[l], params[l])
    grad_x = dh[None]

    contrib = [jnp.stack([_shards_from_full(n, gws[l][n]) for l in range(DEPTH)], axis=1) for n in BIG]
    received = _scatter_exchange("scatter_grads", contrib)
    out = {"loss": loss, "grad_x": grad_x}
    for n, q in zip(BIG, received):
        out["grad_" + n], out["delta_" + n], out["new_m_" + n], out["new_v_" + n] = _adamw_sum(
            "adamw_" + n, inp[n], inp["m_" + n], inp["v_" + n], q)

    small_full = {n: jnp.stack([gss[l][n] for l in range(DEPTH)]) for n in SMALL if n != "norm_f"}
    small_full["norm_f"] = d_nf.reshape(-1)
    part = _pack([small_full[n] for n in SMALL], F32, SMALL_ROWS)
    (everyone,) = _all_gather("gather_small_grads", [part])
    g_small_flat = _sum_blocks("sum_small_grads", everyone.reshape(-1, LANES), SMALL_ROWS)
    g_small = dict(zip(SMALL, _unpack(g_small_flat, [small_full[n].shape for n in SMALL])))
    for n in ("lru_conv_w", "ssd_conv_w"):
        w = inp[n].shape[-1]
        g_small[n] = lax.dynamic_slice_in_dim(g_small[n], dev * w, w, axis=2)

    shapes = [inp[n].shape for n in SMALL]
    packs = [_pack([src[pre + n] for n in SMALL], F32, SMALL_ROWS)
             for src, pre in ((inp, ""), (g_small, ""), (inp, "m_"), (inp, "v_"))]
    d, m2, v2 = _adamw("adamw_small", *packs)
    for n, dd, mm, vv in zip(SMALL, _unpack(d, shapes), _unpack(m2, shapes), _unpack(v2, shapes)):
        out["grad_" + n] = g_small[n]
        out["delta_" + n], out["new_m_" + n], out["new_v_" + n] = dd, mm, vv
    return out


WEIGHTS = ("norm1_g", "w_in", "b_gate", "lru_conv_w", "lru_conv_b", "lru_w_a", "lru_b_a", "lru_w_x", "lru_b_x",
           "lru_lambda", "ssd_conv_w", "ssd_conv_b", "ssd_dt_bias", "ssd_A_log", "ssd_D", "ssd_norm_g", "w_branch",
           "w_out", "norm2_g", "w_ffn_in", "w_ffn_out", "norm_f")


def kernel(x, norm1_g, w_in, b_gate, lru_conv_w, lru_conv_b, lru_w_a, lru_b_a, lru_w_x, lru_b_x, lru_lambda, ssd_conv_w, ssd_conv_b, ssd_dt_bias, ssd_A_log, ssd_D, ssd_norm_g, w_branch, w_out, norm2_g, w_ffn_in, w_ffn_out, norm_f, loss_target, m_norm1_g, m_w_in, m_b_gate, m_lru_conv_w, m_lru_conv_b, m_lru_w_a, m_lru_b_a, m_lru_w_x, m_lru_b_x, m_lru_lambda, m_ssd_conv_w, m_ssd_conv_b, m_ssd_dt_bias, m_ssd_A_log, m_ssd_D, m_ssd_norm_g, m_w_branch, m_w_out, m_norm2_g, m_w_ffn_in, m_w_ffn_out, m_norm_f, v_norm1_g, v_w_in, v_b_gate, v_lru_conv_w, v_lru_conv_b, v_lru_w_a, v_lru_b_a, v_lru_w_x, v_lru_b_x, v_lru_lambda, v_ssd_conv_w, v_ssd_conv_b, v_ssd_dt_bias, v_ssd_A_log, v_ssd_D, v_ssd_norm_g, v_w_branch, v_w_out, v_norm2_g, v_w_ffn_in, v_w_ffn_out, v_norm_f):
    out = _step(dict(locals()))
    res = [out["loss"], out["grad_x"]]
    for pre in ("grad_", "delta_", "new_m_", "new_v_"):
        res += [out[pre + n] for n in WEIGHTS]
    return tuple(res)
```

```python
import functools

import numpy as np
import jax
import jax.numpy as jnp
from jax import lax
from jax.experimental import pallas as pl
from jax.experimental.pallas import tpu as pltpu

F32 = jnp.float32
BF16 = jnp.bfloat16
HIGHEST = lax.Precision.HIGHEST

D_MODEL = 1024
DEPTH = 2
CHUNK = 64
LRU_C = 8.0
SSD_INNER = 2048
SSD_HEADS = 32
SSD_HEAD_DIM = 64
SSD_GROUPS = 4
SSD_STATE = 128
SSD_CONV_DIM = 3072
D_FF = 2816
EPS = 1e-6
N_DEV = 8
LANES = 128
DT_PAD = LANES
IN_WIDTHS = (1024, 1024, 2048, 3072, 32, 2048)
IN_OFFS = tuple(int(v) for v in np.cumsum((0,) + IN_WIDTHS))

ADAM_LR = 0.001
ADAM_B1 = 0.9
ADAM_B2 = 0.999
ADAM_EPS = 1e-08
ADAM_WD = 0.01
ADAM_STEP = 10

VMEM_LIMIT = 56 * 1024 * 1024
MESH = pl.DeviceIdType.MESH


def _cparams(sem=None):
    return pltpu.CompilerParams(dimension_semantics=sem, vmem_limit_bytes=VMEM_LIMIT)


def _sds(shape, dtype):
    return jax.ShapeDtypeStruct(tuple(shape), dtype)


def _pick(n, cap):
    if n <= cap:
        return n
    best = LANES
    for t in range(LANES, cap + 1, LANES):
        if n % t == 0:
            best = t
    assert n % best == 0, (n, cap)
    return best


def _mm(name, pairs, mode, out_dtype=F32, res=None):
    M = pairs[0][0].shape[0]
    N = pairs[0][1].shape[1] if mode == "nn" else pairs[0][1].shape[0]
    npair = len(pairs)
    tm = min(M, 1024 if npair <= 3 else 512)
    tn = _pick(N, 1536 if npair == 1 else 1024)
    tks, nks, starts = [], [], []
    s = 0
    for a, _ in pairs:
        k = a.shape[1]
        tk = _pick(k, 1536)
        tks.append(tk)
        nks.append(k // tk)
        starts.append(s)
        s += k // tk
    nk = s
    dims = (((1,), (0,)), ((), ())) if mode == "nn" else (((1,), (1,)), ((), ()))

    def body(*refs):
        ab = refs[:2 * npair]
        pos = 2 * npair
        r_ref = None
        if res is not None:
            r_ref = refs[pos]
            pos += 1
        o_ref = refs[pos]
        acc = refs[pos + 1] if nk > 1 else None
        k = pl.program_id(2)

        def finish(r):
            if r_ref is not None:
                r = r + r_ref[...].astype(F32)
            o_ref[...] = r.astype(o_ref.dtype)

        for p in range(npair):
            a_ref, b_ref = ab[2 * p], ab[2 * p + 1]
            lo, hi = starts[p], starts[p] + nks[p]

            def step(a_ref=a_ref, b_ref=b_ref, lo=lo, hi=hi):
                d = lax.dot_general(a_ref[...].astype(BF16), b_ref[...].astype(BF16), dims,
                                    preferred_element_type=F32)
                if nk == 1:
                    finish(d)
                    return
                if lo == 0:
                    @pl.when(k == 0)
                    def _():
                        acc[...] = d
                if hi == nk:
                    @pl.when(k == nk - 1)
                    def _():
                        finish(acc[...] + d)
                if max(lo, 1) < min(hi, nk - 1):
                    @pl.when((k > 0) & (k < nk - 1))
                    def _():
                        acc[...] += d

            if npair == 1:
                step()
            else:
                pl.when((k >= lo) & (k < hi))(step)

    in_specs, args = [], []
    for p, (a, b) in enumerate(pairs):
        def kk(k, p=p):
            return jnp.clip(k - starts[p], 0, nks[p] - 1)
        in_specs.append(pl.BlockSpec((tm, tks[p]), lambda i, j, k, kk=kk: (i, kk(k))))
        if mode == "nn":
            in_specs.append(pl.BlockSpec((tks[p], tn), lambda i, j, k, kk=kk: (kk(k), j)))
        else:
            in_specs.append(pl.BlockSpec((tn, tks[p]), lambda i, j, k, kk=kk: (j, kk(k))))
        args += [a, b]
    if res is not None:
        in_specs.append(pl.BlockSpec((tm, tn), lambda i, j, k: (i, j)))
        args.append(res)
    return pl.pallas_call(
        body, grid=(M // tm, N // tn, nk), in_specs=in_specs,
        out_specs=pl.BlockSpec((tm, tn), lambda i, j, k: (i, j)),
        out_shape=_sds((M, N), out_dtype),
        scratch_shapes=[pltpu.VMEM((tm, tn), F32)] if nk > 1 else [],
        compiler_params=_cparams(("parallel", "parallel", "arbitrary")), name=name,
    )(*args)


def _mm_tn(name, a, b, out_dtype=BF16):
    M, Ka = a.shape
    N = b.shape[1]
    tm = min(M, 2048)
    tka = _pick(Ka, 1024)
    tn = _pick(N, 1024)
    nm = M // tm

    def body(a_ref, b_ref, o_ref, *scratch):
        k = pl.program_id(2)
        d = lax.dot_general(a_ref[...].astype(BF16), b_ref[...].astype(BF16),
                            (((0,), (0,)), ((), ())), preferred_element_type=F32)
        if nm == 1:
            o_ref[...] = d.astype(o_ref.dtype)
            return
        acc = scratch[0]

        @pl.when(k == 0)
        def _():
            acc[...] = d

        @pl.when((k > 0) & (k < nm - 1))
        def _():
            acc[...] += d

        @pl.when(k == nm - 1)
        def _():
            o_ref[...] = (acc[...] + d).astype(o_ref.dtype)

    return pl.pallas_call(
        body, grid=(Ka // tka, N // tn, nm),
        in_specs=[pl.BlockSpec((tm, tka), lambda i, j, k: (k, i)),
                  pl.BlockSpec((tm, tn), lambda i, j, k: (k, j))],
        out_specs=pl.BlockSpec((tka, tn), lambda i, j, k: (i, j)),
        out_shape=_sds((Ka, N), out_dtype),
        scratch_shapes=[pltpu.VMEM((tka, tn), F32)] if nm > 1 else [],
        compiler_params=_cparams(("parallel", "parallel", "arbitrary")), name=name,
    )(a, b)


def _rowwise_fwd(name, fn, rows, pars, outs, tb):
    rows = [r if isinstance(r, tuple) else (r, 0, r.shape[1]) for r in rows]
    T = rows[0][0].shape[0]
    tb = min(tb, T)
    nr, npar = len(rows), len(pars)

    def body(*refs):
        rv = [r[...].astype(F32) for r in refs[:nr]]
        pv = [p[...] for p in refs[nr:nr + npar]]
        res = fn(*rv, *pv)
        for o, r in zip(refs[nr + npar:], res):
            o[...] = r.astype(o.dtype)

    in_specs = [pl.BlockSpec((tb, w), lambda i, c=c: (i, c)) for _, c, w in rows]
    in_specs += [pl.BlockSpec(p.shape, lambda i, n=p.ndim: (0,) * n) for p in pars]
    return pl.pallas_call(
        body, grid=(T // tb,), in_specs=in_specs,
        out_specs=[pl.BlockSpec((tb, w), lambda i: (i, 0)) for w, _ in outs],
        out_shape=[_sds((T, w), dt) for w, dt in outs],
        compiler_params=_cparams(("parallel",)), name=name,
    )(*[r[0] for r in rows], *pars)


def _rowwise_bwd(name, fn, rows, pars, cot_rows, cot_fn, row_out, tb):
    rows = [r if isinstance(r, tuple) else (r, 0, r.shape[1]) for r in rows]
    cot_rows = [r if isinstance(r, tuple) else (r, 0, r.shape[1]) for r in cot_rows]
    T = rows[0][0].shape[0]
    tb = min(tb, T)
    nr, npar, nc = len(rows), len(pars), len(cot_rows)
    want = [k for k, dt in enumerate(row_out) if dt is not None]

    def body(*refs):
        i = pl.program_id(0)
        rv = [r[...].astype(F32) for r in refs[:nr]]
        pv = [p[...] for p in refs[nr:nr + npar]]
        cv = [c[...].astype(F32) for c in refs[nr + npar:nr + npar + nc]]
        o_refs = refs[nr + npar + nc:]
        _, vjp = jax.vjp(fn, *rv, *pv)
        grads = vjp(tuple(cot_fn(*cv)))
        for o, k in zip(o_refs[:len(want)], want):
            o[...] = grads[k].astype(o.dtype)
        p_refs = o_refs[len(want):]

        @pl.when(i == 0)
        def _():
            for o in p_refs:
                o[...] = jnp.zeros_like(o)

        for o, g in zip(p_refs, grads[nr:]):
            o[...] += g

    in_specs = [pl.BlockSpec((tb, w), lambda i, c=c: (i, c)) for _, c, w in rows]
    in_specs += [pl.BlockSpec(p.shape, lambda i, n=p.ndim: (0,) * n) for p in pars]
    in_specs += [pl.BlockSpec((tb, w), lambda i, c=c: (i, c)) for _, c, w in cot_rows]
    out_specs = [pl.BlockSpec((tb, rows[k][2]), lambda i: (i, 0)) for k in want]
    out_specs += [pl.BlockSpec(p.shape, lambda i, n=p.ndim: (0,) * n) for p in pars]
    out_shape = [_sds((T, rows[k][2]), row_out[k]) for k in want] + [_sds(p.shape, F32) for p in pars]
    return pl.pallas_call(
        body, grid=(T // tb,), in_specs=in_specs, out_specs=out_specs, out_shape=out_shape,
        compiler_params=_cparams(("arbitrary",)), name=name,
    )(*[r[0] for r in rows], *pars, *[r[0] for r in cot_rows])


def _f_rms(x, g):
    r = lax.rsqrt(jnp.mean(x * x, axis=-1, keepdims=True) + EPS)
    return (x * r * g,)


def _f_lru_gates(u, wa, ba, wx, bx, lam):
    ub = u.astype(BF16)
    ra, rx = [], []
    for k in range(D_MODEL // LANES):
        uk = ub[:, LANES * k:LANES * (k + 1)]
        ra.append(jnp.dot(uk, wa[k].astype(BF16), preferred_element_type=F32))
        rx.append(jnp.dot(uk, wx[k].astype(BF16), preferred_element_type=F32))
    r = jax.nn.sigmoid(jnp.concatenate(ra, axis=1) + ba)
    i = jax.nn.sigmoid(jnp.concatenate(rx, axis=1) + bx)
    log_a = -LRU_C * r * jax.nn.softplus(-lam)
    a = jnp.exp(log_a)
    t = jnp.tanh(log_a)
    b = jnp.sqrt(-2.0 * t / (1.0 - t)) * (i * u)
    return a, b


def _f_post(hl, lgate, yssd, xpre_s, z, d_e, ng):
    ya = jax.nn.gelu(lgate) * hl
    y = (yssd + d_e * jax.nn.silu(xpre_s)) * jax.nn.silu(z)
    gw = SSD_INNER // SSD_GROUPS
    parts = []
    for g in range(SSD_GROUPS):
        yg = y[:, gw * g:gw * (g + 1)]
        parts.append(yg * lax.rsqrt(jnp.mean(yg * yg, axis=-1, keepdims=True) + EPS))
    return ya, jnp.concatenate(parts, axis=1) * ng


def _f_merge(m_a, m_b, gates, bg):
    g = jax.nn.sigmoid(gates + bg)
    return (g[:, :D_MODEL] * m_a + g[:, D_MODEL:] * m_b,)


def _f_act(gu):
    return (jax.nn.silu(gu[:, :D_FF]) * gu[:, D_FF:],)


def _ident(*c):
    return c


def _loss_head(h, target, nf):
    T = h.shape[0]
    tb = min(256, T)

    def body(h_ref, t_ref, nf_ref, dh_ref, dnf_ref, loss_ref, dhb_ref):
        i = pl.program_id(0)

        @pl.when(i == 0)
        def _():
            dnf_ref[...] = jnp.zeros_like(dnf_ref)
            loss_ref[...] = jnp.zeros_like(loss_ref)

        (y,), vjp = jax.vjp(_f_rms, h_ref[...], nf_ref[...])
        err = y - t_ref[...]
        dh, dnf = vjp((err * (1.0 / D_MODEL),))
        dh_ref[...] = dh
        dhb_ref[...] = dh.astype(BF16)
        dnf_ref[...] += dnf
        part = 0.5 * jnp.sum(jnp.mean(err * err, axis=-1, keepdims=True), axis=0, keepdims=True)
        loss_ref[...] += jnp.broadcast_to(part, loss_ref.shape)

    return pl.pallas_call(
        body, grid=(T // tb,),
        in_specs=[pl.BlockSpec((tb, D_MODEL), lambda i: (i, 0)), pl.BlockSpec((tb, D_MODEL), lambda i: (i, 0)),
                  pl.BlockSpec((1, D_MODEL), lambda i: (0, 0))],
        out_specs=[pl.BlockSpec((tb, D_MODEL), lambda i: (i, 0)), pl.BlockSpec((1, D_MODEL), lambda i: (0, 0)),
                   pl.BlockSpec((8, LANES), lambda i: (0, 0)), pl.BlockSpec((tb, D_MODEL), lambda i: (i, 0))],
        out_shape=[_sds((T, D_MODEL), F32), _sds((1, D_MODEL), F32), _sds((8, LANES), F32),
                   _sds((T, D_MODEL), BF16)],
        compiler_params=_cparams(("arbitrary",)), name="loss_head",
    )(h, target, nf)


CONV_TC = 1024
HALO = 16


def _conv_fwd(name, x, w8, b):
    T, C = x.shape
    tb = min(512, T)
    nb = tb // HALO

    def body(x_ref, xp_ref, w_ref, b_ref, y_ref, sc):
        i = pl.program_id(0)
        xv = x_ref[...].astype(F32)
        sc[pl.ds(HALO, tb), :] = xv
        sc[pl.ds(0, HALO), :] = jnp.where(i > 0, xp_ref[...].astype(F32), 0.0)
        acc = b_ref[...] + w_ref[3:4, :] * xv
        for k in range(3):
            acc = acc + w_ref[k:k + 1, :] * sc[pl.ds(HALO - 3 + k, tb), :]
        y_ref[...] = acc

    return pl.pallas_call(
        body, grid=(T // tb, C // CONV_TC),
        in_specs=[pl.BlockSpec((tb, CONV_TC), lambda i, j: (i, j)),
                  pl.BlockSpec((HALO, CONV_TC), lambda i, j: (jnp.maximum(i * nb - 1, 0), j)),
                  pl.BlockSpec((8, CONV_TC), lambda i, j: (0, j)),
                  pl.BlockSpec((1, CONV_TC), lambda i, j: (0, j))],
        out_specs=pl.BlockSpec((tb, CONV_TC), lambda i, j: (i, j)),
        out_shape=_sds((T, C), F32),
        scratch_shapes=[pltpu.VMEM((tb + HALO, CONV_TC), F32)],
        compiler_params=_cparams(("parallel", "parallel")), name=name,
    )(x, x, w8, b)


def _conv_bwd(name, x, dy, w8):
    T, C = x.shape
    tb = min(512, T)
    nb = tb // HALO
    nt = T // tb

    def body(x_ref, xp_ref, dy_ref, dyn_ref, w_ref, dx_ref, dwb_ref, scx, scd):
        i = pl.program_id(1)
        dyv = dy_ref[...]
        xv = x_ref[...].astype(F32)
        scx[pl.ds(HALO, tb), :] = xv
        scx[pl.ds(0, HALO), :] = jnp.where(i > 0, xp_ref[...].astype(F32), 0.0)
        scd[pl.ds(0, tb), :] = dyv
        scd[pl.ds(tb, HALO), :] = jnp.where(i < nt - 1, dyn_ref[...], 0.0)
        dx = w_ref[3:4, :] * dyv
        rows = []
        for k in range(3):
            dx = dx + w_ref[k:k + 1, :] * scd[pl.ds(3 - k, tb), :]
            rows.append(jnp.sum(dyv * scx[pl.ds(HALO - 3 + k, tb), :], axis=0, keepdims=True))
        rows.append(jnp.sum(dyv * xv, axis=0, keepdims=True))
        rows.append(jnp.sum(dyv, axis=0, keepdims=True))
        rows.append(jnp.zeros((3, CONV_TC), F32))
        dx_ref[...] = dx.astype(dx_ref.dtype)

        @pl.when(i == 0)
        def _():
            dwb_ref[...] = jnp.zeros_like(dwb_ref)

        dwb_ref[...] += jnp.concatenate(rows, axis=0)

    return pl.pallas_call(
        body, grid=(C // CONV_TC, nt),
        in_specs=[pl.BlockSpec((tb, CONV_TC), lambda j, i: (i, j)),
                  pl.BlockSpec((HALO, CONV_TC), lambda j, i: (jnp.maximum(i * nb - 1, 0), j)),
                  pl.BlockSpec((tb, CONV_TC), lambda j, i: (i, j)),
                  pl.BlockSpec((HALO, CONV_TC), lambda j, i: (jnp.minimum((i + 1) * nb, T // HALO - 1), j)),
                  pl.BlockSpec((8, CONV_TC), lambda j, i: (0, j))],
        out_specs=[pl.BlockSpec((tb, CONV_TC), lambda j, i: (i, j)),
                   pl.BlockSpec((8, CONV_TC), lambda j, i: (0, j))],
        out_shape=[_sds((T, C), BF16), _sds((8, C), F32)],
        scratch_shapes=[pltpu.VMEM((tb + HALO, CONV_TC), F32), pltpu.VMEM((tb + HALO, CONV_TC), F32)],
        compiler_params=_cparams(("parallel", "arbitrary")), name=name,
    )(x, x, dy, dy, w8)


SCAN_TB = 512


def _lru_scan_fwd(a, b):
    T, C = a.shape
    tb = min(SCAN_TB, T)

    def body(a_ref, b_ref, h_ref, hp_ref, carry):
        @pl.when(pl.program_id(0) == 0)
        def _():
            carry[...] = jnp.zeros_like(carry)

        def group(gi, h):
            r0 = pl.multiple_of(gi * 8, 8)
            at = a_ref[pl.ds(r0, 8), :]
            bt = b_ref[pl.ds(r0, 8), :]
            hs, hps = [], []
            for r in range(8):
                hps.append(h)
                h = at[r:r + 1, :] * h + bt[r:r + 1, :]
                hs.append(h)
            h_ref[pl.ds(r0, 8), :] = jnp.concatenate(hs, axis=0)
            hp_ref[pl.ds(r0, 8), :] = jnp.concatenate(hps, axis=0)
            return h

        carry[0:1, :] = lax.fori_loop(0, tb // 8, group, carry[0:1, :])

    spec = pl.BlockSpec((tb, C), lambda i: (i, 0))
    return pl.pallas_call(
        body, grid=(T // tb,), in_specs=[spec, spec], out_specs=[spec, spec],
        out_shape=[_sds((T, C), F32), _sds((T, C), F32)],
        scratch_shapes=[pltpu.VMEM((8, C), F32)],
        compiler_params=_cparams(("arbitrary",)), name="lru_scan_fwd",
    )(a, b)


def _lru_scan_bwd(a, dh):
    T, C = a.shape
    tb = min(SCAN_TB, T)
    nt = T // tb

    def body(a_ref, dh_ref, g_ref, carry):
        @pl.when(pl.program_id(0) == 0)
        def _():
            carry[...] = jnp.zeros_like(carry)

        def group(gi, c):
            r0 = pl.multiple_of((tb // 8 - 1 - gi) * 8, 8)
            at = a_ref[pl.ds(r0, 8), :]
            dt = dh_ref[pl.ds(r0, 8), :]
            gs = [None] * 8
            for r in range(7, -1, -1):
                g = dt[r:r + 1, :] + c
                c = at[r:r + 1, :] * g
                gs[r] = g
            g_ref[pl.ds(r0, 8), :] = jnp.concatenate(gs, axis=0)
            return c

        carry[0:1, :] = lax.fori_loop(0, tb // 8, group, carry[0:1, :])

    spec = pl.BlockSpec((tb, C), lambda i: (nt - 1 - i, 0))
    return pl.pallas_call(
        body, grid=(nt,), in_specs=[spec, spec], out_specs=spec,
        out_shape=_sds((T, C), F32),
        scratch_shapes=[pltpu.VMEM((8, C), F32)],
        compiler_params=_cparams(("arbitrary",)), name="lru_scan_bwd",
    )(a, dh)


def _ssd_chunk(xpre, dtraw, state, dtb, alog):
    xc = jax.nn.silu(xpre)
    xs = xc[:, :SSD_INNER]
    bm = xc[:, SSD_INNER:SSD_INNER + SSD_GROUPS * SSD_STATE]
    cm = xc[:, SSD_INNER + SSD_GROUPS * SSD_STATE:]
    dt = jax.nn.softplus(dtraw + dtb)
    a = dt * (-jnp.exp(alog))
    ltri = (lax.broadcasted_iota(jnp.int32, (CHUNK, CHUNK), 0)
            >= lax.broadcasted_iota(jnp.int32, (CHUNK, CHUNK), 1)).astype(F32)
    a_cs = jnp.dot(ltri, a, precision=HIGHEST, preferred_element_type=F32)
    npair = SSD_HEADS // 2
    pi = lax.broadcasted_iota(jnp.int32, (npair, LANES), 0)
    hi = lax.broadcasted_iota(jnp.int32, (npair, LANES), 1)
    sel_even = (hi == 2 * pi).astype(F32)
    sel_odd = (hi == 2 * pi + 1).astype(F32)
    top = lax.broadcasted_iota(jnp.int32, (2 * CHUNK, LANES), 0) < CHUNK

    def pair_transpose(v):
        v2 = jnp.concatenate([v, v], axis=0)
        dn = (((1,), (1,)), ((), ()))
        return (lax.dot_general(sel_even, jnp.where(top, v2, 0.0), dn, precision=HIGHEST, preferred_element_type=F32)
                + lax.dot_general(sel_odd, jnp.where(top, 0.0, v2), dn, precision=HIGHEST, preferred_element_type=F32))

    a_t2 = pair_transpose(a_cs)
    dt_t2 = pair_transpose(dt)
    a_last = a_cs[CHUNK - 1:CHUNK, :]
    dte = jnp.exp(a_last - a_cs) * dt
    cd = jnp.exp(a_last)
    lane = lax.broadcasted_iota(jnp.int32, (CHUNK, LANES), 1)
    left = lane < SSD_HEAD_DIM
    right = jnp.logical_not(left)
    left1 = left[0:1]
    tril2 = lax.broadcasted_iota(jnp.int32, (CHUNK, LANES), 0) >= (lane & (SSD_HEAD_DIM - 1))
    gw = SSD_INNER // SSD_GROUPS
    ys, news = [], []
    for g in range(SSD_GROUPS):
        bg = bm[:, SSD_STATE * g:SSD_STATE * (g + 1)].astype(BF16)
        cg = cm[:, SSD_STATE * g:SSD_STATE * (g + 1)].astype(BF16)
        bg2 = jnp.concatenate([bg, bg], axis=0)
        scores2 = lax.dot_general(cg, bg2, (((1,), (1,)), ((), ())), preferred_element_type=F32)
        coff = jnp.dot(cg, state[:, gw * g:gw * (g + 1)].astype(BF16), preferred_element_type=F32)
        for j in range(gw // LANES):
            lo = gw * g + LANES * j
            p = lo // LANES
            h0 = 2 * p
            xp = xs[:, lo:lo + LANES]
            col = jnp.where(left, a_cs[:, h0:h0 + 1], a_cs[:, h0 + 1:h0 + 2])
            dm = jnp.exp(jnp.where(tril2, col - a_t2[p:p + 1, :], -1e30)) * dt_t2[p:p + 1, :]
            sd = (scores2 * dm).astype(BF16)
            x_bd = jnp.concatenate([jnp.where(left, xp, 0.0), jnp.where(right, xp, 0.0)], axis=0).astype(BF16)
            acc = jnp.dot(sd, x_bd, preferred_element_type=F32)
            ys.append(acc + coff[:, LANES * j:LANES * (j + 1)] * jnp.exp(col))
            dtee = jnp.where(left, dte[:, h0:h0 + 1], dte[:, h0 + 1:h0 + 2])
            xw = (xp * dtee).astype(BF16)
            cde = jnp.where(left1, cd[:, h0:h0 + 1], cd[:, h0 + 1:h0 + 2])
            news.append(state[:, lo:lo + LANES] * cde
                        + lax.dot_general(bg, xw, (((0,), (0,)), ((), ())), preferred_element_type=F32))
    return jnp.concatenate(ys, axis=1), jnp.concatenate(news, axis=1)


def _ssd_fwd(name, xpre, dtraw, dtb, alog, rider=None):
    T = xpre.shape[0]
    n = T // CHUNK
    r_args, r_in, r_out, r_shape, r_scratch = _rider_parts(rider)

    def body(xp, dr, dtb_ref, al_ref, y_ref, st_ref, state):
        @pl.when(pl.program_id(0) == 0)
        def _():
            state[...] = jnp.zeros_like(state)

        st_ref[0] = state[...]
        y, new = _ssd_chunk(xp[...], dr[...], state[...], dtb_ref[...], al_ref[...])
        y_ref[...] = y
        state[...] = new

    small = pl.BlockSpec((1, LANES), lambda c: (0, 0))
    return pl.pallas_call(
        _with_rider(body, 4, 2, rider, n), grid=(n,),
        in_specs=[pl.BlockSpec((CHUNK, SSD_CONV_DIM), lambda c: (c, 0)),
                  pl.BlockSpec((CHUNK, DT_PAD), lambda c: (c, 0)), small, small] + r_in,
        out_specs=[pl.BlockSpec((CHUNK, SSD_INNER), lambda c: (c, 0)),
                   pl.BlockSpec((1, SSD_STATE, SSD_INNER), lambda c: (c, 0, 0))] + r_out,
        out_shape=[_sds((T, SSD_INNER), F32), _sds((n, SSD_STATE, SSD_INNER), F32)] + r_shape,
        scratch_shapes=[pltpu.VMEM((SSD_STATE, SSD_INNER), F32)] + r_scratch,
        compiler_params=_cparams(("arbitrary",)), name=name,
    )(xpre, dtraw, dtb, alog, *r_args)


def _ssd_bwd(name, xpre, dtraw, states, dy, dxs_extra, dtb, alog, rider=None):
    T = xpre.shape[0]
    n = T // CHUNK
    r_args, r_in, r_out, r_shape, r_scratch = _rider_parts(rider)

    def body(xp, dr, st, dy_ref, dx_ref, dtb_ref, al_ref, dxp_ref, ddr_ref, ddtb_ref, dal_ref, dstate):
        @pl.when(pl.program_id(0) == 0)
        def _():
            dstate[...] = jnp.zeros_like(dstate)
            ddtb_ref[...] = jnp.zeros_like(ddtb_ref)
            dal_ref[...] = jnp.zeros_like(dal_ref)

        _, vjp = jax.vjp(_ssd_chunk, xp[...], dr[...], st[0], dtb_ref[...], al_ref[...])
        dxp, ddr, ds, db, da = vjp((dy_ref[...], dstate[...]))
        dxp_ref[:, :SSD_INNER] = dxp[:, :SSD_INNER] + dx_ref[...]
        dxp_ref[:, SSD_INNER:] = dxp[:, SSD_INNER:]
        ddr_ref[...] = ddr.astype(ddr_ref.dtype)
        dstate[...] = ds
        ddtb_ref[...] += db
        dal_ref[...] += da

    def rev(c):
        return (n - 1 - c, 0)

    small = pl.BlockSpec((1, LANES), lambda c: (0, 0))
    return pl.pallas_call(
        _with_rider(body, 7, 4, rider, n), grid=(n,),
        in_specs=[pl.BlockSpec((CHUNK, SSD_CONV_DIM), rev), pl.BlockSpec((CHUNK, DT_PAD), rev),
                  pl.BlockSpec((1, SSD_STATE, SSD_INNER), lambda c: (n - 1 - c, 0, 0)),
                  pl.BlockSpec((CHUNK, SSD_INNER), rev), pl.BlockSpec((CHUNK, SSD_INNER), rev), small, small] + r_in,
        out_specs=[pl.BlockSpec((CHUNK, SSD_CONV_DIM), rev), pl.BlockSpec((CHUNK, DT_PAD), rev), small, small] + r_out,
        out_shape=[_sds((T, SSD_CONV_DIM), F32), _sds((T, DT_PAD), BF16), _sds((1, LANES), F32),
                   _sds((1, LANES), F32)] + r_shape,
        scratch_shapes=[pltpu.VMEM((SSD_STATE, SSD_INNER), F32)] + r_scratch,
        compiler_params=_cparams(("arbitrary",)), name=name,
    )(xpre, dtraw, states, dy, dxs_extra, dtb, alog, *r_args)


HBM_SPEC = pl.BlockSpec(memory_space=pltpu.HBM)
N_PEER = N_DEV - 1


def _position():
    return lax.axis_index("x"), lax.axis_index("y"), lax.axis_index("c")


def _all_gather(name, blks):
    return _exchange_call(name, _gather_phases, blks, [_sds((N_DEV,) + b.shape, b.dtype) for b in blks])


def _scatter_exchange(name, gs):
    return _exchange_call(name, _scatter_phases, gs, [_sds(g.shape, g.dtype) for g in gs])


def _exchange_scratch(na):
    return [pltpu.SemaphoreType.DMA((na * N_PEER,)), pltpu.SemaphoreType.DMA((na * N_PEER,)),
            pltpu.SemaphoreType.DMA((na,))]


def _exchange_call(name, phases, arrays, out_shape):
    na = len(arrays)

    def body(*refs):
        begin, finish = phases(refs[:na], refs[na:2 * na], *refs[2 * na:])
        begin()
        finish()

    return pl.pallas_call(
        body, out_shape=out_shape, in_specs=[HBM_SPEC] * na, out_specs=[HBM_SPEC] * na,
        scratch_shapes=_exchange_scratch(na), name=name,
    )(*arrays)


def _gather_phases(x_refs, out_refs, send_sems, recv_sems, local_sems):
    na = len(x_refs)
    x, y, c = _position()
    me, sibling = (x, y, c), (x, y, 1 - c)
    chips = [(1 - x, y), (x, 1 - y), (1 - x, 1 - y)]

    def slot(a, px, py, pc):
        return out_refs[a].at[4 * px + 2 * py + pc]

    def copy(a, k, block, to, src=None):
        return pltpu.make_async_remote_copy(
            src_ref=slot(a, *block) if src is None else src, dst_ref=slot(a, *block),
            send_sem=send_sems.at[a * N_PEER + k], recv_sem=recv_sems.at[a * N_PEER + k],
            device_id=to, device_id_type=MESH)

    mine = [pltpu.make_async_copy(x_refs[a], slot(a, *me), local_sems.at[a]) for a in range(na)]
    first = []
    for a in range(na):
        first.append(copy(a, 0, me, sibling, src=x_refs[a]))
        first += [copy(a, 1 + j, me, (*chip, c), src=x_refs[a]) for j, chip in enumerate(chips)]
    passed = [copy(a, 4 + j, (*chip, c), sibling) for j, chip in enumerate(chips) for a in range(na)]

    def begin():
        for cp in mine + first:
            cp.start()

    def finish():
        for j, chip in enumerate(chips):
            for a in range(na):
                copy(a, 1 + j, (*chip, c), me).wait_recv()
                passed[j * na + a].start()
        for a in range(na):
            copy(a, 0, sibling, me).wait_recv()
            for j, chip in enumerate(chips):
                copy(a, 4 + j, (*chip, 1 - c), me).wait_recv()
        for cp in first + passed:
            cp.wait_send()
        for cp in mine:
            cp.wait()

    return begin, finish


def _scatter_phases(g_refs, q_refs, send_sems, recv_sems, local_sems):
    na = len(g_refs)
    x, y, c = _position()
    me = 4 * x + 2 * y + c
    mine, sends, recvs = [], [], []
    for a in range(na):
        mine.append(pltpu.make_async_copy(g_refs[a].at[me], q_refs[a].at[me], local_sems.at[a]))
        for k in range(1, N_DEV):
            px, py, pc = (x + (k >> 2)) % 2, (y + ((k >> 1) & 1)) % 2, (c + (k & 1)) % 2
            peer = 4 * px + 2 * py + pc
            sem = a * N_PEER + k - 1
            sends.append(pltpu.make_async_remote_copy(
                src_ref=g_refs[a].at[peer], dst_ref=q_refs[a].at[me], send_sem=send_sems.at[sem],
                recv_sem=recv_sems.at[sem], device_id=(px, py, pc), device_id_type=MESH))
            recvs.append(pltpu.make_async_remote_copy(
                src_ref=g_refs[a].at[me], dst_ref=q_refs[a].at[peer], send_sem=send_sems.at[sem],
                recv_sem=recv_sems.at[sem], device_id=(px, py, pc), device_id_type=MESH))

    def begin():
        for cp in mine + sends:
            cp.start()

    def finish():
        for cp in recvs:
            cp.wait_recv()
        for cp in sends:
            cp.wait_send()
        for cp in mine:
            cp.wait()

    return begin, finish


def _with_rider(body, n_in, n_out, rider, nsteps):
    if rider is None:
        return body
    phases, arrays, _ = rider
    na = len(arrays)

    def carried(*refs):
        ins, r_in = refs[:n_in], refs[n_in:n_in + na]
        outs = refs[n_in + na:n_in + na + n_out]
        r_out = refs[n_in + na + n_out:n_in + 2 * na + n_out]
        scratch, r_scratch = refs[n_in + 2 * na + n_out:-3], refs[-3:]
        begin, finish = phases(r_in, r_out, *r_scratch)
        step = pl.program_id(0)
        pl.when(step == 0)(begin)
        body(*ins, *outs, *scratch)
        pl.when(step == nsteps - 1)(finish)

    return carried


def _rider_parts(rider):
    if rider is None:
        return [], [], [], [], []
    _, arrays, out_shape = rider
    na = len(arrays)
    return list(arrays), [HBM_SPEC] * na, [HBM_SPEC] * na, list(out_shape), _exchange_scratch(na)


def _gather_rider(blks):
    return (_gather_phases, blks, [_sds((N_DEV,) + b.shape, b.dtype) for b in blks])


def _scatter_rider(gs):
    return (_scatter_phases, gs, [_sds(g.shape, g.dtype) for g in gs])


def _adamw_sum(name, w, m, v, recvs):
    L, r, c = w.shape
    tr = r if r <= 512 else 256
    assert r % tr == 0 and len(recvs) == L
    nr = r // tr

    def body(w_ref, m_ref, v_ref, *refs):
        q_refs = refs[:L]
        g_ref, d_ref, m2_ref, v2_ref = refs[L:]
        l = pl.program_id(0)
        for ll in range(L):
            @pl.when(l == ll)
            def _(q_ref=q_refs[ll]):
                gv = q_ref[0].astype(F32)
                for s in range(1, N_DEV):
                    gv = gv + q_ref[s].astype(F32)
                m2 = ADAM_B1 * m_ref[0] + (1.0 - ADAM_B1) * gv
                v2 = ADAM_B2 * v_ref[0] + (1.0 - ADAM_B2) * jnp.square(gv)
                m_hat = m2 / (1.0 - ADAM_B1 ** ADAM_STEP)
                v_hat = v2 / (1.0 - ADAM_B2 ** ADAM_STEP)
                g_ref[0] = gv
                d_ref[0] = -ADAM_LR * (m_hat / (jnp.sqrt(v_hat) + ADAM_EPS) + ADAM_WD * w_ref[0])
                m2_ref[0] = m2
                v2_ref[0] = v2

    def q_index(ll):
        return lambda l, i: (0, jnp.where(l == ll, i, jnp.where(l > ll, nr - 1, 0)), 0)

    spec = pl.BlockSpec((1, tr, c), lambda l, i: (l, i, 0))
    return pl.pallas_call(
        body, grid=(L, nr),
        in_specs=[spec] * 3 + [pl.BlockSpec((N_DEV, tr, c), q_index(ll)) for ll in range(L)],
        out_specs=[spec] * 4, out_shape=[_sds((L, r, c), F32)] * 4,
        compiler_params=_cparams(("arbitrary", "arbitrary")), name=name,
    )(w, m, v, *recvs)


def _sum_blocks(name, q, tr):
    R = q.shape[0] // N_DEV
    W = q.shape[1]
    tr = min(tr, R)
    assert R % tr == 0
    nb = R // tr

    def body(*refs):
        acc = refs[0][...].astype(F32)
        for r in refs[1:N_DEV]:
            acc = acc + r[...].astype(F32)
        refs[N_DEV][...] = acc

    return pl.pallas_call(
        body, grid=(nb,),
        in_specs=[pl.BlockSpec((tr, W), lambda i, s=s: (s * nb + i, 0)) for s in range(N_DEV)],
        out_specs=pl.BlockSpec((tr, W), lambda i: (i, 0)), out_shape=_sds((R, W), F32),
        compiler_params=_cparams(("parallel",)), name=name,
    )(*([q] * N_DEV))


def _adamw(name, w, g, m, v):
    R, C = w.shape
    tr = R
    if R > 512:
        tr = max(t for t in range(8, 513, 8) if R % t == 0)

    def body(w_ref, g_ref, m_ref, v_ref, d_ref, m2_ref, v2_ref):
        gv = g_ref[...]
        m2 = ADAM_B1 * m_ref[...] + (1.0 - ADAM_B1) * gv
        v2 = ADAM_B2 * v_ref[...] + (1.0 - ADAM_B2) * jnp.square(gv)
        m_hat = m2 / (1.0 - ADAM_B1 ** ADAM_STEP)
        v_hat = v2 / (1.0 - ADAM_B2 ** ADAM_STEP)
        d_ref[...] = -ADAM_LR * (m_hat / (jnp.sqrt(v_hat) + ADAM_EPS) + ADAM_WD * w_ref[...])
        m2_ref[...] = m2
        v2_ref[...] = v2

    spec = pl.BlockSpec((tr, C), lambda i: (i, 0))
    return pl.pallas_call(
        body, grid=(R // tr,), in_specs=[spec] * 4, out_specs=[spec] * 3,
        out_shape=[_sds((R, C), F32)] * 3, compiler_params=_cparams(("parallel",)), name=name,
    )(w, g, m, v)


BIG = ("w_in", "w_branch", "w_out", "w_ffn_in", "w_ffn_out")
BIG_ROW_SHARDED = {"w_in": False, "w_branch": True, "w_out": True, "w_ffn_in": False, "w_ffn_out": True}
SMALL = ("norm1_g", "b_gate", "lru_conv_w", "lru_conv_b", "lru_w_a", "lru_b_a", "lru_w_x", "lru_b_x", "lru_lambda",
         "ssd_conv_w", "ssd_conv_b", "ssd_dt_bias", "ssd_A_log", "ssd_D", "ssd_norm_g", "norm2_g", "norm_f")
SMALL_ROWS = 256


def _pack(arrs, dtype, row_mult):
    parts = []
    for a in arrs:
        f = a.reshape(-1).astype(dtype)
        pad = (-f.shape[0]) % LANES
        if pad:
            f = jnp.concatenate([f, jnp.zeros((pad,), dtype)])
        parts.append(f)
    f = jnp.concatenate(parts)
    pad = (-f.shape[0]) % (LANES * row_mult)
    if pad:
        f = jnp.concatenate([f, jnp.zeros((pad,), dtype)])
    return f.reshape(-1, LANES)


def _unpack(flat, shapes, lead=()):
    f = flat.reshape(lead + (-1,))
    out, off = [], 0
    for s in shapes:
        n = int(np.prod(s))
        out.append(f[..., off:off + n].reshape(lead + tuple(s)))
        off += n + (-n) % LANES
    return out


def _full_from_shards(name, st):
    if BIG_ROW_SHARDED[name]:
        return st.reshape((-1,) + st.shape[2:])
    return jnp.transpose(st, (1, 0, 2)).reshape(st.shape[1], -1)


def _shards_from_full(name, full):
    if BIG_ROW_SHARDED[name]:
        return full.reshape((N_DEV, full.shape[0] // N_DEV) + full.shape[1:])
    return jnp.transpose(full.reshape(full.shape[0], N_DEV, -1), (1, 0, 2))


def _block_diag_tiles(w):
    z = jnp.zeros((8, 64, 64), w.dtype)
    w2 = w.reshape(8, 2, 64, 64)
    top = jnp.concatenate([w2[:, 0], z], axis=2)
    bot = jnp.concatenate([z, w2[:, 1]], axis=2)
    return jnp.concatenate([top, bot], axis=1)


def _block_diag_untile(t):
    return jnp.stack([t[:, :64, :64], t[:, 64:, 64:]], axis=1).reshape(16, 64, 64)


def _pad_lanes(a, width):
    return jnp.concatenate([a, jnp.zeros(a.shape[:-1] + (width - a.shape[-1],), a.dtype)], axis=-1)


def _pad_rows8(w):
    return jnp.concatenate([w, jnp.zeros((8 - w.shape[0],) + w.shape[1:], w.dtype)], axis=0)


def _layer_params(l, shards, small):
    full = {n: _full_from_shards(n, shards[n]) for n in BIG}
    w_in = full["w_in"]
    seg = [w_in[:, IN_OFFS[k]:IN_OFFS[k + 1]] for k in range(6)]
    p = {
        "w_lx": seg[0], "w_lg": seg[1], "w_z": seg[2], "w_xbc": seg[3], "w_dt": _pad_lanes(seg[4], DT_PAD), "w_g": seg[5],
        "w_bra": full["w_branch"][:D_MODEL], "w_brb": full["w_branch"][D_MODEL:],
        "w_out": full["w_out"], "w_ffn_in": full["w_ffn_in"], "w_ffn_out": full["w_ffn_out"],
    }
    for n in ("norm1_g", "b_gate", "lru_conv_b", "lru_b_a", "lru_b_x", "lru_lambda", "ssd_conv_b", "ssd_norm_g", "norm2_g"):
        p[n] = small[n][l].reshape(1, -1)
    p["lru_conv_w8"] = _pad_rows8(small["lru_conv_w"][l])
    p["ssd_conv_w8"] = _pad_rows8(small["ssd_conv_w"][l])
    p["wa"] = _block_diag_tiles(small["lru_w_a"][l])
    p["wx"] = _block_diag_tiles(small["lru_w_x"][l])
    p["dtb"] = _pad_lanes(small["ssd_dt_bias"][l].reshape(1, -1), DT_PAD)
    p["alog"] = _pad_lanes(small["ssd_A_log"][l].reshape(1, -1), DT_PAD)
    p["d_e"] = jnp.repeat(small["ssd_D"][l], SSD_HEAD_DIM).reshape(1, -1)
    return p


def _layer_fwd(l, h, p, rider=None):
    n = f"l{l}_"
    s = {"h_in": h}
    (xn,) = _rowwise_fwd(n + "rms1", _f_rms, [h], [p["norm1_g"]], [(D_MODEL, BF16)], 256)
    s["xn"] = xn
    for k in ("lx", "lg", "z", "xbc", "g", "dt"):
        s[k] = _mm(n + "in_" + k, [(xn, p["w_" + k])], "nn", out_dtype=F32 if k == "dt" else BF16)
    s["u"] = _conv_fwd(n + "lru_conv", s["lx"], p["lru_conv_w8"], p["lru_conv_b"])
    lru_pars = [p["wa"], p["lru_b_a"], p["wx"], p["lru_b_x"], p["lru_lambda"]]
    s["a"], b = _rowwise_fwd(n + "lru_gates", _f_lru_gates, [s["u"]], lru_pars, [(D_MODEL, F32)] * 2, 256)
    s["hl"], s["hprev"] = _lru_scan_fwd(s["a"], b)
    s["xpre"] = _conv_fwd(n + "ssd_conv", s["xbc"], p["ssd_conv_w8"], p["ssd_conv_b"])
    s["yssd"], s["states"], *carried = _ssd_fwd(n + "ssd", s["xpre"], s["dt"], p["dtb"], p["alog"], rider)
    post_rows = [s["hl"], s["lg"], s["yssd"], (s["xpre"], 0, SSD_INNER), s["z"]]
    s["ya"], s["yb"] = _rowwise_fwd(n + "post", _f_post, post_rows, [p["d_e"], p["ssd_norm_g"]],
                                    [(D_MODEL, BF16), (SSD_INNER, BF16)], 128)
    s["ma"] = _mm(n + "br_a", [(s["ya"], p["w_bra"])], "nn", out_dtype=BF16)
    s["mb"] = _mm(n + "br_b", [(s["yb"], p["w_brb"])], "nn", out_dtype=BF16)
    (s["merged"],) = _rowwise_fwd(n + "merge", _f_merge, [s["ma"], s["mb"], s["g"]], [p["b_gate"]],
                                  [(D_MODEL, BF16)], 256)
    s["h_mid"] = _mm(n + "out", [(s["merged"], p["w_out"])], "nn", res=h)
    (s["xn2"],) = _rowwise_fwd(n + "rms2", _f_rms, [s["h_mid"]], [p["norm2_g"]], [(D_MODEL, BF16)], 256)
    s["gu"] = _mm(n + "ffn_in", [(s["xn2"], p["w_ffn_in"])], "nn", out_dtype=BF16)
    (s["act"],) = _rowwise_fwd(n + "act", _f_act, [s["gu"]], [], [(D_FF, BF16)], 256)
    h_out = _mm(n + "ffn_out", [(s["act"], p["w_ffn_out"])], "nn", res=s["h_mid"])
    return h_out, s, carried


def _add(name, a, b):
    w = a.shape[1]
    return _rowwise_fwd(name, lambda x, y: (x + y, x + y), [a, b], [], [(w, F32), (w, BF16)], 512)


def _layer_bwd(l, dh, dh_b, s, p, rider=None):
    n = f"l{l}_b_"
    gw, gs = {}, {}
    d_act = _mm(n + "d_act", [(dh_b, p["w_ffn_out"])], "nt")
    gw["w_ffn_out"] = _mm_tn(n + "dw_ffn_out", s["act"], dh_b)
    (d_gu,) = _rowwise_bwd(n + "act", _f_act, [s["gu"]], [], [d_act], _ident, [BF16], 128)
    d_xn2 = _mm(n + "d_xn2", [(d_gu, p["w_ffn_in"])], "nt")
    gw["w_ffn_in"] = _mm_tn(n + "dw_ffn_in", s["xn2"], d_gu)
    d_rms2, gs["norm2_g"] = _rowwise_bwd(n + "rms2", _f_rms, [s["h_mid"]], [p["norm2_g"]], [d_xn2], _ident, [F32], 256)
    dh_mid, dh_mid_b = _add(n + "add2", dh, d_rms2)
    d_merged = _mm(n + "d_merged", [(dh_mid_b, p["w_out"])], "nt")
    gw["w_out"] = _mm_tn(n + "dw_out", s["merged"], dh_mid_b)
    d_ma, d_mb, d_g, gs["b_gate"] = _rowwise_bwd(n + "merge", _f_merge, [s["ma"], s["mb"], s["g"]], [p["b_gate"]],
                                                 [d_merged], _ident, [BF16, BF16, BF16], 256)
    d_ya = _mm(n + "d_ya", [(d_ma, p["w_bra"])], "nt")
    d_yb = _mm(n + "d_yb", [(d_mb, p["w_brb"])], "nt")
    gw["w_branch"] = jnp.concatenate([_mm_tn(n + "dw_bra", s["ya"], d_ma), _mm_tn(n + "dw_brb", s["yb"], d_mb)], axis=0)
    post_rows = [s["hl"], s["lg"], s["yssd"], (s["xpre"], 0, SSD_INNER), s["z"]]
    d_hl, d_lg, d_yssd, d_xs, d_z, d_de, gs["ssd_norm_g"] = _rowwise_bwd(
        n + "post", _f_post, post_rows, [p["d_e"], p["ssd_norm_g"]], [d_ya, d_yb], _ident,
        [F32, BF16, F32, F32, BF16], 64)
    gs["ssd_D"] = d_de.reshape(SSD_HEADS, SSD_HEAD_DIM).sum(axis=1)
    d_xpre, d_dt, d_dtb, d_alog, *carried = _ssd_bwd(n + "ssd", s["xpre"], s["dt"], s["states"], d_yssd, d_xs,
                                                     p["dtb"], p["alog"], rider)
    gs["ssd_dt_bias"] = d_dtb[0, :SSD_HEADS]
    gs["ssd_A_log"] = d_alog[0, :SSD_HEADS]
    d_xbc, dwb = _conv_bwd(n + "ssd_conv", s["xbc"], d_xpre, p["ssd_conv_w8"])
    gs["ssd_conv_w"], gs["ssd_conv_b"] = dwb[:4], dwb[4]
    g_scan = _lru_scan_bwd(s["a"], d_hl)
    lru_pars = [p["wa"], p["lru_b_a"], p["wx"], p["lru_b_x"], p["lru_lambda"]]
    d_u, d_wa, gs["lru_b_a"], d_wx, gs["lru_b_x"], gs["lru_lambda"] = _rowwise_bwd(
        n + "lru_gates", _f_lru_gates, [s["u"]], lru_pars, [g_scan, s["hprev"]],
        lambda g, hp: (g * hp, g), [F32], 128)
    gs["lru_w_a"], gs["lru_w_x"] = _block_diag_untile(d_wa), _block_diag_untile(d_wx)
    d_lx, dwb = _conv_bwd(n + "lru_conv", s["lx"], d_u, p["lru_conv_w8"])
    gs["lru_conv_w"], gs["lru_conv_b"] = dwb[:4], dwb[4]
    segs = [("lx", d_lx), ("lg", d_lg), ("z", d_z), ("xbc", d_xbc), ("dt", d_dt), ("g", d_g)]
    d_xn = _mm(n + "d_xn_a", [(d, p["w_" + k]) for k, d in segs[:3]], "nt")
    d_xn = _mm(n + "d_xn_b", [(d, p["w_" + k]) for k, d in segs[3:]], "nt", res=d_xn)
    dws = [_mm_tn(n + "dw_in_" + k, s["xn"], d) for k, d in segs]
    dws[4] = dws[4][:, :IN_WIDTHS[4]]
    gw["w_in"] = jnp.concatenate(dws, axis=1)
    d_rms1, gs["norm1_g"] = _rowwise_bwd(n + "rms1", _f_rms, [s["h_in"]], [p["norm1_g"]], [d_xn], _ident, [F32], 256)
    dh_in, dh_in_b = _add(n + "add1", dh_mid, d_rms1)
    for k in ("norm1_g", "norm2_g", "b_gate", "ssd_norm_g", "lru_b_a", "lru_b_x", "lru_lambda"):
        gs[k] = gs[k].reshape(-1)
    return dh_in, dh_in_b, gw, gs, carried


def _step(inp):
    x = inp["x"][0]
    target = inp["loss_target"][0]
    dev = 4 * lax.axis_index("x") + 2 * lax.axis_index("y") + lax.axis_index("c")

    def mine(l):
        return [inp[n][l].astype(BF16) for n in BIG]

    gathered = _all_gather("gather_first", mine(0) + [inp["lru_conv_w"], inp["ssd_conv_w"]])
    small = {n: inp[n] for n in SMALL}
    small["lru_conv_w"] = jnp.moveaxis(gathered[-2], 0, 2).reshape(DEPTH, 4, -1)
    small["ssd_conv_w"] = jnp.moveaxis(gathered[-1], 0, 2).reshape(DEPTH, 4, -1)
    gathered = gathered[:len(BIG)]

    h, saved, params = x, [], []
    for l in range(DEPTH):
        params.append(_layer_params(l, dict(zip(BIG, gathered)), small))
        rider = _gather_rider(mine(l + 1)) if l + 1 < DEPTH else None
        h, s, gathered = _layer_fwd(l, h, params[l], rider)
        saved.append(s)
    dh, d_nf, loss_acc, dh_b = _loss_head(h, target, small["norm_f"].reshape(1, -1))
    loss = lax.psum(loss_acc[0, 0], ("x", "y", "c"))

    gss, received = [None] * DEPTH, [None] * DEPTH
    rider = None
    for l in reversed(range(DEPTH)):
        dh, dh_b, gw, gss[l], carried = _layer_bwd(l, dh, dh_b, saved[l], params[l], rider)
        if rider is not None:
            received[l + 1] = carried
        contrib = [_shards_from_full(n, gw[n]) for n in BIG]
        rider = _scatter_rider(contrib)
    received[0] = _scatter_exchange("scatter_last", contrib)
    grad_x = dh[None]
    out = {"loss": loss, "grad_x": grad_x}
    for k, n in enumerate(BIG):
        out["grad_" + n], out["delta_" + n], out["new_m_" + n], out["new_v_" + n] = _adamw_sum(
            "adamw_" + n, inp[n], inp["m_" + n], inp["v_" + n], [received[l][k] for l in range(DEPTH)])

    small_full = {n: jnp.stack([gss[l][n] for l in range(DEPTH)]) for n in SMALL if n != "norm_f"}
    small_full["norm_f"] = d_nf.reshape(-1)
    part = _pack([small_full[n] for n in SMALL], F32, SMALL_ROWS)
    (everyone,) = _all_gather("gather_small_grads", [part])
    g_small_flat = _sum_blocks("sum_small_grads", everyone.reshape(-1, LANES), SMALL_ROWS)
    g_small = dict(zip(SMALL, _unpack(g_small_flat, [small_full[n].shape for n in SMALL])))
    for n in ("lru_conv_w", "ssd_conv_w"):
        w = inp[n].shape[-1]
        g_small[n] = lax.dynamic_slice_in_dim(g_small[n], dev * w, w, axis=2)

    shapes = [inp[n].shape for n in SMALL]
    packs = [_pack([src[pre + n] for n in SMALL], F32, SMALL_ROWS)
             for src, pre in ((inp, ""), (g_small, ""), (inp, "m_"), (inp, "v_"))]
    d, m2, v2 = _adamw("adamw_small", *packs)
    for n, dd, mm, vv in zip(SMALL, _unpack(d, shapes), _unpack(m2, shapes), _unpack(v2, shapes)):
        out["grad_" + n] = g_small[n]
        out["delta_" + n], out["new_m_" + n], out["new_v_" + n] = dd, mm, vv
    return out


WEIGHTS = ("norm1_g", "w_in", "b_gate", "lru_conv_w", "lru_conv_b", "lru_w_a", "lru_b_a", "lru_w_x", "lru_b_x",
           "lru_lambda", "ssd_conv_w", "ssd_conv_b", "ssd_dt_bias", "ssd_A_log", "ssd_D", "ssd_norm_g", "w_branch",
           "w_out", "norm2_g", "w_ffn_in", "w_ffn_out", "norm_f")


def kernel(x, norm1_g, w_in, b_gate, lru_conv_w, lru_conv_b, lru_w_a, lru_b_a, lru_w_x, lru_b_x, lru_lambda, ssd_conv_w, ssd_conv_b, ssd_dt_bias, ssd_A_log, ssd_D, ssd_norm_g, w_branch, w_out, norm2_g, w_ffn_in, w_ffn_out, norm_f, loss_target, m_norm1_g, m_w_in, m_b_gate, m_lru_conv_w, m_lru_conv_b, m_lru_w_a, m_lru_b_a, m_lru_w_x, m_lru_b_x, m_lru_lambda, m_ssd_conv_w, m_ssd_conv_b, m_ssd_dt_bias, m_ssd_A_log, m_ssd_D, m_ssd_norm_g, m_w_branch, m_w_out, m_norm2_g, m_w_ffn_in, m_w_ffn_out, m_norm_f, v_norm1_g, v_w_in, v_b_gate, v_lru_conv_w, v_lru_conv_b, v_lru_w_a, v_lru_b_a, v_lru_w_x, v_lru_b_x, v_lru_lambda, v_ssd_conv_w, v_ssd_conv_b, v_ssd_dt_bias, v_ssd_A_log, v_ssd_D, v_ssd_norm_g, v_w_branch, v_w_out, v_norm2_g, v_w_ffn_in, v_w_ffn_out, v_norm_f):
    out = _step(dict(locals()))
    res = [out["loss"], out["grad_x"]]
    for pre in ("grad_", "delta_", "new_m_", "new_v_"):
        res += [out[pre + n] for n in WEIGHTS]
    return tuple(res)
```

```python
import functools

import numpy as np
import jax
import jax.numpy as jnp
from jax import lax
from jax.experimental import pallas as pl
from jax.experimental.pallas import tpu as pltpu

F32 = jnp.float32
BF16 = jnp.bfloat16
HIGHEST = lax.Precision.HIGHEST

D_MODEL = 1024
DEPTH = 2
CHUNK = 64
LRU_C = 8.0
SSD_INNER = 2048
SSD_HEADS = 32
SSD_HEAD_DIM = 64
SSD_GROUPS = 4
SSD_STATE = 128
SSD_CONV_DIM = 3072
D_FF = 2816
EPS = 1e-6
N_DEV = 8
LANES = 128
DT_PAD = LANES
IN_WIDTHS = (1024, 1024, 2048, 3072, 32, 2048)
IN_OFFS = tuple(int(v) for v in np.cumsum((0,) + IN_WIDTHS))

ADAM_LR = 0.001
ADAM_B1 = 0.9
ADAM_B2 = 0.999
ADAM_EPS = 1e-08
ADAM_WD = 0.01
ADAM_STEP = 10

VMEM_LIMIT = 56 * 1024 * 1024
MESH = pl.DeviceIdType.MESH


def _cparams(sem=None):
    return pltpu.CompilerParams(dimension_semantics=sem, vmem_limit_bytes=VMEM_LIMIT)


def _sds(shape, dtype):
    return jax.ShapeDtypeStruct(tuple(shape), dtype)


def _pick(n, cap):
    if n <= cap:
        return n
    best = LANES
    for t in range(LANES, cap + 1, LANES):
        if n % t == 0:
            best = t
    assert n % best == 0, (n, cap)
    return best


def _mm(name, pairs, mode, out_dtype=F32, res=None, rider=None):
    M = pairs[0][0].shape[0]
    N = pairs[0][1].shape[1] if mode == "nn" else pairs[0][1].shape[0]
    npair = len(pairs)
    tm = min(M, 1024 if npair <= 3 else 512)
    tn = _pick(N, 1536 if npair == 1 else 1024)
    tks, nks, starts = [], [], []
    s = 0
    for a, _ in pairs:
        k = a.shape[1]
        tk = _pick(k, 1536)
        tks.append(tk)
        nks.append(k // tk)
        starts.append(s)
        s += k // tk
    nk = s
    dims = (((1,), (0,)), ((), ())) if mode == "nn" else (((1,), (1,)), ((), ()))

    def body(*refs):
        ab = refs[:2 * npair]
        pos = 2 * npair
        r_ref = None
        if res is not None:
            r_ref = refs[pos]
            pos += 1
        o_ref = refs[pos]
        acc = refs[pos + 1] if nk > 1 else None
        k = pl.program_id(2)

        def finish(r):
            if r_ref is not None:
                r = r + r_ref[...].astype(F32)
            o_ref[...] = r.astype(o_ref.dtype)

        for p in range(npair):
            a_ref, b_ref = ab[2 * p], ab[2 * p + 1]
            lo, hi = starts[p], starts[p] + nks[p]

            def step(a_ref=a_ref, b_ref=b_ref, lo=lo, hi=hi):
                d = lax.dot_general(a_ref[...].astype(BF16), b_ref[...].astype(BF16), dims,
                                    preferred_element_type=F32)
                if nk == 1:
                    finish(d)
                    return
                if lo == 0:
                    @pl.when(k == 0)
                    def _():
                        acc[...] = d
                if hi == nk:
                    @pl.when(k == nk - 1)
                    def _():
                        finish(acc[...] + d)
                if max(lo, 1) < min(hi, nk - 1):
                    @pl.when((k > 0) & (k < nk - 1))
                    def _():
                        acc[...] += d

            if npair == 1:
                step()
            else:
                pl.when((k >= lo) & (k < hi))(step)

    in_specs, args = [], []
    for p, (a, b) in enumerate(pairs):
        def kk(k, p=p):
            return jnp.clip(k - starts[p], 0, nks[p] - 1)
        in_specs.append(pl.BlockSpec((tm, tks[p]), lambda i, j, k, kk=kk: (i, kk(k))))
        if mode == "nn":
            in_specs.append(pl.BlockSpec((tks[p], tn), lambda i, j, k, kk=kk: (kk(k), j)))
        else:
            in_specs.append(pl.BlockSpec((tn, tks[p]), lambda i, j, k, kk=kk: (j, kk(k))))
        args += [a, b]
    if res is not None:
        in_specs.append(pl.BlockSpec((tm, tn), lambda i, j, k: (i, j)))
        args.append(res)
    r_args, r_in, r_out, r_shape, r_scratch = _rider_parts(rider)
    grid = (M // tm, N // tn, nk)
    out = pl.pallas_call(
        _with_rider(body, len(args), 1, rider, grid), grid=grid, in_specs=in_specs + r_in,
        out_specs=[pl.BlockSpec((tm, tn), lambda i, j, k: (i, j))] + r_out,
        out_shape=[_sds((M, N), out_dtype)] + r_shape,
        scratch_shapes=([pltpu.VMEM((tm, tn), F32)] if nk > 1 else []) + r_scratch,
        compiler_params=_cparams(("parallel", "parallel", "arbitrary") if rider is None else ("arbitrary",) * 3),
        name=name,
    )(*args, *r_args)
    return out[0] if rider is None else (out[0], out[1:])


def _mm_tn(name, a, b, out_dtype=BF16):
    M, Ka = a.shape
    N = b.shape[1]
    tm = min(M, 2048)
    tka = _pick(Ka, 1024)
    tn = _pick(N, 1024)
    nm = M // tm

    def body(a_ref, b_ref, o_ref, *scratch):
        k = pl.program_id(2)
        d = lax.dot_general(a_ref[...].astype(BF16), b_ref[...].astype(BF16),
                            (((0,), (0,)), ((), ())), preferred_element_type=F32)
        if nm == 1:
            o_ref[...] = d.astype(o_ref.dtype)
            return
        acc = scratch[0]

        @pl.when(k == 0)
        def _():
            acc[...] = d

        @pl.when((k > 0) & (k < nm - 1))
        def _():
            acc[...] += d

        @pl.when(k == nm - 1)
        def _():
            o_ref[...] = (acc[...] + d).astype(o_ref.dtype)

    return pl.pallas_call(
        body, grid=(Ka // tka, N // tn, nm),
        in_specs=[pl.BlockSpec((tm, tka), lambda i, j, k: (k, i)),
                  pl.BlockSpec((tm, tn), lambda i, j, k: (k, j))],
        out_specs=pl.BlockSpec((tka, tn), lambda i, j, k: (i, j)),
        out_shape=_sds((Ka, N), out_dtype),
        scratch_shapes=[pltpu.VMEM((tka, tn), F32)] if nm > 1 else [],
        compiler_params=_cparams(("parallel", "parallel", "arbitrary")), name=name,
    )(a, b)


def _rowwise_fwd(name, fn, rows, pars, outs, tb, rider=None):
    rows = [r if isinstance(r, tuple) else (r, 0, r.shape[1]) for r in rows]
    T = rows[0][0].shape[0]
    tb = min(tb, T)
    nr, npar = len(rows), len(pars)
    r_args, r_in, r_out, r_shape, r_scratch = _rider_parts(rider)

    def body(*refs):
        rv = [r[...].astype(F32) for r in refs[:nr]]
        pv = [p[...] for p in refs[nr:nr + npar]]
        res = fn(*rv, *pv)
        for o, r in zip(refs[nr + npar:], res):
            o[...] = r.astype(o.dtype)

    in_specs = [pl.BlockSpec((tb, w), lambda i, c=c: (i, c)) for _, c, w in rows]
    in_specs += [pl.BlockSpec(p.shape, lambda i, n=p.ndim: (0,) * n) for p in pars]
    return pl.pallas_call(
        _with_rider(body, nr + npar, len(outs), rider, (T // tb,)), grid=(T // tb,), in_specs=in_specs + r_in,
        out_specs=[pl.BlockSpec((tb, w), lambda i: (i, 0)) for w, _ in outs] + r_out,
        out_shape=[_sds((T, w), dt) for w, dt in outs] + r_shape, scratch_shapes=r_scratch,
        compiler_params=_cparams(("parallel",) if rider is None else ("arbitrary",)), name=name,
    )(*[r[0] for r in rows], *pars, *r_args)


def _rowwise_bwd(name, fn, rows, pars, cot_rows, cot_fn, row_out, tb):
    rows = [r if isinstance(r, tuple) else (r, 0, r.shape[1]) for r in rows]
    cot_rows = [r if isinstance(r, tuple) else (r, 0, r.shape[1]) for r in cot_rows]
    T = rows[0][0].shape[0]
    tb = min(tb, T)
    nr, npar, nc = len(rows), len(pars), len(cot_rows)
    want = [k for k, dt in enumerate(row_out) if dt is not None]

    def body(*refs):
        i = pl.program_id(0)
        rv = [r[...].astype(F32) for r in refs[:nr]]
        pv = [p[...] for p in refs[nr:nr + npar]]
        cv = [c[...].astype(F32) for c in refs[nr + npar:nr + npar + nc]]
        o_refs = refs[nr + npar + nc:]
        _, vjp = jax.vjp(fn, *rv, *pv)
        grads = vjp(tuple(cot_fn(*cv)))
        for o, k in zip(o_refs[:len(want)], want):
            o[...] = grads[k].astype(o.dtype)
        p_refs = o_refs[len(want):]

        @pl.when(i == 0)
        def _():
            for o in p_refs:
                o[...] = jnp.zeros_like(o)

        for o, g in zip(p_refs, grads[nr:]):
            o[...] += g

    in_specs = [pl.BlockSpec((tb, w), lambda i, c=c: (i, c)) for _, c, w in rows]
    in_specs += [pl.BlockSpec(p.shape, lambda i, n=p.ndim: (0,) * n) for p in pars]
    in_specs += [pl.BlockSpec((tb, w), lambda i, c=c: (i, c)) for _, c, w in cot_rows]
    out_specs = [pl.BlockSpec((tb, rows[k][2]), lambda i: (i, 0)) for k in want]
    out_specs += [pl.BlockSpec(p.shape, lambda i, n=p.ndim: (0,) * n) for p in pars]
    out_shape = [_sds((T, rows[k][2]), row_out[k]) for k in want] + [_sds(p.shape, F32) for p in pars]
    return pl.pallas_call(
        body, grid=(T // tb,), in_specs=in_specs, out_specs=out_specs, out_shape=out_shape,
        compiler_params=_cparams(("arbitrary",)), name=name,
    )(*[r[0] for r in rows], *pars, *[r[0] for r in cot_rows])


def _f_rms(x, g):
    r = lax.rsqrt(jnp.mean(x * x, axis=-1, keepdims=True) + EPS)
    return (x * r * g,)


def _f_lru_gates(u, wa, ba, wx, bx, lam):
    ub = u.astype(BF16)
    ra, rx = [], []
    for k in range(D_MODEL // LANES):
        uk = ub[:, LANES * k:LANES * (k + 1)]
        ra.append(jnp.dot(uk, wa[k].astype(BF16), preferred_element_type=F32))
        rx.append(jnp.dot(uk, wx[k].astype(BF16), preferred_element_type=F32))
    r = jax.nn.sigmoid(jnp.concatenate(ra, axis=1) + ba)
    i = jax.nn.sigmoid(jnp.concatenate(rx, axis=1) + bx)
    log_a = -LRU_C * r * jax.nn.softplus(-lam)
    a = jnp.exp(log_a)
    t = jnp.tanh(log_a)
    b = jnp.sqrt(-2.0 * t / (1.0 - t)) * (i * u)
    return a, b


def _f_post(hl, lgate, yssd, xpre_s, z, d_e, ng):
    ya = jax.nn.gelu(lgate) * hl
    y = (yssd + d_e * jax.nn.silu(xpre_s)) * jax.nn.silu(z)
    gw = SSD_INNER // SSD_GROUPS
    parts = []
    for g in range(SSD_GROUPS):
        yg = y[:, gw * g:gw * (g + 1)]
        parts.append(yg * lax.rsqrt(jnp.mean(yg * yg, axis=-1, keepdims=True) + EPS))
    return ya, jnp.concatenate(parts, axis=1) * ng


def _f_merge(m_a, m_b, gates, bg):
    g = jax.nn.sigmoid(gates + bg)
    return (g[:, :D_MODEL] * m_a + g[:, D_MODEL:] * m_b,)


def _f_act(gu):
    return (jax.nn.silu(gu[:, :D_FF]) * gu[:, D_FF:],)


def _ident(*c):
    return c


def _loss_head(h, target, nf):
    T = h.shape[0]
    tb = min(256, T)

    def body(h_ref, t_ref, nf_ref, dh_ref, dnf_ref, loss_ref, dhb_ref):
        i = pl.program_id(0)

        @pl.when(i == 0)
        def _():
            dnf_ref[...] = jnp.zeros_like(dnf_ref)
            loss_ref[...] = jnp.zeros_like(loss_ref)

        (y,), vjp = jax.vjp(_f_rms, h_ref[...], nf_ref[...])
        err = y - t_ref[...]
        dh, dnf = vjp((err * (1.0 / D_MODEL),))
        dh_ref[...] = dh
        dhb_ref[...] = dh.astype(BF16)
        dnf_ref[...] += dnf
        part = 0.5 * jnp.sum(jnp.mean(err * err, axis=-1, keepdims=True), axis=0, keepdims=True)
        loss_ref[...] += jnp.broadcast_to(part, loss_ref.shape)

    return pl.pallas_call(
        body, grid=(T // tb,),
        in_specs=[pl.BlockSpec((tb, D_MODEL), lambda i: (i, 0)), pl.BlockSpec((tb, D_MODEL), lambda i: (i, 0)),
                  pl.BlockSpec((1, D_MODEL), lambda i: (0, 0))],
        out_specs=[pl.BlockSpec((tb, D_MODEL), lambda i: (i, 0)), pl.BlockSpec((1, D_MODEL), lambda i: (0, 0)),
                   pl.BlockSpec((8, LANES), lambda i: (0, 0)), pl.BlockSpec((tb, D_MODEL), lambda i: (i, 0))],
        out_shape=[_sds((T, D_MODEL), F32), _sds((1, D_MODEL), F32), _sds((8, LANES), F32),
                   _sds((T, D_MODEL), BF16)],
        compiler_params=_cparams(("arbitrary",)), name="loss_head",
    )(h, target, nf)


CONV_TC = 1024
HALO = 16


def _conv_fwd(name, x, w8, b):
    T, C = x.shape
    tb = min(512, T)
    nb = tb // HALO

    def body(x_ref, xp_ref, w_ref, b_ref, y_ref, sc):
        i = pl.program_id(0)
        xv = x_ref[...].astype(F32)
        sc[pl.ds(HALO, tb), :] = xv
        sc[pl.ds(0, HALO), :] = jnp.where(i > 0, xp_ref[...].astype(F32), 0.0)
        acc = b_ref[...] + w_ref[3:4, :] * xv
        for k in range(3):
            acc = acc + w_ref[k:k + 1, :] * sc[pl.ds(HALO - 3 + k, tb), :]
        y_ref[...] = acc

    return pl.pallas_call(
        body, grid=(T // tb, C // CONV_TC),
        in_specs=[pl.BlockSpec((tb, CONV_TC), lambda i, j: (i, j)),
                  pl.BlockSpec((HALO, CONV_TC), lambda i, j: (jnp.maximum(i * nb - 1, 0), j)),
                  pl.BlockSpec((8, CONV_TC), lambda i, j: (0, j)),
                  pl.BlockSpec((1, CONV_TC), lambda i, j: (0, j))],
        out_specs=pl.BlockSpec((tb, CONV_TC), lambda i, j: (i, j)),
        out_shape=_sds((T, C), F32),
        scratch_shapes=[pltpu.VMEM((tb + HALO, CONV_TC), F32)],
        compiler_params=_cparams(("parallel", "parallel")), name=name,
    )(x, x, w8, b)


def _conv_bwd(name, x, dy, w8):
    T, C = x.shape
    tb = min(512, T)
    nb = tb // HALO
    nt = T // tb

    def body(x_ref, xp_ref, dy_ref, dyn_ref, w_ref, dx_ref, dwb_ref, scx, scd):
        i = pl.program_id(1)
        dyv = dy_ref[...]
        xv = x_ref[...].astype(F32)
        scx[pl.ds(HALO, tb), :] = xv
        scx[pl.ds(0, HALO), :] = jnp.where(i > 0, xp_ref[...].astype(F32), 0.0)
        scd[pl.ds(0, tb), :] = dyv
        scd[pl.ds(tb, HALO), :] = jnp.where(i < nt - 1, dyn_ref[...], 0.0)
        dx = w_ref[3:4, :] * dyv
        rows = []
        for k in range(3):
            dx = dx + w_ref[k:k + 1, :] * scd[pl.ds(3 - k, tb), :]
            rows.append(jnp.sum(dyv * scx[pl.ds(HALO - 3 + k, tb), :], axis=0, keepdims=True))
        rows.append(jnp.sum(dyv * xv, axis=0, keepdims=True))
        rows.append(jnp.sum(dyv, axis=0, keepdims=True))
        rows.append(jnp.zeros((3, CONV_TC), F32))
        dx_ref[...] = dx.astype(dx_ref.dtype)

        @pl.when(i == 0)
        def _():
            dwb_ref[...] = jnp.zeros_like(dwb_ref)

        dwb_ref[...] += jnp.concatenate(rows, axis=0)

    return pl.pallas_call(
        body, grid=(C // CONV_TC, nt),
        in_specs=[pl.BlockSpec((tb, CONV_TC), lambda j, i: (i, j)),
                  pl.BlockSpec((HALO, CONV_TC), lambda j, i: (jnp.maximum(i * nb - 1, 0), j)),
                  pl.BlockSpec((tb, CONV_TC), lambda j, i: (i, j)),
                  pl.BlockSpec((HALO, CONV_TC), lambda j, i: (jnp.minimum((i + 1) * nb, T // HALO - 1), j)),
                  pl.BlockSpec((8, CONV_TC), lambda j, i: (0, j))],
        out_specs=[pl.BlockSpec((tb, CONV_TC), lambda j, i: (i, j)),
                   pl.BlockSpec((8, CONV_TC), lambda j, i: (0, j))],
        out_shape=[_sds((T, C), BF16), _sds((8, C), F32)],
        scratch_shapes=[pltpu.VMEM((tb + HALO, CONV_TC), F32), pltpu.VMEM((tb + HALO, CONV_TC), F32)],
        compiler_params=_cparams(("parallel", "arbitrary")), name=name,
    )(x, x, dy, dy, w8)


SCAN_TB = 512


def _lru_scan_fwd(a, b):
    T, C = a.shape
    tb = min(SCAN_TB, T)

    def body(a_ref, b_ref, h_ref, hp_ref, carry):
        @pl.when(pl.program_id(0) == 0)
        def _():
            carry[...] = jnp.zeros_like(carry)

        def group(gi, h):
            r0 = pl.multiple_of(gi * 8, 8)
            at = a_ref[pl.ds(r0, 8), :]
            bt = b_ref[pl.ds(r0, 8), :]
            hs, hps = [], []
            for r in range(8):
                hps.append(h)
                h = at[r:r + 1, :] * h + bt[r:r + 1, :]
                hs.append(h)
            h_ref[pl.ds(r0, 8), :] = jnp.concatenate(hs, axis=0)
            hp_ref[pl.ds(r0, 8), :] = jnp.concatenate(hps, axis=0)
            return h

        carry[0:1, :] = lax.fori_loop(0, tb // 8, group, carry[0:1, :])

    spec = pl.BlockSpec((tb, C), lambda i: (i, 0))
    return pl.pallas_call(
        body, grid=(T // tb,), in_specs=[spec, spec], out_specs=[spec, spec],
        out_shape=[_sds((T, C), F32), _sds((T, C), F32)],
        scratch_shapes=[pltpu.VMEM((8, C), F32)],
        compiler_params=_cparams(("arbitrary",)), name="lru_scan_fwd",
    )(a, b)


def _lru_scan_bwd(a, dh):
    T, C = a.shape
    tb = min(SCAN_TB, T)
    nt = T // tb

    def body(a_ref, dh_ref, g_ref, carry):
        @pl.when(pl.program_id(0) == 0)
        def _():
            carry[...] = jnp.zeros_like(carry)

        def group(gi, c):
            r0 = pl.multiple_of((tb // 8 - 1 - gi) * 8, 8)
            at = a_ref[pl.ds(r0, 8), :]
            dt = dh_ref[pl.ds(r0, 8), :]
            gs = [None] * 8
            for r in range(7, -1, -1):
                g = dt[r:r + 1, :] + c
                c = at[r:r + 1, :] * g
                gs[r] = g
            g_ref[pl.ds(r0, 8), :] = jnp.concatenate(gs, axis=0)
            return c

        carry[0:1, :] = lax.fori_loop(0, tb // 8, group, carry[0:1, :])

    spec = pl.BlockSpec((tb, C), lambda i: (nt - 1 - i, 0))
    return pl.pallas_call(
        body, grid=(nt,), in_specs=[spec, spec], out_specs=spec,
        out_shape=_sds((T, C), F32),
        scratch_shapes=[pltpu.VMEM((8, C), F32)],
        compiler_params=_cparams(("arbitrary",)), name="lru_scan_bwd",
    )(a, dh)


def _ssd_chunk(xpre, dtraw, state, dtb, alog):
    xc = jax.nn.silu(xpre)
    xs = xc[:, :SSD_INNER]
    bm = xc[:, SSD_INNER:SSD_INNER + SSD_GROUPS * SSD_STATE]
    cm = xc[:, SSD_INNER + SSD_GROUPS * SSD_STATE:]
    dt = jax.nn.softplus(dtraw + dtb)
    a = dt * (-jnp.exp(alog))
    ltri = (lax.broadcasted_iota(jnp.int32, (CHUNK, CHUNK), 0)
            >= lax.broadcasted_iota(jnp.int32, (CHUNK, CHUNK), 1)).astype(F32)
    a_cs = jnp.dot(ltri, a, precision=HIGHEST, preferred_element_type=F32)
    npair = SSD_HEADS // 2
    pi = lax.broadcasted_iota(jnp.int32, (npair, LANES), 0)
    hi = lax.broadcasted_iota(jnp.int32, (npair, LANES), 1)
    sel_even = (hi == 2 * pi).astype(F32)
    sel_odd = (hi == 2 * pi + 1).astype(F32)
    top = lax.broadcasted_iota(jnp.int32, (2 * CHUNK, LANES), 0) < CHUNK

    def pair_transpose(v):
        v2 = jnp.concatenate([v, v], axis=0)
        dn = (((1,), (1,)), ((), ()))
        return (lax.dot_general(sel_even, jnp.where(top, v2, 0.0), dn, precision=HIGHEST, preferred_element_type=F32)
                + lax.dot_general(sel_odd, jnp.where(top, 0.0, v2), dn, precision=HIGHEST, preferred_element_type=F32))

    a_t2 = pair_transpose(a_cs)
    dt_t2 = pair_transpose(dt)
    a_last = a_cs[CHUNK - 1:CHUNK, :]
    dte = jnp.exp(a_last - a_cs) * dt
    cd = jnp.exp(a_last)
    lane = lax.broadcasted_iota(jnp.int32, (CHUNK, LANES), 1)
    left = lane < SSD_HEAD_DIM
    right = jnp.logical_not(left)
    left1 = left[0:1]
    tril2 = lax.broadcasted_iota(jnp.int32, (CHUNK, LANES), 0) >= (lane & (SSD_HEAD_DIM - 1))
    gw = SSD_INNER // SSD_GROUPS
    ys, news = [], []
    for g in range(SSD_GROUPS):
        bg = bm[:, SSD_STATE * g:SSD_STATE * (g + 1)].astype(BF16)
        cg = cm[:, SSD_STATE * g:SSD_STATE * (g + 1)].astype(BF16)
        bg2 = jnp.concatenate([bg, bg], axis=0)
        scores2 = lax.dot_general(cg, bg2, (((1,), (1,)), ((), ())), preferred_element_type=F32)
        coff = jnp.dot(cg, state[:, gw * g:gw * (g + 1)].astype(BF16), preferred_element_type=F32)
        for j in range(gw // LANES):
            lo = gw * g + LANES * j
            p = lo // LANES
            h0 = 2 * p
            xp = xs[:, lo:lo + LANES]
            col = jnp.where(left, a_cs[:, h0:h0 + 1], a_cs[:, h0 + 1:h0 + 2])
            dm = jnp.exp(jnp.where(tril2, col - a_t2[p:p + 1, :], -1e30)) * dt_t2[p:p + 1, :]
            sd = (scores2 * dm).astype(BF16)
            x_bd = jnp.concatenate([jnp.where(left, xp, 0.0), jnp.where(right, xp, 0.0)], axis=0).astype(BF16)
            acc = jnp.dot(sd, x_bd, preferred_element_type=F32)
            ys.append(acc + coff[:, LANES * j:LANES * (j + 1)] * jnp.exp(col))
            dtee = jnp.where(left, dte[:, h0:h0 + 1], dte[:, h0 + 1:h0 + 2])
            xw = (xp * dtee).astype(BF16)
            cde = jnp.where(left1, cd[:, h0:h0 + 1], cd[:, h0 + 1:h0 + 2])
            news.append(state[:, lo:lo + LANES] * cde
                        + lax.dot_general(bg, xw, (((0,), (0,)), ((), ())), preferred_element_type=F32))
    return jnp.concatenate(ys, axis=1), jnp.concatenate(news, axis=1)


def _ssd_fwd(name, xpre, dtraw, dtb, alog, rider=None):
    T = xpre.shape[0]
    n = T // CHUNK
    r_args, r_in, r_out, r_shape, r_scratch = _rider_parts(rider)

    def body(xp, dr, dtb_ref, al_ref, y_ref, st_ref, state):
        @pl.when(pl.program_id(0) == 0)
        def _():
            state[...] = jnp.zeros_like(state)

        st_ref[0] = state[...]
        y, new = _ssd_chunk(xp[...], dr[...], state[...], dtb_ref[...], al_ref[...])
        y_ref[...] = y
        state[...] = new

    small = pl.BlockSpec((1, LANES), lambda c: (0, 0))
    return pl.pallas_call(
        _with_rider(body, 4, 2, rider, (n,)), grid=(n,),
        in_specs=[pl.BlockSpec((CHUNK, SSD_CONV_DIM), lambda c: (c, 0)),
                  pl.BlockSpec((CHUNK, DT_PAD), lambda c: (c, 0)), small, small] + r_in,
        out_specs=[pl.BlockSpec((CHUNK, SSD_INNER), lambda c: (c, 0)),
                   pl.BlockSpec((1, SSD_STATE, SSD_INNER), lambda c: (c, 0, 0))] + r_out,
        out_shape=[_sds((T, SSD_INNER), F32), _sds((n, SSD_STATE, SSD_INNER), F32)] + r_shape,
        scratch_shapes=[pltpu.VMEM((SSD_STATE, SSD_INNER), F32)] + r_scratch,
        compiler_params=_cparams(("arbitrary",)), name=name,
    )(xpre, dtraw, dtb, alog, *r_args)


def _ssd_bwd(name, xpre, dtraw, states, dy, dxs_extra, dtb, alog, rider=None):
    T = xpre.shape[0]
    n = T // CHUNK
    r_args, r_in, r_out, r_shape, r_scratch = _rider_parts(rider)

    def body(xp, dr, st, dy_ref, dx_ref, dtb_ref, al_ref, dxp_ref, ddr_ref, ddtb_ref, dal_ref, dstate):
        @pl.when(pl.program_id(0) == 0)
        def _():
            dstate[...] = jnp.zeros_like(dstate)
            ddtb_ref[...] = jnp.zeros_like(ddtb_ref)
            dal_ref[...] = jnp.zeros_like(dal_ref)

        _, vjp = jax.vjp(_ssd_chunk, xp[...], dr[...], st[0], dtb_ref[...], al_ref[...])
        dxp, ddr, ds, db, da = vjp((dy_ref[...], dstate[...]))
        dxp_ref[:, :SSD_INNER] = dxp[:, :SSD_INNER] + dx_ref[...]
        dxp_ref[:, SSD_INNER:] = dxp[:, SSD_INNER:]
        ddr_ref[...] = ddr.astype(ddr_ref.dtype)
        dstate[...] = ds
        ddtb_ref[...] += db
        dal_ref[...] += da

    def rev(c):
        return (n - 1 - c, 0)

    small = pl.BlockSpec((1, LANES), lambda c: (0, 0))
    return pl.pallas_call(
        _with_rider(body, 7, 4, rider, (n,)), grid=(n,),
        in_specs=[pl.BlockSpec((CHUNK, SSD_CONV_DIM), rev), pl.BlockSpec((CHUNK, DT_PAD), rev),
                  pl.BlockSpec((1, SSD_STATE, SSD_INNER), lambda c: (n - 1 - c, 0, 0)),
                  pl.BlockSpec((CHUNK, SSD_INNER), rev), pl.BlockSpec((CHUNK, SSD_INNER), rev), small, small] + r_in,
        out_specs=[pl.BlockSpec((CHUNK, SSD_CONV_DIM), rev), pl.BlockSpec((CHUNK, DT_PAD), rev), small, small] + r_out,
        out_shape=[_sds((T, SSD_CONV_DIM), F32), _sds((T, DT_PAD), BF16), _sds((1, LANES), F32),
                   _sds((1, LANES), F32)] + r_shape,
        scratch_shapes=[pltpu.VMEM((SSD_STATE, SSD_INNER), F32)] + r_scratch,
        compiler_params=_cparams(("arbitrary",)), name=name,
    )(xpre, dtraw, states, dy, dxs_extra, dtb, alog, *r_args)


HBM_SPEC = pl.BlockSpec(memory_space=pltpu.HBM)
N_PEER = N_DEV - 1


def _position():
    return lax.axis_index("x"), lax.axis_index("y"), lax.axis_index("c")


def _all_gather(name, blks):
    return _exchange_call(name, _gather_phases, blks, [_sds((N_DEV,) + b.shape, b.dtype) for b in blks])


def _scatter_exchange(name, gs):
    return _exchange_call(name, _scatter_phases, gs, [_sds(g.shape, g.dtype) for g in gs])


def _exchange_scratch(na):
    return [pltpu.SemaphoreType.DMA((na * N_PEER,)), pltpu.SemaphoreType.DMA((na * N_PEER,)),
            pltpu.SemaphoreType.DMA((na,))]


def _exchange_call(name, phases, arrays, out_shape):
    na = len(arrays)

    def body(*refs):
        begin, finish = phases(refs[:na], refs[na:2 * na], *refs[2 * na:])
        begin()
        finish()

    return pl.pallas_call(
        body, out_shape=out_shape, in_specs=[HBM_SPEC] * na, out_specs=[HBM_SPEC] * na,
        scratch_shapes=_exchange_scratch(na), name=name,
    )(*arrays)


def _gather_phases(x_refs, out_refs, send_sems, recv_sems, local_sems):
    na = len(x_refs)
    x, y, c = _position()
    me, sibling = (x, y, c), (x, y, 1 - c)
    chips = [(1 - x, y), (x, 1 - y), (1 - x, 1 - y)]

    def slot(a, px, py, pc):
        return out_refs[a].at[4 * px + 2 * py + pc]

    def copy(a, k, block, to, src=None):
        return pltpu.make_async_remote_copy(
            src_ref=slot(a, *block) if src is None else src, dst_ref=slot(a, *block),
            send_sem=send_sems.at[a * N_PEER + k], recv_sem=recv_sems.at[a * N_PEER + k],
            device_id=to, device_id_type=MESH)

    mine = [pltpu.make_async_copy(x_refs[a], slot(a, *me), local_sems.at[a]) for a in range(na)]
    first = []
    for a in range(na):
        first.append(copy(a, 0, me, sibling, src=x_refs[a]))
        first += [copy(a, 1 + j, me, (*chip, c), src=x_refs[a]) for j, chip in enumerate(chips)]
    passed = [copy(a, 4 + j, (*chip, c), sibling) for j, chip in enumerate(chips) for a in range(na)]

    def begin():
        for cp in mine + first:
            cp.start()

    def finish():
        for j, chip in enumerate(chips):
            for a in range(na):
                copy(a, 1 + j, (*chip, c), me).wait_recv()
                passed[j * na + a].start()
        for a in range(na):
            copy(a, 0, sibling, me).wait_recv()
            for j, chip in enumerate(chips):
                copy(a, 4 + j, (*chip, 1 - c), me).wait_recv()
        for cp in first + passed:
            cp.wait_send()
        for cp in mine:
            cp.wait()

    return begin, finish


def _scatter_phases(g_refs, q_refs, send_sems, recv_sems, local_sems):
    na = len(g_refs)
    x, y, c = _position()
    me = 4 * x + 2 * y + c
    mine, sends, recvs = [], [], []
    for a in range(na):
        mine.append(pltpu.make_async_copy(g_refs[a].at[me], q_refs[a].at[me], local_sems.at[a]))
        for k in range(1, N_DEV):
            px, py, pc = (x + (k >> 2)) % 2, (y + ((k >> 1) & 1)) % 2, (c + (k & 1)) % 2
            peer = 4 * px + 2 * py + pc
            sem = a * N_PEER + k - 1
            sends.append(pltpu.make_async_remote_copy(
                src_ref=g_refs[a].at[peer], dst_ref=q_refs[a].at[me], send_sem=send_sems.at[sem],
                recv_sem=recv_sems.at[sem], device_id=(px, py, pc), device_id_type=MESH))
            recvs.append(pltpu.make_async_remote_copy(
                src_ref=g_refs[a].at[me], dst_ref=q_refs[a].at[peer], send_sem=send_sems.at[sem],
                recv_sem=recv_sems.at[sem], device_id=(px, py, pc), device_id_type=MESH))

    def begin():
        for cp in mine + sends:
            cp.start()

    def finish():
        for cp in recvs:
            cp.wait_recv()
        for cp in sends:
            cp.wait_send()
        for cp in mine:
            cp.wait()

    return begin, finish


def _with_rider(body, n_in, n_out, rider, grid):
    if rider is None:
        return body
    phases, arrays, _ = rider
    na = len(arrays)

    def carried(*refs):
        ins, r_in = refs[:n_in], refs[n_in:n_in + na]
        outs = refs[n_in + na:n_in + na + n_out]
        r_out = refs[n_in + na + n_out:n_in + 2 * na + n_out]
        scratch, r_scratch = refs[n_in + 2 * na + n_out:-3], refs[-3:]
        begin, finish = phases(r_in, r_out, *r_scratch)
        first = last = None
        for d, g in enumerate(grid):
            at_start, at_end = pl.program_id(d) == 0, pl.program_id(d) == g - 1
            first = at_start if first is None else first & at_start
            last = at_end if last is None else last & at_end
        pl.when(first)(begin)
        body(*ins, *outs, *scratch)
        pl.when(last)(finish)

    return carried


def _rider_parts(rider):
    if rider is None:
        return [], [], [], [], []
    _, arrays, out_shape = rider
    na = len(arrays)
    return list(arrays), [HBM_SPEC] * na, [HBM_SPEC] * na, list(out_shape), _exchange_scratch(na)


def _gather_rider(blks):
    return (_gather_phases, blks, [_sds((N_DEV,) + b.shape, b.dtype) for b in blks])


def _scatter_rider(gs):
    return (_scatter_phases, gs, [_sds(g.shape, g.dtype) for g in gs])


def _adamw_sum(name, w, m, v, recvs):
    L, r, c = w.shape
    tr = r if r <= 512 else 256
    assert r % tr == 0 and len(recvs) == L
    nr = r // tr

    def body(w_ref, m_ref, v_ref, *refs):
        q_refs = refs[:L]
        g_ref, d_ref, m2_ref, v2_ref = refs[L:]
        l = pl.program_id(0)
        for ll in range(L):
            @pl.when(l == ll)
            def _(q_ref=q_refs[ll]):
                gv = q_ref[0].astype(F32)
                for s in range(1, N_DEV):
                    gv = gv + q_ref[s].astype(F32)
                m2 = ADAM_B1 * m_ref[0] + (1.0 - ADAM_B1) * gv
                v2 = ADAM_B2 * v_ref[0] + (1.0 - ADAM_B2) * jnp.square(gv)
                m_hat = m2 / (1.0 - ADAM_B1 ** ADAM_STEP)
                v_hat = v2 / (1.0 - ADAM_B2 ** ADAM_STEP)
                g_ref[0] = gv
                d_ref[0] = -ADAM_LR * (m_hat / (jnp.sqrt(v_hat) + ADAM_EPS) + ADAM_WD * w_ref[0])
                m2_ref[0] = m2
                v2_ref[0] = v2

    def q_index(ll):
        return lambda l, i: (0, jnp.where(l == ll, i, jnp.where(l > ll, nr - 1, 0)), 0)

    spec = pl.BlockSpec((1, tr, c), lambda l, i: (l, i, 0))
    return pl.pallas_call(
        body, grid=(L, nr),
        in_specs=[spec] * 3 + [pl.BlockSpec((N_DEV, tr, c), q_index(ll)) for ll in range(L)],
        out_specs=[spec] * 4, out_shape=[_sds((L, r, c), F32)] * 4,
        compiler_params=_cparams(("arbitrary", "arbitrary")), name=name,
    )(w, m, v, *recvs)


def _sum_blocks(name, q, tr):
    R = q.shape[0] // N_DEV
    W = q.shape[1]
    tr = min(tr, R)
    assert R % tr == 0
    nb = R // tr

    def body(*refs):
        acc = refs[0][...].astype(F32)
        for r in refs[1:N_DEV]:
            acc = acc + r[...].astype(F32)
        refs[N_DEV][...] = acc

    return pl.pallas_call(
        body, grid=(nb,),
        in_specs=[pl.BlockSpec((tr, W), lambda i, s=s: (s * nb + i, 0)) for s in range(N_DEV)],
        out_specs=pl.BlockSpec((tr, W), lambda i: (i, 0)), out_shape=_sds((R, W), F32),
        compiler_params=_cparams(("parallel",)), name=name,
    )(*([q] * N_DEV))


def _adamw(name, w, g, m, v):
    R, C = w.shape
    tr = R
    if R > 512:
        tr = max(t for t in range(8, 513, 8) if R % t == 0)

    def body(w_ref, g_ref, m_ref, v_ref, d_ref, m2_ref, v2_ref):
        gv = g_ref[...]
        m2 = ADAM_B1 * m_ref[...] + (1.0 - ADAM_B1) * gv
        v2 = ADAM_B2 * v_ref[...] + (1.0 - ADAM_B2) * jnp.square(gv)
        m_hat = m2 / (1.0 - ADAM_B1 ** ADAM_STEP)
        v_hat = v2 / (1.0 - ADAM_B2 ** ADAM_STEP)
        d_ref[...] = -ADAM_LR * (m_hat / (jnp.sqrt(v_hat) + ADAM_EPS) + ADAM_WD * w_ref[...])
        m2_ref[...] = m2
        v2_ref[...] = v2

    spec = pl.BlockSpec((tr, C), lambda i: (i, 0))
    return pl.pallas_call(
        body, grid=(R // tr,), in_specs=[spec] * 4, out_specs=[spec] * 3,
        out_shape=[_sds((R, C), F32)] * 3, compiler_params=_cparams(("parallel",)), name=name,
    )(w, g, m, v)


BIG = ("w_in", "w_branch", "w_out", "w_ffn_in", "w_ffn_out")
BIG_ROW_SHARDED = {"w_in": False, "w_branch": True, "w_out": True, "w_ffn_in": False, "w_ffn_out": True}
SMALL = ("norm1_g", "b_gate", "lru_conv_w", "lru_conv_b", "lru_w_a", "lru_b_a", "lru_w_x", "lru_b_x", "lru_lambda",
         "ssd_conv_w", "ssd_conv_b", "ssd_dt_bias", "ssd_A_log", "ssd_D", "ssd_norm_g", "norm2_g", "norm_f")
SMALL_ROWS = 256


def _pack(arrs, dtype, row_mult):
    parts = []
    for a in arrs:
        f = a.reshape(-1).astype(dtype)
        pad = (-f.shape[0]) % LANES
        if pad:
            f = jnp.concatenate([f, jnp.zeros((pad,), dtype)])
        parts.append(f)
    f = jnp.concatenate(parts)
    pad = (-f.shape[0]) % (LANES * row_mult)
    if pad:
        f = jnp.concatenate([f, jnp.zeros((pad,), dtype)])
    return f.reshape(-1, LANES)


def _unpack(flat, shapes, lead=()):
    f = flat.reshape(lead + (-1,))
    out, off = [], 0
    for s in shapes:
        n = int(np.prod(s))
        out.append(f[..., off:off + n].reshape(lead + tuple(s)))
        off += n + (-n) % LANES
    return out


def _full_from_shards(name, st):
    if BIG_ROW_SHARDED[name]:
        return st.reshape((-1,) + st.shape[2:])
    return jnp.transpose(st, (1, 0, 2)).reshape(st.shape[1], -1)


def _shards_from_full(name, full):
    if BIG_ROW_SHARDED[name]:
        return full.reshape((N_DEV, full.shape[0] // N_DEV) + full.shape[1:])
    return jnp.transpose(full.reshape(full.shape[0], N_DEV, -1), (1, 0, 2))


def _block_diag_tiles(w):
    z = jnp.zeros((8, 64, 64), w.dtype)
    w2 = w.reshape(8, 2, 64, 64)
    top = jnp.concatenate([w2[:, 0], z], axis=2)
    bot = jnp.concatenate([z, w2[:, 1]], axis=2)
    return jnp.concatenate([top, bot], axis=1)


def _block_diag_untile(t):
    return jnp.stack([t[:, :64, :64], t[:, 64:, 64:]], axis=1).reshape(16, 64, 64)


def _pad_lanes(a, width):
    return jnp.concatenate([a, jnp.zeros(a.shape[:-1] + (width - a.shape[-1],), a.dtype)], axis=-1)


def _pad_rows8(w):
    return jnp.concatenate([w, jnp.zeros((8 - w.shape[0],) + w.shape[1:], w.dtype)], axis=0)


REST = BIG[1:]


def _in_weights(shards):
    w_in = _full_from_shards("w_in", shards)
    seg = [w_in[:, IN_OFFS[k]:IN_OFFS[k + 1]] for k in range(6)]
    return {"w_lx": seg[0], "w_lg": seg[1], "w_z": seg[2], "w_xbc": seg[3], "w_dt": _pad_lanes(seg[4], DT_PAD),
            "w_g": seg[5]}


def _rest_weights(shards):
    full = {n: _full_from_shards(n, st) for n, st in zip(REST, shards)}
    return {"w_bra": full["w_branch"][:D_MODEL], "w_brb": full["w_branch"][D_MODEL:], "w_out": full["w_out"],
            "w_ffn_in": full["w_ffn_in"], "w_ffn_out": full["w_ffn_out"]}


def _small_params(l, small):
    p = {}
    for n in ("norm1_g", "b_gate", "lru_conv_b", "lru_b_a", "lru_b_x", "lru_lambda", "ssd_conv_b", "ssd_norm_g", "norm2_g"):
        p[n] = small[n][l].reshape(1, -1)
    p["lru_conv_w8"] = _pad_rows8(small["lru_conv_w"][l])
    p["ssd_conv_w8"] = _pad_rows8(small["ssd_conv_w"][l])
    p["wa"] = _block_diag_tiles(small["lru_w_a"][l])
    p["wx"] = _block_diag_tiles(small["lru_w_x"][l])
    p["dtb"] = _pad_lanes(small["ssd_dt_bias"][l].reshape(1, -1), DT_PAD)
    p["alog"] = _pad_lanes(small["ssd_A_log"][l].reshape(1, -1), DT_PAD)
    p["d_e"] = jnp.repeat(small["ssd_D"][l], SSD_HEAD_DIM).reshape(1, -1)
    return p


def _layer_fwd(l, h, p, riders):
    n = f"l{l}_"
    s = {"h_in": h}
    got = {}
    (xn,) = _rowwise_fwd(n + "rms1", _f_rms, [h], [p["norm1_g"]], [(D_MODEL, BF16)], 256)
    s["xn"] = xn
    for k in ("lx", "lg", "z", "xbc", "g", "dt"):
        s[k] = _mm(n + "in_" + k, [(xn, p["w_" + k])], "nn", out_dtype=F32 if k == "dt" else BF16)
    s["u"] = _conv_fwd(n + "lru_conv", s["lx"], p["lru_conv_w8"], p["lru_conv_b"])
    lru_pars = [p["wa"], p["lru_b_a"], p["wx"], p["lru_b_x"], p["lru_lambda"]]
    s["a"], b = _rowwise_fwd(n + "lru_gates", _f_lru_gates, [s["u"]], lru_pars, [(D_MODEL, F32)] * 2, 256)
    s["hl"], s["hprev"] = _lru_scan_fwd(s["a"], b)
    s["xpre"] = _conv_fwd(n + "ssd_conv", s["xbc"], p["ssd_conv_w8"], p["ssd_conv_b"])
    s["yssd"], s["states"], *got["ssd"] = _ssd_fwd(n + "ssd", s["xpre"], s["dt"], p["dtb"], p["alog"],
                                                   riders.get("ssd"))
    if "ssd" in riders:
        p.update(_rest_weights(got["ssd"]))
    post_rows = [s["hl"], s["lg"], s["yssd"], (s["xpre"], 0, SSD_INNER), s["z"]]
    s["ya"], s["yb"], *got["post"] = _rowwise_fwd(n + "post", _f_post, post_rows, [p["d_e"], p["ssd_norm_g"]],
                                                  [(D_MODEL, BF16), (SSD_INNER, BF16)], 128, riders.get("post"))
    s["ma"] = _mm(n + "br_a", [(s["ya"], p["w_bra"])], "nn", out_dtype=BF16)
    s["mb"] = _mm(n + "br_b", [(s["yb"], p["w_brb"])], "nn", out_dtype=BF16)
    (s["merged"],) = _rowwise_fwd(n + "merge", _f_merge, [s["ma"], s["mb"], s["g"]], [p["b_gate"]],
                                  [(D_MODEL, BF16)], 256)
    s["h_mid"] = _mm(n + "out", [(s["merged"], p["w_out"])], "nn", res=h)
    (s["xn2"],) = _rowwise_fwd(n + "rms2", _f_rms, [s["h_mid"]], [p["norm2_g"]], [(D_MODEL, BF16)], 256)
    s["gu"] = _mm(n + "ffn_in", [(s["xn2"], p["w_ffn_in"])], "nn", out_dtype=BF16, rider=riders.get("ffn_in"))
    if "ffn_in" in riders:
        s["gu"], got["ffn_in"] = s["gu"]
    (s["act"],) = _rowwise_fwd(n + "act", _f_act, [s["gu"]], [], [(D_FF, BF16)], 256)
    h_out = _mm(n + "ffn_out", [(s["act"], p["w_ffn_out"])], "nn", res=s["h_mid"], rider=riders.get("ffn_out"))
    if "ffn_out" in riders:
        h_out, got["ffn_out"] = h_out
    return h_out, s, got


def _add(name, a, b):
    w = a.shape[1]
    return _rowwise_fwd(name, lambda x, y: (x + y, x + y), [a, b], [], [(w, F32), (w, BF16)], 512)


def _layer_bwd(l, dh, dh_b, s, p, above, first_layer):
    n = f"l{l}_b_"
    gw, gs = {}, {}
    d_act = _mm(n + "d_act", [(dh_b, p["w_ffn_out"])], "nt")
    gw["w_ffn_out"] = _mm_tn(n + "dw_ffn_out", s["act"], dh_b)
    (d_gu,) = _rowwise_bwd(n + "act", _f_act, [s["gu"]], [], [d_act], _ident, [BF16], 128)
    d_xn2 = _mm(n + "d_xn2", [(d_gu, p["w_ffn_in"])], "nt")
    gw["w_ffn_in"] = _mm_tn(n + "dw_ffn_in", s["xn2"], d_gu)
    d_rms2, gs["norm2_g"] = _rowwise_bwd(n + "rms2", _f_rms, [s["h_mid"]], [p["norm2_g"]], [d_xn2], _ident, [F32], 256)
    dh_mid, dh_mid_b = _add(n + "add2", dh, d_rms2)
    d_merged = _mm(n + "d_merged", [(dh_mid_b, p["w_out"])], "nt")
    gw["w_out"] = _mm_tn(n + "dw_out", s["merged"], dh_mid_b)
    d_ma, d_mb, d_g, gs["b_gate"] = _rowwise_bwd(n + "merge", _f_merge, [s["ma"], s["mb"], s["g"]], [p["b_gate"]],
                                                 [d_merged], _ident, [BF16, BF16, BF16], 256)
    d_ya = _mm(n + "d_ya", [(d_ma, p["w_bra"])], "nt")
    d_yb = _mm(n + "d_yb", [(d_mb, p["w_brb"])], "nt")
    gw["w_branch"] = jnp.concatenate([_mm_tn(n + "dw_bra", s["ya"], d_ma), _mm_tn(n + "dw_brb", s["yb"], d_mb)], axis=0)
    post_rows = [s["hl"], s["lg"], s["yssd"], (s["xpre"], 0, SSD_INNER), s["z"]]
    d_hl, d_lg, d_yssd, d_xs, d_z, d_de, gs["ssd_norm_g"] = _rowwise_bwd(
        n + "post", _f_post, post_rows, [p["d_e"], p["ssd_norm_g"]], [d_ya, d_yb], _ident,
        [F32, BF16, F32, F32, BF16], 64)
    gs["ssd_D"] = d_de.reshape(SSD_HEADS, SSD_HEAD_DIM).sum(axis=1)
    contrib = [_shards_from_full(k, gw[k]) for k in REST] + ([] if above is None else [above])
    d_xpre, d_dt, d_dtb, d_alog, *carried = _ssd_bwd(n + "ssd", s["xpre"], s["dt"], s["states"], d_yssd, d_xs,
                                                     p["dtb"], p["alog"], _scatter_rider(contrib))
    arrived = dict(zip(REST, carried))
    arrived_above = None if above is None else carried[len(REST)]
    gs["ssd_dt_bias"] = d_dtb[0, :SSD_HEADS]
    gs["ssd_A_log"] = d_alog[0, :SSD_HEADS]
    d_xbc, dwb = _conv_bwd(n + "ssd_conv", s["xbc"], d_xpre, p["ssd_conv_w8"])
    gs["ssd_conv_w"], gs["ssd_conv_b"] = dwb[:4], dwb[4]
    g_scan = _lru_scan_bwd(s["a"], d_hl)
    lru_pars = [p["wa"], p["lru_b_a"], p["wx"], p["lru_b_x"], p["lru_lambda"]]
    d_u, d_wa, gs["lru_b_a"], d_wx, gs["lru_b_x"], gs["lru_lambda"] = _rowwise_bwd(
        n + "lru_gates", _f_lru_gates, [s["u"]], lru_pars, [g_scan, s["hprev"]],
        lambda g, hp: (g * hp, g), [F32], 128)
    gs["lru_w_a"], gs["lru_w_x"] = _block_diag_untile(d_wa), _block_diag_untile(d_wx)
    d_lx, dwb = _conv_bwd(n + "lru_conv", s["lx"], d_u, p["lru_conv_w8"])
    gs["lru_conv_w"], gs["lru_conv_b"] = dwb[:4], dwb[4]
    segs = [("lx", d_lx), ("lg", d_lg), ("z", d_z), ("xbc", d_xbc), ("dt", d_dt), ("g", d_g)]
    dws = [_mm_tn(n + "dw_in_" + k, s["xn"], d) for k, d in segs]
    dws[4] = dws[4][:, :IN_WIDTHS[4]]
    below = _shards_from_full("w_in", jnp.concatenate(dws, axis=1))
    d_xn = _mm(n + "d_xn_a", [(d, p["w_" + k]) for k, d in segs[:3]], "nt")
    pairs_b = [(d, p["w_" + k]) for k, d in segs[3:]]
    if first_layer:
        d_xn, (arrived["w_in"],) = _mm(n + "d_xn_b", pairs_b, "nt", res=d_xn, rider=_scatter_rider([below]))
        below = None
    else:
        d_xn = _mm(n + "d_xn_b", pairs_b, "nt", res=d_xn)
    d_rms1, gs["norm1_g"] = _rowwise_bwd(n + "rms1", _f_rms, [s["h_in"]], [p["norm1_g"]], [d_xn], _ident, [F32], 256)
    dh_in, dh_in_b = _add(n + "add1", dh_mid, d_rms1)
    for k in ("norm1_g", "norm2_g", "b_gate", "ssd_norm_g", "lru_b_a", "lru_b_x", "lru_lambda"):
        gs[k] = gs[k].reshape(-1)
    return dh_in, dh_in_b, gs, arrived, arrived_above, below, gw


def _step(inp):
    x = inp["x"][0]
    target = inp["loss_target"][0]
    dev = 4 * lax.axis_index("x") + 2 * lax.axis_index("y") + lax.axis_index("c")

    def mine(l, names):
        return [inp[n][l].astype(BF16) for n in names]

    first = _all_gather("gather_first", mine(0, ("w_in",)) + [inp["lru_conv_w"], inp["ssd_conv_w"]])
    small = {n: inp[n] for n in SMALL}
    small["lru_conv_w"] = jnp.moveaxis(first[1], 0, 2).reshape(DEPTH, 4, -1)
    small["ssd_conv_w"] = jnp.moveaxis(first[2], 0, 2).reshape(DEPTH, 4, -1)
    shards = {"w_in": first[0]}

    h, saved, params = x, [], []
    for l in range(DEPTH):
        p = _small_params(l, small)
        p.update(_in_weights(shards["w_in"]))
        riders = {}
        if l == 0:
            riders["ssd"] = _gather_rider(mine(0, REST))
        else:
            p.update(_rest_weights([shards[n] for n in REST]))
        if l + 1 < DEPTH:
            riders["post"] = _gather_rider(mine(l + 1, ("w_ffn_in", "w_ffn_out")))
            riders["ffn_in"] = _gather_rider(mine(l + 1, ("w_in",)))
            riders["ffn_out"] = _gather_rider(mine(l + 1, ("w_branch", "w_out")))
        h, s, got = _layer_fwd(l, h, p, riders)
        if l + 1 < DEPTH:
            shards = {"w_ffn_in": got["post"][0], "w_ffn_out": got["post"][1], "w_in": got["ffn_in"][0],
                      "w_branch": got["ffn_out"][0], "w_out": got["ffn_out"][1]}
        saved.append(s)
        params.append(p)
    dh, d_nf, loss_acc, dh_b = _loss_head(h, target, small["norm_f"].reshape(1, -1))
    loss = lax.psum(loss_acc[0, 0], ("x", "y", "c"))

    gss, received = [None] * DEPTH, [None] * DEPTH
    handed_down = None
    for l in reversed(range(DEPTH)):
        dh, dh_b, gss[l], received[l], arrived_above, handed_down, _ = _layer_bwd(
            l, dh, dh_b, saved[l], params[l], handed_down, l == 0)
        if arrived_above is not None:
            received[l + 1]["w_in"] = arrived_above
    grad_x = dh[None]
    out = {"loss": loss, "grad_x": grad_x}
    for n in BIG:
        out["grad_" + n], out["delta_" + n], out["new_m_" + n], out["new_v_" + n] = _adamw_sum(
            "adamw_" + n, inp[n], inp["m_" + n], inp["v_" + n], [received[l][n] for l in range(DEPTH)])

    small_full = {n: jnp.stack([gss[l][n] for l in range(DEPTH)]) for n in SMALL if n != "norm_f"}
    small_full["norm_f"] = d_nf.reshape(-1)
    part = _pack([small_full[n] for n in SMALL], F32, SMALL_ROWS)
    (everyone,) = _all_gather("gather_small_grads", [part])
    g_small_flat = _sum_blocks("sum_small_grads", everyone.reshape(-1, LANES), SMALL_ROWS)
    g_small = dict(zip(SMALL, _unpack(g_small_flat, [small_full[n].shape for n in SMALL])))
    for n in ("lru_conv_w", "ssd_conv_w"):
        w = inp[n].shape[-1]
        g_small[n] = lax.dynamic_slice_in_dim(g_small[n], dev * w, w, axis=2)

    shapes = [inp[n].shape for n in SMALL]
    packs = [_pack([src[pre + n] for n in SMALL], F32, SMALL_ROWS)
             for src, pre in ((inp, ""), (g_small, ""), (inp, "m_"), (inp, "v_"))]
    d, m2, v2 = _adamw("adamw_small", *packs)
    for n, dd, mm, vv in zip(SMALL, _unpack(d, shapes), _unpack(m2, shapes), _unpack(v2, shapes)):
        out["grad_" + n] = g_small[n]
        out["delta_" + n], out["new_m_" + n], out["new_v_" + n] = dd, mm, vv
    return out


WEIGHTS = ("norm1_g", "w_in", "b_gate", "lru_conv_w", "lru_conv_b", "lru_w_a", "lru_b_a", "lru_w_x", "lru_b_x",
           "lru_lambda", "ssd_conv_w", "ssd_conv_b", "ssd_dt_bias", "ssd_A_log", "ssd_D", "ssd_norm_g", "w_branch",
           "w_out", "norm2_g", "w_ffn_in", "w_ffn_out", "norm_f")


def kernel(x, norm1_g, w_in, b_gate, lru_conv_w, lru_conv_b, lru_w_a, lru_b_a, lru_w_x, lru_b_x, lru_lambda, ssd_conv_w, ssd_conv_b, ssd_dt_bias, ssd_A_log, ssd_D, ssd_norm_g, w_branch, w_out, norm2_g, w_ffn_in, w_ffn_out, norm_f, loss_target, m_norm1_g, m_w_in, m_b_gate, m_lru_conv_w, m_lru_conv_b, m_lru_w_a, m_lru_b_a, m_lru_w_x, m_lru_b_x, m_lru_lambda, m_ssd_conv_w, m_ssd_conv_b, m_ssd_dt_bias, m_ssd_A_log, m_ssd_D, m_ssd_norm_g, m_w_branch, m_w_out, m_norm2_g, m_w_ffn_in, m_w_ffn_out, m_norm_f, v_norm1_g, v_w_in, v_b_gate, v_lru_conv_w, v_lru_conv_b, v_lru_w_a, v_lru_b_a, v_lru_w_x, v_lru_b_x, v_lru_lambda, v_ssd_conv_w, v_ssd_conv_b, v_ssd_dt_bias, v_ssd_A_log, v_ssd_D, v_ssd_norm_g, v_w_branch, v_w_out, v_norm2_g, v_w_ffn_in, v_w_ffn_out, v_norm_f):
    out = _step(dict(locals()))
    res = [out["loss"], out["grad_x"]]
    for pre in ("grad_", "delta_", "new_m_", "new_v_"):
        res += [out[pre + n] for n in WEIGHTS]
    return tuple(res)
```

```python
import functools

import numpy as np
import jax
import jax.numpy as jnp
from jax import lax
from jax.experimental import pallas as pl
from jax.experimental.pallas import tpu as pltpu

F32 = jnp.float32
BF16 = jnp.bfloat16
HIGHEST = lax.Precision.HIGHEST

D_MODEL = 1024
DEPTH = 2
CHUNK = 64
LRU_C = 8.0
SSD_INNER = 2048
SSD_HEADS = 32
SSD_HEAD_DIM = 64
SSD_GROUPS = 4
SSD_STATE = 128
SSD_CONV_DIM = 3072
D_FF = 2816
EPS = 1e-6
N_DEV = 8
LANES = 128
DT_PAD = LANES
IN_WIDTHS = (1024, 1024, 2048, 3072, 32, 2048)
IN_OFFS = tuple(int(v) for v in np.cumsum((0,) + IN_WIDTHS))

ADAM_LR = 0.001
ADAM_B1 = 0.9
ADAM_B2 = 0.999
ADAM_EPS = 1e-08
ADAM_WD = 0.01
ADAM_STEP = 10

VMEM_LIMIT = 56 * 1024 * 1024
MESH = pl.DeviceIdType.MESH


def _cparams(sem=None):
    return pltpu.CompilerParams(dimension_semantics=sem, vmem_limit_bytes=VMEM_LIMIT)


def _sds(shape, dtype):
    return jax.ShapeDtypeStruct(tuple(shape), dtype)


def _pick(n, cap):
    if n <= cap:
        return n
    best = LANES
    for t in range(LANES, cap + 1, LANES):
        if n % t == 0:
            best = t
    assert n % best == 0, (n, cap)
    return best


def _mm(name, pairs, mode, out_dtype=F32, res=None, rider=None):
    M = pairs[0][0].shape[0]
    N = pairs[0][1].shape[1] if mode == "nn" else pairs[0][1].shape[0]
    npair = len(pairs)
    tm = min(M, 1024 if npair <= 3 else 512)
    tn = _pick(N, 1536 if npair == 1 else 1024)
    tks, nks, starts = [], [], []
    s = 0
    for a, _ in pairs:
        k = a.shape[1]
        tk = _pick(k, 1536)
        tks.append(tk)
        nks.append(k // tk)
        starts.append(s)
        s += k // tk
    nk = s
    dims = (((1,), (0,)), ((), ())) if mode == "nn" else (((1,), (1,)), ((), ()))

    def body(*refs):
        ab = refs[:2 * npair]
        pos = 2 * npair
        r_ref = None
        if res is not None:
            r_ref = refs[pos]
            pos += 1
        o_ref = refs[pos]
        acc = refs[pos + 1] if nk > 1 else None
        k = pl.program_id(2)

        def finish(r):
            if r_ref is not None:
                r = r + r_ref[...].astype(F32)
            o_ref[...] = r.astype(o_ref.dtype)

        for p in range(npair):
            a_ref, b_ref = ab[2 * p], ab[2 * p + 1]
            lo, hi = starts[p], starts[p] + nks[p]

            def step(a_ref=a_ref, b_ref=b_ref, lo=lo, hi=hi):
                d = lax.dot_general(a_ref[...].astype(BF16), b_ref[...].astype(BF16), dims,
                                    preferred_element_type=F32)
                if nk == 1:
                    finish(d)
                    return
                if lo == 0:
                    @pl.when(k == 0)
                    def _():
                        acc[...] = d
                if hi == nk:
                    @pl.when(k == nk - 1)
                    def _():
                        finish(acc[...] + d)
                if max(lo, 1) < min(hi, nk - 1):
                    @pl.when((k > 0) & (k < nk - 1))
                    def _():
                        acc[...] += d

            if npair == 1:
                step()
            else:
                pl.when((k >= lo) & (k < hi))(step)

    in_specs, args = [], []
    for p, (a, b) in enumerate(pairs):
        def kk(k, p=p):
            return jnp.clip(k - starts[p], 0, nks[p] - 1)
        in_specs.append(pl.BlockSpec((tm, tks[p]), lambda i, j, k, kk=kk: (i, kk(k))))
        if mode == "nn":
            in_specs.append(pl.BlockSpec((tks[p], tn), lambda i, j, k, kk=kk: (kk(k), j)))
        else:
            in_specs.append(pl.BlockSpec((tn, tks[p]), lambda i, j, k, kk=kk: (j, kk(k))))
        args += [a, b]
    if res is not None:
        in_specs.append(pl.BlockSpec((tm, tn), lambda i, j, k: (i, j)))
        args.append(res)
    r_args, r_in, r_out, r_shape, r_scratch = _rider_parts(rider)
    grid = (M // tm, N // tn, nk)
    out = pl.pallas_call(
        _with_rider(body, len(args), 1, rider, grid), grid=grid, in_specs=in_specs + r_in,
        out_specs=[pl.BlockSpec((tm, tn), lambda i, j, k: (i, j))] + r_out,
        out_shape=[_sds((M, N), out_dtype)] + r_shape,
        scratch_shapes=([pltpu.VMEM((tm, tn), F32)] if nk > 1 else []) + r_scratch,
        compiler_params=_cparams(("parallel", "parallel", "arbitrary") if rider is None else ("arbitrary",) * 3),
        name=name,
    )(*args, *r_args)
    return out[0] if rider is None else (out[0], out[1:])


def _mm_tn(name, a, b, out_dtype=BF16):
    M, Ka = a.shape
    N = b.shape[1]
    tm = min(M, 2048)
    tka = _pick(Ka, 1024)
    tn = _pick(N, 1024)
    nm = M // tm

    def body(a_ref, b_ref, o_ref, *scratch):
        k = pl.program_id(2)
        d = lax.dot_general(a_ref[...].astype(BF16), b_ref[...].astype(BF16),
                            (((0,), (0,)), ((), ())), preferred_element_type=F32)
        if nm == 1:
            o_ref[...] = d.astype(o_ref.dtype)
            return
        acc = scratch[0]

        @pl.when(k == 0)
        def _():
            acc[...] = d

        @pl.when((k > 0) & (k < nm - 1))
        def _():
            acc[...] += d

        @pl.when(k == nm - 1)
        def _():
            o_ref[...] = (acc[...] + d).astype(o_ref.dtype)

    return pl.pallas_call(
        body, grid=(Ka // tka, N // tn, nm),
        in_specs=[pl.BlockSpec((tm, tka), lambda i, j, k: (k, i)),
                  pl.BlockSpec((tm, tn), lambda i, j, k: (k, j))],
        out_specs=pl.BlockSpec((tka, tn), lambda i, j, k: (i, j)),
        out_shape=_sds((Ka, N), out_dtype),
        scratch_shapes=[pltpu.VMEM((tka, tn), F32)] if nm > 1 else [],
        compiler_params=_cparams(("parallel", "parallel", "arbitrary")), name=name,
    )(a, b)


def _rowwise_fwd(name, fn, rows, pars, outs, tb, rider=None):
    rows = [r if isinstance(r, tuple) else (r, 0, r.shape[1]) for r in rows]
    T = rows[0][0].shape[0]
    tb = min(tb, T)
    nr, npar = len(rows), len(pars)
    r_args, r_in, r_out, r_shape, r_scratch = _rider_parts(rider)

    def body(*refs):
        rv = [r[...].astype(F32) for r in refs[:nr]]
        pv = [p[...] for p in refs[nr:nr + npar]]
        res = fn(*rv, *pv)
        for o, r in zip(refs[nr + npar:], res):
            o[...] = r.astype(o.dtype)

    in_specs = [pl.BlockSpec((tb, w), lambda i, c=c: (i, c)) for _, c, w in rows]
    in_specs += [pl.BlockSpec(p.shape, lambda i, n=p.ndim: (0,) * n) for p in pars]
    return pl.pallas_call(
        _with_rider(body, nr + npar, len(outs), rider, (T // tb,)), grid=(T // tb,), in_specs=in_specs + r_in,
        out_specs=[pl.BlockSpec((tb, w), lambda i: (i, 0)) for w, _ in outs] + r_out,
        out_shape=[_sds((T, w), dt) for w, dt in outs] + r_shape, scratch_shapes=r_scratch,
        compiler_params=_cparams(("parallel",) if rider is None else ("arbitrary",)), name=name,
    )(*[r[0] for r in rows], *pars, *r_args)


def _rowwise_bwd(name, fn, rows, pars, cot_rows, cot_fn, row_out, tb):
    rows = [r if isinstance(r, tuple) else (r, 0, r.shape[1]) for r in rows]
    cot_rows = [r if isinstance(r, tuple) else (r, 0, r.shape[1]) for r in cot_rows]
    T = rows[0][0].shape[0]
    tb = min(tb, T)
    nr, npar, nc = len(rows), len(pars), len(cot_rows)
    want, want_dt = [], []
    for k, dt in enumerate(row_out):
        for d in (() if dt is None else dt if isinstance(dt, tuple) else (dt,)):
            want.append(k)
            want_dt.append(d)

    def body(*refs):
        i = pl.program_id(0)
        rv = [r[...].astype(F32) for r in refs[:nr]]
        pv = [p[...] for p in refs[nr:nr + npar]]
        cv = [c[...].astype(F32) for c in refs[nr + npar:nr + npar + nc]]
        o_refs = refs[nr + npar + nc:]
        _, vjp = jax.vjp(fn, *rv, *pv)
        grads = vjp(tuple(cot_fn(*cv)))
        for o, k in zip(o_refs[:len(want)], want):
            o[...] = grads[k].astype(o.dtype)
        p_refs = o_refs[len(want):]

        @pl.when(i == 0)
        def _():
            for o in p_refs:
                o[...] = jnp.zeros_like(o)

        for o, g in zip(p_refs, grads[nr:]):
            o[...] += g

    in_specs = [pl.BlockSpec((tb, w), lambda i, c=c: (i, c)) for _, c, w in rows]
    in_specs += [pl.BlockSpec(p.shape, lambda i, n=p.ndim: (0,) * n) for p in pars]
    in_specs += [pl.BlockSpec((tb, w), lambda i, c=c: (i, c)) for _, c, w in cot_rows]
    out_specs = [pl.BlockSpec((tb, rows[k][2]), lambda i: (i, 0)) for k in want]
    out_specs += [pl.BlockSpec(p.shape, lambda i, n=p.ndim: (0,) * n) for p in pars]
    out_shape = [_sds((T, rows[k][2]), d) for k, d in zip(want, want_dt)] + [_sds(p.shape, F32) for p in pars]
    return pl.pallas_call(
        body, grid=(T // tb,), in_specs=in_specs, out_specs=out_specs, out_shape=out_shape,
        compiler_params=_cparams(("arbitrary",)), name=name,
    )(*[r[0] for r in rows], *pars, *[r[0] for r in cot_rows])


def _f_rms(x, g):
    r = lax.rsqrt(jnp.mean(x * x, axis=-1, keepdims=True) + EPS)
    return (x * r * g,)


def _f_lru_gates(u, wa, ba, wx, bx, lam):
    ub = u.astype(BF16)
    ra, rx = [], []
    for k in range(D_MODEL // LANES):
        uk = ub[:, LANES * k:LANES * (k + 1)]
        ra.append(jnp.dot(uk, wa[k].astype(BF16), preferred_element_type=F32))
        rx.append(jnp.dot(uk, wx[k].astype(BF16), preferred_element_type=F32))
    r = jax.nn.sigmoid(jnp.concatenate(ra, axis=1) + ba)
    i = jax.nn.sigmoid(jnp.concatenate(rx, axis=1) + bx)
    log_a = -LRU_C * r * jax.nn.softplus(-lam)
    a = jnp.exp(log_a)
    t = jnp.tanh(log_a)
    b = jnp.sqrt(-2.0 * t / (1.0 - t)) * (i * u)
    return a, b


def _f_post(hl, lgate, yssd, xpre_s, z, d_e, ng):
    ya = jax.nn.gelu(lgate) * hl
    y = (yssd + d_e * jax.nn.silu(xpre_s)) * jax.nn.silu(z)
    gw = SSD_INNER // SSD_GROUPS
    parts = []
    for g in range(SSD_GROUPS):
        yg = y[:, gw * g:gw * (g + 1)]
        parts.append(yg * lax.rsqrt(jnp.mean(yg * yg, axis=-1, keepdims=True) + EPS))
    return ya, jnp.concatenate(parts, axis=1) * ng


def _f_merge(m_a, m_b, gates, bg):
    g = jax.nn.sigmoid(gates + bg)
    return (g[:, :D_MODEL] * m_a + g[:, D_MODEL:] * m_b,)


def _f_act(gu):
    return (jax.nn.silu(gu[:, :D_FF]) * gu[:, D_FF:],)


def _ident(*c):
    return c


def _loss_head(h, target, nf):
    T = h.shape[0]
    tb = min(256, T)

    def body(h_ref, t_ref, nf_ref, dh_ref, dnf_ref, loss_ref, dhb_ref):
        i = pl.program_id(0)

        @pl.when(i == 0)
        def _():
            dnf_ref[...] = jnp.zeros_like(dnf_ref)
            loss_ref[...] = jnp.zeros_like(loss_ref)

        (y,), vjp = jax.vjp(_f_rms, h_ref[...], nf_ref[...])
        err = y - t_ref[...]
        dh, dnf = vjp((err * (1.0 / D_MODEL),))
        dh_ref[...] = dh
        dhb_ref[...] = dh.astype(BF16)
        dnf_ref[...] += dnf
        part = 0.5 * jnp.sum(jnp.mean(err * err, axis=-1, keepdims=True), axis=0, keepdims=True)
        loss_ref[...] += jnp.broadcast_to(part, loss_ref.shape)

    return pl.pallas_call(
        body, grid=(T // tb,),
        in_specs=[pl.BlockSpec((tb, D_MODEL), lambda i: (i, 0)), pl.BlockSpec((tb, D_MODEL), lambda i: (i, 0)),
                  pl.BlockSpec((1, D_MODEL), lambda i: (0, 0))],
        out_specs=[pl.BlockSpec((tb, D_MODEL), lambda i: (i, 0)), pl.BlockSpec((1, D_MODEL), lambda i: (0, 0)),
                   pl.BlockSpec((8, LANES), lambda i: (0, 0)), pl.BlockSpec((tb, D_MODEL), lambda i: (i, 0))],
        out_shape=[_sds((T, D_MODEL), F32), _sds((1, D_MODEL), F32), _sds((8, LANES), F32),
                   _sds((T, D_MODEL), BF16)],
        compiler_params=_cparams(("arbitrary",)), name="loss_head",
    )(h, target, nf)


CONV_TC = 1024
HALO = 16


def _conv_fwd(name, x, w8, b):
    T, C = x.shape
    tb = min(512, T)
    nb = tb // HALO

    def body(x_ref, xp_ref, w_ref, b_ref, y_ref, sc):
        i = pl.program_id(0)
        xv = x_ref[...].astype(F32)
        sc[pl.ds(HALO, tb), :] = xv
        sc[pl.ds(0, HALO), :] = jnp.where(i > 0, xp_ref[...].astype(F32), 0.0)
        acc = b_ref[...] + w_ref[3:4, :] * xv
        for k in range(3):
            acc = acc + w_ref[k:k + 1, :] * sc[pl.ds(HALO - 3 + k, tb), :]
        y_ref[...] = acc

    return pl.pallas_call(
        body, grid=(T // tb, C // CONV_TC),
        in_specs=[pl.BlockSpec((tb, CONV_TC), lambda i, j: (i, j)),
                  pl.BlockSpec((HALO, CONV_TC), lambda i, j: (jnp.maximum(i * nb - 1, 0), j)),
                  pl.BlockSpec((8, CONV_TC), lambda i, j: (0, j)),
                  pl.BlockSpec((1, CONV_TC), lambda i, j: (0, j))],
        out_specs=pl.BlockSpec((tb, CONV_TC), lambda i, j: (i, j)),
        out_shape=_sds((T, C), F32),
        scratch_shapes=[pltpu.VMEM((tb + HALO, CONV_TC), F32)],
        compiler_params=_cparams(("parallel", "parallel")), name=name,
    )(x, x, w8, b)


def _conv_bwd(name, x, dy, w8):
    T, C = x.shape
    tb = min(512, T)
    nb = tb // HALO
    nt = T // tb

    def body(x_ref, xp_ref, dy_ref, dyn_ref, w_ref, dx_ref, dwb_ref, scx, scd):
        i = pl.program_id(1)
        dyv = dy_ref[...]
        xv = x_ref[...].astype(F32)
        scx[pl.ds(HALO, tb), :] = xv
        scx[pl.ds(0, HALO), :] = jnp.where(i > 0, xp_ref[...].astype(F32), 0.0)
        scd[pl.ds(0, tb), :] = dyv
        scd[pl.ds(tb, HALO), :] = jnp.where(i < nt - 1, dyn_ref[...], 0.0)
        dx = w_ref[3:4, :] * dyv
        rows = []
        for k in range(3):
            dx = dx + w_ref[k:k + 1, :] * scd[pl.ds(3 - k, tb), :]
            rows.append(jnp.sum(dyv * scx[pl.ds(HALO - 3 + k, tb), :], axis=0, keepdims=True))
        rows.append(jnp.sum(dyv * xv, axis=0, keepdims=True))
        rows.append(jnp.sum(dyv, axis=0, keepdims=True))
        rows.append(jnp.zeros((3, CONV_TC), F32))
        dx_ref[...] = dx.astype(dx_ref.dtype)

        @pl.when(i == 0)
        def _():
            dwb_ref[...] = jnp.zeros_like(dwb_ref)

        dwb_ref[...] += jnp.concatenate(rows, axis=0)

    return pl.pallas_call(
        body, grid=(C // CONV_TC, nt),
        in_specs=[pl.BlockSpec((tb, CONV_TC), lambda j, i: (i, j)),
                  pl.BlockSpec((HALO, CONV_TC), lambda j, i: (jnp.maximum(i * nb - 1, 0), j)),
                  pl.BlockSpec((tb, CONV_TC), lambda j, i: (i, j)),
                  pl.BlockSpec((HALO, CONV_TC), lambda j, i: (jnp.minimum((i + 1) * nb, T // HALO - 1), j)),
                  pl.BlockSpec((8, CONV_TC), lambda j, i: (0, j))],
        out_specs=[pl.BlockSpec((tb, CONV_TC), lambda j, i: (i, j)),
                   pl.BlockSpec((8, CONV_TC), lambda j, i: (0, j))],
        out_shape=[_sds((T, C), BF16), _sds((8, C), F32)],
        scratch_shapes=[pltpu.VMEM((tb + HALO, CONV_TC), F32), pltpu.VMEM((tb + HALO, CONV_TC), F32)],
        compiler_params=_cparams(("parallel", "arbitrary")), name=name,
    )(x, x, dy, dy, w8)


SCAN_TB = 512


def _lru_scan_fwd(a, b):
    T, C = a.shape
    tb = min(SCAN_TB, T)

    def body(a_ref, b_ref, h_ref, hp_ref, carry):
        @pl.when(pl.program_id(0) == 0)
        def _():
            carry[...] = jnp.zeros_like(carry)

        def group(gi, h):
            r0 = pl.multiple_of(gi * 8, 8)
            at = a_ref[pl.ds(r0, 8), :]
            bt = b_ref[pl.ds(r0, 8), :]
            hs, hps = [], []
            for r in range(8):
                hps.append(h)
                h = at[r:r + 1, :] * h + bt[r:r + 1, :]
                hs.append(h)
            h_ref[pl.ds(r0, 8), :] = jnp.concatenate(hs, axis=0)
            hp_ref[pl.ds(r0, 8), :] = jnp.concatenate(hps, axis=0)
            return h

        carry[0:1, :] = lax.fori_loop(0, tb // 8, group, carry[0:1, :])

    spec = pl.BlockSpec((tb, C), lambda i: (i, 0))
    return pl.pallas_call(
        body, grid=(T // tb,), in_specs=[spec, spec], out_specs=[spec, spec],
        out_shape=[_sds((T, C), F32), _sds((T, C), F32)],
        scratch_shapes=[pltpu.VMEM((8, C), F32)],
        compiler_params=_cparams(("arbitrary",)), name="lru_scan_fwd",
    )(a, b)


def _lru_scan_bwd(a, dh):
    T, C = a.shape
    tb = min(SCAN_TB, T)
    nt = T // tb

    def body(a_ref, dh_ref, g_ref, carry):
        @pl.when(pl.program_id(0) == 0)
        def _():
            carry[...] = jnp.zeros_like(carry)

        def group(gi, c):
            r0 = pl.multiple_of((tb // 8 - 1 - gi) * 8, 8)
            at = a_ref[pl.ds(r0, 8), :]
            dt = dh_ref[pl.ds(r0, 8), :]
            gs = [None] * 8
            for r in range(7, -1, -1):
                g = dt[r:r + 1, :] + c
                c = at[r:r + 1, :] * g
                gs[r] = g
            g_ref[pl.ds(r0, 8), :] = jnp.concatenate(gs, axis=0)
            return c

        carry[0:1, :] = lax.fori_loop(0, tb // 8, group, carry[0:1, :])

    spec = pl.BlockSpec((tb, C), lambda i: (nt - 1 - i, 0))
    return pl.pallas_call(
        body, grid=(nt,), in_specs=[spec, spec], out_specs=spec,
        out_shape=_sds((T, C), F32),
        scratch_shapes=[pltpu.VMEM((8, C), F32)],
        compiler_params=_cparams(("arbitrary",)), name="lru_scan_bwd",
    )(a, dh)


def _ssd_chunk(xpre, dtraw, state, dtb, alog):
    xc = jax.nn.silu(xpre)
    xs = xc[:, :SSD_INNER]
    bm = xc[:, SSD_INNER:SSD_INNER + SSD_GROUPS * SSD_STATE]
    cm = xc[:, SSD_INNER + SSD_GROUPS * SSD_STATE:]
    dt = jax.nn.softplus(dtraw + dtb)
    a = dt * (-jnp.exp(alog))
    ltri = (lax.broadcasted_iota(jnp.int32, (CHUNK, CHUNK), 0)
            >= lax.broadcasted_iota(jnp.int32, (CHUNK, CHUNK), 1)).astype(F32)
    a_cs = jnp.dot(ltri, a, precision=HIGHEST, preferred_element_type=F32)
    npair = SSD_HEADS // 2
    pi = lax.broadcasted_iota(jnp.int32, (npair, LANES), 0)
    hi = lax.broadcasted_iota(jnp.int32, (npair, LANES), 1)
    sel_even = (hi == 2 * pi).astype(F32)
    sel_odd = (hi == 2 * pi + 1).astype(F32)
    top = lax.broadcasted_iota(jnp.int32, (2 * CHUNK, LANES), 0) < CHUNK

    def pair_transpose(v):
        v2 = jnp.concatenate([v, v], axis=0)
        dn = (((1,), (1,)), ((), ()))
        return (lax.dot_general(sel_even, jnp.where(top, v2, 0.0), dn, precision=HIGHEST, preferred_element_type=F32)
                + lax.dot_general(sel_odd, jnp.where(top, 0.0, v2), dn, precision=HIGHEST, preferred_element_type=F32))

    a_t2 = pair_transpose(a_cs)
    dt_t2 = pair_transpose(dt)
    a_last = a_cs[CHUNK - 1:CHUNK, :]
    dte = jnp.exp(a_last - a_cs) * dt
    cd = jnp.exp(a_last)
    lane = lax.broadcasted_iota(jnp.int32, (CHUNK, LANES), 1)
    left = lane < SSD_HEAD_DIM
    right = jnp.logical_not(left)
    left1 = left[0:1]
    tril2 = lax.broadcasted_iota(jnp.int32, (CHUNK, LANES), 0) >= (lane & (SSD_HEAD_DIM - 1))
    gw = SSD_INNER // SSD_GROUPS
    ys, news = [], []
    for g in range(SSD_GROUPS):
        bg = bm[:, SSD_STATE * g:SSD_STATE * (g + 1)].astype(BF16)
        cg = cm[:, SSD_STATE * g:SSD_STATE * (g + 1)].astype(BF16)
        bg2 = jnp.concatenate([bg, bg], axis=0)
        scores2 = lax.dot_general(cg, bg2, (((1,), (1,)), ((), ())), preferred_element_type=F32)
        coff = jnp.dot(cg, state[:, gw * g:gw * (g + 1)].astype(BF16), preferred_element_type=F32)
        for j in range(gw // LANES):
            lo = gw * g + LANES * j
            p = lo // LANES
            h0 = 2 * p
            xp = xs[:, lo:lo + LANES]
            col = jnp.where(left, a_cs[:, h0:h0 + 1], a_cs[:, h0 + 1:h0 + 2])
            dm = jnp.exp(jnp.where(tril2, col - a_t2[p:p + 1, :], -1e30)) * dt_t2[p:p + 1, :]
            sd = (scores2 * dm).astype(BF16)
            x_bd = jnp.concatenate([jnp.where(left, xp, 0.0), jnp.where(right, xp, 0.0)], axis=0).astype(BF16)
            acc = jnp.dot(sd, x_bd, preferred_element_type=F32)
            ys.append(acc + coff[:, LANES * j:LANES * (j + 1)] * jnp.exp(col))
            dtee = jnp.where(left, dte[:, h0:h0 + 1], dte[:, h0 + 1:h0 + 2])
            xw = (xp * dtee).astype(BF16)
            cde = jnp.where(left1, cd[:, h0:h0 + 1], cd[:, h0 + 1:h0 + 2])
            news.append(state[:, lo:lo + LANES] * cde
                        + lax.dot_general(bg, xw, (((0,), (0,)), ((), ())), preferred_element_type=F32))
    return jnp.concatenate(ys, axis=1), jnp.concatenate(news, axis=1)


def _ssd_fwd(name, xpre, dtraw, dtb, alog, rider=None):
    T = xpre.shape[0]
    n = T // CHUNK
    r_args, r_in, r_out, r_shape, r_scratch = _rider_parts(rider)

    def body(xp, dr, dtb_ref, al_ref, y_ref, st_ref, state):
        @pl.when(pl.program_id(0) == 0)
        def _():
            state[...] = jnp.zeros_like(state)

        st_ref[0] = state[...]
        y, new = _ssd_chunk(xp[...], dr[...], state[...], dtb_ref[...], al_ref[...])
        y_ref[...] = y
        state[...] = new

    small = pl.BlockSpec((1, LANES), lambda c: (0, 0))
    return pl.pallas_call(
        _with_rider(body, 4, 2, rider, (n,)), grid=(n,),
        in_specs=[pl.BlockSpec((CHUNK, SSD_CONV_DIM), lambda c: (c, 0)),
                  pl.BlockSpec((CHUNK, DT_PAD), lambda c: (c, 0)), small, small] + r_in,
        out_specs=[pl.BlockSpec((CHUNK, SSD_INNER), lambda c: (c, 0)),
                   pl.BlockSpec((1, SSD_STATE, SSD_INNER), lambda c: (c, 0, 0))] + r_out,
        out_shape=[_sds((T, SSD_INNER), F32), _sds((n, SSD_STATE, SSD_INNER), F32)] + r_shape,
        scratch_shapes=[pltpu.VMEM((SSD_STATE, SSD_INNER), F32)] + r_scratch,
        compiler_params=_cparams(("arbitrary",)), name=name,
    )(xpre, dtraw, dtb, alog, *r_args)


def _ssd_bwd(name, xpre, dtraw, states, dy, dxs_extra, dtb, alog, rider=None):
    T = xpre.shape[0]
    n = T // CHUNK
    r_args, r_in, r_out, r_shape, r_scratch = _rider_parts(rider)

    def body(xp, dr, st, dy_ref, dx_ref, dtb_ref, al_ref, dxp_ref, ddr_ref, ddtb_ref, dal_ref, dstate):
        @pl.when(pl.program_id(0) == 0)
        def _():
            dstate[...] = jnp.zeros_like(dstate)
            ddtb_ref[...] = jnp.zeros_like(ddtb_ref)
            dal_ref[...] = jnp.zeros_like(dal_ref)

        _, vjp = jax.vjp(_ssd_chunk, xp[...], dr[...], st[0], dtb_ref[...], al_ref[...])
        dxp, ddr, ds, db, da = vjp((dy_ref[...], dstate[...]))
        dxp_ref[:, :SSD_INNER] = dxp[:, :SSD_INNER] + dx_ref[...]
        dxp_ref[:, SSD_INNER:] = dxp[:, SSD_INNER:]
        ddr_ref[...] = ddr.astype(ddr_ref.dtype)
        dstate[...] = ds
        ddtb_ref[...] += db
        dal_ref[...] += da

    def rev(c):
        return (n - 1 - c, 0)

    small = pl.BlockSpec((1, LANES), lambda c: (0, 0))
    return pl.pallas_call(
        _with_rider(body, 7, 4, rider, (n,)), grid=(n,),
        in_specs=[pl.BlockSpec((CHUNK, SSD_CONV_DIM), rev), pl.BlockSpec((CHUNK, DT_PAD), rev),
                  pl.BlockSpec((1, SSD_STATE, SSD_INNER), lambda c: (n - 1 - c, 0, 0)),
                  pl.BlockSpec((CHUNK, SSD_INNER), rev), pl.BlockSpec((CHUNK, SSD_INNER), rev), small, small] + r_in,
        out_specs=[pl.BlockSpec((CHUNK, SSD_CONV_DIM), rev), pl.BlockSpec((CHUNK, DT_PAD), rev), small, small] + r_out,
        out_shape=[_sds((T, SSD_CONV_DIM), F32), _sds((T, DT_PAD), BF16), _sds((1, LANES), F32),
                   _sds((1, LANES), F32)] + r_shape,
        scratch_shapes=[pltpu.VMEM((SSD_STATE, SSD_INNER), F32)] + r_scratch,
        compiler_params=_cparams(("arbitrary",)), name=name,
    )(xpre, dtraw, states, dy, dxs_extra, dtb, alog, *r_args)


HBM_SPEC = pl.BlockSpec(memory_space=pltpu.HBM)
N_PEER = N_DEV - 1


def _position():
    return lax.axis_index("x"), lax.axis_index("y"), lax.axis_index("c")


def _all_gather(name, blks):
    return _exchange_call(name, _gather_phases, blks, [_sds((N_DEV,) + b.shape, b.dtype) for b in blks])


def _scatter_exchange(name, gs):
    return _exchange_call(name, _scatter_phases, gs, [_sds(g.shape, g.dtype) for g in gs])


def _exchange_scratch(na):
    return [pltpu.SemaphoreType.DMA((na * N_PEER,)), pltpu.SemaphoreType.DMA((na * N_PEER,)),
            pltpu.SemaphoreType.DMA((na,))]


def _exchange_call(name, phases, arrays, out_shape):
    na = len(arrays)

    def body(*refs):
        begin, finish = phases(refs[:na], refs[na:2 * na], *refs[2 * na:])
        begin()
        finish()

    return pl.pallas_call(
        body, out_shape=out_shape, in_specs=[HBM_SPEC] * na, out_specs=[HBM_SPEC] * na,
        scratch_shapes=_exchange_scratch(na), name=name,
    )(*arrays)


def _gather_phases(x_refs, out_refs, send_sems, recv_sems, local_sems):
    na = len(x_refs)
    x, y, c = _position()
    me, sibling = (x, y, c), (x, y, 1 - c)
    chips = [(1 - x, y), (x, 1 - y), (1 - x, 1 - y)]

    def slot(a, px, py, pc):
        return out_refs[a].at[4 * px + 2 * py + pc]

    def copy(a, k, block, to, src=None):
        return pltpu.make_async_remote_copy(
            src_ref=slot(a, *block) if src is None else src, dst_ref=slot(a, *block),
            send_sem=send_sems.at[a * N_PEER + k], recv_sem=recv_sems.at[a * N_PEER + k],
            device_id=to, device_id_type=MESH)

    mine = [pltpu.make_async_copy(x_refs[a], slot(a, *me), local_sems.at[a]) for a in range(na)]
    first = []
    for a in range(na):
        first.append(copy(a, 0, me, sibling, src=x_refs[a]))
        first += [copy(a, 1 + j, me, (*chip, c), src=x_refs[a]) for j, chip in enumerate(chips)]
    passed = [copy(a, 4 + j, (*chip, c), sibling) for j, chip in enumerate(chips) for a in range(na)]

    def begin():
        for cp in mine + first:
            cp.start()

    def finish():
        for j, chip in enumerate(chips):
            for a in range(na):
                copy(a, 1 + j, (*chip, c), me).wait_recv()
                passed[j * na + a].start()
        for a in range(na):
            copy(a, 0, sibling, me).wait_recv()
            for j, chip in enumerate(chips):
                copy(a, 4 + j, (*chip, 1 - c), me).wait_recv()
        for cp in first + passed:
            cp.wait_send()
        for cp in mine:
            cp.wait()

    return begin, finish


def _scatter_phases(g_refs, q_refs, send_sems, recv_sems, local_sems):
    na = len(g_refs)
    x, y, c = _position()
    me = 4 * x + 2 * y + c
    mine, sends, recvs = [], [], []
    for a in range(na):
        mine.append(pltpu.make_async_copy(g_refs[a].at[me], q_refs[a].at[me], local_sems.at[a]))
        for k in range(1, N_DEV):
            px, py, pc = (x + (k >> 2)) % 2, (y + ((k >> 1) & 1)) % 2, (c + (k & 1)) % 2
            peer = 4 * px + 2 * py + pc
            sem = a * N_PEER + k - 1
            sends.append(pltpu.make_async_remote_copy(
                src_ref=g_refs[a].at[peer], dst_ref=q_refs[a].at[me], send_sem=send_sems.at[sem],
                recv_sem=recv_sems.at[sem], device_id=(px, py, pc), device_id_type=MESH))
            recvs.append(pltpu.make_async_remote_copy(
                src_ref=g_refs[a].at[me], dst_ref=q_refs[a].at[peer], send_sem=send_sems.at[sem],
                recv_sem=recv_sems.at[sem], device_id=(px, py, pc), device_id_type=MESH))

    def begin():
        for cp in mine + sends:
            cp.start()

    def finish():
        for cp in recvs:
            cp.wait_recv()
        for cp in sends:
            cp.wait_send()
        for cp in mine:
            cp.wait()

    return begin, finish


def _with_rider(body, n_in, n_out, rider, grid):
    if rider is None:
        return body
    phases, arrays, _ = rider
    na = len(arrays)

    def carried(*refs):
        ins, r_in = refs[:n_in], refs[n_in:n_in + na]
        outs = refs[n_in + na:n_in + na + n_out]
        r_out = refs[n_in + na + n_out:n_in + 2 * na + n_out]
        scratch, r_scratch = refs[n_in + 2 * na + n_out:-3], refs[-3:]
        begin, finish = phases(r_in, r_out, *r_scratch)
        first = last = None
        for d, g in enumerate(grid):
            at_start, at_end = pl.program_id(d) == 0, pl.program_id(d) == g - 1
            first = at_start if first is None else first & at_start
            last = at_end if last is None else last & at_end
        pl.when(first)(begin)
        body(*ins, *outs, *scratch)
        pl.when(last)(finish)

    return carried


def _rider_parts(rider):
    if rider is None:
        return [], [], [], [], []
    _, arrays, out_shape = rider
    na = len(arrays)
    return list(arrays), [HBM_SPEC] * na, [HBM_SPEC] * na, list(out_shape), _exchange_scratch(na)


def _gather_rider(blks):
    return (_gather_phases, blks, [_sds((N_DEV,) + b.shape, b.dtype) for b in blks])


def _scatter_rider(gs):
    return (_scatter_phases, gs, [_sds(g.shape, g.dtype) for g in gs])


def _adamw_sum(name, w, m, v, recvs):
    L, r, c = w.shape
    tr = r if r <= 512 else max(t for t in range(16, 257, 16) if r % t == 0)
    assert len(recvs) == L
    nr = r // tr

    def body(w_ref, m_ref, v_ref, *refs):
        q_refs = refs[:L]
        g_ref, d_ref, m2_ref, v2_ref = refs[L:]
        l = pl.program_id(0)
        for ll in range(L):
            @pl.when(l == ll)
            def _(q_ref=q_refs[ll]):
                gv = q_ref[0].astype(F32)
                for s in range(1, N_DEV):
                    gv = gv + q_ref[s].astype(F32)
                m2 = ADAM_B1 * m_ref[0] + (1.0 - ADAM_B1) * gv
                v2 = ADAM_B2 * v_ref[0] + (1.0 - ADAM_B2) * jnp.square(gv)
                m_hat = m2 / (1.0 - ADAM_B1 ** ADAM_STEP)
                v_hat = v2 / (1.0 - ADAM_B2 ** ADAM_STEP)
                g_ref[0] = gv
                d_ref[0] = -ADAM_LR * (m_hat / (jnp.sqrt(v_hat) + ADAM_EPS) + ADAM_WD * w_ref[0])
                m2_ref[0] = m2
                v2_ref[0] = v2

    def q_index(ll):
        return lambda l, i: (0, jnp.where(l == ll, i, jnp.where(l > ll, nr - 1, 0)), 0)

    spec = pl.BlockSpec((1, tr, c), lambda l, i: (l, i, 0))
    return pl.pallas_call(
        body, grid=(L, nr),
        in_specs=[spec] * 3 + [pl.BlockSpec((N_DEV, tr, c), q_index(ll)) for ll in range(L)],
        out_specs=[spec] * 4, out_shape=[_sds((L, r, c), F32)] * 4,
        compiler_params=_cparams(("arbitrary", "arbitrary")), name=name,
    )(w, m, v, *recvs)


def _sum_blocks(name, q, tr):
    R = q.shape[0] // N_DEV
    W = q.shape[1]
    tr = min(tr, R)
    assert R % tr == 0
    nb = R // tr

    def body(*refs):
        acc = refs[0][...].astype(F32)
        for r in refs[1:N_DEV]:
            acc = acc + r[...].astype(F32)
        refs[N_DEV][...] = acc

    return pl.pallas_call(
        body, grid=(nb,),
        in_specs=[pl.BlockSpec((tr, W), lambda i, s=s: (s * nb + i, 0)) for s in range(N_DEV)],
        out_specs=pl.BlockSpec((tr, W), lambda i: (i, 0)), out_shape=_sds((R, W), F32),
        compiler_params=_cparams(("parallel",)), name=name,
    )(*([q] * N_DEV))


def _adamw(name, w, g, m, v):
    R, C = w.shape
    tr = R
    if R > 512:
        tr = max(t for t in range(8, 513, 8) if R % t == 0)

    def body(w_ref, g_ref, m_ref, v_ref, d_ref, m2_ref, v2_ref):
        gv = g_ref[...]
        m2 = ADAM_B1 * m_ref[...] + (1.0 - ADAM_B1) * gv
        v2 = ADAM_B2 * v_ref[...] + (1.0 - ADAM_B2) * jnp.square(gv)
        m_hat = m2 / (1.0 - ADAM_B1 ** ADAM_STEP)
        v_hat = v2 / (1.0 - ADAM_B2 ** ADAM_STEP)
        d_ref[...] = -ADAM_LR * (m_hat / (jnp.sqrt(v_hat) + ADAM_EPS) + ADAM_WD * w_ref[...])
        m2_ref[...] = m2
        v2_ref[...] = v2

    spec = pl.BlockSpec((tr, C), lambda i: (i, 0))
    return pl.pallas_call(
        body, grid=(R // tr,), in_specs=[spec] * 4, out_specs=[spec] * 3,
        out_shape=[_sds((R, C), F32)] * 3, compiler_params=_cparams(("parallel",)), name=name,
    )(w, g, m, v)


BIG = ("w_in", "w_branch", "w_out", "w_ffn_in", "w_ffn_out")
BIG_ROW_SHARDED = {"w_in": False, "w_branch": True, "w_out": True, "w_ffn_in": True, "w_ffn_out": True}
TRANSPOSED = ("w_ffn_in",)
SMALL = ("norm1_g", "b_gate", "lru_conv_w", "lru_conv_b", "lru_w_a", "lru_b_a", "lru_w_x", "lru_b_x", "lru_lambda",
         "ssd_conv_w", "ssd_conv_b", "ssd_dt_bias", "ssd_A_log", "ssd_D", "ssd_norm_g", "norm2_g", "norm_f")
SMALL_ROWS = 256


def _pack(arrs, dtype, row_mult):
    parts = []
    for a in arrs:
        f = a.reshape(-1).astype(dtype)
        pad = (-f.shape[0]) % LANES
        if pad:
            f = jnp.concatenate([f, jnp.zeros((pad,), dtype)])
        parts.append(f)
    f = jnp.concatenate(parts)
    pad = (-f.shape[0]) % (LANES * row_mult)
    if pad:
        f = jnp.concatenate([f, jnp.zeros((pad,), dtype)])
    return f.reshape(-1, LANES)


def _unpack(flat, shapes, lead=()):
    f = flat.reshape(lead + (-1,))
    out, off = [], 0
    for s in shapes:
        n = int(np.prod(s))
        out.append(f[..., off:off + n].reshape(lead + tuple(s)))
        off += n + (-n) % LANES
    return out


def _full_from_shards(name, st):
    if BIG_ROW_SHARDED[name]:
        return st.reshape((-1,) + st.shape[2:])
    return jnp.transpose(st, (1, 0, 2)).reshape(st.shape[1], -1)


def _shards_from_full(name, full):
    if BIG_ROW_SHARDED[name]:
        return full.reshape((N_DEV, full.shape[0] // N_DEV) + full.shape[1:])
    return jnp.transpose(full.reshape(full.shape[0], N_DEV, -1), (1, 0, 2))


def _block_diag_tiles(w):
    z = jnp.zeros((8, 64, 64), w.dtype)
    w2 = w.reshape(8, 2, 64, 64)
    top = jnp.concatenate([w2[:, 0], z], axis=2)
    bot = jnp.concatenate([z, w2[:, 1]], axis=2)
    return jnp.concatenate([top, bot], axis=1)


def _block_diag_untile(t):
    return jnp.stack([t[:, :64, :64], t[:, 64:, 64:]], axis=1).reshape(16, 64, 64)


def _pad_lanes(a, width):
    return jnp.concatenate([a, jnp.zeros(a.shape[:-1] + (width - a.shape[-1],), a.dtype)], axis=-1)


def _pad_rows8(w):
    return jnp.concatenate([w, jnp.zeros((8 - w.shape[0],) + w.shape[1:], w.dtype)], axis=0)


REST = BIG[1:]


def _in_weights(shards):
    w_in = _full_from_shards("w_in", shards)
    seg = [w_in[:, IN_OFFS[k]:IN_OFFS[k + 1]] for k in range(6)]
    return {"w_lx": seg[0], "w_lg": seg[1], "w_z": seg[2], "w_xbc": seg[3], "w_dt": _pad_lanes(seg[4], DT_PAD),
            "w_g": seg[5]}


def _rest_weights(shards):
    full = {n: _full_from_shards(n, st) for n, st in zip(REST, shards)}
    return {"w_bra": full["w_branch"][:D_MODEL], "w_brb": full["w_branch"][D_MODEL:], "w_out": full["w_out"],
            "w_ffn_in_t": full["w_ffn_in"], "w_ffn_out": full["w_ffn_out"]}


def _small_params(l, small):
    p = {}
    for n in ("norm1_g", "b_gate", "lru_conv_b", "lru_b_a", "lru_b_x", "lru_lambda", "ssd_conv_b", "ssd_norm_g", "norm2_g"):
        p[n] = small[n][l].reshape(1, -1)
    p["lru_conv_w8"] = _pad_rows8(small["lru_conv_w"][l])
    p["ssd_conv_w8"] = _pad_rows8(small["ssd_conv_w"][l])
    p["wa"] = _block_diag_tiles(small["lru_w_a"][l])
    p["wx"] = _block_diag_tiles(small["lru_w_x"][l])
    p["dtb"] = _pad_lanes(small["ssd_dt_bias"][l].reshape(1, -1), DT_PAD)
    p["alog"] = _pad_lanes(small["ssd_A_log"][l].reshape(1, -1), DT_PAD)
    p["d_e"] = jnp.repeat(small["ssd_D"][l], SSD_HEAD_DIM).reshape(1, -1)
    return p


def _layer_fwd(l, h, p, riders):
    n = f"l{l}_"
    s = {"h_in": h}
    got = {}
    (xn,) = _rowwise_fwd(n + "rms1", _f_rms, [h], [p["norm1_g"]], [(D_MODEL, BF16)], 256)
    s["xn"] = xn
    for k in ("lx", "lg", "z", "xbc", "g", "dt"):
        s[k] = _mm(n + "in_" + k, [(xn, p["w_" + k])], "nn", out_dtype=F32 if k == "dt" else BF16)
    s["u"] = _conv_fwd(n + "lru_conv", s["lx"], p["lru_conv_w8"], p["lru_conv_b"])
    lru_pars = [p["wa"], p["lru_b_a"], p["wx"], p["lru_b_x"], p["lru_lambda"]]
    s["a"], b = _rowwise_fwd(n + "lru_gates", _f_lru_gates, [s["u"]], lru_pars, [(D_MODEL, F32)] * 2, 256)
    s["hl"], s["hprev"] = _lru_scan_fwd(s["a"], b)
    s["xpre"] = _conv_fwd(n + "ssd_conv", s["xbc"], p["ssd_conv_w8"], p["ssd_conv_b"])
    s["yssd"], s["states"], *got["ssd"] = _ssd_fwd(n + "ssd", s["xpre"], s["dt"], p["dtb"], p["alog"],
                                                   riders.get("ssd"))
    if "ssd" in riders:
        p.update(_rest_weights(got["ssd"]))
    post_rows = [s["hl"], s["lg"], s["yssd"], (s["xpre"], 0, SSD_INNER), s["z"]]
    s["ya"], s["yb"], *got["post"] = _rowwise_fwd(n + "post", _f_post, post_rows, [p["d_e"], p["ssd_norm_g"]],
                                                  [(D_MODEL, BF16), (SSD_INNER, BF16)], 128, riders.get("post"))
    s["ma"] = _mm(n + "br_a", [(s["ya"], p["w_bra"])], "nn", out_dtype=BF16)
    s["mb"] = _mm(n + "br_b", [(s["yb"], p["w_brb"])], "nn", out_dtype=BF16)
    (s["merged"],) = _rowwise_fwd(n + "merge", _f_merge, [s["ma"], s["mb"], s["g"]], [p["b_gate"]],
                                  [(D_MODEL, BF16)], 256)
    s["h_mid"] = _mm(n + "out", [(s["merged"], p["w_out"])], "nn", res=h)
    (s["xn2"],) = _rowwise_fwd(n + "rms2", _f_rms, [s["h_mid"]], [p["norm2_g"]], [(D_MODEL, BF16)], 256)
    s["gu"] = _mm(n + "ffn_in", [(s["xn2"], p["w_ffn_in_t"])], "nt", out_dtype=BF16, rider=riders.get("ffn_in"))
    if "ffn_in" in riders:
        s["gu"], got["ffn_in"] = s["gu"]
    (s["act"],) = _rowwise_fwd(n + "act", _f_act, [s["gu"]], [], [(D_FF, BF16)], 256)
    h_out = _mm(n + "ffn_out", [(s["act"], p["w_ffn_out"])], "nn", res=s["h_mid"], rider=riders.get("ffn_out"))
    if "ffn_out" in riders:
        h_out, got["ffn_out"] = h_out
    return h_out, s, got


def _f_rms_res(x, g):
    return _f_rms(x, g)[0], x


def _layer_bwd(l, dh, dh_b, s, p, above, first_layer):
    n = f"l{l}_b_"
    gw, gs = {}, {}
    d_act = _mm(n + "d_act", [(dh_b, p["w_ffn_out"])], "nt")
    gw["w_ffn_out"] = _mm_tn(n + "dw_ffn_out", s["act"], dh_b)
    (d_gu,) = _rowwise_bwd(n + "act", _f_act, [s["gu"]], [], [d_act], _ident, [BF16], 128)
    d_xn2 = _mm(n + "d_xn2", [(d_gu, p["w_ffn_in_t"])], "nn")
    gw["w_ffn_in"] = _mm_tn(n + "dw_ffn_in", d_gu, s["xn2"])
    dh_mid, dh_mid_b, gs["norm2_g"] = _rowwise_bwd(n + "rms2", _f_rms_res, [s["h_mid"]], [p["norm2_g"]], [d_xn2, dh],
                                                   _ident, [(F32, BF16)], 256)
    d_merged = _mm(n + "d_merged", [(dh_mid_b, p["w_out"])], "nt")
    gw["w_out"] = _mm_tn(n + "dw_out", s["merged"], dh_mid_b)
    d_ma, d_mb, d_g, gs["b_gate"] = _rowwise_bwd(n + "merge", _f_merge, [s["ma"], s["mb"], s["g"]], [p["b_gate"]],
                                                 [d_merged], _ident, [BF16, BF16, BF16], 256)
    d_ya = _mm(n + "d_ya", [(d_ma, p["w_bra"])], "nt")
    d_yb = _mm(n + "d_yb", [(d_mb, p["w_brb"])], "nt")
    gw["w_branch"] = jnp.concatenate([_mm_tn(n + "dw_bra", s["ya"], d_ma), _mm_tn(n + "dw_brb", s["yb"], d_mb)], axis=0)
    post_rows = [s["hl"], s["lg"], s["yssd"], (s["xpre"], 0, SSD_INNER), s["z"]]
    d_hl, d_lg, d_yssd, d_xs, d_z, d_de, gs["ssd_norm_g"] = _rowwise_bwd(
        n + "post", _f_post, post_rows, [p["d_e"], p["ssd_norm_g"]], [d_ya, d_yb], _ident,
        [F32, BF16, F32, F32, BF16], 64)
    gs["ssd_D"] = d_de.reshape(SSD_HEADS, SSD_HEAD_DIM).sum(axis=1)
    contrib = [_shards_from_full(k, gw[k]) for k in REST] + ([] if above is None else [above])
    d_xpre, d_dt, d_dtb, d_alog, *carried = _ssd_bwd(n + "ssd", s["xpre"], s["dt"], s["states"], d_yssd, d_xs,
                                                     p["dtb"], p["alog"], _scatter_rider(contrib))
    arrived = dict(zip(REST, carried))
    arrived_above = None if above is None else carried[len(REST)]
    gs["ssd_dt_bias"] = d_dtb[0, :SSD_HEADS]
    gs["ssd_A_log"] = d_alog[0, :SSD_HEADS]
    d_xbc, dwb = _conv_bwd(n + "ssd_conv", s["xbc"], d_xpre, p["ssd_conv_w8"])
    gs["ssd_conv_w"], gs["ssd_conv_b"] = dwb[:4], dwb[4]
    g_scan = _lru_scan_bwd(s["a"], d_hl)
    lru_pars = [p["wa"], p["lru_b_a"], p["wx"], p["lru_b_x"], p["lru_lambda"]]
    d_u, d_wa, gs["lru_b_a"], d_wx, gs["lru_b_x"], gs["lru_lambda"] = _rowwise_bwd(
        n + "lru_gates", _f_lru_gates, [s["u"]], lru_pars, [g_scan, s["hprev"]],
        lambda g, hp: (g * hp, g), [F32], 128)
    gs["lru_w_a"], gs["lru_w_x"] = _block_diag_untile(d_wa), _block_diag_untile(d_wx)
    d_lx, dwb = _conv_bwd(n + "lru_conv", s["lx"], d_u, p["lru_conv_w8"])
    gs["lru_conv_w"], gs["lru_conv_b"] = dwb[:4], dwb[4]
    segs = [("lx", d_lx), ("lg", d_lg), ("z", d_z), ("xbc", d_xbc), ("dt", d_dt), ("g", d_g)]
    dws = [_mm_tn(n + "dw_in_" + k, s["xn"], d) for k, d in segs]
    dws[4] = dws[4][:, :IN_WIDTHS[4]]
    below = _shards_from_full("w_in", jnp.concatenate(dws, axis=1))
    d_xn = _mm(n + "d_xn_a", [(d, p["w_" + k]) for k, d in segs[:3]], "nt")
    pairs_b = [(d, p["w_" + k]) for k, d in segs[3:]]
    if first_layer:
        d_xn, (arrived["w_in"],) = _mm(n + "d_xn_b", pairs_b, "nt", res=d_xn, rider=_scatter_rider([below]))
        below = None
    else:
        d_xn = _mm(n + "d_xn_b", pairs_b, "nt", res=d_xn)
    dh_in, dh_in_b, gs["norm1_g"] = _rowwise_bwd(n + "rms1", _f_rms_res, [s["h_in"]], [p["norm1_g"]], [d_xn, dh_mid],
                                                 _ident, [(F32, BF16)], 256)
    for k in ("norm1_g", "norm2_g", "b_gate", "ssd_norm_g", "lru_b_a", "lru_b_x", "lru_lambda"):
        gs[k] = gs[k].reshape(-1)
    return dh_in, dh_in_b, gs, arrived, arrived_above, below, gw


def _step(inp):
    x = inp["x"][0]
    target = inp["loss_target"][0]
    dev = 4 * lax.axis_index("x") + 2 * lax.axis_index("y") + lax.axis_index("c")

    def mine(l, names):
        return [(inp[n][l].T if n in TRANSPOSED else inp[n][l]).astype(BF16) for n in names]

    first = _all_gather("gather_first", mine(0, ("w_in",)) + [inp["lru_conv_w"], inp["ssd_conv_w"]])
    small = {n: inp[n] for n in SMALL}
    small["lru_conv_w"] = jnp.moveaxis(first[1], 0, 2).reshape(DEPTH, 4, -1)
    small["ssd_conv_w"] = jnp.moveaxis(first[2], 0, 2).reshape(DEPTH, 4, -1)
    shards = {"w_in": first[0]}

    h, saved, params = x, [], []
    for l in range(DEPTH):
        p = _small_params(l, small)
        p.update(_in_weights(shards["w_in"]))
        riders = {}
        if l == 0:
            riders["ssd"] = _gather_rider(mine(0, REST))
        else:
            p.update(_rest_weights([shards[n] for n in REST]))
        if l + 1 < DEPTH:
            riders["post"] = _gather_rider(mine(l + 1, ("w_ffn_in", "w_ffn_out")))
            riders["ffn_in"] = _gather_rider(mine(l + 1, ("w_in",)))
            riders["ffn_out"] = _gather_rider(mine(l + 1, ("w_branch", "w_out")))
        h, s, got = _layer_fwd(l, h, p, riders)
        if l + 1 < DEPTH:
            shards = {"w_ffn_in": got["post"][0], "w_ffn_out": got["post"][1], "w_in": got["ffn_in"][0],
                      "w_branch": got["ffn_out"][0], "w_out": got["ffn_out"][1]}
        saved.append(s)
        params.append(p)
    dh, d_nf, loss_acc, dh_b = _loss_head(h, target, small["norm_f"].reshape(1, -1))
    loss = lax.psum(loss_acc[0, 0], ("x", "y", "c"))

    gss, received = [None] * DEPTH, [None] * DEPTH
    handed_down = None
    for l in reversed(range(DEPTH)):
        dh, dh_b, gss[l], received[l], arrived_above, handed_down, _ = _layer_bwd(
            l, dh, dh_b, saved[l], params[l], handed_down, l == 0)
        if arrived_above is not None:
            received[l + 1]["w_in"] = arrived_above
    grad_x = dh[None]
    out = {"loss": loss, "grad_x": grad_x}
    for n in BIG:
        flip = (lambda a: jnp.transpose(a, (0, 2, 1))) if n in TRANSPOSED else (lambda a: a)
        res = _adamw_sum("adamw_" + n, flip(inp[n]), flip(inp["m_" + n]), flip(inp["v_" + n]),
                         [received[l][n] for l in range(DEPTH)])
        out["grad_" + n], out["delta_" + n], out["new_m_" + n], out["new_v_" + n] = [flip(a) for a in res]

    small_full = {n: jnp.stack([gss[l][n] for l in range(DEPTH)]) for n in SMALL if n != "norm_f"}
    small_full["norm_f"] = d_nf.reshape(-1)
    part = _pack([small_full[n] for n in SMALL], F32, SMALL_ROWS)
    (everyone,) = _all_gather("gather_small_grads", [part])
    g_small_flat = _sum_blocks("sum_small_grads", everyone.reshape(-1, LANES), SMALL_ROWS)
    g_small = dict(zip(SMALL, _unpack(g_small_flat, [small_full[n].shape for n in SMALL])))
    for n in ("lru_conv_w", "ssd_conv_w"):
        w = inp[n].shape[-1]
        g_small[n] = lax.dynamic_slice_in_dim(g_small[n], dev * w, w, axis=2)

    shapes = [inp[n].shape for n in SMALL]
    packs = [_pack([src[pre + n] for n in SMALL], F32, SMALL_ROWS)
             for src, pre in ((inp, ""), (g_small, ""), (inp, "m_"), (inp, "v_"))]
    d, m2, v2 = _adamw("adamw_small", *packs)
    for n, dd, mm, vv in zip(SMALL, _unpack(d, shapes), _unpack(m2, shapes), _unpack(v2, shapes)):
        out["grad_" + n] = g_small[n]
        out["delta_" + n], out["new_m_" + n], out["new_v_" + n] = dd, mm, vv
    return out


WEIGHTS = ("norm1_g", "w_in", "b_gate", "lru_conv_w", "lru_conv_b", "lru_w_a", "lru_b_a", "lru_w_x", "lru_b_x",
           "lru_lambda", "ssd_conv_w", "ssd_conv_b", "ssd_dt_bias", "ssd_A_log", "ssd_D", "ssd_norm_g", "w_branch",
           "w_out", "norm2_g", "w_ffn_in", "w_ffn_out", "norm_f")


def kernel(x, norm1_g, w_in, b_gate, lru_conv_w, lru_conv_b, lru_w_a, lru_b_a, lru_w_x, lru_b_x, lru_lambda, ssd_conv_w, ssd_conv_b, ssd_dt_bias, ssd_A_log, ssd_D, ssd_norm_g, w_branch, w_out, norm2_g, w_ffn_in, w_ffn_out, norm_f, loss_target, m_norm1_g, m_w_in, m_b_gate, m_lru_conv_w, m_lru_conv_b, m_lru_w_a, m_lru_b_a, m_lru_w_x, m_lru_b_x, m_lru_lambda, m_ssd_conv_w, m_ssd_conv_b, m_ssd_dt_bias, m_ssd_A_log, m_ssd_D, m_ssd_norm_g, m_w_branch, m_w_out, m_norm2_g, m_w_ffn_in, m_w_ffn_out, m_norm_f, v_norm1_g, v_w_in, v_b_gate, v_lru_conv_w, v_lru_conv_b, v_lru_w_a, v_lru_b_a, v_lru_w_x, v_lru_b_x, v_lru_lambda, v_ssd_conv_w, v_ssd_conv_b, v_ssd_dt_bias, v_ssd_A_log, v_ssd_D, v_ssd_norm_g, v_w_branch, v_w_out, v_norm2_g, v_w_ffn_in, v_w_ffn_out, v_norm_f):
    out = _step(dict(locals()))
    res = [out["loss"], out["grad_x"]]
    for pre in ("grad_", "delta_", "new_m_", "new_v_"):
        res += [out[pre + n] for n in WEIGHTS]
    return tuple(res)
```

```python
import functools

import numpy as np
import jax
import jax.numpy as jnp
from jax import lax
from jax.experimental import pallas as pl
from jax.experimental.pallas import tpu as pltpu

F32 = jnp.float32
BF16 = jnp.bfloat16
HIGHEST = lax.Precision.HIGHEST

D_MODEL = 1024
DEPTH = 2
CHUNK = 64
LRU_C = 8.0
SSD_INNER = 2048
SSD_HEADS = 32
SSD_HEAD_DIM = 64
SSD_GROUPS = 4
SSD_STATE = 128
SSD_CONV_DIM = 3072
D_FF = 2816
EPS = 1e-6
N_DEV = 8
LANES = 128
DT_PAD = LANES
IN_WIDTHS = (1024, 1024, 2048, 3072, 32, 2048)
IN_OFFS = tuple(int(v) for v in np.cumsum((0,) + IN_WIDTHS))

ADAM_LR = 0.001
ADAM_B1 = 0.9
ADAM_B2 = 0.999
ADAM_EPS = 1e-08
ADAM_WD = 0.01
ADAM_STEP = 10

VMEM_LIMIT = 56 * 1024 * 1024
MESH = pl.DeviceIdType.MESH


def _cparams(sem=None):
    return pltpu.CompilerParams(dimension_semantics=sem, vmem_limit_bytes=VMEM_LIMIT)


def _sds(shape, dtype):
    return jax.ShapeDtypeStruct(tuple(shape), dtype)


def _pick(n, cap):
    if n <= cap:
        return n
    best = LANES
    for t in range(LANES, cap + 1, LANES):
        if n % t == 0:
            best = t
    assert n % best == 0, (n, cap)
    return best


def _mm(name, pairs, mode, out_dtype=F32, res=None, rider=None):
    M = pairs[0][0].shape[0]
    N = pairs[0][1].shape[1] if mode == "nn" else pairs[0][1].shape[0]
    npair = len(pairs)
    tm = min(M, 1024 if npair <= 3 else 512)
    tn = _pick(N, 1536 if npair == 1 else 1024)
    tks, nks, starts = [], [], []
    s = 0
    for a, _ in pairs:
        k = a.shape[1]
        tk = _pick(k, 1536)
        tks.append(tk)
        nks.append(k // tk)
        starts.append(s)
        s += k // tk
    nk = s
    dims = (((1,), (0,)), ((), ())) if mode == "nn" else (((1,), (1,)), ((), ()))

    def body(*refs):
        ab = refs[:2 * npair]
        pos = 2 * npair
        r_ref = None
        if res is not None:
            r_ref = refs[pos]
            pos += 1
        o_ref = refs[pos]
        acc = refs[pos + 1] if nk > 1 else None
        k = pl.program_id(2)

        def finish(r):
            if r_ref is not None:
                r = r + r_ref[...].astype(F32)
            o_ref[...] = r.astype(o_ref.dtype)

        for p in range(npair):
            a_ref, b_ref = ab[2 * p], ab[2 * p + 1]
            lo, hi = starts[p], starts[p] + nks[p]

            def step(a_ref=a_ref, b_ref=b_ref, lo=lo, hi=hi):
                d = lax.dot_general(a_ref[...].astype(BF16), b_ref[...].astype(BF16), dims,
                                    preferred_element_type=F32)
                if nk == 1:
                    finish(d)
                    return
                if lo == 0:
                    @pl.when(k == 0)
                    def _():
                        acc[...] = d
                if hi == nk:
                    @pl.when(k == nk - 1)
                    def _():
                        finish(acc[...] + d)
                if max(lo, 1) < min(hi, nk - 1):
                    @pl.when((k > 0) & (k < nk - 1))
                    def _():
                        acc[...] += d

            if npair == 1:
                step()
            else:
                pl.when((k >= lo) & (k < hi))(step)

    in_specs, args = [], []
    for p, (a, b) in enumerate(pairs):
        def kk(k, p=p):
            return jnp.clip(k - starts[p], 0, nks[p] - 1)
        in_specs.append(pl.BlockSpec((tm, tks[p]), lambda i, j, k, kk=kk: (i, kk(k))))
        if mode == "nn":
            in_specs.append(pl.BlockSpec((tks[p], tn), lambda i, j, k, kk=kk: (kk(k), j)))
        else:
            in_specs.append(pl.BlockSpec((tn, tks[p]), lambda i, j, k, kk=kk: (j, kk(k))))
        args += [a, b]
    if res is not None:
        in_specs.append(pl.BlockSpec((tm, tn), lambda i, j, k: (i, j)))
        args.append(res)
    r_args, r_in, r_out, r_shape, r_scratch = _rider_parts(rider)
    grid = (M // tm, N // tn, nk)
    out = pl.pallas_call(
        _with_rider(body, len(args), 1, rider, grid), grid=grid, in_specs=in_specs + r_in,
        out_specs=[pl.BlockSpec((tm, tn), lambda i, j, k: (i, j))] + r_out,
        out_shape=[_sds((M, N), out_dtype)] + r_shape,
        scratch_shapes=([pltpu.VMEM((tm, tn), F32)] if nk > 1 else []) + r_scratch,
        compiler_params=_cparams(("parallel", "parallel", "arbitrary") if rider is None else ("arbitrary",) * 3),
        name=name,
    )(*args, *r_args)
    return out[0] if rider is None else (out[0], out[1:])


def _mm_tn(name, a, b, out_dtype=BF16):
    M, Ka = a.shape
    N = b.shape[1]
    tm = min(M, 2048)
    tka = _pick(Ka, 1024)
    tn = _pick(N, 1024)
    nm = M // tm

    def body(a_ref, b_ref, o_ref, *scratch):
        k = pl.program_id(2)
        d = lax.dot_general(a_ref[...].astype(BF16), b_ref[...].astype(BF16),
                            (((0,), (0,)), ((), ())), preferred_element_type=F32)
        if nm == 1:
            o_ref[...] = d.astype(o_ref.dtype)
            return
        acc = scratch[0]

        @pl.when(k == 0)
        def _():
            acc[...] = d

        @pl.when((k > 0) & (k < nm - 1))
        def _():
            acc[...] += d

        @pl.when(k == nm - 1)
        def _():
            o_ref[...] = (acc[...] + d).astype(o_ref.dtype)

    return pl.pallas_call(
        body, grid=(Ka // tka, N // tn, nm),
        in_specs=[pl.BlockSpec((tm, tka), lambda i, j, k: (k, i)),
                  pl.BlockSpec((tm, tn), lambda i, j, k: (k, j))],
        out_specs=pl.BlockSpec((tka, tn), lambda i, j, k: (i, j)),
        out_shape=_sds((Ka, N), out_dtype),
        scratch_shapes=[pltpu.VMEM((tka, tn), F32)] if nm > 1 else [],
        compiler_params=_cparams(("parallel", "parallel", "arbitrary")), name=name,
    )(a, b)


def _rowwise_fwd(name, fn, rows, pars, outs, tb, rider=None):
    rows = [r if isinstance(r, tuple) else (r, 0, r.shape[1]) for r in rows]
    T = rows[0][0].shape[0]
    tb = min(tb, T)
    nr, npar = len(rows), len(pars)
    r_args, r_in, r_out, r_shape, r_scratch = _rider_parts(rider)

    def body(*refs):
        rv = [r[...].astype(F32) for r in refs[:nr]]
        pv = [p[...] for p in refs[nr:nr + npar]]
        res = fn(*rv, *pv)
        for o, r in zip(refs[nr + npar:], res):
            o[...] = r.astype(o.dtype)

    in_specs = [pl.BlockSpec((tb, w), lambda i, c=c: (i, c)) for _, c, w in rows]
    in_specs += [pl.BlockSpec(p.shape, lambda i, n=p.ndim: (0,) * n) for p in pars]
    return pl.pallas_call(
        _with_rider(body, nr + npar, len(outs), rider, (T // tb,)), grid=(T // tb,), in_specs=in_specs + r_in,
        out_specs=[pl.BlockSpec((tb, w), lambda i: (i, 0)) for w, _ in outs] + r_out,
        out_shape=[_sds((T, w), dt) for w, dt in outs] + r_shape, scratch_shapes=r_scratch,
        compiler_params=_cparams(("parallel",) if rider is None else ("arbitrary",)), name=name,
    )(*[r[0] for r in rows], *pars, *r_args)


def _rowwise_bwd(name, fn, rows, pars, cot_rows, cot_fn, row_out, tb):
    rows = [r if isinstance(r, tuple) else (r, 0, r.shape[1]) for r in rows]
    cot_rows = [r if isinstance(r, tuple) else (r, 0, r.shape[1]) for r in cot_rows]
    T = rows[0][0].shape[0]
    tb = min(tb, T)
    nr, npar, nc = len(rows), len(pars), len(cot_rows)
    want, want_dt = [], []
    for k, dt in enumerate(row_out):
        for d in (() if dt is None else dt if isinstance(dt, tuple) else (dt,)):
            want.append(k)
            want_dt.append(d)

    def body(*refs):
        i = pl.program_id(0)
        rv = [r[...].astype(F32) for r in refs[:nr]]
        pv = [p[...] for p in refs[nr:nr + npar]]
        cv = [c[...].astype(F32) for c in refs[nr + npar:nr + npar + nc]]
        o_refs = refs[nr + npar + nc:]
        _, vjp = jax.vjp(fn, *rv, *pv)
        grads = vjp(tuple(cot_fn(*cv)))
        for o, k in zip(o_refs[:len(want)], want):
            o[...] = grads[k].astype(o.dtype)
        p_refs = o_refs[len(want):]

        @pl.when(i == 0)
        def _():
            for o in p_refs:
                o[...] = jnp.zeros_like(o)

        for o, g in zip(p_refs, grads[nr:]):
            o[...] += g

    in_specs = [pl.BlockSpec((tb, w), lambda i, c=c: (i, c)) for _, c, w in rows]
    in_specs += [pl.BlockSpec(p.shape, lambda i, n=p.ndim: (0,) * n) for p in pars]
    in_specs += [pl.BlockSpec((tb, w), lambda i, c=c: (i, c)) for _, c, w in cot_rows]
    out_specs = [pl.BlockSpec((tb, rows[k][2]), lambda i: (i, 0)) for k in want]
    out_specs += [pl.BlockSpec(p.shape, lambda i, n=p.ndim: (0,) * n) for p in pars]
    out_shape = [_sds((T, rows[k][2]), d) for k, d in zip(want, want_dt)] + [_sds(p.shape, F32) for p in pars]
    return pl.pallas_call(
        body, grid=(T // tb,), in_specs=in_specs, out_specs=out_specs, out_shape=out_shape,
        compiler_params=_cparams(("arbitrary",)), name=name,
    )(*[r[0] for r in rows], *pars, *[r[0] for r in cot_rows])


def _f_rms(x, g):
    r = lax.rsqrt(jnp.mean(x * x, axis=-1, keepdims=True) + EPS)
    return (x * r * g,)


def _f_lru_gates(u, wa, ba, wx, bx, lam):
    ub = u.astype(BF16)
    ra, rx = [], []
    for k in range(D_MODEL // LANES):
        uk = ub[:, LANES * k:LANES * (k + 1)]
        ra.append(jnp.dot(uk, wa[k].astype(BF16), preferred_element_type=F32))
        rx.append(jnp.dot(uk, wx[k].astype(BF16), preferred_element_type=F32))
    r = jax.nn.sigmoid(jnp.concatenate(ra, axis=1) + ba)
    i = jax.nn.sigmoid(jnp.concatenate(rx, axis=1) + bx)
    log_a = -LRU_C * r * jax.nn.softplus(-lam)
    a = jnp.exp(log_a)
    t = jnp.tanh(log_a)
    b = jnp.sqrt(-2.0 * t / (1.0 - t)) * (i * u)
    return a, b


def _f_post(hl, lgate, yssd, xpre_s, z, d_e, ng):
    ya = jax.nn.gelu(lgate) * hl
    y = (yssd + d_e * jax.nn.silu(xpre_s)) * jax.nn.silu(z)
    gw = SSD_INNER // SSD_GROUPS
    parts = []
    for g in range(SSD_GROUPS):
        yg = y[:, gw * g:gw * (g + 1)]
        parts.append(yg * lax.rsqrt(jnp.mean(yg * yg, axis=-1, keepdims=True) + EPS))
    return ya, jnp.concatenate(parts, axis=1) * ng


def _f_merge(m_a, m_b, gates, bg):
    g = jax.nn.sigmoid(gates + bg)
    return (g[:, :D_MODEL] * m_a + g[:, D_MODEL:] * m_b,)


def _f_act(gu):
    return (jax.nn.silu(gu[:, :D_FF]) * gu[:, D_FF:],)


def _ident(*c):
    return c


def _loss_head(h, target, nf):
    T = h.shape[0]
    tb = min(256, T)

    def body(h_ref, t_ref, nf_ref, dh_ref, dnf_ref, loss_ref, dhb_ref):
        i = pl.program_id(0)

        @pl.when(i == 0)
        def _():
            dnf_ref[...] = jnp.zeros_like(dnf_ref)
            loss_ref[...] = jnp.zeros_like(loss_ref)

        (y,), vjp = jax.vjp(_f_rms, h_ref[...], nf_ref[...])
        err = y - t_ref[...]
        dh, dnf = vjp((err * (1.0 / D_MODEL),))
        dh_ref[...] = dh
        dhb_ref[...] = dh.astype(BF16)
        dnf_ref[...] += dnf
        part = 0.5 * jnp.sum(jnp.mean(err * err, axis=-1, keepdims=True), axis=0, keepdims=True)
        loss_ref[...] += jnp.broadcast_to(part, loss_ref.shape)

    return pl.pallas_call(
        body, grid=(T // tb,),
        in_specs=[pl.BlockSpec((tb, D_MODEL), lambda i: (i, 0)), pl.BlockSpec((tb, D_MODEL), lambda i: (i, 0)),
                  pl.BlockSpec((1, D_MODEL), lambda i: (0, 0))],
        out_specs=[pl.BlockSpec((tb, D_MODEL), lambda i: (i, 0)), pl.BlockSpec((1, D_MODEL), lambda i: (0, 0)),
                   pl.BlockSpec((8, LANES), lambda i: (0, 0)), pl.BlockSpec((tb, D_MODEL), lambda i: (i, 0))],
        out_shape=[_sds((T, D_MODEL), F32), _sds((1, D_MODEL), F32), _sds((8, LANES), F32),
                   _sds((T, D_MODEL), BF16)],
        compiler_params=_cparams(("arbitrary",)), name="loss_head",
    )(h, target, nf)


CONV_TC = 1024
HALO = 16


def _conv_fwd(name, x, w8, b):
    T, C = x.shape
    tb = min(512, T)
    nb = tb // HALO

    def body(x_ref, xp_ref, w_ref, b_ref, y_ref, sc):
        i = pl.program_id(0)
        xv = x_ref[...].astype(F32)
        sc[pl.ds(HALO, tb), :] = xv
        sc[pl.ds(0, HALO), :] = jnp.where(i > 0, xp_ref[...].astype(F32), 0.0)
        acc = b_ref[...] + w_ref[3:4, :] * xv
        for k in range(3):
            acc = acc + w_ref[k:k + 1, :] * sc[pl.ds(HALO - 3 + k, tb), :]
        y_ref[...] = acc

    return pl.pallas_call(
        body, grid=(T // tb, C // CONV_TC),
        in_specs=[pl.BlockSpec((tb, CONV_TC), lambda i, j: (i, j)),
                  pl.BlockSpec((HALO, CONV_TC), lambda i, j: (jnp.maximum(i * nb - 1, 0), j)),
                  pl.BlockSpec((8, CONV_TC), lambda i, j: (0, j)),
                  pl.BlockSpec((1, CONV_TC), lambda i, j: (0, j))],
        out_specs=pl.BlockSpec((tb, CONV_TC), lambda i, j: (i, j)),
        out_shape=_sds((T, C), F32),
        scratch_shapes=[pltpu.VMEM((tb + HALO, CONV_TC), F32)],
        compiler_params=_cparams(("parallel", "parallel")), name=name,
    )(x, x, w8, b)


def _conv_bwd(name, x, dy, w8):
    T, C = x.shape
    tb = min(512, T)
    nb = tb // HALO
    nt = T // tb

    def body(x_ref, xp_ref, dy_ref, dyn_ref, w_ref, dx_ref, dwb_ref, scx, scd):
        i = pl.program_id(1)
        dyv = dy_ref[...]
        xv = x_ref[...].astype(F32)
        scx[pl.ds(HALO, tb), :] = xv
        scx[pl.ds(0, HALO), :] = jnp.where(i > 0, xp_ref[...].astype(F32), 0.0)
        scd[pl.ds(0, tb), :] = dyv
        scd[pl.ds(tb, HALO), :] = jnp.where(i < nt - 1, dyn_ref[...], 0.0)
        dx = w_ref[3:4, :] * dyv
        rows = []
        for k in range(3):
            dx = dx + w_ref[k:k + 1, :] * scd[pl.ds(3 - k, tb), :]
            rows.append(jnp.sum(dyv * scx[pl.ds(HALO - 3 + k, tb), :], axis=0, keepdims=True))
        rows.append(jnp.sum(dyv * xv, axis=0, keepdims=True))
        rows.append(jnp.sum(dyv, axis=0, keepdims=True))
        rows.append(jnp.zeros((3, CONV_TC), F32))
        dx_ref[...] = dx.astype(dx_ref.dtype)

        @pl.when(i == 0)
        def _():
            dwb_ref[...] = jnp.zeros_like(dwb_ref)

        dwb_ref[...] += jnp.concatenate(rows, axis=0)

    return pl.pallas_call(
        body, grid=(C // CONV_TC, nt),
        in_specs=[pl.BlockSpec((tb, CONV_TC), lambda j, i: (i, j)),
                  pl.BlockSpec((HALO, CONV_TC), lambda j, i: (jnp.maximum(i * nb - 1, 0), j)),
                  pl.BlockSpec((tb, CONV_TC), lambda j, i: (i, j)),
                  pl.BlockSpec((HALO, CONV_TC), lambda j, i: (jnp.minimum((i + 1) * nb, T // HALO - 1), j)),
                  pl.BlockSpec((8, CONV_TC), lambda j, i: (0, j))],
        out_specs=[pl.BlockSpec((tb, CONV_TC), lambda j, i: (i, j)),
                   pl.BlockSpec((8, CONV_TC), lambda j, i: (0, j))],
        out_shape=[_sds((T, C), BF16), _sds((8, C), F32)],
        scratch_shapes=[pltpu.VMEM((tb + HALO, CONV_TC), F32), pltpu.VMEM((tb + HALO, CONV_TC), F32)],
        compiler_params=_cparams(("parallel", "arbitrary")), name=name,
    )(x, x, dy, dy, w8)


SCAN_TB = 512


def _lru_scan_fwd(a, b):
    T, C = a.shape
    tb = min(SCAN_TB, T)

    def body(a_ref, b_ref, h_ref, hp_ref, carry):
        @pl.when(pl.program_id(0) == 0)
        def _():
            carry[...] = jnp.zeros_like(carry)

        def group(gi, h):
            r0 = pl.multiple_of(gi * 8, 8)
            at = a_ref[pl.ds(r0, 8), :]
            bt = b_ref[pl.ds(r0, 8), :]
            hs, hps = [], []
            for r in range(8):
                hps.append(h)
                h = at[r:r + 1, :] * h + bt[r:r + 1, :]
                hs.append(h)
            h_ref[pl.ds(r0, 8), :] = jnp.concatenate(hs, axis=0)
            hp_ref[pl.ds(r0, 8), :] = jnp.concatenate(hps, axis=0)
            return h

        carry[0:1, :] = lax.fori_loop(0, tb // 8, group, carry[0:1, :])

    spec = pl.BlockSpec((tb, C), lambda i: (i, 0))
    return pl.pallas_call(
        body, grid=(T // tb,), in_specs=[spec, spec], out_specs=[spec, spec],
        out_shape=[_sds((T, C), F32), _sds((T, C), F32)],
        scratch_shapes=[pltpu.VMEM((8, C), F32)],
        compiler_params=_cparams(("arbitrary",)), name="lru_scan_fwd",
    )(a, b)


def _lru_scan_bwd(a, dh):
    T, C = a.shape
    tb = min(SCAN_TB, T)
    nt = T // tb

    def body(a_ref, dh_ref, g_ref, carry):
        @pl.when(pl.program_id(0) == 0)
        def _():
            carry[...] = jnp.zeros_like(carry)

        def group(gi, c):
            r0 = pl.multiple_of((tb // 8 - 1 - gi) * 8, 8)
            at = a_ref[pl.ds(r0, 8), :]
            dt = dh_ref[pl.ds(r0, 8), :]
            gs = [None] * 8
            for r in range(7, -1, -1):
                g = dt[r:r + 1, :] + c
                c = at[r:r + 1, :] * g
                gs[r] = g
            g_ref[pl.ds(r0, 8), :] = jnp.concatenate(gs, axis=0)
            return c

        carry[0:1, :] = lax.fori_loop(0, tb // 8, group, carry[0:1, :])

    spec = pl.BlockSpec((tb, C), lambda i: (nt - 1 - i, 0))
    return pl.pallas_call(
        body, grid=(nt,), in_specs=[spec, spec], out_specs=spec,
        out_shape=_sds((T, C), F32),
        scratch_shapes=[pltpu.VMEM((8, C), F32)],
        compiler_params=_cparams(("arbitrary",)), name="lru_scan_bwd",
    )(a, dh)


def _ssd_chunk(xpre, dtraw, state, dtb, alog):
    xc = jax.nn.silu(xpre)
    xs = xc[:, :SSD_INNER]
    bm = xc[:, SSD_INNER:SSD_INNER + SSD_GROUPS * SSD_STATE]
    cm = xc[:, SSD_INNER + SSD_GROUPS * SSD_STATE:]
    dt = jax.nn.softplus(dtraw + dtb)
    a = dt * (-jnp.exp(alog))
    ltri = (lax.broadcasted_iota(jnp.int32, (CHUNK, CHUNK), 0)
            >= lax.broadcasted_iota(jnp.int32, (CHUNK, CHUNK), 1)).astype(F32)
    a_cs = jnp.dot(ltri, a, precision=HIGHEST, preferred_element_type=F32)
    npair = SSD_HEADS // 2
    pi = lax.broadcasted_iota(jnp.int32, (npair, LANES), 0)
    hi = lax.broadcasted_iota(jnp.int32, (npair, LANES), 1)
    sel_even = (hi == 2 * pi).astype(F32)
    sel_odd = (hi == 2 * pi + 1).astype(F32)
    top = lax.broadcasted_iota(jnp.int32, (2 * CHUNK, LANES), 0) < CHUNK

    def pair_transpose(v):
        v2 = jnp.concatenate([v, v], axis=0)
        dn = (((1,), (1,)), ((), ()))
        return (lax.dot_general(sel_even, jnp.where(top, v2, 0.0), dn, precision=HIGHEST, preferred_element_type=F32)
                + lax.dot_general(sel_odd, jnp.where(top, 0.0, v2), dn, precision=HIGHEST, preferred_element_type=F32))

    a_t2 = pair_transpose(a_cs)
    dt_t2 = pair_transpose(dt)
    a_last = a_cs[CHUNK - 1:CHUNK, :]
    dte = jnp.exp(a_last - a_cs) * dt
    cd = jnp.exp(a_last)
    lane = lax.broadcasted_iota(jnp.int32, (CHUNK, LANES), 1)
    left = lane < SSD_HEAD_DIM
    right = jnp.logical_not(left)
    left1 = left[0:1]
    tril2 = lax.broadcasted_iota(jnp.int32, (CHUNK, LANES), 0) >= (lane & (SSD_HEAD_DIM - 1))
    gw = SSD_INNER // SSD_GROUPS
    ys, news = [], []
    for g in range(SSD_GROUPS):
        bg = bm[:, SSD_STATE * g:SSD_STATE * (g + 1)].astype(BF16)
        cg = cm[:, SSD_STATE * g:SSD_STATE * (g + 1)].astype(BF16)
        bg2 = jnp.concatenate([bg, bg], axis=0)
        scores2 = lax.dot_general(cg, bg2, (((1,), (1,)), ((), ())), preferred_element_type=F32)
        coff = jnp.dot(cg, state[:, gw * g:gw * (g + 1)].astype(BF16), preferred_element_type=F32)
        for j in range(gw // LANES):
            lo = gw * g + LANES * j
            p = lo // LANES
            h0 = 2 * p
            xp = xs[:, lo:lo + LANES]
            col = jnp.where(left, a_cs[:, h0:h0 + 1], a_cs[:, h0 + 1:h0 + 2])
            dm = jnp.exp(jnp.where(tril2, col - a_t2[p:p + 1, :], -1e30)) * dt_t2[p:p + 1, :]
            sd = (scores2 * dm).astype(BF16)
            x_bd = jnp.concatenate([jnp.where(left, xp, 0.0), jnp.where(right, xp, 0.0)], axis=0).astype(BF16)
            acc = jnp.dot(sd, x_bd, preferred_element_type=F32)
            ys.append(acc + coff[:, LANES * j:LANES * (j + 1)] * jnp.exp(col))
            dtee = jnp.where(left, dte[:, h0:h0 + 1], dte[:, h0 + 1:h0 + 2])
            xw = (xp * dtee).astype(BF16)
            cde = jnp.where(left1, cd[:, h0:h0 + 1], cd[:, h0 + 1:h0 + 2])
            news.append(state[:, lo:lo + LANES] * cde
                        + lax.dot_general(bg, xw, (((0,), (0,)), ((), ())), preferred_element_type=F32))
    return jnp.concatenate(ys, axis=1), jnp.concatenate(news, axis=1)


def _ssd_fwd(name, xpre, dtraw, dtb, alog, rider=None):
    T = xpre.shape[0]
    n = T // CHUNK
    r_args, r_in, r_out, r_shape, r_scratch = _rider_parts(rider)

    def body(xp, dr, dtb_ref, al_ref, y_ref, st_ref, state):
        @pl.when(pl.program_id(0) == 0)
        def _():
            state[...] = jnp.zeros_like(state)

        st_ref[0] = state[...]
        y, new = _ssd_chunk(xp[...], dr[...], state[...], dtb_ref[...], al_ref[...])
        y_ref[...] = y
        state[...] = new

    small = pl.BlockSpec((1, LANES), lambda c: (0, 0))
    return pl.pallas_call(
        _with_rider(body, 4, 2, rider, (n,)), grid=(n,),
        in_specs=[pl.BlockSpec((CHUNK, SSD_CONV_DIM), lambda c: (c, 0)),
                  pl.BlockSpec((CHUNK, DT_PAD), lambda c: (c, 0)), small, small] + r_in,
        out_specs=[pl.BlockSpec((CHUNK, SSD_INNER), lambda c: (c, 0)),
                   pl.BlockSpec((1, SSD_STATE, SSD_INNER), lambda c: (c, 0, 0))] + r_out,
        out_shape=[_sds((T, SSD_INNER), F32), _sds((n, SSD_STATE, SSD_INNER), F32)] + r_shape,
        scratch_shapes=[pltpu.VMEM((SSD_STATE, SSD_INNER), F32)] + r_scratch,
        compiler_params=_cparams(("arbitrary",)), name=name,
    )(xpre, dtraw, dtb, alog, *r_args)


def _ssd_bwd(name, xpre, dtraw, states, dy, dxs_extra, dtb, alog, rider=None):
    T = xpre.shape[0]
    n = T // CHUNK
    r_args, r_in, r_out, r_shape, r_scratch = _rider_parts(rider)

    def body(xp, dr, st, dy_ref, dx_ref, dtb_ref, al_ref, dxp_ref, ddr_ref, ddtb_ref, dal_ref, dstate):
        @pl.when(pl.program_id(0) == 0)
        def _():
            dstate[...] = jnp.zeros_like(dstate)
            ddtb_ref[...] = jnp.zeros_like(ddtb_ref)
            dal_ref[...] = jnp.zeros_like(dal_ref)

        _, vjp = jax.vjp(_ssd_chunk, xp[...], dr[...], st[0], dtb_ref[...], al_ref[...])
        dxp, ddr, ds, db, da = vjp((dy_ref[...], dstate[...]))
        dxp_ref[:, :SSD_INNER] = dxp[:, :SSD_INNER] + dx_ref[...]
        dxp_ref[:, SSD_INNER:] = dxp[:, SSD_INNER:]
        ddr_ref[...] = ddr.astype(ddr_ref.dtype)
        dstate[...] = ds
        ddtb_ref[...] += db
        dal_ref[...] += da

    def rev(c):
        return (n - 1 - c, 0)

    small = pl.BlockSpec((1, LANES), lambda c: (0, 0))
    return pl.pallas_call(
        _with_rider(body, 7, 4, rider, (n,)), grid=(n,),
        in_specs=[pl.BlockSpec((CHUNK, SSD_CONV_DIM), rev), pl.BlockSpec((CHUNK, DT_PAD), rev),
                  pl.BlockSpec((1, SSD_STATE, SSD_INNER), lambda c: (n - 1 - c, 0, 0)),
                  pl.BlockSpec((CHUNK, SSD_INNER), rev), pl.BlockSpec((CHUNK, SSD_INNER), rev), small, small] + r_in,
        out_specs=[pl.BlockSpec((CHUNK, SSD_CONV_DIM), rev), pl.BlockSpec((CHUNK, DT_PAD), rev), small, small] + r_out,
        out_shape=[_sds((T, SSD_CONV_DIM), F32), _sds((T, DT_PAD), BF16), _sds((1, LANES), F32),
                   _sds((1, LANES), F32)] + r_shape,
        scratch_shapes=[pltpu.VMEM((SSD_STATE, SSD_INNER), F32)] + r_scratch,
        compiler_params=_cparams(("arbitrary",)), name=name,
    )(xpre, dtraw, states, dy, dxs_extra, dtb, alog, *r_args)


HBM_SPEC = pl.BlockSpec(memory_space=pltpu.HBM)
N_PEER = N_DEV - 1


def _position():
    return lax.axis_index("x"), lax.axis_index("y"), lax.axis_index("c")


def _all_gather(name, blks):
    return _exchange_call(name, _gather_phases, blks, [_sds((N_DEV,) + b.shape, b.dtype) for b in blks])


def _scatter_exchange(name, gs):
    return _exchange_call(name, _scatter_phases, gs, [_sds(g.shape, g.dtype) for g in gs])


def _exchange_scratch(na):
    return [pltpu.SemaphoreType.DMA((na * N_PEER,)), pltpu.SemaphoreType.DMA((na * N_PEER,)),
            pltpu.SemaphoreType.DMA((na,))]


def _exchange_call(name, phases, arrays, out_shape):
    na = len(arrays)

    def body(*refs):
        begin, finish = phases(refs[:na], refs[na:2 * na], *refs[2 * na:])
        begin()
        finish()

    return pl.pallas_call(
        body, out_shape=out_shape, in_specs=[HBM_SPEC] * na, out_specs=[HBM_SPEC] * na,
        scratch_shapes=_exchange_scratch(na), name=name,
    )(*arrays)


def _gather_phases(x_refs, out_refs, send_sems, recv_sems, local_sems):
    na = len(x_refs)
    x, y, c = _position()
    me, sibling = (x, y, c), (x, y, 1 - c)
    chips = [(1 - x, y), (x, 1 - y), (1 - x, 1 - y)]

    def slot(a, px, py, pc):
        return out_refs[a].at[4 * px + 2 * py + pc]

    def copy(a, k, block, to, src=None):
        return pltpu.make_async_remote_copy(
            src_ref=slot(a, *block) if src is None else src, dst_ref=slot(a, *block),
            send_sem=send_sems.at[a * N_PEER + k], recv_sem=recv_sems.at[a * N_PEER + k],
            device_id=to, device_id_type=MESH)

    mine = [pltpu.make_async_copy(x_refs[a], slot(a, *me), local_sems.at[a]) for a in range(na)]
    first = []
    for a in range(na):
        first.append(copy(a, 0, me, sibling, src=x_refs[a]))
        first += [copy(a, 1 + j, me, (*chip, c), src=x_refs[a]) for j, chip in enumerate(chips)]
    passed = [copy(a, 4 + j, (*chip, c), sibling) for j, chip in enumerate(chips) for a in range(na)]

    def begin():
        for cp in mine + first:
            cp.start()

    def finish():
        for j, chip in enumerate(chips):
            for a in range(na):
                copy(a, 1 + j, (*chip, c), me).wait_recv()
                passed[j * na + a].start()
        for a in range(na):
            copy(a, 0, sibling, me).wait_recv()
            for j, chip in enumerate(chips):
                copy(a, 4 + j, (*chip, 1 - c), me).wait_recv()
        for cp in first + passed:
            cp.wait_send()
        for cp in mine:
            cp.wait()

    return begin, finish


def _scatter_phases(g_refs, q_refs, send_sems, recv_sems, local_sems):
    na = len(g_refs)
    x, y, c = _position()
    me = 4 * x + 2 * y + c
    mine, sends, recvs = [], [], []
    for a in range(na):
        mine.append(pltpu.make_async_copy(g_refs[a].at[me], q_refs[a].at[me], local_sems.at[a]))
        for k in range(1, N_DEV):
            px, py, pc = (x + (k >> 2)) % 2, (y + ((k >> 1) & 1)) % 2, (c + (k & 1)) % 2
            peer = 4 * px + 2 * py + pc
            sem = a * N_PEER + k - 1
            sends.append(pltpu.make_async_remote_copy(
                src_ref=g_refs[a].at[peer], dst_ref=q_refs[a].at[me], send_sem=send_sems.at[sem],
                recv_sem=recv_sems.at[sem], device_id=(px, py, pc), device_id_type=MESH))
            recvs.append(pltpu.make_async_remote_copy(
                src_ref=g_refs[a].at[me], dst_ref=q_refs[a].at[peer], send_sem=send_sems.at[sem],
                recv_sem=recv_sems.at[sem], device_id=(px, py, pc), device_id_type=MESH))

    def begin():
        for cp in mine + sends:
            cp.start()

    def finish():
        for cp in recvs:
            cp.wait_recv()
        for cp in sends:
            cp.wait_send()
        for cp in mine:
            cp.wait()

    return begin, finish


def _with_rider(body, n_in, n_out, rider, grid):
    if rider is None:
        return body
    phases, arrays, _ = rider
    na = len(arrays)

    def carried(*refs):
        ins, r_in = refs[:n_in], refs[n_in:n_in + na]
        outs = refs[n_in + na:n_in + na + n_out]
        r_out = refs[n_in + na + n_out:n_in + 2 * na + n_out]
        scratch, r_scratch = refs[n_in + 2 * na + n_out:-3], refs[-3:]
        begin, finish = phases(r_in, r_out, *r_scratch)
        first = last = None
        for d, g in enumerate(grid):
            at_start, at_end = pl.program_id(d) == 0, pl.program_id(d) == g - 1
            first = at_start if first is None else first & at_start
            last = at_end if last is None else last & at_end
        pl.when(first)(begin)
        body(*ins, *outs, *scratch)
        pl.when(last)(finish)

    return carried


def _rider_parts(rider):
    if rider is None:
        return [], [], [], [], []
    _, arrays, out_shape = rider
    na = len(arrays)
    return list(arrays), [HBM_SPEC] * na, [HBM_SPEC] * na, list(out_shape), _exchange_scratch(na)


def _gather_rider(blks):
    return (_gather_phases, blks, [_sds((N_DEV,) + b.shape, b.dtype) for b in blks])


def _scatter_rider(gs):
    return (_scatter_phases, gs, [_sds(g.shape, g.dtype) for g in gs])


def _adamw_sum(name, w, m, v, recvs, rider=None):
    L, r, c = w.shape
    tr = r if r <= 512 else max(t for t in range(16, 257, 16) if r % t == 0)
    assert len(recvs) == L
    nr = r // tr

    def body(w_ref, m_ref, v_ref, *refs):
        q_refs = refs[:L]
        g_ref, d_ref, m2_ref, v2_ref = refs[L:]
        l = pl.program_id(0)
        for ll in range(L):
            @pl.when(l == ll)
            def _(q_ref=q_refs[ll]):
                gv = q_ref[0].astype(F32)
                for s in range(1, N_DEV):
                    gv = gv + q_ref[s].astype(F32)
                m2 = ADAM_B1 * m_ref[0] + (1.0 - ADAM_B1) * gv
                v2 = ADAM_B2 * v_ref[0] + (1.0 - ADAM_B2) * jnp.square(gv)
                m_hat = m2 / (1.0 - ADAM_B1 ** ADAM_STEP)
                v_hat = v2 / (1.0 - ADAM_B2 ** ADAM_STEP)
                g_ref[0] = gv
                d_ref[0] = -ADAM_LR * (m_hat / (jnp.sqrt(v_hat) + ADAM_EPS) + ADAM_WD * w_ref[0])
                m2_ref[0] = m2
                v2_ref[0] = v2

    def q_index(ll):
        return lambda l, i: (0, jnp.where(l == ll, i, jnp.where(l > ll, nr - 1, 0)), 0)

    spec = pl.BlockSpec((1, tr, c), lambda l, i: (l, i, 0))
    r_args, r_in, r_out, r_shape, r_scratch = _rider_parts(rider)
    return pl.pallas_call(
        _with_rider(body, 3 + L, 4, rider, (L, nr)), grid=(L, nr),
        in_specs=[spec] * 3 + [pl.BlockSpec((N_DEV, tr, c), q_index(ll)) for ll in range(L)] + r_in,
        out_specs=[spec] * 4 + r_out, out_shape=[_sds((L, r, c), F32)] * 4 + r_shape, scratch_shapes=r_scratch,
        compiler_params=_cparams(("arbitrary", "arbitrary")), name=name,
    )(w, m, v, *recvs, *r_args)


def _sum_blocks(name, q, tr):
    R = q.shape[0] // N_DEV
    W = q.shape[1]
    tr = min(tr, R)
    assert R % tr == 0
    nb = R // tr

    def body(*refs):
        acc = refs[0][...].astype(F32)
        for r in refs[1:N_DEV]:
            acc = acc + r[...].astype(F32)
        refs[N_DEV][...] = acc

    return pl.pallas_call(
        body, grid=(nb,),
        in_specs=[pl.BlockSpec((tr, W), lambda i, s=s: (s * nb + i, 0)) for s in range(N_DEV)],
        out_specs=pl.BlockSpec((tr, W), lambda i: (i, 0)), out_shape=_sds((R, W), F32),
        compiler_params=_cparams(("parallel",)), name=name,
    )(*([q] * N_DEV))


def _adamw(name, w, g, m, v):
    R, C = w.shape
    tr = R
    if R > 512:
        tr = max(t for t in range(8, 513, 8) if R % t == 0)

    def body(w_ref, g_ref, m_ref, v_ref, d_ref, m2_ref, v2_ref):
        gv = g_ref[...]
        m2 = ADAM_B1 * m_ref[...] + (1.0 - ADAM_B1) * gv
        v2 = ADAM_B2 * v_ref[...] + (1.0 - ADAM_B2) * jnp.square(gv)
        m_hat = m2 / (1.0 - ADAM_B1 ** ADAM_STEP)
        v_hat = v2 / (1.0 - ADAM_B2 ** ADAM_STEP)
        d_ref[...] = -ADAM_LR * (m_hat / (jnp.sqrt(v_hat) + ADAM_EPS) + ADAM_WD * w_ref[...])
        m2_ref[...] = m2
        v2_ref[...] = v2

    spec = pl.BlockSpec((tr, C), lambda i: (i, 0))
    return pl.pallas_call(
        body, grid=(R // tr,), in_specs=[spec] * 4, out_specs=[spec] * 3,
        out_shape=[_sds((R, C), F32)] * 3, compiler_params=_cparams(("parallel",)), name=name,
    )(w, g, m, v)


BIG = ("w_in", "w_branch", "w_out", "w_ffn_in", "w_ffn_out")
BIG_ROW_SHARDED = {"w_in": False, "w_branch": True, "w_out": True, "w_ffn_in": True, "w_ffn_out": True}
TRANSPOSED = ("w_ffn_in",)
SMALL = ("norm1_g", "b_gate", "lru_conv_w", "lru_conv_b", "lru_w_a", "lru_b_a", "lru_w_x", "lru_b_x", "lru_lambda",
         "ssd_conv_w", "ssd_conv_b", "ssd_dt_bias", "ssd_A_log", "ssd_D", "ssd_norm_g", "norm2_g", "norm_f")
SMALL_ROWS = 256


def _pack(arrs, dtype, row_mult):
    parts = []
    for a in arrs:
        f = a.reshape(-1).astype(dtype)
        pad = (-f.shape[0]) % LANES
        if pad:
            f = jnp.concatenate([f, jnp.zeros((pad,), dtype)])
        parts.append(f)
    f = jnp.concatenate(parts)
    pad = (-f.shape[0]) % (LANES * row_mult)
    if pad:
        f = jnp.concatenate([f, jnp.zeros((pad,), dtype)])
    return f.reshape(-1, LANES)


def _unpack(flat, shapes, lead=()):
    f = flat.reshape(lead + (-1,))
    out, off = [], 0
    for s in shapes:
        n = int(np.prod(s))
        out.append(f[..., off:off + n].reshape(lead + tuple(s)))
        off += n + (-n) % LANES
    return out


def _full_from_shards(name, st):
    if BIG_ROW_SHARDED[name]:
        return st.reshape((-1,) + st.shape[2:])
    return jnp.transpose(st, (1, 0, 2)).reshape(st.shape[1], -1)


def _shards_from_full(name, full):
    if BIG_ROW_SHARDED[name]:
        return full.reshape((N_DEV, full.shape[0] // N_DEV) + full.shape[1:])
    return jnp.transpose(full.reshape(full.shape[0], N_DEV, -1), (1, 0, 2))


def _block_diag_tiles(w):
    z = jnp.zeros((8, 64, 64), w.dtype)
    w2 = w.reshape(8, 2, 64, 64)
    top = jnp.concatenate([w2[:, 0], z], axis=2)
    bot = jnp.concatenate([z, w2[:, 1]], axis=2)
    return jnp.concatenate([top, bot], axis=1)


def _block_diag_untile(t):
    return jnp.stack([t[:, :64, :64], t[:, 64:, 64:]], axis=1).reshape(16, 64, 64)


def _pad_lanes(a, width):
    return jnp.concatenate([a, jnp.zeros(a.shape[:-1] + (width - a.shape[-1],), a.dtype)], axis=-1)


def _pad_rows8(w):
    return jnp.concatenate([w, jnp.zeros((8 - w.shape[0],) + w.shape[1:], w.dtype)], axis=0)


REST = BIG[1:]


def _in_weights(shards):
    w_in = _full_from_shards("w_in", shards)
    seg = [w_in[:, IN_OFFS[k]:IN_OFFS[k + 1]] for k in range(6)]
    return {"w_lx": seg[0], "w_lg": seg[1], "w_z": seg[2], "w_xbc": seg[3], "w_dt": _pad_lanes(seg[4], DT_PAD),
            "w_g": seg[5]}


def _rest_weights(shards):
    full = {n: _full_from_shards(n, st) for n, st in zip(REST, shards)}
    return {"w_bra": full["w_branch"][:D_MODEL], "w_brb": full["w_branch"][D_MODEL:], "w_out": full["w_out"],
            "w_ffn_in_t": full["w_ffn_in"], "w_ffn_out": full["w_ffn_out"]}


def _small_params(l, small):
    p = {}
    for n in ("norm1_g", "b_gate", "lru_conv_b", "lru_b_a", "lru_b_x", "lru_lambda", "ssd_conv_b", "ssd_norm_g", "norm2_g"):
        p[n] = small[n][l].reshape(1, -1)
    p["lru_conv_w8"] = _pad_rows8(small["lru_conv_w"][l])
    p["ssd_conv_w8"] = _pad_rows8(small["ssd_conv_w"][l])
    p["wa"] = _block_diag_tiles(small["lru_w_a"][l])
    p["wx"] = _block_diag_tiles(small["lru_w_x"][l])
    p["dtb"] = _pad_lanes(small["ssd_dt_bias"][l].reshape(1, -1), DT_PAD)
    p["alog"] = _pad_lanes(small["ssd_A_log"][l].reshape(1, -1), DT_PAD)
    p["d_e"] = jnp.repeat(small["ssd_D"][l], SSD_HEAD_DIM).reshape(1, -1)
    return p


def _layer_fwd(l, h, p, riders):
    n = f"l{l}_"
    s = {"h_in": h}
    got = {}
    (xn,) = _rowwise_fwd(n + "rms1", _f_rms, [h], [p["norm1_g"]], [(D_MODEL, BF16)], 256)
    s["xn"] = xn
    for k in ("lx", "lg", "z", "xbc", "g", "dt"):
        s[k] = _mm(n + "in_" + k, [(xn, p["w_" + k])], "nn", out_dtype=F32 if k == "dt" else BF16)
    s["u"] = _conv_fwd(n + "lru_conv", s["lx"], p["lru_conv_w8"], p["lru_conv_b"])
    lru_pars = [p["wa"], p["lru_b_a"], p["wx"], p["lru_b_x"], p["lru_lambda"]]
    s["a"], b = _rowwise_fwd(n + "lru_gates", _f_lru_gates, [s["u"]], lru_pars, [(D_MODEL, F32)] * 2, 256)
    s["hl"], s["hprev"] = _lru_scan_fwd(s["a"], b)
    s["xpre"] = _conv_fwd(n + "ssd_conv", s["xbc"], p["ssd_conv_w8"], p["ssd_conv_b"])
    s["yssd"], s["states"], *got["ssd"] = _ssd_fwd(n + "ssd", s["xpre"], s["dt"], p["dtb"], p["alog"],
                                                   riders.get("ssd"))
    if "ssd" in riders:
        p.update(_rest_weights(got["ssd"]))
    post_rows = [s["hl"], s["lg"], s["yssd"], (s["xpre"], 0, SSD_INNER), s["z"]]
    s["ya"], s["yb"], *got["post"] = _rowwise_fwd(n + "post", _f_post, post_rows, [p["d_e"], p["ssd_norm_g"]],
                                                  [(D_MODEL, BF16), (SSD_INNER, BF16)], 128, riders.get("post"))
    s["ma"] = _mm(n + "br_a", [(s["ya"], p["w_bra"])], "nn", out_dtype=BF16)
    s["mb"] = _mm(n + "br_b", [(s["yb"], p["w_brb"])], "nn", out_dtype=BF16)
    (s["merged"],) = _rowwise_fwd(n + "merge", _f_merge, [s["ma"], s["mb"], s["g"]], [p["b_gate"]],
                                  [(D_MODEL, BF16)], 256)
    s["h_mid"] = _mm(n + "out", [(s["merged"], p["w_out"])], "nn", res=h)
    (s["xn2"],) = _rowwise_fwd(n + "rms2", _f_rms, [s["h_mid"]], [p["norm2_g"]], [(D_MODEL, BF16)], 256)
    s["gu"] = _mm(n + "ffn_in", [(s["xn2"], p["w_ffn_in_t"])], "nt", out_dtype=BF16, rider=riders.get("ffn_in"))
    if "ffn_in" in riders:
        s["gu"], got["ffn_in"] = s["gu"]
    (s["act"],) = _rowwise_fwd(n + "act", _f_act, [s["gu"]], [], [(D_FF, BF16)], 256)
    h_out = _mm(n + "ffn_out", [(s["act"], p["w_ffn_out"])], "nn", res=s["h_mid"], rider=riders.get("ffn_out"))
    if "ffn_out" in riders:
        h_out, got["ffn_out"] = h_out
    return h_out, s, got


def _f_rms_res(x, g):
    return _f_rms(x, g)[0], x


def _layer_bwd(l, dh, dh_b, s, p, above, first_layer):
    n = f"l{l}_b_"
    gw, gs = {}, {}
    d_act = _mm(n + "d_act", [(dh_b, p["w_ffn_out"])], "nt", out_dtype=BF16)
    gw["w_ffn_out"] = _mm_tn(n + "dw_ffn_out", s["act"], dh_b)
    (d_gu,) = _rowwise_bwd(n + "act", _f_act, [s["gu"]], [], [d_act], _ident, [BF16], 128)
    d_xn2 = _mm(n + "d_xn2", [(d_gu, p["w_ffn_in_t"])], "nn")
    gw["w_ffn_in"] = _mm_tn(n + "dw_ffn_in", d_gu, s["xn2"])
    dh_mid, dh_mid_b, gs["norm2_g"] = _rowwise_bwd(n + "rms2", _f_rms_res, [s["h_mid"]], [p["norm2_g"]], [d_xn2, dh],
                                                   _ident, [(F32, BF16)], 256)
    d_merged = _mm(n + "d_merged", [(dh_mid_b, p["w_out"])], "nt", out_dtype=BF16)
    gw["w_out"] = _mm_tn(n + "dw_out", s["merged"], dh_mid_b)
    d_ma, d_mb, d_g, gs["b_gate"] = _rowwise_bwd(n + "merge", _f_merge, [s["ma"], s["mb"], s["g"]], [p["b_gate"]],
                                                 [d_merged], _ident, [BF16, BF16, BF16], 256)
    d_ya = _mm(n + "d_ya", [(d_ma, p["w_bra"])], "nt", out_dtype=BF16)
    d_yb = _mm(n + "d_yb", [(d_mb, p["w_brb"])], "nt", out_dtype=BF16)
    gw["w_branch"] = jnp.concatenate([_mm_tn(n + "dw_bra", s["ya"], d_ma), _mm_tn(n + "dw_brb", s["yb"], d_mb)], axis=0)
    post_rows = [s["hl"], s["lg"], s["yssd"], (s["xpre"], 0, SSD_INNER), s["z"]]
    d_hl, d_lg, d_yssd, d_xs, d_z, d_de, gs["ssd_norm_g"] = _rowwise_bwd(
        n + "post", _f_post, post_rows, [p["d_e"], p["ssd_norm_g"]], [d_ya, d_yb], _ident,
        [F32, BF16, F32, F32, BF16], 64)
    gs["ssd_D"] = d_de.reshape(SSD_HEADS, SSD_HEAD_DIM).sum(axis=1)
    contrib = [_shards_from_full(k, gw[k]) for k in REST] + ([] if above is None else [above])
    d_xpre, d_dt, d_dtb, d_alog, *carried = _ssd_bwd(n + "ssd", s["xpre"], s["dt"], s["states"], d_yssd, d_xs,
                                                     p["dtb"], p["alog"], _scatter_rider(contrib))
    arrived = dict(zip(REST, carried))
    arrived_above = None if above is None else carried[len(REST)]
    gs["ssd_dt_bias"] = d_dtb[0, :SSD_HEADS]
    gs["ssd_A_log"] = d_alog[0, :SSD_HEADS]
    d_xbc, dwb = _conv_bwd(n + "ssd_conv", s["xbc"], d_xpre, p["ssd_conv_w8"])
    gs["ssd_conv_w"], gs["ssd_conv_b"] = dwb[:4], dwb[4]
    g_scan = _lru_scan_bwd(s["a"], d_hl)
    lru_pars = [p["wa"], p["lru_b_a"], p["wx"], p["lru_b_x"], p["lru_lambda"]]
    d_u, d_wa, gs["lru_b_a"], d_wx, gs["lru_b_x"], gs["lru_lambda"] = _rowwise_bwd(
        n + "lru_gates", _f_lru_gates, [s["u"]], lru_pars, [g_scan, s["hprev"]],
        lambda g, hp: (g * hp, g), [F32], 128)
    gs["lru_w_a"], gs["lru_w_x"] = _block_diag_untile(d_wa), _block_diag_untile(d_wx)
    d_lx, dwb = _conv_bwd(n + "lru_conv", s["lx"], d_u, p["lru_conv_w8"])
    gs["lru_conv_w"], gs["lru_conv_b"] = dwb[:4], dwb[4]
    segs = [("lx", d_lx), ("lg", d_lg), ("z", d_z), ("xbc", d_xbc), ("dt", d_dt), ("g", d_g)]
    dws = [_mm_tn(n + "dw_in_" + k, s["xn"], d) for k, d in segs]
    dws[4] = dws[4][:, :IN_WIDTHS[4]]
    below = _shards_from_full("w_in", jnp.concatenate(dws, axis=1))
    d_xn = _mm(n + "d_xn_a", [(d, p["w_" + k]) for k, d in segs[:3]], "nt")
    pairs_b = [(d, p["w_" + k]) for k, d in segs[3:]]
    if first_layer:
        d_xn, (arrived["w_in"],) = _mm(n + "d_xn_b", pairs_b, "nt", res=d_xn, rider=_scatter_rider([below]))
        below = None
    else:
        d_xn = _mm(n + "d_xn_b", pairs_b, "nt", res=d_xn)
    dh_in, dh_in_b, gs["norm1_g"] = _rowwise_bwd(n + "rms1", _f_rms_res, [s["h_in"]], [p["norm1_g"]], [d_xn, dh_mid],
                                                 _ident, [(F32, BF16)], 256)
    for k in ("norm1_g", "norm2_g", "b_gate", "ssd_norm_g", "lru_b_a", "lru_b_x", "lru_lambda"):
        gs[k] = gs[k].reshape(-1)
    return dh_in, dh_in_b, gs, arrived, arrived_above, below, gw


def _step(inp):
    x = inp["x"][0]
    target = inp["loss_target"][0]
    dev = 4 * lax.axis_index("x") + 2 * lax.axis_index("y") + lax.axis_index("c")

    def mine(l, names):
        return [(inp[n][l].T if n in TRANSPOSED else inp[n][l]).astype(BF16) for n in names]

    first = _all_gather("gather_first", mine(0, ("w_in",)) + [inp["lru_conv_w"], inp["ssd_conv_w"]])
    small = {n: inp[n] for n in SMALL}
    small["lru_conv_w"] = jnp.moveaxis(first[1], 0, 2).reshape(DEPTH, 4, -1)
    small["ssd_conv_w"] = jnp.moveaxis(first[2], 0, 2).reshape(DEPTH, 4, -1)
    shards = {"w_in": first[0]}

    h, saved, params = x, [], []
    for l in range(DEPTH):
        p = _small_params(l, small)
        p.update(_in_weights(shards["w_in"]))
        riders = {}
        if l == 0:
            riders["ssd"] = _gather_rider(mine(0, REST))
        else:
            p.update(_rest_weights([shards[n] for n in REST]))
        if l + 1 < DEPTH:
            riders["post"] = _gather_rider(mine(l + 1, ("w_ffn_in", "w_ffn_out")))
            riders["ffn_in"] = _gather_rider(mine(l + 1, ("w_in",)))
            riders["ffn_out"] = _gather_rider(mine(l + 1, ("w_branch", "w_out")))
        h, s, got = _layer_fwd(l, h, p, riders)
        if l + 1 < DEPTH:
            shards = {"w_ffn_in": got["post"][0], "w_ffn_out": got["post"][1], "w_in": got["ffn_in"][0],
                      "w_branch": got["ffn_out"][0], "w_out": got["ffn_out"][1]}
        saved.append(s)
        params.append(p)
    dh, d_nf, loss_acc, dh_b = _loss_head(h, target, small["norm_f"].reshape(1, -1))
    loss = lax.psum(loss_acc[0, 0], ("x", "y", "c"))

    gss, received = [None] * DEPTH, [None] * DEPTH
    handed_down = None
    for l in reversed(range(DEPTH)):
        dh, dh_b, gss[l], received[l], arrived_above, handed_down, _ = _layer_bwd(
            l, dh, dh_b, saved[l], params[l], handed_down, l == 0)
        if arrived_above is not None:
            received[l + 1]["w_in"] = arrived_above
    grad_x = dh[None]
    out = {"loss": loss, "grad_x": grad_x}
    small_full = {n: jnp.stack([gss[l][n] for l in range(DEPTH)]) for n in SMALL if n != "norm_f"}
    small_full["norm_f"] = d_nf.reshape(-1)
    part = _pack([small_full[n] for n in SMALL], F32, SMALL_ROWS)
    for n in BIG:
        flip = (lambda a: jnp.transpose(a, (0, 2, 1))) if n in TRANSPOSED else (lambda a: a)
        res = _adamw_sum("adamw_" + n, flip(inp[n]), flip(inp["m_" + n]), flip(inp["v_" + n]),
                         [received[l][n] for l in range(DEPTH)], _gather_rider([part]) if n == "w_in" else None)
        if n == "w_in":
            everyone = res[4]
        out["grad_" + n], out["delta_" + n], out["new_m_" + n], out["new_v_" + n] = [flip(a) for a in res[:4]]
    g_small_flat = _sum_blocks("sum_small_grads", everyone.reshape(-1, LANES), SMALL_ROWS)
    g_small = dict(zip(SMALL, _unpack(g_small_flat, [small_full[n].shape for n in SMALL])))
    for n in ("lru_conv_w", "ssd_conv_w"):
        w = inp[n].shape[-1]
        g_small[n] = lax.dynamic_slice_in_dim(g_small[n], dev * w, w, axis=2)

    shapes = [inp[n].shape for n in SMALL]
    packs = [_pack([src[pre + n] for n in SMALL], F32, SMALL_ROWS)
             for src, pre in ((inp, ""), (g_small, ""), (inp, "m_"), (inp, "v_"))]
    d, m2, v2 = _adamw("adamw_small", *packs)
    for n, dd, mm, vv in zip(SMALL, _unpack(d, shapes), _unpack(m2, shapes), _unpack(v2, shapes)):
        out["grad_" + n] = g_small[n]
        out["delta_" + n], out["new_m_" + n], out["new_v_" + n] = dd, mm, vv
    return out


WEIGHTS = ("norm1_g", "w_in", "b_gate", "lru_conv_w", "lru_conv_b", "lru_w_a", "lru_b_a", "lru_w_x", "lru_b_x",
           "lru_lambda", "ssd_conv_w", "ssd_conv_b", "ssd_dt_bias", "ssd_A_log", "ssd_D", "ssd_norm_g", "w_branch",
           "w_out", "norm2_g", "w_ffn_in", "w_ffn_out", "norm_f")


def kernel(x, norm1_g, w_in, b_gate, lru_conv_w, lru_conv_b, lru_w_a, lru_b_a, lru_w_x, lru_b_x, lru_lambda, ssd_conv_w, ssd_conv_b, ssd_dt_bias, ssd_A_log, ssd_D, ssd_norm_g, w_branch, w_out, norm2_g, w_ffn_in, w_ffn_out, norm_f, loss_target, m_norm1_g, m_w_in, m_b_gate, m_lru_conv_w, m_lru_conv_b, m_lru_w_a, m_lru_b_a, m_lru_w_x, m_lru_b_x, m_lru_lambda, m_ssd_conv_w, m_ssd_conv_b, m_ssd_dt_bias, m_ssd_A_log, m_ssd_D, m_ssd_norm_g, m_w_branch, m_w_out, m_norm2_g, m_w_ffn_in, m_w_ffn_out, m_norm_f, v_norm1_g, v_w_in, v_b_gate, v_lru_conv_w, v_lru_conv_b, v_lru_w_a, v_lru_b_a, v_lru_w_x, v_lru_b_x, v_lru_lambda, v_ssd_conv_w, v_ssd_conv_b, v_ssd_dt_bias, v_ssd_A_log, v_ssd_D, v_ssd_norm_g, v_w_branch, v_w_out, v_norm2_g, v_w_ffn_in, v_w_ffn_out, v_norm_f):
    out = _step(dict(locals()))
    res = [out["loss"], out["grad_x"]]
    for pre in ("grad_", "delta_", "new_m_", "new_v_"):
        res += [out[pre + n] for n in WEIGHTS]
    return tuple(res)
```

```python
import functools

import numpy as np
import jax
import jax.numpy as jnp
from jax import lax
from jax.experimental import pallas as pl
from jax.experimental.pallas import tpu as pltpu

F32 = jnp.float32
BF16 = jnp.bfloat16
HIGHEST = lax.Precision.HIGHEST

D_MODEL = 1024
DEPTH = 2
CHUNK = 64
LRU_C = 8.0
SSD_INNER = 2048
SSD_HEADS = 32
SSD_HEAD_DIM = 64
SSD_GROUPS = 4
SSD_STATE = 128
SSD_CONV_DIM = 3072
D_FF = 2816
EPS = 1e-6
N_DEV = 8
LANES = 128
DT_PAD = LANES
IN_WIDTHS = (1024, 1024, 2048, 3072, 32, 2048)
IN_OFFS = tuple(int(v) for v in np.cumsum((0,) + IN_WIDTHS))

ADAM_LR = 0.001
ADAM_B1 = 0.9
ADAM_B2 = 0.999
ADAM_EPS = 1e-08
ADAM_WD = 0.01
ADAM_STEP = 10

VMEM_LIMIT = 56 * 1024 * 1024
MESH = pl.DeviceIdType.MESH


def _cparams(sem=None):
    return pltpu.CompilerParams(dimension_semantics=sem, vmem_limit_bytes=VMEM_LIMIT)


def _sds(shape, dtype):
    return jax.ShapeDtypeStruct(tuple(shape), dtype)


def _pick(n, cap):
    if n <= cap:
        return n
    best = LANES
    for t in range(LANES, cap + 1, LANES):
        if n % t == 0:
            best = t
    assert n % best == 0, (n, cap)
    return best


def _mm(name, pairs, mode, out_dtype=F32, res=None, rider=None):
    M = pairs[0][0].shape[0]
    N = pairs[0][1].shape[1] if mode == "nn" else pairs[0][1].shape[0]
    npair = len(pairs)
    tm = min(M, 1024 if npair <= 3 else 512)
    tn = _pick(N, 1536 if npair == 1 else 1024)
    tks, nks, starts = [], [], []
    s = 0
    for a, _ in pairs:
        k = a.shape[1]
        tk = _pick(k, 1536)
        tks.append(tk)
        nks.append(k // tk)
        starts.append(s)
        s += k // tk
    nk = s
    dims = (((1,), (0,)), ((), ())) if mode == "nn" else (((1,), (1,)), ((), ()))

    def body(*refs):
        ab = refs[:2 * npair]
        pos = 2 * npair
        r_ref = None
        if res is not None:
            r_ref = refs[pos]
            pos += 1
        o_ref = refs[pos]
        acc = refs[pos + 1] if nk > 1 else None
        k = pl.program_id(2)

        def finish(r):
            if r_ref is not None:
                r = r + r_ref[...].astype(F32)
            o_ref[...] = r.astype(o_ref.dtype)

        for p in range(npair):
            a_ref, b_ref = ab[2 * p], ab[2 * p + 1]
            lo, hi = starts[p], starts[p] + nks[p]

            def step(a_ref=a_ref, b_ref=b_ref, lo=lo, hi=hi):
                d = lax.dot_general(a_ref[...].astype(BF16), b_ref[...].astype(BF16), dims,
                                    preferred_element_type=F32)
                if nk == 1:
                    finish(d)
                    return
                if lo == 0:
                    @pl.when(k == 0)
                    def _():
                        acc[...] = d
                if hi == nk:
                    @pl.when(k == nk - 1)
                    def _():
                        finish(acc[...] + d)
                if max(lo, 1) < min(hi, nk - 1):
                    @pl.when((k > 0) & (k < nk - 1))
                    def _():
                        acc[...] += d

            if npair == 1:
                step()
            else:
                pl.when((k >= lo) & (k < hi))(step)

    in_specs, args = [], []
    for p, (a, b) in enumerate(pairs):
        def kk(k, p=p):
            return jnp.clip(k - starts[p], 0, nks[p] - 1)
        in_specs.append(pl.BlockSpec((tm, tks[p]), lambda i, j, k, kk=kk: (i, kk(k))))
        if mode == "nn":
            in_specs.append(pl.BlockSpec((tks[p], tn), lambda i, j, k, kk=kk: (kk(k), j)))
        else:
            in_specs.append(pl.BlockSpec((tn, tks[p]), lambda i, j, k, kk=kk: (j, kk(k))))
        args += [a, b]
    if res is not None:
        in_specs.append(pl.BlockSpec((tm, tn), lambda i, j, k: (i, j)))
        args.append(res)
    r_args, r_in, r_out, r_shape, r_scratch = _rider_parts(rider)
    grid = (M // tm, N // tn, nk)
    out = pl.pallas_call(
        _with_rider(body, len(args), 1, rider, grid), grid=grid, in_specs=in_specs + r_in,
        out_specs=[pl.BlockSpec((tm, tn), lambda i, j, k: (i, j))] + r_out,
        out_shape=[_sds((M, N), out_dtype)] + r_shape,
        scratch_shapes=([pltpu.VMEM((tm, tn), F32)] if nk > 1 else []) + r_scratch,
        compiler_params=_cparams(("parallel", "parallel", "arbitrary") if rider is None else ("arbitrary",) * 3),
        name=name,
    )(*args, *r_args)
    return out[0] if rider is None else (out[0], out[1:])


def _mm_tn(name, a, b, out_dtype=BF16):
    M, Ka = a.shape
    N = b.shape[1]
    tm = min(M, 2048)
    tka = _pick(Ka, 1024)
    tn = _pick(N, 1024)
    nm = M // tm

    def body(a_ref, b_ref, o_ref, *scratch):
        k = pl.program_id(2)
        d = lax.dot_general(a_ref[...].astype(BF16), b_ref[...].astype(BF16),
                            (((0,), (0,)), ((), ())), preferred_element_type=F32)
        if nm == 1:
            o_ref[...] = d.astype(o_ref.dtype)
            return
        acc = scratch[0]

        @pl.when(k == 0)
        def _():
            acc[...] = d

        @pl.when((k > 0) & (k < nm - 1))
        def _():
            acc[...] += d

        @pl.when(k == nm - 1)
        def _():
            o_ref[...] = (acc[...] + d).astype(o_ref.dtype)

    return pl.pallas_call(
        body, grid=(Ka // tka, N // tn, nm),
        in_specs=[pl.BlockSpec((tm, tka), lambda i, j, k: (k, i)),
                  pl.BlockSpec((tm, tn), lambda i, j, k: (k, j))],
        out_specs=pl.BlockSpec((tka, tn), lambda i, j, k: (i, j)),
        out_shape=_sds((Ka, N), out_dtype),
        scratch_shapes=[pltpu.VMEM((tka, tn), F32)] if nm > 1 else [],
        compiler_params=_cparams(("parallel", "parallel", "arbitrary")), name=name,
    )(a, b)


def _rowwise_fwd(name, fn, rows, pars, outs, tb, rider=None):
    rows = [r if isinstance(r, tuple) else (r, 0, r.shape[1]) for r in rows]
    T = rows[0][0].shape[0]
    tb = min(tb, T)
    nr, npar = len(rows), len(pars)
    r_args, r_in, r_out, r_shape, r_scratch = _rider_parts(rider)

    def body(*refs):
        rv = [r[...].astype(F32) for r in refs[:nr]]
        pv = [p[...] for p in refs[nr:nr + npar]]
        res = fn(*rv, *pv)
        for o, r in zip(refs[nr + npar:], res):
            o[...] = r.astype(o.dtype)

    in_specs = [pl.BlockSpec((tb, w), lambda i, c=c: (i, c)) for _, c, w in rows]
    in_specs += [pl.BlockSpec(p.shape, lambda i, n=p.ndim: (0,) * n) for p in pars]
    return pl.pallas_call(
        _with_rider(body, nr + npar, len(outs), rider, (T // tb,)), grid=(T // tb,), in_specs=in_specs + r_in,
        out_specs=[pl.BlockSpec((tb, w), lambda i: (i, 0)) for w, _ in outs] + r_out,
        out_shape=[_sds((T, w), dt) for w, dt in outs] + r_shape, scratch_shapes=r_scratch,
        compiler_params=_cparams(("parallel",) if rider is None else ("arbitrary",)), name=name,
    )(*[r[0] for r in rows], *pars, *r_args)


def _rowwise_bwd(name, fn, rows, pars, cot_rows, cot_fn, row_out, tb):
    rows = [r if isinstance(r, tuple) else (r, 0, r.shape[1]) for r in rows]
    cot_rows = [r if isinstance(r, tuple) else (r, 0, r.shape[1]) for r in cot_rows]
    T = rows[0][0].shape[0]
    tb = min(tb, T)
    nr, npar, nc = len(rows), len(pars), len(cot_rows)
    want, want_dt = [], []
    for k, dt in enumerate(row_out):
        for d in (() if dt is None else dt if isinstance(dt, tuple) else (dt,)):
            want.append(k)
            want_dt.append(d)

    def body(*refs):
        i = pl.program_id(0)
        rv = [r[...].astype(F32) for r in refs[:nr]]
        pv = [p[...] for p in refs[nr:nr + npar]]
        cv = [c[...].astype(F32) for c in refs[nr + npar:nr + npar + nc]]
        o_refs = refs[nr + npar + nc:]
        _, vjp = jax.vjp(fn, *rv, *pv)
        grads = vjp(tuple(cot_fn(*cv)))
        for o, k in zip(o_refs[:len(want)], want):
            o[...] = grads[k].astype(o.dtype)
        p_refs = o_refs[len(want):]

        @pl.when(i == 0)
        def _():
            for o in p_refs:
                o[...] = jnp.zeros_like(o)

        for o, g in zip(p_refs, grads[nr:]):
            o[...] += g

    in_specs = [pl.BlockSpec((tb, w), lambda i, c=c: (i, c)) for _, c, w in rows]
    in_specs += [pl.BlockSpec(p.shape, lambda i, n=p.ndim: (0,) * n) for p in pars]
    in_specs += [pl.BlockSpec((tb, w), lambda i, c=c: (i, c)) for _, c, w in cot_rows]
    out_specs = [pl.BlockSpec((tb, rows[k][2]), lambda i: (i, 0)) for k in want]
    out_specs += [pl.BlockSpec(p.shape, lambda i, n=p.ndim: (0,) * n) for p in pars]
    out_shape = [_sds((T, rows[k][2]), d) for k, d in zip(want, want_dt)] + [_sds(p.shape, F32) for p in pars]
    return pl.pallas_call(
        body, grid=(T // tb,), in_specs=in_specs, out_specs=out_specs, out_shape=out_shape,
        compiler_params=_cparams(("arbitrary",)), name=name,
    )(*[r[0] for r in rows], *pars, *[r[0] for r in cot_rows])


def _f_rms(x, g):
    r = lax.rsqrt(jnp.mean(x * x, axis=-1, keepdims=True) + EPS)
    return (x * r * g,)


def _f_lru_gates(u, wa, ba, wx, bx, lam):
    ub = u.astype(BF16)
    ra, rx = [], []
    for k in range(D_MODEL // LANES):
        uk = ub[:, LANES * k:LANES * (k + 1)]
        ra.append(jnp.dot(uk, wa[k].astype(BF16), preferred_element_type=F32))
        rx.append(jnp.dot(uk, wx[k].astype(BF16), preferred_element_type=F32))
    r = jax.nn.sigmoid(jnp.concatenate(ra, axis=1) + ba)
    i = jax.nn.sigmoid(jnp.concatenate(rx, axis=1) + bx)
    log_a = -LRU_C * r * jax.nn.softplus(-lam)
    a = jnp.exp(log_a)
    t = jnp.tanh(log_a)
    b = jnp.sqrt(-2.0 * t / (1.0 - t)) * (i * u)
    return a, b


def _f_post(hl, lgate, yssd, xpre_s, z, d_e, ng):
    ya = jax.nn.gelu(lgate) * hl
    y = (yssd + d_e * jax.nn.silu(xpre_s)) * jax.nn.silu(z)
    gw = SSD_INNER // SSD_GROUPS
    parts = []
    for g in range(SSD_GROUPS):
        yg = y[:, gw * g:gw * (g + 1)]
        parts.append(yg * lax.rsqrt(jnp.mean(yg * yg, axis=-1, keepdims=True) + EPS))
    return ya, jnp.concatenate(parts, axis=1) * ng


def _f_merge(m_a, m_b, gates, bg):
    g = jax.nn.sigmoid(gates + bg)
    return (g[:, :D_MODEL] * m_a + g[:, D_MODEL:] * m_b,)


def _f_act(gu):
    return (jax.nn.silu(gu[:, :D_FF]) * gu[:, D_FF:],)


def _ident(*c):
    return c


def _loss_head(h, target, nf):
    T = h.shape[0]
    tb = min(256, T)

    def body(h_ref, t_ref, nf_ref, dh_ref, dnf_ref, loss_ref, dhb_ref):
        i = pl.program_id(0)

        @pl.when(i == 0)
        def _():
            dnf_ref[...] = jnp.zeros_like(dnf_ref)
            loss_ref[...] = jnp.zeros_like(loss_ref)

        (y,), vjp = jax.vjp(_f_rms, h_ref[...], nf_ref[...])
        err = y - t_ref[...]
        dh, dnf = vjp((err * (1.0 / D_MODEL),))
        dh_ref[...] = dh
        dhb_ref[...] = dh.astype(BF16)
        dnf_ref[...] += dnf
        part = 0.5 * jnp.sum(jnp.mean(err * err, axis=-1, keepdims=True), axis=0, keepdims=True)
        loss_ref[...] += jnp.broadcast_to(part, loss_ref.shape)

    return pl.pallas_call(
        body, grid=(T // tb,),
        in_specs=[pl.BlockSpec((tb, D_MODEL), lambda i: (i, 0)), pl.BlockSpec((tb, D_MODEL), lambda i: (i, 0)),
                  pl.BlockSpec((1, D_MODEL), lambda i: (0, 0))],
        out_specs=[pl.BlockSpec((tb, D_MODEL), lambda i: (i, 0)), pl.BlockSpec((1, D_MODEL), lambda i: (0, 0)),
                   pl.BlockSpec((8, LANES), lambda i: (0, 0)), pl.BlockSpec((tb, D_MODEL), lambda i: (i, 0))],
        out_shape=[_sds((T, D_MODEL), F32), _sds((1, D_MODEL), F32), _sds((8, LANES), F32),
                   _sds((T, D_MODEL), BF16)],
        compiler_params=_cparams(("arbitrary",)), name="loss_head",
    )(h, target, nf)


CONV_TC = 1024
HALO = 16


def _conv_fwd(name, x, w8, b):
    T, C = x.shape
    tb = min(512, T)
    nb = tb // HALO

    def body(x_ref, xp_ref, w_ref, b_ref, y_ref, sc):
        i = pl.program_id(0)
        xv = x_ref[...].astype(F32)
        sc[pl.ds(HALO, tb), :] = xv
        sc[pl.ds(0, HALO), :] = jnp.where(i > 0, xp_ref[...].astype(F32), 0.0)
        acc = b_ref[...] + w_ref[3:4, :] * xv
        for k in range(3):
            acc = acc + w_ref[k:k + 1, :] * sc[pl.ds(HALO - 3 + k, tb), :]
        y_ref[...] = acc

    return pl.pallas_call(
        body, grid=(T // tb, C // CONV_TC),
        in_specs=[pl.BlockSpec((tb, CONV_TC), lambda i, j: (i, j)),
                  pl.BlockSpec((HALO, CONV_TC), lambda i, j: (jnp.maximum(i * nb - 1, 0), j)),
                  pl.BlockSpec((8, CONV_TC), lambda i, j: (0, j)),
                  pl.BlockSpec((1, CONV_TC), lambda i, j: (0, j))],
        out_specs=pl.BlockSpec((tb, CONV_TC), lambda i, j: (i, j)),
        out_shape=_sds((T, C), F32),
        scratch_shapes=[pltpu.VMEM((tb + HALO, CONV_TC), F32)],
        compiler_params=_cparams(("parallel", "parallel")), name=name,
    )(x, x, w8, b)


def _conv_bwd(name, x, dy, w8):
    T, C = x.shape
    tb = min(512, T)
    nb = tb // HALO
    nt = T // tb

    def body(x_ref, xp_ref, dy_ref, dyn_ref, w_ref, dx_ref, dwb_ref, scx, scd):
        i = pl.program_id(1)
        dyv = dy_ref[...]
        xv = x_ref[...].astype(F32)
        scx[pl.ds(HALO, tb), :] = xv
        scx[pl.ds(0, HALO), :] = jnp.where(i > 0, xp_ref[...].astype(F32), 0.0)
        scd[pl.ds(0, tb), :] = dyv
        scd[pl.ds(tb, HALO), :] = jnp.where(i < nt - 1, dyn_ref[...], 0.0)
        dx = w_ref[3:4, :] * dyv
        rows = []
        for k in range(3):
            dx = dx + w_ref[k:k + 1, :] * scd[pl.ds(3 - k, tb), :]
            rows.append(jnp.sum(dyv * scx[pl.ds(HALO - 3 + k, tb), :], axis=0, keepdims=True))
        rows.append(jnp.sum(dyv * xv, axis=0, keepdims=True))
        rows.append(jnp.sum(dyv, axis=0, keepdims=True))
        rows.append(jnp.zeros((3, CONV_TC), F32))
        dx_ref[...] = dx.astype(dx_ref.dtype)

        @pl.when(i == 0)
        def _():
            dwb_ref[...] = jnp.zeros_like(dwb_ref)

        dwb_ref[...] += jnp.concatenate(rows, axis=0)

    return pl.pallas_call(
        body, grid=(C // CONV_TC, nt),
        in_specs=[pl.BlockSpec((tb, CONV_TC), lambda j, i: (i, j)),
                  pl.BlockSpec((HALO, CONV_TC), lambda j, i: (jnp.maximum(i * nb - 1, 0), j)),
                  pl.BlockSpec((tb, CONV_TC), lambda j, i: (i, j)),
                  pl.BlockSpec((HALO, CONV_TC), lambda j, i: (jnp.minimum((i + 1) * nb, T // HALO - 1), j)),
                  pl.BlockSpec((8, CONV_TC), lambda j, i: (0, j))],
        out_specs=[pl.BlockSpec((tb, CONV_TC), lambda j, i: (i, j)),
                   pl.BlockSpec((8, CONV_TC), lambda j, i: (0, j))],
        out_shape=[_sds((T, C), BF16), _sds((8, C), F32)],
        scratch_shapes=[pltpu.VMEM((tb + HALO, CONV_TC), F32), pltpu.VMEM((tb + HALO, CONV_TC), F32)],
        compiler_params=_cparams(("parallel", "arbitrary")), name=name,
    )(x, x, dy, dy, w8)


SCAN_TB = 512


def _lru_scan_fwd(a, b):
    T, C = a.shape
    tb = min(SCAN_TB, T)

    def body(a_ref, b_ref, h_ref, hp_ref, carry):
        @pl.when(pl.program_id(0) == 0)
        def _():
            carry[...] = jnp.zeros_like(carry)

        def group(gi, h):
            r0 = pl.multiple_of(gi * 8, 8)
            at = a_ref[pl.ds(r0, 8), :]
            bt = b_ref[pl.ds(r0, 8), :]
            hs, hps = [], []
            for r in range(8):
                hps.append(h)
                h = at[r:r + 1, :] * h + bt[r:r + 1, :]
                hs.append(h)
            h_ref[pl.ds(r0, 8), :] = jnp.concatenate(hs, axis=0)
            hp_ref[pl.ds(r0, 8), :] = jnp.concatenate(hps, axis=0)
            return h

        carry[0:1, :] = lax.fori_loop(0, tb // 8, group, carry[0:1, :])

    spec = pl.BlockSpec((tb, C), lambda i: (i, 0))
    return pl.pallas_call(
        body, grid=(T // tb,), in_specs=[spec, spec], out_specs=[spec, spec],
        out_shape=[_sds((T, C), F32), _sds((T, C), F32)],
        scratch_shapes=[pltpu.VMEM((8, C), F32)],
        compiler_params=_cparams(("arbitrary",)), name="lru_scan_fwd",
    )(a, b)


def _lru_scan_bwd(a, dh):
    T, C = a.shape
    tb = min(SCAN_TB, T)
    nt = T // tb

    def body(a_ref, dh_ref, g_ref, carry):
        @pl.when(pl.program_id(0) == 0)
        def _():
            carry[...] = jnp.zeros_like(carry)

        def group(gi, c):
            r0 = pl.multiple_of((tb // 8 - 1 - gi) * 8, 8)
            at = a_ref[pl.ds(r0, 8), :]
            dt = dh_ref[pl.ds(r0, 8), :]
            gs = [None] * 8
            for r in range(7, -1, -1):
                g = dt[r:r + 1, :] + c
                c = at[r:r + 1, :] * g
                gs[r] = g
            g_ref[pl.ds(r0, 8), :] = jnp.concatenate(gs, axis=0)
            return c

        carry[0:1, :] = lax.fori_loop(0, tb // 8, group, carry[0:1, :])

    spec = pl.BlockSpec((tb, C), lambda i: (nt - 1 - i, 0))
    return pl.pallas_call(
        body, grid=(nt,), in_specs=[spec, spec], out_specs=spec,
        out_shape=_sds((T, C), F32),
        scratch_shapes=[pltpu.VMEM((8, C), F32)],
        compiler_params=_cparams(("arbitrary",)), name="lru_scan_bwd",
    )(a, dh)


def _ssd_chunk(xpre, dtraw, state, dtb, alog):
    xc = jax.nn.silu(xpre)
    xs = xc[:, :SSD_INNER]
    bm = xc[:, SSD_INNER:SSD_INNER + SSD_GROUPS * SSD_STATE]
    cm = xc[:, SSD_INNER + SSD_GROUPS * SSD_STATE:]
    dt = jax.nn.softplus(dtraw + dtb)
    a = dt * (-jnp.exp(alog))
    ltri = (lax.broadcasted_iota(jnp.int32, (CHUNK, CHUNK), 0)
            >= lax.broadcasted_iota(jnp.int32, (CHUNK, CHUNK), 1)).astype(F32)
    a_cs = jnp.dot(ltri, a, precision=HIGHEST, preferred_element_type=F32)
    npair = SSD_HEADS // 2
    pi = lax.broadcasted_iota(jnp.int32, (npair, LANES), 0)
    hi = lax.broadcasted_iota(jnp.int32, (npair, LANES), 1)
    sel_even = (hi == 2 * pi).astype(F32)
    sel_odd = (hi == 2 * pi + 1).astype(F32)
    top = lax.broadcasted_iota(jnp.int32, (2 * CHUNK, LANES), 0) < CHUNK

    def pair_transpose(v):
        v2 = jnp.concatenate([v, v], axis=0)
        dn = (((1,), (1,)), ((), ()))
        return (lax.dot_general(sel_even, jnp.where(top, v2, 0.0), dn, precision=HIGHEST, preferred_element_type=F32)
                + lax.dot_general(sel_odd, jnp.where(top, 0.0, v2), dn, precision=HIGHEST, preferred_element_type=F32))

    a_t2 = pair_transpose(a_cs)
    dt_t2 = pair_transpose(dt)
    a_last = a_cs[CHUNK - 1:CHUNK, :]
    dte = jnp.exp(a_last - a_cs) * dt
    cd = jnp.exp(a_last)
    lane = lax.broadcasted_iota(jnp.int32, (CHUNK, LANES), 1)
    left = lane < SSD_HEAD_DIM
    right = jnp.logical_not(left)
    left1 = left[0:1]
    tril2 = lax.broadcasted_iota(jnp.int32, (CHUNK, LANES), 0) >= (lane & (SSD_HEAD_DIM - 1))
    gw = SSD_INNER // SSD_GROUPS
    ys, news = [], []
    for g in range(SSD_GROUPS):
        bg = bm[:, SSD_STATE * g:SSD_STATE * (g + 1)].astype(BF16)
        cg = cm[:, SSD_STATE * g:SSD_STATE * (g + 1)].astype(BF16)
        bg2 = jnp.concatenate([bg, bg], axis=0)
        scores2 = lax.dot_general(cg, bg2, (((1,), (1,)), ((), ())), preferred_element_type=F32)
        coff = jnp.dot(cg, state[:, gw * g:gw * (g + 1)].astype(BF16), preferred_element_type=F32)
        for j in range(gw // LANES):
            lo = gw * g + LANES * j
            p = lo // LANES
            h0 = 2 * p
            xp = xs[:, lo:lo + LANES]
            col = jnp.where(left, a_cs[:, h0:h0 + 1], a_cs[:, h0 + 1:h0 + 2])
            dm = jnp.exp(jnp.where(tril2, col - a_t2[p:p + 1, :], -1e30)) * dt_t2[p:p + 1, :]
            sd = (scores2 * dm).astype(BF16)
            x_bd = jnp.concatenate([jnp.where(left, xp, 0.0), jnp.where(right, xp, 0.0)], axis=0).astype(BF16)
            acc = jnp.dot(sd, x_bd, preferred_element_type=F32)
            ys.append(acc + coff[:, LANES * j:LANES * (j + 1)] * jnp.exp(col))
            dtee = jnp.where(left, dte[:, h0:h0 + 1], dte[:, h0 + 1:h0 + 2])
            xw = (xp * dtee).astype(BF16)
            cde = jnp.where(left1, cd[:, h0:h0 + 1], cd[:, h0 + 1:h0 + 2])
            news.append(state[:, lo:lo + LANES] * cde
                        + lax.dot_general(bg, xw, (((0,), (0,)), ((), ())), preferred_element_type=F32))
    return jnp.concatenate(ys, axis=1), jnp.concatenate(news, axis=1)


def _ssd_fwd(name, xpre, dtraw, dtb, alog, rider=None):
    T = xpre.shape[0]
    n = T // CHUNK
    r_args, r_in, r_out, r_shape, r_scratch = _rider_parts(rider)

    def body(xp, dr, dtb_ref, al_ref, y_ref, st_ref, state):
        @pl.when(pl.program_id(0) == 0)
        def _():
            state[...] = jnp.zeros_like(state)

        st_ref[0] = state[...]
        y, new = _ssd_chunk(xp[...], dr[...], state[...], dtb_ref[...], al_ref[...])
        y_ref[...] = y
        state[...] = new

    small = pl.BlockSpec((1, LANES), lambda c: (0, 0))
    return pl.pallas_call(
        _with_rider(body, 4, 2, rider, (n,)), grid=(n,),
        in_specs=[pl.BlockSpec((CHUNK, SSD_CONV_DIM), lambda c: (c, 0)),
                  pl.BlockSpec((CHUNK, DT_PAD), lambda c: (c, 0)), small, small] + r_in,
        out_specs=[pl.BlockSpec((CHUNK, SSD_INNER), lambda c: (c, 0)),
                   pl.BlockSpec((1, SSD_STATE, SSD_INNER), lambda c: (c, 0, 0))] + r_out,
        out_shape=[_sds((T, SSD_INNER), F32), _sds((n, SSD_STATE, SSD_INNER), F32)] + r_shape,
        scratch_shapes=[pltpu.VMEM((SSD_STATE, SSD_INNER), F32)] + r_scratch,
        compiler_params=_cparams(("arbitrary",)), name=name,
    )(xpre, dtraw, dtb, alog, *r_args)


def _ssd_bwd(name, xpre, dtraw, states, dy, dxs_extra, dtb, alog, rider=None):
    T = xpre.shape[0]
    n = T // CHUNK
    r_args, r_in, r_out, r_shape, r_scratch = _rider_parts(rider)

    def body(xp, dr, st, dy_ref, dx_ref, dtb_ref, al_ref, dxp_ref, ddr_ref, ddtb_ref, dal_ref, dstate):
        @pl.when(pl.program_id(0) == 0)
        def _():
            dstate[...] = jnp.zeros_like(dstate)
            ddtb_ref[...] = jnp.zeros_like(ddtb_ref)
            dal_ref[...] = jnp.zeros_like(dal_ref)

        _, vjp = jax.vjp(_ssd_chunk, xp[...], dr[...], st[0], dtb_ref[...], al_ref[...])
        dxp, ddr, ds, db, da = vjp((dy_ref[...], dstate[...]))
        dxp_ref[:, :SSD_INNER] = dxp[:, :SSD_INNER] + dx_ref[...]
        dxp_ref[:, SSD_INNER:] = dxp[:, SSD_INNER:]
        ddr_ref[...] = ddr.astype(ddr_ref.dtype)
        dstate[...] = ds
        ddtb_ref[...] += db
        dal_ref[...] += da

    def rev(c):
        return (n - 1 - c, 0)

    small = pl.BlockSpec((1, LANES), lambda c: (0, 0))
    return pl.pallas_call(
        _with_rider(body, 7, 4, rider, (n,)), grid=(n,),
        in_specs=[pl.BlockSpec((CHUNK, SSD_CONV_DIM), rev), pl.BlockSpec((CHUNK, DT_PAD), rev),
                  pl.BlockSpec((1, SSD_STATE, SSD_INNER), lambda c: (n - 1 - c, 0, 0)),
                  pl.BlockSpec((CHUNK, SSD_INNER), rev), pl.BlockSpec((CHUNK, SSD_INNER), rev), small, small] + r_in,
        out_specs=[pl.BlockSpec((CHUNK, SSD_CONV_DIM), rev), pl.BlockSpec((CHUNK, DT_PAD), rev), small, small] + r_out,
        out_shape=[_sds((T, SSD_CONV_DIM), F32), _sds((T, DT_PAD), BF16), _sds((1, LANES), F32),
                   _sds((1, LANES), F32)] + r_shape,
        scratch_shapes=[pltpu.VMEM((SSD_STATE, SSD_INNER), F32)] + r_scratch,
        compiler_params=_cparams(("arbitrary",)), name=name,
    )(xpre, dtraw, states, dy, dxs_extra, dtb, alog, *r_args)


HBM_SPEC = pl.BlockSpec(memory_space=pltpu.HBM)
N_PEER = N_DEV - 1


def _position():
    return lax.axis_index("x"), lax.axis_index("y"), lax.axis_index("c")


def _all_gather(name, blks):
    return _exchange_call(name, _gather_phases, blks, [_sds((N_DEV,) + b.shape, b.dtype) for b in blks])


def _scatter_exchange(name, gs):
    return _exchange_call(name, _scatter_phases, gs, [_sds(g.shape, g.dtype) for g in gs])


def _exchange_scratch(na):
    return [pltpu.SemaphoreType.DMA((na * N_PEER,)), pltpu.SemaphoreType.DMA((na * N_PEER,)),
            pltpu.SemaphoreType.DMA((na,))]


def _exchange_call(name, phases, arrays, out_shape):
    na = len(arrays)

    def body(*refs):
        begin, finish = phases(refs[:na], refs[na:2 * na], *refs[2 * na:])
        begin()
        finish()

    return pl.pallas_call(
        body, out_shape=out_shape, in_specs=[HBM_SPEC] * na, out_specs=[HBM_SPEC] * na,
        scratch_shapes=_exchange_scratch(na), name=name,
    )(*arrays)


def _gather_phases(x_refs, out_refs, send_sems, recv_sems, local_sems):
    na = len(x_refs)
    x, y, c = _position()
    me, sibling = (x, y, c), (x, y, 1 - c)
    chips = [(1 - x, y), (x, 1 - y), (1 - x, 1 - y)]

    def slot(a, px, py, pc):
        return out_refs[a].at[4 * px + 2 * py + pc]

    def copy(a, k, block, to, src=None):
        return pltpu.make_async_remote_copy(
            src_ref=slot(a, *block) if src is None else src, dst_ref=slot(a, *block),
            send_sem=send_sems.at[a * N_PEER + k], recv_sem=recv_sems.at[a * N_PEER + k],
            device_id=to, device_id_type=MESH)

    mine = [pltpu.make_async_copy(x_refs[a], slot(a, *me), local_sems.at[a]) for a in range(na)]
    first = []
    for a in range(na):
        first.append(copy(a, 0, me, sibling, src=x_refs[a]))
        first += [copy(a, 1 + j, me, (*chip, c), src=x_refs[a]) for j, chip in enumerate(chips)]
    passed = [copy(a, 4 + j, (*chip, c), sibling) for j, chip in enumerate(chips) for a in range(na)]

    def begin():
        for cp in mine + first:
            cp.start()

    def finish():
        for j, chip in enumerate(chips):
            for a in range(na):
                copy(a, 1 + j, (*chip, c), me).wait_recv()
                passed[j * na + a].start()
        for a in range(na):
            copy(a, 0, sibling, me).wait_recv()
            for j, chip in enumerate(chips):
                copy(a, 4 + j, (*chip, 1 - c), me).wait_recv()
        for cp in first + passed:
            cp.wait_send()
        for cp in mine:
            cp.wait()

    return begin, finish


def _scatter_phases(g_refs, q_refs, send_sems, recv_sems, local_sems):
    na = len(g_refs)
    x, y, c = _position()
    me = 4 * x + 2 * y + c
    mine, sends, recvs = [], [], []
    for a in range(na):
        mine.append(pltpu.make_async_copy(g_refs[a].at[me], q_refs[a].at[me], local_sems.at[a]))
        for k in range(1, N_DEV):
            px, py, pc = (x + (k >> 2)) % 2, (y + ((k >> 1) & 1)) % 2, (c + (k & 1)) % 2
            peer = 4 * px + 2 * py + pc
            sem = a * N_PEER + k - 1
            sends.append(pltpu.make_async_remote_copy(
                src_ref=g_refs[a].at[peer], dst_ref=q_refs[a].at[me], send_sem=send_sems.at[sem],
                recv_sem=recv_sems.at[sem], device_id=(px, py, pc), device_id_type=MESH))
            recvs.append(pltpu.make_async_remote_copy(
                src_ref=g_refs[a].at[me], dst_ref=q_refs[a].at[peer], send_sem=send_sems.at[sem],
                recv_sem=recv_sems.at[sem], device_id=(px, py, pc), device_id_type=MESH))

    def begin():
        for cp in mine + sends:
            cp.start()

    def finish():
        for cp in recvs:
            cp.wait_recv()
        for cp in sends:
            cp.wait_send()
        for cp in mine:
            cp.wait()

    return begin, finish


def _with_rider(body, n_in, n_out, rider, grid):
    if rider is None:
        return body
    phases, arrays, _ = rider
    na = len(arrays)

    def carried(*refs):
        ins, r_in = refs[:n_in], refs[n_in:n_in + na]
        outs = refs[n_in + na:n_in + na + n_out]
        r_out = refs[n_in + na + n_out:n_in + 2 * na + n_out]
        scratch, r_scratch = refs[n_in + 2 * na + n_out:-3], refs[-3:]
        begin, finish = phases(r_in, r_out, *r_scratch)
        first = last = None
        for d, g in enumerate(grid):
            at_start, at_end = pl.program_id(d) == 0, pl.program_id(d) == g - 1
            first = at_start if first is None else first & at_start
            last = at_end if last is None else last & at_end
        pl.when(first)(begin)
        body(*ins, *outs, *scratch)
        pl.when(last)(finish)

    return carried


def _rider_parts(rider):
    if rider is None:
        return [], [], [], [], []
    _, arrays, out_shape = rider
    na = len(arrays)
    return list(arrays), [HBM_SPEC] * na, [HBM_SPEC] * na, list(out_shape), _exchange_scratch(na)


def _gather_rider(blks):
    return (_gather_phases, blks, [_sds((N_DEV,) + b.shape, b.dtype) for b in blks])


def _scatter_rider(gs):
    return (_scatter_phases, gs, [_sds(g.shape, g.dtype) for g in gs])


def _adamw_sum(name, w, m, v, recvs, rider=None):
    L, r, c = w.shape
    tr = r if r <= 512 else max(t for t in range(16, 257, 16) if r % t == 0)
    assert len(recvs) == L
    nr = r // tr

    def body(w_ref, m_ref, v_ref, *refs):
        q_refs = refs[:L]
        g_ref, d_ref, m2_ref, v2_ref = refs[L:]
        l = pl.program_id(0)
        for ll in range(L):
            @pl.when(l == ll)
            def _(q_ref=q_refs[ll]):
                gv = q_ref[0].astype(F32)
                for s in range(1, N_DEV):
                    gv = gv + q_ref[s].astype(F32)
                m2 = ADAM_B1 * m_ref[0] + (1.0 - ADAM_B1) * gv
                v2 = ADAM_B2 * v_ref[0] + (1.0 - ADAM_B2) * jnp.square(gv)
                m_hat = m2 / (1.0 - ADAM_B1 ** ADAM_STEP)
                v_hat = v2 / (1.0 - ADAM_B2 ** ADAM_STEP)
                g_ref[0] = gv
                d_ref[0] = -ADAM_LR * (m_hat / (jnp.sqrt(v_hat) + ADAM_EPS) + ADAM_WD * w_ref[0])
                m2_ref[0] = m2
                v2_ref[0] = v2

    def q_index(ll):
        return lambda l, i: (0, jnp.where(l == ll, i, jnp.where(l > ll, nr - 1, 0)), 0)

    spec = pl.BlockSpec((1, tr, c), lambda l, i: (l, i, 0))
    r_args, r_in, r_out, r_shape, r_scratch = _rider_parts(rider)
    return pl.pallas_call(
        _with_rider(body, 3 + L, 4, rider, (L, nr)), grid=(L, nr),
        in_specs=[spec] * 3 + [pl.BlockSpec((N_DEV, tr, c), q_index(ll)) for ll in range(L)] + r_in,
        out_specs=[spec] * 4 + r_out, out_shape=[_sds((L, r, c), F32)] * 4 + r_shape, scratch_shapes=r_scratch,
        compiler_params=_cparams(("arbitrary", "arbitrary")), name=name,
    )(w, m, v, *recvs, *r_args)


def _sum_blocks(name, q, tr):
    R = q.shape[0] // N_DEV
    W = q.shape[1]
    tr = min(tr, R)
    assert R % tr == 0
    nb = R // tr

    def body(*refs):
        acc = refs[0][...].astype(F32)
        for r in refs[1:N_DEV]:
            acc = acc + r[...].astype(F32)
        refs[N_DEV][...] = acc

    return pl.pallas_call(
        body, grid=(nb,),
        in_specs=[pl.BlockSpec((tr, W), lambda i, s=s: (s * nb + i, 0)) for s in range(N_DEV)],
        out_specs=pl.BlockSpec((tr, W), lambda i: (i, 0)), out_shape=_sds((R, W), F32),
        compiler_params=_cparams(("parallel",)), name=name,
    )(*([q] * N_DEV))


def _adamw(name, w, g, m, v):
    R, C = w.shape
    tr = R
    if R > 512:
        tr = max(t for t in range(8, 513, 8) if R % t == 0)

    def body(w_ref, g_ref, m_ref, v_ref, d_ref, m2_ref, v2_ref):
        gv = g_ref[...]
        m2 = ADAM_B1 * m_ref[...] + (1.0 - ADAM_B1) * gv
        v2 = ADAM_B2 * v_ref[...] + (1.0 - ADAM_B2) * jnp.square(gv)
        m_hat = m2 / (1.0 - ADAM_B1 ** ADAM_STEP)
        v_hat = v2 / (1.0 - ADAM_B2 ** ADAM_STEP)
        d_ref[...] = -ADAM_LR * (m_hat / (jnp.sqrt(v_hat) + ADAM_EPS) + ADAM_WD * w_ref[...])
        m2_ref[...] = m2
        v2_ref[...] = v2

    spec = pl.BlockSpec((tr, C), lambda i: (i, 0))
    return pl.pallas_call(
        body, grid=(R // tr,), in_specs=[spec] * 4, out_specs=[spec] * 3,
        out_shape=[_sds((R, C), F32)] * 3, compiler_params=_cparams(("parallel",)), name=name,
    )(w, g, m, v)


BIG = ("w_in", "w_branch", "w_out", "w_ffn_in", "w_ffn_out")
BIG_ROW_SHARDED = {"w_in": False, "w_branch": True, "w_out": True, "w_ffn_in": True, "w_ffn_out": True}
TRANSPOSED = ("w_ffn_in",)
SMALL = ("norm1_g", "b_gate", "lru_conv_w", "lru_conv_b", "lru_w_a", "lru_b_a", "lru_w_x", "lru_b_x", "lru_lambda",
         "ssd_conv_w", "ssd_conv_b", "ssd_dt_bias", "ssd_A_log", "ssd_D", "ssd_norm_g", "norm2_g", "norm_f")
SMALL_ROWS = 256


def _pack(arrs, dtype, row_mult):
    parts = []
    for a in arrs:
        f = a.reshape(-1).astype(dtype)
        pad = (-f.shape[0]) % LANES
        if pad:
            f = jnp.concatenate([f, jnp.zeros((pad,), dtype)])
        parts.append(f)
    f = jnp.concatenate(parts)
    pad = (-f.shape[0]) % (LANES * row_mult)
    if pad:
        f = jnp.concatenate([f, jnp.zeros((pad,), dtype)])
    return f.reshape(-1, LANES)


def _unpack(flat, shapes, lead=()):
    f = flat.reshape(lead + (-1,))
    out, off = [], 0
    for s in shapes:
        n = int(np.prod(s))
        out.append(f[..., off:off + n].reshape(lead + tuple(s)))
        off += n + (-n) % LANES
    return out


def _full_from_shards(name, st):
    if BIG_ROW_SHARDED[name]:
        return st.reshape((-1,) + st.shape[2:])
    return jnp.transpose(st, (1, 0, 2)).reshape(st.shape[1], -1)


def _shards_from_full(name, full):
    if BIG_ROW_SHARDED[name]:
        return full.reshape((N_DEV, full.shape[0] // N_DEV) + full.shape[1:])
    return jnp.transpose(full.reshape(full.shape[0], N_DEV, -1), (1, 0, 2))


def _block_diag_tiles(w):
    z = jnp.zeros((8, 64, 64), w.dtype)
    w2 = w.reshape(8, 2, 64, 64)
    top = jnp.concatenate([w2[:, 0], z], axis=2)
    bot = jnp.concatenate([z, w2[:, 1]], axis=2)
    return jnp.concatenate([top, bot], axis=1)


def _block_diag_untile(t):
    return jnp.stack([t[:, :64, :64], t[:, 64:, 64:]], axis=1).reshape(16, 64, 64)


def _pad_lanes(a, width):
    return jnp.concatenate([a, jnp.zeros(a.shape[:-1] + (width - a.shape[-1],), a.dtype)], axis=-1)


def _pad_rows8(w):
    return jnp.concatenate([w, jnp.zeros((8 - w.shape[0],) + w.shape[1:], w.dtype)], axis=0)


REST = BIG[1:]


def _in_weights(shards):
    w_in = _full_from_shards("w_in", shards)
    seg = [w_in[:, IN_OFFS[k]:IN_OFFS[k + 1]] for k in range(6)]
    return {"w_lx": seg[0], "w_lg": seg[1], "w_z": seg[2], "w_xbc": seg[3], "w_dt": _pad_lanes(seg[4], DT_PAD),
            "w_g": seg[5]}


def _rest_weights(shards):
    full = {n: _full_from_shards(n, st) for n, st in zip(REST, shards)}
    return {"w_bra": full["w_branch"][:D_MODEL], "w_brb": full["w_branch"][D_MODEL:], "w_out": full["w_out"],
            "w_ffn_in_t": full["w_ffn_in"], "w_ffn_out": full["w_ffn_out"]}


def _small_params(l, small):
    p = {}
    for n in ("norm1_g", "b_gate", "lru_conv_b", "lru_b_a", "lru_b_x", "lru_lambda", "ssd_conv_b", "ssd_norm_g", "norm2_g"):
        p[n] = small[n][l].reshape(1, -1)
    p["lru_conv_w8"] = _pad_rows8(small["lru_conv_w"][l])
    p["ssd_conv_w8"] = _pad_rows8(small["ssd_conv_w"][l])
    p["wa"] = _block_diag_tiles(small["lru_w_a"][l])
    p["wx"] = _block_diag_tiles(small["lru_w_x"][l])
    p["dtb"] = _pad_lanes(small["ssd_dt_bias"][l].reshape(1, -1), DT_PAD)
    p["alog"] = _pad_lanes(small["ssd_A_log"][l].reshape(1, -1), DT_PAD)
    p["d_e"] = jnp.repeat(small["ssd_D"][l], SSD_HEAD_DIM).reshape(1, -1)
    return p


def _layer_fwd(l, h, p, riders):
    n = f"l{l}_"
    s = {"h_in": h}
    got = {}
    (xn,) = _rowwise_fwd(n + "rms1", _f_rms, [h], [p["norm1_g"]], [(D_MODEL, BF16)], 256)
    s["xn"] = xn
    for k in ("lx", "lg", "z", "xbc", "g", "dt"):
        s[k] = _mm(n + "in_" + k, [(xn, p["w_" + k])], "nn", out_dtype=F32 if k == "dt" else BF16)
    s["u"] = _conv_fwd(n + "lru_conv", s["lx"], p["lru_conv_w8"], p["lru_conv_b"])
    lru_pars = [p["wa"], p["lru_b_a"], p["wx"], p["lru_b_x"], p["lru_lambda"]]
    s["a"], b = _rowwise_fwd(n + "lru_gates", _f_lru_gates, [s["u"]], lru_pars, [(D_MODEL, F32)] * 2, 256)
    s["hl"], s["hprev"] = _lru_scan_fwd(s["a"], b)
    s["xpre"] = _conv_fwd(n + "ssd_conv", s["xbc"], p["ssd_conv_w8"], p["ssd_conv_b"])
    s["yssd"], s["states"], *got["ssd"] = _ssd_fwd(n + "ssd", s["xpre"], s["dt"], p["dtb"], p["alog"],
                                                   riders.get("ssd"))
    if "ssd" in riders:
        p.update(_rest_weights(got["ssd"]))
    post_rows = [s["hl"], s["lg"], s["yssd"], (s["xpre"], 0, SSD_INNER), s["z"]]
    s["ya"], s["yb"], *got["post"] = _rowwise_fwd(n + "post", _f_post, post_rows, [p["d_e"], p["ssd_norm_g"]],
                                                  [(D_MODEL, BF16), (SSD_INNER, BF16)], 128, riders.get("post"))
    s["ma"] = _mm(n + "br_a", [(s["ya"], p["w_bra"])], "nn", out_dtype=BF16)
    s["mb"] = _mm(n + "br_b", [(s["yb"], p["w_brb"])], "nn", out_dtype=BF16)
    (s["merged"],) = _rowwise_fwd(n + "merge", _f_merge, [s["ma"], s["mb"], s["g"]], [p["b_gate"]],
                                  [(D_MODEL, BF16)], 256)
    s["h_mid"] = _mm(n + "out", [(s["merged"], p["w_out"])], "nn", res=h)
    (s["xn2"],) = _rowwise_fwd(n + "rms2", _f_rms, [s["h_mid"]], [p["norm2_g"]], [(D_MODEL, BF16)], 256)
    s["gu"] = _mm(n + "ffn_in", [(s["xn2"], p["w_ffn_in_t"])], "nt", out_dtype=BF16, rider=riders.get("ffn_in"))
    if "ffn_in" in riders:
        s["gu"], got["ffn_in"] = s["gu"]
    (s["act"],) = _rowwise_fwd(n + "act", _f_act, [s["gu"]], [], [(D_FF, BF16)], 256)
    h_out = _mm(n + "ffn_out", [(s["act"], p["w_ffn_out"])], "nn", res=s["h_mid"], rider=riders.get("ffn_out"))
    if "ffn_out" in riders:
        h_out, got["ffn_out"] = h_out
    return h_out, s, got


def _f_rms_res(x, g):
    return _f_rms(x, g)[0], x


def _layer_bwd(l, dh, dh_b, s, p, above, first_layer):
    n = f"l{l}_b_"
    gw, gs = {}, {}
    d_act = _mm(n + "d_act", [(dh_b, p["w_ffn_out"])], "nt", out_dtype=BF16)
    gw["w_ffn_out"] = _mm_tn(n + "dw_ffn_out", s["act"], dh_b)
    (d_gu,) = _rowwise_bwd(n + "act", _f_act, [s["gu"]], [], [d_act], _ident, [BF16], 256)
    d_xn2 = _mm(n + "d_xn2", [(d_gu, p["w_ffn_in_t"])], "nn")
    gw["w_ffn_in"] = _mm_tn(n + "dw_ffn_in", d_gu, s["xn2"])
    dh_mid, dh_mid_b, gs["norm2_g"] = _rowwise_bwd(n + "rms2", _f_rms_res, [s["h_mid"]], [p["norm2_g"]], [d_xn2, dh],
                                                   _ident, [(F32, BF16)], 256)
    d_merged = _mm(n + "d_merged", [(dh_mid_b, p["w_out"])], "nt", out_dtype=BF16)
    gw["w_out"] = _mm_tn(n + "dw_out", s["merged"], dh_mid_b)
    d_ma, d_mb, d_g, gs["b_gate"] = _rowwise_bwd(n + "merge", _f_merge, [s["ma"], s["mb"], s["g"]], [p["b_gate"]],
                                                 [d_merged], _ident, [BF16, BF16, BF16], 256)
    d_ya = _mm(n + "d_ya", [(d_ma, p["w_bra"])], "nt", out_dtype=BF16)
    d_yb = _mm(n + "d_yb", [(d_mb, p["w_brb"])], "nt", out_dtype=BF16)
    gw["w_branch"] = jnp.concatenate([_mm_tn(n + "dw_bra", s["ya"], d_ma), _mm_tn(n + "dw_brb", s["yb"], d_mb)], axis=0)
    post_rows = [s["hl"], s["lg"], s["yssd"], (s["xpre"], 0, SSD_INNER), s["z"]]
    d_hl, d_lg, d_yssd, d_xs, d_z, d_de, gs["ssd_norm_g"] = _rowwise_bwd(
        n + "post", _f_post, post_rows, [p["d_e"], p["ssd_norm_g"]], [d_ya, d_yb], _ident,
        [F32, BF16, F32, F32, BF16], 128)
    gs["ssd_D"] = d_de.reshape(SSD_HEADS, SSD_HEAD_DIM).sum(axis=1)
    contrib = [_shards_from_full(k, gw[k]) for k in REST] + ([] if above is None else [above])
    d_xpre, d_dt, d_dtb, d_alog, *carried = _ssd_bwd(n + "ssd", s["xpre"], s["dt"], s["states"], d_yssd, d_xs,
                                                     p["dtb"], p["alog"], _scatter_rider(contrib))
    arrived = dict(zip(REST, carried))
    arrived_above = None if above is None else carried[len(REST)]
    gs["ssd_dt_bias"] = d_dtb[0, :SSD_HEADS]
    gs["ssd_A_log"] = d_alog[0, :SSD_HEADS]
    d_xbc, dwb = _conv_bwd(n + "ssd_conv", s["xbc"], d_xpre, p["ssd_conv_w8"])
    gs["ssd_conv_w"], gs["ssd_conv_b"] = dwb[:4], dwb[4]
    g_scan = _lru_scan_bwd(s["a"], d_hl)
    lru_pars = [p["wa"], p["lru_b_a"], p["wx"], p["lru_b_x"], p["lru_lambda"]]
    d_u, d_wa, gs["lru_b_a"], d_wx, gs["lru_b_x"], gs["lru_lambda"] = _rowwise_bwd(
        n + "lru_gates", _f_lru_gates, [s["u"]], lru_pars, [g_scan, s["hprev"]],
        lambda g, hp: (g * hp, g), [F32], 256)
    gs["lru_w_a"], gs["lru_w_x"] = _block_diag_untile(d_wa), _block_diag_untile(d_wx)
    d_lx, dwb = _conv_bwd(n + "lru_conv", s["lx"], d_u, p["lru_conv_w8"])
    gs["lru_conv_w"], gs["lru_conv_b"] = dwb[:4], dwb[4]
    segs = [("lx", d_lx), ("lg", d_lg), ("z", d_z), ("xbc", d_xbc), ("dt", d_dt), ("g", d_g)]
    dws = [_mm_tn(n + "dw_in_" + k, s["xn"], d) for k, d in segs]
    dws[4] = dws[4][:, :IN_WIDTHS[4]]
    below = _shards_from_full("w_in", jnp.concatenate(dws, axis=1))
    d_xn = _mm(n + "d_xn_a", [(d, p["w_" + k]) for k, d in segs[:3]], "nt")
    pairs_b = [(d, p["w_" + k]) for k, d in segs[3:]]
    if first_layer:
        d_xn, (arrived["w_in"],) = _mm(n + "d_xn_b", pairs_b, "nt", res=d_xn, rider=_scatter_rider([below]))
        below = None
    else:
        d_xn = _mm(n + "d_xn_b", pairs_b, "nt", res=d_xn)
    dh_in, dh_in_b, gs["norm1_g"] = _rowwise_bwd(n + "rms1", _f_rms_res, [s["h_in"]], [p["norm1_g"]], [d_xn, dh_mid],
                                                 _ident, [(F32, BF16)], 256)
    for k in ("norm1_g", "norm2_g", "b_gate", "ssd_norm_g", "lru_b_a", "lru_b_x", "lru_lambda"):
        gs[k] = gs[k].reshape(-1)
    return dh_in, dh_in_b, gs, arrived, arrived_above, below, gw


def _step(inp):
    x = inp["x"][0]
    target = inp["loss_target"][0]
    dev = 4 * lax.axis_index("x") + 2 * lax.axis_index("y") + lax.axis_index("c")

    def mine(l, names):
        return [(inp[n][l].T if n in TRANSPOSED else inp[n][l]).astype(BF16) for n in names]

    first = _all_gather("gather_first", mine(0, ("w_in",)) + [inp["lru_conv_w"], inp["ssd_conv_w"]])
    small = {n: inp[n] for n in SMALL}
    small["lru_conv_w"] = jnp.moveaxis(first[1], 0, 2).reshape(DEPTH, 4, -1)
    small["ssd_conv_w"] = jnp.moveaxis(first[2], 0, 2).reshape(DEPTH, 4, -1)
    shards = {"w_in": first[0]}

    h, saved, params = x, [], []
    for l in range(DEPTH):
        p = _small_params(l, small)
        p.update(_in_weights(shards["w_in"]))
        riders = {}
        if l == 0:
            riders["ssd"] = _gather_rider(mine(0, REST))
        else:
            p.update(_rest_weights([shards[n] for n in REST]))
        if l + 1 < DEPTH:
            riders["post"] = _gather_rider(mine(l + 1, ("w_ffn_in", "w_ffn_out")))
            riders["ffn_in"] = _gather_rider(mine(l + 1, ("w_in",)))
            riders["ffn_out"] = _gather_rider(mine(l + 1, ("w_branch", "w_out")))
        h, s, got = _layer_fwd(l, h, p, riders)
        if l + 1 < DEPTH:
            shards = {"w_ffn_in": got["post"][0], "w_ffn_out": got["post"][1], "w_in": got["ffn_in"][0],
                      "w_branch": got["ffn_out"][0], "w_out": got["ffn_out"][1]}
        saved.append(s)
        params.append(p)
    dh, d_nf, loss_acc, dh_b = _loss_head(h, target, small["norm_f"].reshape(1, -1))
    loss = lax.psum(loss_acc[0, 0], ("x", "y", "c"))

    gss, received = [None] * DEPTH, [None] * DEPTH
    handed_down = None
    for l in reversed(range(DEPTH)):
        dh, dh_b, gss[l], received[l], arrived_above, handed_down, _ = _layer_bwd(
            l, dh, dh_b, saved[l], params[l], handed_down, l == 0)
        if arrived_above is not None:
            received[l + 1]["w_in"] = arrived_above
    grad_x = dh[None]
    out = {"loss": loss, "grad_x": grad_x}
    small_full = {n: jnp.stack([gss[l][n] for l in range(DEPTH)]) for n in SMALL if n != "norm_f"}
    small_full["norm_f"] = d_nf.reshape(-1)
    part = _pack([small_full[n] for n in SMALL], F32, SMALL_ROWS)
    for n in BIG:
        flip = (lambda a: jnp.transpose(a, (0, 2, 1))) if n in TRANSPOSED else (lambda a: a)
        res = _adamw_sum("adamw_" + n, flip(inp[n]), flip(inp["m_" + n]), flip(inp["v_" + n]),
                         [received[l][n] for l in range(DEPTH)], _gather_rider([part]) if n == "w_in" else None)
        if n == "w_in":
            everyone = res[4]
        out["grad_" + n], out["delta_" + n], out["new_m_" + n], out["new_v_" + n] = [flip(a) for a in res[:4]]
    g_small_flat = _sum_blocks("sum_small_grads", everyone.reshape(-1, LANES), SMALL_ROWS)
    g_small = dict(zip(SMALL, _unpack(g_small_flat, [small_full[n].shape for n in SMALL])))
    for n in ("lru_conv_w", "ssd_conv_w"):
        w = inp[n].shape[-1]
        g_small[n] = lax.dynamic_slice_in_dim(g_small[n], dev * w, w, axis=2)

    shapes = [inp[n].shape for n in SMALL]
    packs = [_pack([src[pre + n] for n in SMALL], F32, SMALL_ROWS)
             for src, pre in ((inp, ""), (g_small, ""), (inp, "m_"), (inp, "v_"))]
    d, m2, v2 = _adamw("adamw_small", *packs)
    for n, dd, mm, vv in zip(SMALL, _unpack(d, shapes), _unpack(m2, shapes), _unpack(v2, shapes)):
        out["grad_" + n] = g_small[n]
        out["delta_" + n], out["new_m_" + n], out["new_v_" + n] = dd, mm, vv
    return out


WEIGHTS = ("norm1_g", "w_in", "b_gate", "lru_conv_w", "lru_conv_b", "lru_w_a", "lru_b_a", "lru_w_x", "lru_b_x",
           "lru_lambda", "ssd_conv_w", "ssd_conv_b", "ssd_dt_bias", "ssd_A_log", "ssd_D", "ssd_norm_g", "w_branch",
           "w_out", "norm2_g", "w_ffn_in", "w_ffn_out", "norm_f")


def kernel(x, norm1_g, w_in, b_gate, lru_conv_w, lru_conv_b, lru_w_a, lru_b_a, lru_w_x, lru_b_x, lru_lambda, ssd_conv_w, ssd_conv_b, ssd_dt_bias, ssd_A_log, ssd_D, ssd_norm_g, w_branch, w_out, norm2_g, w_ffn_in, w_ffn_out, norm_f, loss_target, m_norm1_g, m_w_in, m_b_gate, m_lru_conv_w, m_lru_conv_b, m_lru_w_a, m_lru_b_a, m_lru_w_x, m_lru_b_x, m_lru_lambda, m_ssd_conv_w, m_ssd_conv_b, m_ssd_dt_bias, m_ssd_A_log, m_ssd_D, m_ssd_norm_g, m_w_branch, m_w_out, m_norm2_g, m_w_ffn_in, m_w_ffn_out, m_norm_f, v_norm1_g, v_w_in, v_b_gate, v_lru_conv_w, v_lru_conv_b, v_lru_w_a, v_lru_b_a, v_lru_w_x, v_lru_b_x, v_lru_lambda, v_ssd_conv_w, v_ssd_conv_b, v_ssd_dt_bias, v_ssd_A_log, v_ssd_D, v_ssd_norm_g, v_w_branch, v_w_out, v_norm2_g, v_w_ffn_in, v_w_ffn_out, v_norm_f):
    out = _step(dict(locals()))
    res = [out["loss"], out["grad_x"]]
    for pre in ("grad_", "delta_", "new_m_", "new_v_"):
        res += [out[pre + n] for n in WEIGHTS]
    return tuple(res)
```

```python
import functools

import numpy as np
import jax
import jax.numpy as jnp
from jax import lax
from jax.experimental import pallas as pl
from jax.experimental.pallas import tpu as pltpu

F32 = jnp.float32
BF16 = jnp.bfloat16
HIGHEST = lax.Precision.HIGHEST

D_MODEL = 1024
DEPTH = 2
CHUNK = 64
LRU_C = 8.0
SSD_INNER = 2048
SSD_HEADS = 32
SSD_HEAD_DIM = 64
SSD_GROUPS = 4
SSD_STATE = 128
SSD_CONV_DIM = 3072
D_FF = 2816
EPS = 1e-6
N_DEV = 8
LANES = 128
DT_PAD = LANES
IN_WIDTHS = (1024, 1024, 2048, 3072, 32, 2048)
IN_OFFS = tuple(int(v) for v in np.cumsum((0,) + IN_WIDTHS))

ADAM_LR = 0.001
ADAM_B1 = 0.9
ADAM_B2 = 0.999
ADAM_EPS = 1e-08
ADAM_WD = 0.01
ADAM_STEP = 10

VMEM_LIMIT = 56 * 1024 * 1024
MESH = pl.DeviceIdType.MESH


def _cparams(sem=None):
    return pltpu.CompilerParams(dimension_semantics=sem, vmem_limit_bytes=VMEM_LIMIT)


def _sds(shape, dtype):
    return jax.ShapeDtypeStruct(tuple(shape), dtype)


def _pick(n, cap):
    if n <= cap:
        return n
    best = LANES
    for t in range(LANES, cap + 1, LANES):
        if n % t == 0:
            best = t
    assert n % best == 0, (n, cap)
    return best


def _mm(name, pairs, mode, out_dtype=F32, res=None, rider=None):
    M = pairs[0][0].shape[0]
    N = pairs[0][1].shape[1] if mode == "nn" else pairs[0][1].shape[0]
    npair = len(pairs)
    tm = min(M, 1024 if npair <= 3 else 512)
    tn = _pick(N, 1536 if npair == 1 else 1024)
    tks, nks, starts = [], [], []
    s = 0
    for a, _ in pairs:
        k = a.shape[1]
        tk = _pick(k, 1536)
        tks.append(tk)
        nks.append(k // tk)
        starts.append(s)
        s += k // tk
    nk = s
    dims = (((1,), (0,)), ((), ())) if mode == "nn" else (((1,), (1,)), ((), ()))

    def body(*refs):
        ab = refs[:2 * npair]
        pos = 2 * npair
        r_ref = None
        if res is not None:
            r_ref = refs[pos]
            pos += 1
        o_ref = refs[pos]
        acc = refs[pos + 1] if nk > 1 else None
        k = pl.program_id(2)

        def finish(r):
            if r_ref is not None:
                r = r + r_ref[...].astype(F32)
            o_ref[...] = r.astype(o_ref.dtype)

        for p in range(npair):
            a_ref, b_ref = ab[2 * p], ab[2 * p + 1]
            lo, hi = starts[p], starts[p] + nks[p]

            def step(a_ref=a_ref, b_ref=b_ref, lo=lo, hi=hi):
                d = lax.dot_general(a_ref[...].astype(BF16), b_ref[...].astype(BF16), dims,
                                    preferred_element_type=F32)
                if nk == 1:
                    finish(d)
                    return
                if lo == 0:
                    @pl.when(k == 0)
                    def _():
                        acc[...] = d
                if hi == nk:
                    @pl.when(k == nk - 1)
                    def _():
                        finish(acc[...] + d)
                if max(lo, 1) < min(hi, nk - 1):
                    @pl.when((k > 0) & (k < nk - 1))
                    def _():
                        acc[...] += d

            if npair == 1:
                step()
            else:
                pl.when((k >= lo) & (k < hi))(step)

    in_specs, args = [], []
    for p, (a, b) in enumerate(pairs):
        def kk(k, p=p):
            return jnp.clip(k - starts[p], 0, nks[p] - 1)
        in_specs.append(pl.BlockSpec((tm, tks[p]), lambda i, j, k, kk=kk: (i, kk(k))))
        if mode == "nn":
            in_specs.append(pl.BlockSpec((tks[p], tn), lambda i, j, k, kk=kk: (kk(k), j)))
        else:
            in_specs.append(pl.BlockSpec((tn, tks[p]), lambda i, j, k, kk=kk: (j, kk(k))))
        args += [a, b]
    if res is not None:
        in_specs.append(pl.BlockSpec((tm, tn), lambda i, j, k: (i, j)))
        args.append(res)
    r_args, r_in, r_out, r_shape, r_scratch = _rider_parts(rider)
    grid = (M // tm, N // tn, nk)
    out = pl.pallas_call(
        _with_rider(body, len(args), 1, rider, grid), grid=grid, in_specs=in_specs + r_in,
        out_specs=[pl.BlockSpec((tm, tn), lambda i, j, k: (i, j))] + r_out,
        out_shape=[_sds((M, N), out_dtype)] + r_shape,
        scratch_shapes=([pltpu.VMEM((tm, tn), F32)] if nk > 1 else []) + r_scratch,
        compiler_params=_cparams(("parallel", "parallel", "arbitrary") if rider is None else ("arbitrary",) * 3),
        name=name,
    )(*args, *r_args)
    return out[0] if rider is None else (out[0], out[1:])


def _mm_tn(name, a, b, out_dtype=BF16):
    M, Ka = a.shape
    N = b.shape[1]
    tm = min(M, 2048)
    tka = _pick(Ka, 1024)
    tn = _pick(N, 1024)
    nm = M // tm

    def body(a_ref, b_ref, o_ref, *scratch):
        k = pl.program_id(2)
        d = lax.dot_general(a_ref[...].astype(BF16), b_ref[...].astype(BF16),
                            (((0,), (0,)), ((), ())), preferred_element_type=F32)
        if nm == 1:
            o_ref[...] = d.astype(o_ref.dtype)
            return
        acc = scratch[0]

        @pl.when(k == 0)
        def _():
            acc[...] = d

        @pl.when((k > 0) & (k < nm - 1))
        def _():
            acc[...] += d

        @pl.when(k == nm - 1)
        def _():
            o_ref[...] = (acc[...] + d).astype(o_ref.dtype)

    return pl.pallas_call(
        body, grid=(Ka // tka, N // tn, nm),
        in_specs=[pl.BlockSpec((tm, tka), lambda i, j, k: (k, i)),
                  pl.BlockSpec((tm, tn), lambda i, j, k: (k, j))],
        out_specs=pl.BlockSpec((tka, tn), lambda i, j, k: (i, j)),
        out_shape=_sds((Ka, N), out_dtype),
        scratch_shapes=[pltpu.VMEM((tka, tn), F32)] if nm > 1 else [],
        compiler_params=_cparams(("parallel", "parallel", "arbitrary")), name=name,
    )(a, b)


def _rowwise_fwd(name, fn, rows, pars, outs, tb, rider=None):
    rows = [r if isinstance(r, tuple) else (r, 0, r.shape[1]) for r in rows]
    T = rows[0][0].shape[0]
    tb = min(tb, T)
    nr, npar = len(rows), len(pars)
    r_args, r_in, r_out, r_shape, r_scratch = _rider_parts(rider)

    def body(*refs):
        rv = [r[...].astype(F32) for r in refs[:nr]]
        pv = [p[...] for p in refs[nr:nr + npar]]
        res = fn(*rv, *pv)
        for o, r in zip(refs[nr + npar:], res):
            o[...] = r.astype(o.dtype)

    in_specs = [pl.BlockSpec((tb, w), lambda i, c=c: (i, c)) for _, c, w in rows]
    in_specs += [pl.BlockSpec(p.shape, lambda i, n=p.ndim: (0,) * n) for p in pars]
    return pl.pallas_call(
        _with_rider(body, nr + npar, len(outs), rider, (T // tb,)), grid=(T // tb,), in_specs=in_specs + r_in,
        out_specs=[pl.BlockSpec((tb, w), lambda i: (i, 0)) for w, _ in outs] + r_out,
        out_shape=[_sds((T, w), dt) for w, dt in outs] + r_shape, scratch_shapes=r_scratch,
        compiler_params=_cparams(("parallel",) if rider is None else ("arbitrary",)), name=name,
    )(*[r[0] for r in rows], *pars, *r_args)


def _rowwise_bwd(name, fn, rows, pars, cot_rows, cot_fn, row_out, tb):
    rows = [r if isinstance(r, tuple) else (r, 0, r.shape[1]) for r in rows]
    cot_rows = [r if isinstance(r, tuple) else (r, 0, r.shape[1]) for r in cot_rows]
    T = rows[0][0].shape[0]
    tb = min(tb, T)
    nr, npar, nc = len(rows), len(pars), len(cot_rows)
    want, want_dt = [], []
    for k, dt in enumerate(row_out):
        for d in (() if dt is None else dt if isinstance(dt, tuple) else (dt,)):
            want.append(k)
            want_dt.append(d)

    def body(*refs):
        i = pl.program_id(0)
        rv = [r[...].astype(F32) for r in refs[:nr]]
        pv = [p[...] for p in refs[nr:nr + npar]]
        cv = [c[...].astype(F32) for c in refs[nr + npar:nr + npar + nc]]
        o_refs = refs[nr + npar + nc:]
        _, vjp = jax.vjp(fn, *rv, *pv)
        grads = vjp(tuple(cot_fn(*cv)))
        for o, k in zip(o_refs[:len(want)], want):
            o[...] = grads[k].astype(o.dtype)
        p_refs = o_refs[len(want):]

        @pl.when(i == 0)
        def _():
            for o in p_refs:
                o[...] = jnp.zeros_like(o)

        for o, g in zip(p_refs, grads[nr:]):
            o[...] += g

    in_specs = [pl.BlockSpec((tb, w), lambda i, c=c: (i, c)) for _, c, w in rows]
    in_specs += [pl.BlockSpec(p.shape, lambda i, n=p.ndim: (0,) * n) for p in pars]
    in_specs += [pl.BlockSpec((tb, w), lambda i, c=c: (i, c)) for _, c, w in cot_rows]
    out_specs = [pl.BlockSpec((tb, rows[k][2]), lambda i: (i, 0)) for k in want]
    out_specs += [pl.BlockSpec(p.shape, lambda i, n=p.ndim: (0,) * n) for p in pars]
    out_shape = [_sds((T, rows[k][2]), d) for k, d in zip(want, want_dt)] + [_sds(p.shape, F32) for p in pars]
    return pl.pallas_call(
        body, grid=(T // tb,), in_specs=in_specs, out_specs=out_specs, out_shape=out_shape,
        compiler_params=_cparams(("arbitrary",)), name=name,
    )(*[r[0] for r in rows], *pars, *[r[0] for r in cot_rows])


def _f_rms(x, g):
    r = lax.rsqrt(jnp.mean(x * x, axis=-1, keepdims=True) + EPS)
    return (x * r * g,)


def _f_lru_gates(u, wa, ba, wx, bx, lam):
    ub = u.astype(BF16)
    ra, rx = [], []
    for k in range(D_MODEL // LANES):
        uk = ub[:, LANES * k:LANES * (k + 1)]
        ra.append(jnp.dot(uk, wa[k].astype(BF16), preferred_element_type=F32))
        rx.append(jnp.dot(uk, wx[k].astype(BF16), preferred_element_type=F32))
    r = jax.nn.sigmoid(jnp.concatenate(ra, axis=1) + ba)
    i = jax.nn.sigmoid(jnp.concatenate(rx, axis=1) + bx)
    log_a = -LRU_C * r * jax.nn.softplus(-lam)
    a = jnp.exp(log_a)
    t = jnp.tanh(log_a)
    b = jnp.sqrt(-2.0 * t / (1.0 - t)) * (i * u)
    return a, b


def _f_post(hl, lgate, yssd, xpre_s, z, d_e, ng):
    ya = jax.nn.gelu(lgate) * hl
    y = (yssd + d_e * jax.nn.silu(xpre_s)) * jax.nn.silu(z)
    gw = SSD_INNER // SSD_GROUPS
    parts = []
    for g in range(SSD_GROUPS):
        yg = y[:, gw * g:gw * (g + 1)]
        parts.append(yg * lax.rsqrt(jnp.mean(yg * yg, axis=-1, keepdims=True) + EPS))
    return ya, jnp.concatenate(parts, axis=1) * ng


def _f_merge(m_a, m_b, gates, bg):
    g = jax.nn.sigmoid(gates + bg)
    return (g[:, :D_MODEL] * m_a + g[:, D_MODEL:] * m_b,)


def _f_act(gu):
    return (jax.nn.silu(gu[:, :D_FF]) * gu[:, D_FF:],)


def _ident(*c):
    return c


def _loss_head(h, target, nf):
    T = h.shape[0]
    tb = min(256, T)

    def body(h_ref, t_ref, nf_ref, dh_ref, dnf_ref, loss_ref, dhb_ref):
        i = pl.program_id(0)

        @pl.when(i == 0)
        def _():
            dnf_ref[...] = jnp.zeros_like(dnf_ref)
            loss_ref[...] = jnp.zeros_like(loss_ref)

        (y,), vjp = jax.vjp(_f_rms, h_ref[...], nf_ref[...])
        err = y - t_ref[...]
        dh, dnf = vjp((err * (1.0 / D_MODEL),))
        dh_ref[...] = dh
        dhb_ref[...] = dh.astype(BF16)
        dnf_ref[...] += dnf
        part = 0.5 * jnp.sum(jnp.mean(err * err, axis=-1, keepdims=True), axis=0, keepdims=True)
        loss_ref[...] += jnp.broadcast_to(part, loss_ref.shape)

    return pl.pallas_call(
        body, grid=(T // tb,),
        in_specs=[pl.BlockSpec((tb, D_MODEL), lambda i: (i, 0)), pl.BlockSpec((tb, D_MODEL), lambda i: (i, 0)),
                  pl.BlockSpec((1, D_MODEL), lambda i: (0, 0))],
        out_specs=[pl.BlockSpec((tb, D_MODEL), lambda i: (i, 0)), pl.BlockSpec((1, D_MODEL), lambda i: (0, 0)),
                   pl.BlockSpec((8, LANES), lambda i: (0, 0)), pl.BlockSpec((tb, D_MODEL), lambda i: (i, 0))],
        out_shape=[_sds((T, D_MODEL), F32), _sds((1, D_MODEL), F32), _sds((8, LANES), F32),
                   _sds((T, D_MODEL), BF16)],
        compiler_params=_cparams(("arbitrary",)), name="loss_head",
    )(h, target, nf)


CONV_TC = 1024
HALO = 16


def _conv_fwd(name, x, w8, b):
    T, C = x.shape
    tb = min(512, T)
    nb = tb // HALO

    def body(x_ref, xp_ref, w_ref, b_ref, y_ref, sc):
        i = pl.program_id(0)
        xv = x_ref[...].astype(F32)
        sc[pl.ds(HALO, tb), :] = xv
        sc[pl.ds(0, HALO), :] = jnp.where(i > 0, xp_ref[...].astype(F32), 0.0)
        acc = b_ref[...] + w_ref[3:4, :] * xv
        for k in range(3):
            acc = acc + w_ref[k:k + 1, :] * sc[pl.ds(HALO - 3 + k, tb), :]
        y_ref[...] = acc

    return pl.pallas_call(
        body, grid=(T // tb, C // CONV_TC),
        in_specs=[pl.BlockSpec((tb, CONV_TC), lambda i, j: (i, j)),
                  pl.BlockSpec((HALO, CONV_TC), lambda i, j: (jnp.maximum(i * nb - 1, 0), j)),
                  pl.BlockSpec((8, CONV_TC), lambda i, j: (0, j)),
                  pl.BlockSpec((1, CONV_TC), lambda i, j: (0, j))],
        out_specs=pl.BlockSpec((tb, CONV_TC), lambda i, j: (i, j)),
        out_shape=_sds((T, C), F32),
        scratch_shapes=[pltpu.VMEM((tb + HALO, CONV_TC), F32)],
        compiler_params=_cparams(("parallel", "parallel")), name=name,
    )(x, x, w8, b)


def _conv_bwd(name, x, dy, w8):
    T, C = x.shape
    tb = min(512, T)
    nb = tb // HALO
    nt = T // tb

    def body(x_ref, xp_ref, dy_ref, dyn_ref, w_ref, dx_ref, dwb_ref, scx, scd):
        i = pl.program_id(1)
        dyv = dy_ref[...]
        xv = x_ref[...].astype(F32)
        scx[pl.ds(HALO, tb), :] = xv
        scx[pl.ds(0, HALO), :] = jnp.where(i > 0, xp_ref[...].astype(F32), 0.0)
        scd[pl.ds(0, tb), :] = dyv
        scd[pl.ds(tb, HALO), :] = jnp.where(i < nt - 1, dyn_ref[...], 0.0)
        dx = w_ref[3:4, :] * dyv
        rows = []
        for k in range(3):
            dx = dx + w_ref[k:k + 1, :] * scd[pl.ds(3 - k, tb), :]
            rows.append(jnp.sum(dyv * scx[pl.ds(HALO - 3 + k, tb), :], axis=0, keepdims=True))
        rows.append(jnp.sum(dyv * xv, axis=0, keepdims=True))
        rows.append(jnp.sum(dyv, axis=0, keepdims=True))
        rows.append(jnp.zeros((3, CONV_TC), F32))
        dx_ref[...] = dx.astype(dx_ref.dtype)

        @pl.when(i == 0)
        def _():
            dwb_ref[...] = jnp.zeros_like(dwb_ref)

        dwb_ref[...] += jnp.concatenate(rows, axis=0)

    return pl.pallas_call(
        body, grid=(C // CONV_TC, nt),
        in_specs=[pl.BlockSpec((tb, CONV_TC), lambda j, i: (i, j)),
                  pl.BlockSpec((HALO, CONV_TC), lambda j, i: (jnp.maximum(i * nb - 1, 0), j)),
                  pl.BlockSpec((tb, CONV_TC), lambda j, i: (i, j)),
                  pl.BlockSpec((HALO, CONV_TC), lambda j, i: (jnp.minimum((i + 1) * nb, T // HALO - 1), j)),
                  pl.BlockSpec((8, CONV_TC), lambda j, i: (0, j))],
        out_specs=[pl.BlockSpec((tb, CONV_TC), lambda j, i: (i, j)),
                   pl.BlockSpec((8, CONV_TC), lambda j, i: (0, j))],
        out_shape=[_sds((T, C), BF16), _sds((8, C), F32)],
        scratch_shapes=[pltpu.VMEM((tb + HALO, CONV_TC), F32), pltpu.VMEM((tb + HALO, CONV_TC), F32)],
        compiler_params=_cparams(("parallel", "arbitrary")), name=name,
    )(x, x, dy, dy, w8)


SCAN_TB = 512


def _lru_scan_fwd(a, b):
    T, C = a.shape
    tb = min(SCAN_TB, T)

    def body(a_ref, b_ref, h_ref, hp_ref, carry):
        @pl.when(pl.program_id(0) == 0)
        def _():
            carry[...] = jnp.zeros_like(carry)

        def group(gi, h):
            r0 = pl.multiple_of(gi * 8, 8)
            at = a_ref[pl.ds(r0, 8), :]
            bt = b_ref[pl.ds(r0, 8), :]
            hs, hps = [], []
            for r in range(8):
                hps.append(h)
                h = at[r:r + 1, :] * h + bt[r:r + 1, :]
                hs.append(h)
            h_ref[pl.ds(r0, 8), :] = jnp.concatenate(hs, axis=0)
            hp_ref[pl.ds(r0, 8), :] = jnp.concatenate(hps, axis=0)
            return h

        carry[0:1, :] = lax.fori_loop(0, tb // 8, group, carry[0:1, :])

    spec = pl.BlockSpec((tb, C), lambda i: (i, 0))
    return pl.pallas_call(
        body, grid=(T // tb,), in_specs=[spec, spec], out_specs=[spec, spec],
        out_shape=[_sds((T, C), F32), _sds((T, C), F32)],
        scratch_shapes=[pltpu.VMEM((8, C), F32)],
        compiler_params=_cparams(("arbitrary",)), name="lru_scan_fwd",
    )(a, b)


def _lru_scan_bwd(a, dh):
    T, C = a.shape
    tb = min(SCAN_TB, T)
    nt = T // tb

    def body(a_ref, dh_ref, g_ref, carry):
        @pl.when(pl.program_id(0) == 0)
        def _():
            carry[...] = jnp.zeros_like(carry)

        def group(gi, c):
            r0 = pl.multiple_of((tb // 8 - 1 - gi) * 8, 8)
            at = a_ref[pl.ds(r0, 8), :]
            dt = dh_ref[pl.ds(r0, 8), :]
            gs = [None] * 8
            for r in range(7, -1, -1):
                g = dt[r:r + 1, :] + c
                c = at[r:r + 1, :] * g
                gs[r] = g
            g_ref[pl.ds(r0, 8), :] = jnp.concatenate(gs, axis=0)
            return c

        carry[0:1, :] = lax.fori_loop(0, tb // 8, group, carry[0:1, :])

    spec = pl.BlockSpec((tb, C), lambda i: (nt - 1 - i, 0))
    return pl.pallas_call(
        body, grid=(nt,), in_specs=[spec, spec], out_specs=spec,
        out_shape=_sds((T, C), F32),
        scratch_shapes=[pltpu.VMEM((8, C), F32)],
        compiler_params=_cparams(("arbitrary",)), name="lru_scan_bwd",
    )(a, dh)


def _ssd_chunk(xpre, dtraw, state, dtb, alog):
    xc = jax.nn.silu(xpre)
    xs = xc[:, :SSD_INNER]
    bm = xc[:, SSD_INNER:SSD_INNER + SSD_GROUPS * SSD_STATE]
    cm = xc[:, SSD_INNER + SSD_GROUPS * SSD_STATE:]
    dt = jax.nn.softplus(dtraw + dtb)
    a = dt * (-jnp.exp(alog))
    ltri = (lax.broadcasted_iota(jnp.int32, (CHUNK, CHUNK), 0)
            >= lax.broadcasted_iota(jnp.int32, (CHUNK, CHUNK), 1)).astype(F32)
    a_cs = jnp.dot(ltri, a, precision=HIGHEST, preferred_element_type=F32)
    npair = SSD_HEADS // 2
    pi = lax.broadcasted_iota(jnp.int32, (npair, LANES), 0)
    hi = lax.broadcasted_iota(jnp.int32, (npair, LANES), 1)
    sel_even = (hi == 2 * pi).astype(F32)
    sel_odd = (hi == 2 * pi + 1).astype(F32)
    top = lax.broadcasted_iota(jnp.int32, (2 * CHUNK, LANES), 0) < CHUNK

    def pair_transpose(v):
        v2 = jnp.concatenate([v, v], axis=0)
        dn = (((1,), (1,)), ((), ()))
        return (lax.dot_general(sel_even, jnp.where(top, v2, 0.0), dn, precision=HIGHEST, preferred_element_type=F32)
                + lax.dot_general(sel_odd, jnp.where(top, 0.0, v2), dn, precision=HIGHEST, preferred_element_type=F32))

    a_t2 = pair_transpose(a_cs)
    dt_t2 = pair_transpose(dt)
    a_last = a_cs[CHUNK - 1:CHUNK, :]
    dte = jnp.exp(a_last - a_cs) * dt
    cd = jnp.exp(a_last)
    lane = lax.broadcasted_iota(jnp.int32, (CHUNK, LANES), 1)
    left = lane < SSD_HEAD_DIM
    right = jnp.logical_not(left)
    left1 = left[0:1]
    tril2 = lax.broadcasted_iota(jnp.int32, (CHUNK, LANES), 0) >= (lane & (SSD_HEAD_DIM - 1))
    gw = SSD_INNER // SSD_GROUPS
    ys, news = [], []
    for g in range(SSD_GROUPS):
        bg = bm[:, SSD_STATE * g:SSD_STATE * (g + 1)].astype(BF16)
        cg = cm[:, SSD_STATE * g:SSD_STATE * (g + 1)].astype(BF16)
        bg2 = jnp.concatenate([bg, bg], axis=0)
        scores2 = lax.dot_general(cg, bg2, (((1,), (1,)), ((), ())), preferred_element_type=F32)
        coff = jnp.dot(cg, state[:, gw * g:gw * (g + 1)].astype(BF16), preferred_element_type=F32)
        for j in range(gw // LANES):
            lo = gw * g + LANES * j
            p = lo // LANES
            h0 = 2 * p
            xp = xs[:, lo:lo + LANES]
            col = jnp.where(left, a_cs[:, h0:h0 + 1], a_cs[:, h0 + 1:h0 + 2])
            dm = jnp.exp(jnp.where(tril2, col - a_t2[p:p + 1, :], -1e30)) * dt_t2[p:p + 1, :]
            sd = (scores2 * dm).astype(BF16)
            x_bd = jnp.concatenate([jnp.where(left, xp, 0.0), jnp.where(right, xp, 0.0)], axis=0).astype(BF16)
            acc = jnp.dot(sd, x_bd, preferred_element_type=F32)
            ys.append(acc + coff[:, LANES * j:LANES * (j + 1)] * jnp.exp(col))
            dtee = jnp.where(left, dte[:, h0:h0 + 1], dte[:, h0 + 1:h0 + 2])
            xw = (xp * dtee).astype(BF16)
            cde = jnp.where(left1, cd[:, h0:h0 + 1], cd[:, h0 + 1:h0 + 2])
            news.append(state[:, lo:lo + LANES] * cde
                        + lax.dot_general(bg, xw, (((0,), (0,)), ((), ())), preferred_element_type=F32))
    return jnp.concatenate(ys, axis=1), jnp.concatenate(news, axis=1)


def _ssd_fwd(name, xpre, dtraw, dtb, alog, rider=None):
    T = xpre.shape[0]
    n = T // CHUNK
    r_args, r_in, r_out, r_shape, r_scratch = _rider_parts(rider)

    def body(xp, dr, dtb_ref, al_ref, y_ref, st_ref, state):
        @pl.when(pl.program_id(0) == 0)
        def _():
            state[...] = jnp.zeros_like(state)

        st_ref[0] = state[...]
        y, new = _ssd_chunk(xp[...], dr[...], state[...], dtb_ref[...], al_ref[...])
        y_ref[...] = y
        state[...] = new

    small = pl.BlockSpec((1, LANES), lambda c: (0, 0))
    return pl.pallas_call(
        _with_rider(body, 4, 2, rider, (n,)), grid=(n,),
        in_specs=[pl.BlockSpec((CHUNK, SSD_CONV_DIM), lambda c: (c, 0)),
                  pl.BlockSpec((CHUNK, DT_PAD), lambda c: (c, 0)), small, small] + r_in,
        out_specs=[pl.BlockSpec((CHUNK, SSD_INNER), lambda c: (c, 0)),
                   pl.BlockSpec((1, SSD_STATE, SSD_INNER), lambda c: (c, 0, 0))] + r_out,
        out_shape=[_sds((T, SSD_INNER), F32), _sds((n, SSD_STATE, SSD_INNER), F32)] + r_shape,
        scratch_shapes=[pltpu.VMEM((SSD_STATE, SSD_INNER), F32)] + r_scratch,
        compiler_params=_cparams(("arbitrary",)), name=name,
    )(xpre, dtraw, dtb, alog, *r_args)


def _ssd_bwd(name, xpre, dtraw, states, dy, dxs_extra, dtb, alog, rider=None):
    T = xpre.shape[0]
    n = T // CHUNK
    r_args, r_in, r_out, r_shape, r_scratch = _rider_parts(rider)

    def body(xp, dr, st, dy_ref, dx_ref, dtb_ref, al_ref, dxp_ref, ddr_ref, ddtb_ref, dal_ref, dstate):
        @pl.when(pl.program_id(0) == 0)
        def _():
            dstate[...] = jnp.zeros_like(dstate)
            ddtb_ref[...] = jnp.zeros_like(ddtb_ref)
            dal_ref[...] = jnp.zeros_like(dal_ref)

        _, vjp = jax.vjp(_ssd_chunk, xp[...], dr[...], st[0], dtb_ref[...], al_ref[...])
        dxp, ddr, ds, db, da = vjp((dy_ref[...], dstate[...]))
        dxp_ref[:, :SSD_INNER] = dxp[:, :SSD_INNER] + dx_ref[...]
        dxp_ref[:, SSD_INNER:] = dxp[:, SSD_INNER:]
        ddr_ref[...] = ddr.astype(ddr_ref.dtype)
        dstate[...] = ds
        ddtb_ref[...] += db
        dal_ref[...] += da

    def rev(c):
        return (n - 1 - c, 0)

    small = pl.BlockSpec((1, LANES), lambda c: (0, 0))
    return pl.pallas_call(
        _with_rider(body, 7, 4, rider, (n,)), grid=(n,),
        in_specs=[pl.BlockSpec((CHUNK, SSD_CONV_DIM), rev), pl.BlockSpec((CHUNK, DT_PAD), rev),
                  pl.BlockSpec((1, SSD_STATE, SSD_INNER), lambda c: (n - 1 - c, 0, 0)),
                  pl.BlockSpec((CHUNK, SSD_INNER), rev), pl.BlockSpec((CHUNK, SSD_INNER), rev), small, small] + r_in,
        out_specs=[pl.BlockSpec((CHUNK, SSD_CONV_DIM), rev), pl.BlockSpec((CHUNK, DT_PAD), rev), small, small] + r_out,
        out_shape=[_sds((T, SSD_CONV_DIM), F32), _sds((T, DT_PAD), BF16), _sds((1, LANES), F32),
                   _sds((1, LANES), F32)] + r_shape,
        scratch_shapes=[pltpu.VMEM((SSD_STATE, SSD_INNER), F32)] + r_scratch,
        compiler_params=_cparams(("arbitrary",)), name=name,
    )(xpre, dtraw, states, dy, dxs_extra, dtb, alog, *r_args)


HBM_SPEC = pl.BlockSpec(memory_space=pltpu.HBM)
N_PEER = N_DEV - 1


def _position():
    return lax.axis_index("x"), lax.axis_index("y"), lax.axis_index("c")


def _all_gather(name, blks):
    return _exchange_call(name, _gather_phases, blks, [_sds((N_DEV,) + b.shape, b.dtype) for b in blks])


def _scatter_exchange(name, gs):
    return _exchange_call(name, _scatter_phases, gs, [_sds(g.shape, g.dtype) for g in gs])


def _exchange_scratch(na):
    return [pltpu.SemaphoreType.DMA((na * N_PEER,)), pltpu.SemaphoreType.DMA((na * N_PEER,)),
            pltpu.SemaphoreType.DMA((na,))]


def _exchange_call(name, phases, arrays, out_shape):
    na = len(arrays)

    def body(*refs):
        begin, finish = phases(refs[:na], refs[na:2 * na], *refs[2 * na:])
        begin()
        finish()

    return pl.pallas_call(
        body, out_shape=out_shape, in_specs=[HBM_SPEC] * na, out_specs=[HBM_SPEC] * na,
        scratch_shapes=_exchange_scratch(na), name=name,
    )(*arrays)


def _gather_phases(x_refs, out_refs, send_sems, recv_sems, local_sems):
    na = len(x_refs)
    x, y, c = _position()
    me, sibling = (x, y, c), (x, y, 1 - c)
    chips = [(1 - x, y), (x, 1 - y), (1 - x, 1 - y)]

    def slot(a, px, py, pc):
        return out_refs[a].at[4 * px + 2 * py + pc]

    def copy(a, k, block, to, src=None):
        return pltpu.make_async_remote_copy(
            src_ref=slot(a, *block) if src is None else src, dst_ref=slot(a, *block),
            send_sem=send_sems.at[a * N_PEER + k], recv_sem=recv_sems.at[a * N_PEER + k],
            device_id=to, device_id_type=MESH)

    mine = [pltpu.make_async_copy(x_refs[a], slot(a, *me), local_sems.at[a]) for a in range(na)]
    first = []
    for a in range(na):
        first.append(copy(a, 0, me, sibling, src=x_refs[a]))
        first += [copy(a, 1 + j, me, (*chip, c), src=x_refs[a]) for j, chip in enumerate(chips)]
    passed = [copy(a, 4 + j, (*chip, c), sibling) for j, chip in enumerate(chips) for a in range(na)]

    def begin():
        for cp in mine + first:
            cp.start()

    def finish():
        for j, chip in enumerate(chips):
            for a in range(na):
                copy(a, 1 + j, (*chip, c), me).wait_recv()
                passed[j * na + a].start()
        for a in range(na):
            copy(a, 0, sibling, me).wait_recv()
            for j, chip in enumerate(chips):
                copy(a, 4 + j, (*chip, 1 - c), me).wait_recv()
        for cp in first + passed:
            cp.wait_send()
        for cp in mine:
            cp.wait()

    return begin, finish


def _scatter_phases(g_refs, q_refs, send_sems, recv_sems, local_sems):
    na = len(g_refs)
    x, y, c = _position()
    me = 4 * x + 2 * y + c
    mine, sends, recvs = [], [], []
    for a in range(na):
        mine.append(pltpu.make_async_copy(g_refs[a].at[me], q_refs[a].at[me], local_sems.at[a]))
        for k in range(1, N_DEV):
            px, py, pc = (x + (k >> 2)) % 2, (y + ((k >> 1) & 1)) % 2, (c + (k & 1)) % 2
            peer = 4 * px + 2 * py + pc
            sem = a * N_PEER + k - 1
            sends.append(pltpu.make_async_remote_copy(
                src_ref=g_refs[a].at[peer], dst_ref=q_refs[a].at[me], send_sem=send_sems.at[sem],
                recv_sem=recv_sems.at[sem], device_id=(px, py, pc), device_id_type=MESH))
            recvs.append(pltpu.make_async_remote_copy(
                src_ref=g_refs[a].at[me], dst_ref=q_refs[a].at[peer], send_sem=send_sems.at[sem],
                recv_sem=recv_sems.at[sem], device_id=(px, py, pc), device_id_type=MESH))

    def begin():
        for cp in mine + sends:
            cp.start()

    def finish():
        for cp in recvs:
            cp.wait_recv()
        for cp in sends:
            cp.wait_send()
        for cp in mine:
            cp.wait()

    return begin, finish


def _with_rider(body, n_in, n_out, rider, grid):
    if rider is None:
        return body
    phases, arrays, _ = rider
    na = len(arrays)

    def carried(*refs):
        ins, r_in = refs[:n_in], refs[n_in:n_in + na]
        outs = refs[n_in + na:n_in + na + n_out]
        r_out = refs[n_in + na + n_out:n_in + 2 * na + n_out]
        scratch, r_scratch = refs[n_in + 2 * na + n_out:-3], refs[-3:]
        begin, finish = phases(r_in, r_out, *r_scratch)
        first = last = None
        for d, g in enumerate(grid):
            at_start, at_end = pl.program_id(d) == 0, pl.program_id(d) == g - 1
            first = at_start if first is None else first & at_start
            last = at_end if last is None else last & at_end
        pl.when(first)(begin)
        body(*ins, *outs, *scratch)
        pl.when(last)(finish)

    return carried


def _rider_parts(rider):
    if rider is None:
        return [], [], [], [], []
    _, arrays, out_shape = rider
    na = len(arrays)
    return list(arrays), [HBM_SPEC] * na, [HBM_SPEC] * na, list(out_shape), _exchange_scratch(na)


def _gather_rider(blks):
    return (_gather_phases, blks, [_sds((N_DEV,) + b.shape, b.dtype) for b in blks])


def _scatter_rider(gs):
    return (_scatter_phases, gs, [_sds(g.shape, g.dtype) for g in gs])


def _adamw_sum(name, w, m, v, recvs, rider=None):
    L, r, c = w.shape
    tr = r if r <= 512 else max(t for t in range(16, 257, 16) if r % t == 0)
    assert len(recvs) == L
    nr = r // tr

    def body(w_ref, m_ref, v_ref, *refs):
        q_refs = refs[:L]
        g_ref, d_ref, m2_ref, v2_ref = refs[L:]
        l = pl.program_id(0)
        for ll in range(L):
            @pl.when(l == ll)
            def _(q_ref=q_refs[ll]):
                gv = q_ref[0].astype(F32)
                for s in range(1, N_DEV):
                    gv = gv + q_ref[s].astype(F32)
                m2 = ADAM_B1 * m_ref[0] + (1.0 - ADAM_B1) * gv
                v2 = ADAM_B2 * v_ref[0] + (1.0 - ADAM_B2) * jnp.square(gv)
                m_hat = m2 / (1.0 - ADAM_B1 ** ADAM_STEP)
                v_hat = v2 / (1.0 - ADAM_B2 ** ADAM_STEP)
                g_ref[0] = gv
                d_ref[0] = -ADAM_LR * (m_hat / (jnp.sqrt(v_hat) + ADAM_EPS) + ADAM_WD * w_ref[0])
                m2_ref[0] = m2
                v2_ref[0] = v2

    def q_index(ll):
        return lambda l, i: (0, jnp.where(l == ll, i, jnp.where(l > ll, nr - 1, 0)), 0)

    spec = pl.BlockSpec((1, tr, c), lambda l, i: (l, i, 0))
    r_args, r_in, r_out, r_shape, r_scratch = _rider_parts(rider)
    return pl.pallas_call(
        _with_rider(body, 3 + L, 4, rider, (L, nr)), grid=(L, nr),
        in_specs=[spec] * 3 + [pl.BlockSpec((N_DEV, tr, c), q_index(ll)) for ll in range(L)] + r_in,
        out_specs=[spec] * 4 + r_out, out_shape=[_sds((L, r, c), F32)] * 4 + r_shape, scratch_shapes=r_scratch,
        compiler_params=_cparams(("arbitrary", "arbitrary")), name=name,
    )(w, m, v, *recvs, *r_args)


def _sum_blocks(name, q, tr):
    R = q.shape[0] // N_DEV
    W = q.shape[1]
    tr = min(tr, R)
    assert R % tr == 0
    nb = R // tr

    def body(*refs):
        acc = refs[0][...].astype(F32)
        for r in refs[1:N_DEV]:
            acc = acc + r[...].astype(F32)
        refs[N_DEV][...] = acc

    return pl.pallas_call(
        body, grid=(nb,),
        in_specs=[pl.BlockSpec((tr, W), lambda i, s=s: (s * nb + i, 0)) for s in range(N_DEV)],
        out_specs=pl.BlockSpec((tr, W), lambda i: (i, 0)), out_shape=_sds((R, W), F32),
        compiler_params=_cparams(("parallel",)), name=name,
    )(*([q] * N_DEV))


def _adamw(name, w, g, m, v):
    R, C = w.shape
    tr = R
    if R > 512:
        tr = max(t for t in range(8, 513, 8) if R % t == 0)

    def body(w_ref, g_ref, m_ref, v_ref, d_ref, m2_ref, v2_ref):
        gv = g_ref[...]
        m2 = ADAM_B1 * m_ref[...] + (1.0 - ADAM_B1) * gv
        v2 = ADAM_B2 * v_ref[...] + (1.0 - ADAM_B2) * jnp.square(gv)
        m_hat = m2 / (1.0 - ADAM_B1 ** ADAM_STEP)
        v_hat = v2 / (1.0 - ADAM_B2 ** ADAM_STEP)
        d_ref[...] = -ADAM_LR * (m_hat / (jnp.sqrt(v_hat) + ADAM_EPS) + ADAM_WD * w_ref[...])
        m2_ref[...] = m2
        v2_ref[...] = v2

    spec = pl.BlockSpec((tr, C), lambda i: (i, 0))
    return pl.pallas_call(
        body, grid=(R // tr,), in_specs=[spec] * 4, out_specs=[spec] * 3,
        out_shape=[_sds((R, C), F32)] * 3, compiler_params=_cparams(("parallel",)), name=name,
    )(w, g, m, v)


BIG = ("w_in", "w_branch", "w_out", "w_ffn_in", "w_ffn_out")
BIG_ROW_SHARDED = {"w_in": False, "w_branch": True, "w_out": True, "w_ffn_in": True, "w_ffn_out": True}
TRANSPOSED = ("w_ffn_in",)
SMALL = ("norm1_g", "b_gate", "lru_conv_w", "lru_conv_b", "lru_w_a", "lru_b_a", "lru_w_x", "lru_b_x", "lru_lambda",
         "ssd_conv_w", "ssd_conv_b", "ssd_dt_bias", "ssd_A_log", "ssd_D", "ssd_norm_g", "norm2_g", "norm_f")
SMALL_ROWS = 256


def _pack(arrs, dtype, row_mult):
    parts = []
    for a in arrs:
        f = a.reshape(-1).astype(dtype)
        pad = (-f.shape[0]) % LANES
        if pad:
            f = jnp.concatenate([f, jnp.zeros((pad,), dtype)])
        parts.append(f)
    f = jnp.concatenate(parts)
    pad = (-f.shape[0]) % (LANES * row_mult)
    if pad:
        f = jnp.concatenate([f, jnp.zeros((pad,), dtype)])
    return f.reshape(-1, LANES)


def _unpack(flat, shapes, lead=()):
    f = flat.reshape(lead + (-1,))
    out, off = [], 0
    for s in shapes:
        n = int(np.prod(s))
        out.append(f[..., off:off + n].reshape(lead + tuple(s)))
        off += n + (-n) % LANES
    return out


def _full_from_shards(name, st):
    if BIG_ROW_SHARDED[name]:
        return st.reshape((-1,) + st.shape[2:])
    return jnp.transpose(st, (1, 0, 2)).reshape(st.shape[1], -1)


def _shards_from_full(name, full):
    if BIG_ROW_SHARDED[name]:
        return full.reshape((N_DEV, full.shape[0] // N_DEV) + full.shape[1:])
    return jnp.transpose(full.reshape(full.shape[0], N_DEV, -1), (1, 0, 2))


def _block_diag_tiles(w):
    z = jnp.zeros((8, 64, 64), w.dtype)
    w2 = w.reshape(8, 2, 64, 64)
    top = jnp.concatenate([w2[:, 0], z], axis=2)
    bot = jnp.concatenate([z, w2[:, 1]], axis=2)
    return jnp.concatenate([top, bot], axis=1)


def _block_diag_untile(t):
    return jnp.stack([t[:, :64, :64], t[:, 64:, 64:]], axis=1).reshape(16, 64, 64)


def _pad_lanes(a, width):
    return jnp.concatenate([a, jnp.zeros(a.shape[:-1] + (width - a.shape[-1],), a.dtype)], axis=-1)


def _pad_rows8(w):
    return jnp.concatenate([w, jnp.zeros((8 - w.shape[0],) + w.shape[1:], w.dtype)], axis=0)


REST = BIG[1:]


def _in_weights(shards):
    w_in = _full_from_shards("w_in", shards)
    seg = [w_in[:, IN_OFFS[k]:IN_OFFS[k + 1]] for k in range(6)]
    return {"w_lx": seg[0], "w_lg": seg[1], "w_z": seg[2], "w_xbc": seg[3], "w_dt": _pad_lanes(seg[4], DT_PAD),
            "w_g": seg[5]}


def _rest_weights(shards):
    full = {n: _full_from_shards(n, st) for n, st in zip(REST, shards)}
    return {"w_bra": full["w_branch"][:D_MODEL], "w_brb": full["w_branch"][D_MODEL:], "w_out": full["w_out"],
            "w_ffn_in_t": full["w_ffn_in"], "w_ffn_out": full["w_ffn_out"]}


def _small_params(l, small):
    p = {}
    for n in ("norm1_g", "b_gate", "lru_conv_b", "lru_b_a", "lru_b_x", "lru_lambda", "ssd_conv_b", "ssd_norm_g", "norm2_g"):
        p[n] = small[n][l].reshape(1, -1)
    p["lru_conv_w8"] = _pad_rows8(small["lru_conv_w"][l])
    p["ssd_conv_w8"] = _pad_rows8(small["ssd_conv_w"][l])
    p["wa"] = _block_diag_tiles(small["lru_w_a"][l])
    p["wx"] = _block_diag_tiles(small["lru_w_x"][l])
    p["dtb"] = _pad_lanes(small["ssd_dt_bias"][l].reshape(1, -1), DT_PAD)
    p["alog"] = _pad_lanes(small["ssd_A_log"][l].reshape(1, -1), DT_PAD)
    p["d_e"] = jnp.repeat(small["ssd_D"][l], SSD_HEAD_DIM).reshape(1, -1)
    return p


def _layer_fwd(l, h, p, riders):
    n = f"l{l}_"
    s = {"h_in": h}
    got = {}
    (xn,) = _rowwise_fwd(n + "rms1", _f_rms, [h], [p["norm1_g"]], [(D_MODEL, BF16)], 256)
    s["xn"] = xn
    for k in ("lx", "lg", "z", "xbc", "g", "dt"):
        s[k] = _mm(n + "in_" + k, [(xn, p["w_" + k])], "nn", out_dtype=F32 if k == "dt" else BF16)
    s["u"] = _conv_fwd(n + "lru_conv", s["lx"], p["lru_conv_w8"], p["lru_conv_b"])
    lru_pars = [p["wa"], p["lru_b_a"], p["wx"], p["lru_b_x"], p["lru_lambda"]]
    s["a"], b = _rowwise_fwd(n + "lru_gates", _f_lru_gates, [s["u"]], lru_pars, [(D_MODEL, F32)] * 2, 256)
    s["hl"], s["hprev"] = _lru_scan_fwd(s["a"], b)
    s["xpre"] = _conv_fwd(n + "ssd_conv", s["xbc"], p["ssd_conv_w8"], p["ssd_conv_b"])
    s["yssd"], s["states"], *got["ssd"] = _ssd_fwd(n + "ssd", s["xpre"], s["dt"], p["dtb"], p["alog"],
                                                   riders.get("ssd"))
    if "ssd" in riders:
        p.update(_rest_weights(got["ssd"]))
    post_rows = [s["hl"], s["lg"], s["yssd"], (s["xpre"], 0, SSD_INNER), s["z"]]
    s["ya"], s["yb"], *got["post"] = _rowwise_fwd(n + "post", _f_post, post_rows, [p["d_e"], p["ssd_norm_g"]],
                                                  [(D_MODEL, BF16), (SSD_INNER, BF16)], 128, riders.get("post"))
    s["ma"] = _mm(n + "br_a", [(s["ya"], p["w_bra"])], "nn", out_dtype=BF16)
    s["mb"] = _mm(n + "br_b", [(s["yb"], p["w_brb"])], "nn", out_dtype=BF16)
    (s["merged"],) = _rowwise_fwd(n + "merge", _f_merge, [s["ma"], s["mb"], s["g"]], [p["b_gate"]],
                                  [(D_MODEL, BF16)], 256)
    s["h_mid"] = _mm(n + "out", [(s["merged"], p["w_out"])], "nn", res=h)
    (s["xn2"],) = _rowwise_fwd(n + "rms2", _f_rms, [s["h_mid"]], [p["norm2_g"]], [(D_MODEL, BF16)], 256)
    s["gu"] = _mm(n + "ffn_in", [(s["xn2"], p["w_ffn_in_t"])], "nt", out_dtype=BF16, rider=riders.get("ffn_in"))
    if "ffn_in" in riders:
        s["gu"], got["ffn_in"] = s["gu"]
    (s["act"],) = _rowwise_fwd(n + "act", _f_act, [s["gu"]], [], [(D_FF, BF16)], 256)
    h_out = _mm(n + "ffn_out", [(s["act"], p["w_ffn_out"])], "nn", res=s["h_mid"], rider=riders.get("ffn_out"))
    if "ffn_out" in riders:
        h_out, got["ffn_out"] = h_out
    return h_out, s, got


def _f_rms_res(x, g):
    return _f_rms(x, g)[0], x


def _layer_bwd(l, dh, dh_b, s, p, above, first_layer):
    n = f"l{l}_b_"
    gw, gs = {}, {}
    d_act = _mm(n + "d_act", [(dh_b, p["w_ffn_out"])], "nt", out_dtype=BF16)
    gw["w_ffn_out"] = _mm_tn(n + "dw_ffn_out", s["act"], dh_b)
    (d_gu,) = _rowwise_bwd(n + "act", _f_act, [s["gu"]], [], [d_act], _ident, [BF16], 256)
    d_xn2 = _mm(n + "d_xn2", [(d_gu, p["w_ffn_in_t"])], "nn")
    gw["w_ffn_in"] = _mm_tn(n + "dw_ffn_in", d_gu, s["xn2"])
    dh_mid, dh_mid_b, gs["norm2_g"] = _rowwise_bwd(n + "rms2", _f_rms_res, [s["h_mid"]], [p["norm2_g"]], [d_xn2, dh],
                                                   _ident, [(F32, BF16)], 256)
    d_merged = _mm(n + "d_merged", [(dh_mid_b, p["w_out"])], "nt", out_dtype=BF16)
    gw["w_out"] = _mm_tn(n + "dw_out", s["merged"], dh_mid_b)
    d_ma, d_mb, d_g, gs["b_gate"] = _rowwise_bwd(n + "merge", _f_merge, [s["ma"], s["mb"], s["g"]], [p["b_gate"]],
                                                 [d_merged], _ident, [BF16, BF16, BF16], 256)
    d_ya = _mm(n + "d_ya", [(d_ma, p["w_bra"])], "nt", out_dtype=BF16)
    d_yb = _mm(n + "d_yb", [(d_mb, p["w_brb"])], "nt", out_dtype=BF16)
    gw["w_branch"] = jnp.concatenate([_mm_tn(n + "dw_bra", s["ya"], d_ma), _mm_tn(n + "dw_brb", s["yb"], d_mb)], axis=0)
    post_rows = [s["hl"], s["lg"], s["yssd"], (s["xpre"], 0, SSD_INNER), s["z"]]
    d_hl, d_lg, d_yssd, d_xs, d_z, d_de, gs["ssd_norm_g"] = _rowwise_bwd(
        n + "post", _f_post, post_rows, [p["d_e"], p["ssd_norm_g"]], [d_ya, d_yb], _ident,
        [F32, BF16, F32, F32, BF16], 128)
    gs["ssd_D"] = d_de.reshape(SSD_HEADS, SSD_HEAD_DIM).sum(axis=1)
    contrib = [_shards_from_full(k, gw[k]) for k in REST] + ([] if above is None else [above])
    d_xpre, d_dt, d_dtb, d_alog, *carried = _ssd_bwd(n + "ssd", s["xpre"], s["dt"], s["states"], d_yssd, d_xs,
                                                     p["dtb"], p["alog"], _scatter_rider(contrib))
    arrived = dict(zip(REST, carried))
    arrived_above = None if above is None else carried[len(REST)]
    gs["ssd_dt_bias"] = d_dtb[0, :SSD_HEADS]
    gs["ssd_A_log"] = d_alog[0, :SSD_HEADS]
    d_xbc, dwb = _conv_bwd(n + "ssd_conv", s["xbc"], d_xpre, p["ssd_conv_w8"])
    gs["ssd_conv_w"], gs["ssd_conv_b"] = dwb[:4], dwb[4]
    g_scan = _lru_scan_bwd(s["a"], d_hl)
    lru_pars = [p["wa"], p["lru_b_a"], p["wx"], p["lru_b_x"], p["lru_lambda"]]
    d_u, d_wa, gs["lru_b_a"], d_wx, gs["lru_b_x"], gs["lru_lambda"] = _rowwise_bwd(
        n + "lru_gates", _f_lru_gates, [s["u"]], lru_pars, [g_scan, s["hprev"]],
        lambda g, hp: (g * hp, g), [F32], 256)
    gs["lru_w_a"], gs["lru_w_x"] = _block_diag_untile(d_wa), _block_diag_untile(d_wx)
    d_lx, dwb = _conv_bwd(n + "lru_conv", s["lx"], d_u, p["lru_conv_w8"])
    gs["lru_conv_w"], gs["lru_conv_b"] = dwb[:4], dwb[4]
    segs = [("lx", d_lx), ("lg", d_lg), ("z", d_z), ("xbc", d_xbc), ("dt", d_dt), ("g", d_g)]
    dws = [_mm_tn(n + "dw_in_" + k, s["xn"], d) for k, d in segs]
    dws[4] = dws[4][:, :IN_WIDTHS[4]]
    below = _shards_from_full("w_in", jnp.concatenate(dws, axis=1))
    pairs_a = [(d, p["w_" + k]) for k, d in segs[:3]]
    pairs_b = [(d, p["w_" + k]) for k, d in segs[3:]]
    if first_layer:
        half = below.shape[1] // 2
        d_xn, (top,) = _mm(n + "d_xn_a", pairs_a, "nt", rider=_scatter_rider([below[:, :half]]))
        d_xn, (bottom,) = _mm(n + "d_xn_b", pairs_b, "nt", res=d_xn, rider=_scatter_rider([below[:, half:]]))
        arrived["w_in"] = jnp.concatenate([top, bottom], axis=1)
        below = None
    else:
        d_xn = _mm(n + "d_xn_a", pairs_a, "nt")
        d_xn = _mm(n + "d_xn_b", pairs_b, "nt", res=d_xn)
    dh_in, dh_in_b, gs["norm1_g"] = _rowwise_bwd(n + "rms1", _f_rms_res, [s["h_in"]], [p["norm1_g"]], [d_xn, dh_mid],
                                                 _ident, [(F32, BF16)], 256)
    for k in ("norm1_g", "norm2_g", "b_gate", "ssd_norm_g", "lru_b_a", "lru_b_x", "lru_lambda"):
        gs[k] = gs[k].reshape(-1)
    return dh_in, dh_in_b, gs, arrived, arrived_above, below, gw


def _step(inp):
    x = inp["x"][0]
    target = inp["loss_target"][0]
    dev = 4 * lax.axis_index("x") + 2 * lax.axis_index("y") + lax.axis_index("c")

    def mine(l, names):
        return [(inp[n][l].T if n in TRANSPOSED else inp[n][l]).astype(BF16) for n in names]

    first = _all_gather("gather_first", mine(0, ("w_in",)) + [inp["lru_conv_w"], inp["ssd_conv_w"]])
    small = {n: inp[n] for n in SMALL}
    small["lru_conv_w"] = jnp.moveaxis(first[1], 0, 2).reshape(DEPTH, 4, -1)
    small["ssd_conv_w"] = jnp.moveaxis(first[2], 0, 2).reshape(DEPTH, 4, -1)
    shards = {"w_in": first[0]}

    h, saved, params = x, [], []
    for l in range(DEPTH):
        p = _small_params(l, small)
        p.update(_in_weights(shards["w_in"]))
        riders = {}
        if l == 0:
            riders["ssd"] = _gather_rider(mine(0, REST))
        else:
            p.update(_rest_weights([shards[n] for n in REST]))
        if l + 1 < DEPTH:
            riders["post"] = _gather_rider(mine(l + 1, ("w_ffn_in", "w_ffn_out")))
            riders["ffn_in"] = _gather_rider(mine(l + 1, ("w_in",)))
            riders["ffn_out"] = _gather_rider(mine(l + 1, ("w_branch", "w_out")))
        h, s, got = _layer_fwd(l, h, p, riders)
        if l + 1 < DEPTH:
            shards = {"w_ffn_in": got["post"][0], "w_ffn_out": got["post"][1], "w_in": got["ffn_in"][0],
                      "w_branch": got["ffn_out"][0], "w_out": got["ffn_out"][1]}
        saved.append(s)
        params.append(p)
    dh, d_nf, loss_acc, dh_b = _loss_head(h, target, small["norm_f"].reshape(1, -1))
    loss = lax.psum(loss_acc[0, 0], ("x", "y", "c"))

    gss, received = [None] * DEPTH, [None] * DEPTH
    handed_down = None
    for l in reversed(range(DEPTH)):
        dh, dh_b, gss[l], received[l], arrived_above, handed_down, _ = _layer_bwd(
            l, dh, dh_b, saved[l], params[l], handed_down, l == 0)
        if arrived_above is not None:
            received[l + 1]["w_in"] = arrived_above
    grad_x = dh[None]
    out = {"loss": loss, "grad_x": grad_x}
    small_full = {n: jnp.stack([gss[l][n] for l in range(DEPTH)]) for n in SMALL if n != "norm_f"}
    small_full["norm_f"] = d_nf.reshape(-1)
    part = _pack([small_full[n] for n in SMALL], F32, SMALL_ROWS)
    for n in BIG:
        flip = (lambda a: jnp.transpose(a, (0, 2, 1))) if n in TRANSPOSED else (lambda a: a)
        res = _adamw_sum("adamw_" + n, flip(inp[n]), flip(inp["m_" + n]), flip(inp["v_" + n]),
                         [received[l][n] for l in range(DEPTH)], _gather_rider([part]) if n == "w_in" else None)
        if n == "w_in":
            everyone = res[4]
        out["grad_" + n], out["delta_" + n], out["new_m_" + n], out["new_v_" + n] = [flip(a) for a in res[:4]]
    g_small_flat = _sum_blocks("sum_small_grads", everyone.reshape(-1, LANES), SMALL_ROWS)
    g_small = dict(zip(SMALL, _unpack(g_small_flat, [small_full[n].shape for n in SMALL])))
    for n in ("lru_conv_w", "ssd_conv_w"):
        w = inp[n].shape[-1]
        g_small[n] = lax.dynamic_slice_in_dim(g_small[n], dev * w, w, axis=2)

    shapes = [inp[n].shape for n in SMALL]
    packs = [_pack([src[pre + n] for n in SMALL], F32, SMALL_ROWS)
             for src, pre in ((inp, ""), (g_small, ""), (inp, "m_"), (inp, "v_"))]
    d, m2, v2 = _adamw("adamw_small", *packs)
    for n, dd, mm, vv in zip(SMALL, _unpack(d, shapes), _unpack(m2, shapes), _unpack(v2, shapes)):
        out["grad_" + n] = g_small[n]
        out["delta_" + n], out["new_m_" + n], out["new_v_" + n] = dd, mm, vv
    return out


WEIGHTS = ("norm1_g", "w_in", "b_gate", "lru_conv_w", "lru_conv_b", "lru_w_a", "lru_b_a", "lru_w_x", "lru_b_x",
           "lru_lambda", "ssd_conv_w", "ssd_conv_b", "ssd_dt_bias", "ssd_A_log", "ssd_D", "ssd_norm_g", "w_branch",
           "w_out", "norm2_g", "w_ffn_in", "w_ffn_out", "norm_f")


def kernel(x, norm1_g, w_in, b_gate, lru_conv_w, lru_conv_b, lru_w_a, lru_b_a, lru_w_x, lru_b_x, lru_lambda, ssd_conv_w, ssd_conv_b, ssd_dt_bias, ssd_A_log, ssd_D, ssd_norm_g, w_branch, w_out, norm2_g, w_ffn_in, w_ffn_out, norm_f, loss_target, m_norm1_g, m_w_in, m_b_gate, m_lru_conv_w, m_lru_conv_b, m_lru_w_a, m_lru_b_a, m_lru_w_x, m_lru_b_x, m_lru_lambda, m_ssd_conv_w, m_ssd_conv_b, m_ssd_dt_bias, m_ssd_A_log, m_ssd_D, m_ssd_norm_g, m_w_branch, m_w_out, m_norm2_g, m_w_ffn_in, m_w_ffn_out, m_norm_f, v_norm1_g, v_w_in, v_b_gate, v_lru_conv_w, v_lru_conv_b, v_lru_w_a, v_lru_b_a, v_lru_w_x, v_lru_b_x, v_lru_lambda, v_ssd_conv_w, v_ssd_conv_b, v_ssd_dt_bias, v_ssd_A_log, v_ssd_D, v_ssd_norm_g, v_w_branch, v_w_out, v_norm2_g, v_w_ffn_in, v_w_ffn_out, v_norm_f):
    out = _step(dict(locals()))
    res = [out["loss"], out["grad_x"]]
    for pre in ("grad_", "delta_", "new_m_", "new_v_"):
        res += [out[pre + n] for n in WEIGHTS]
    return tuple(res)
```

```python
import functools

import numpy as np
import jax
import jax.numpy as jnp
from jax import lax
from jax.experimental import pallas as pl
from jax.experimental.pallas import tpu as pltpu

F32 = jnp.float32
BF16 = jnp.bfloat16
HIGHEST = lax.Precision.HIGHEST

D_MODEL = 1024
DEPTH = 2
CHUNK = 64
LRU_C = 8.0
SSD_INNER = 2048
SSD_HEADS = 32
SSD_HEAD_DIM = 64
SSD_GROUPS = 4
SSD_STATE = 128
SSD_CONV_DIM = 3072
D_FF = 2816
EPS = 1e-6
N_DEV = 8
LANES = 128
DT_PAD = LANES
IN_WIDTHS = (1024, 1024, 2048, 3072, 32, 2048)
IN_OFFS = tuple(int(v) for v in np.cumsum((0,) + IN_WIDTHS))

ADAM_LR = 0.001
ADAM_B1 = 0.9
ADAM_B2 = 0.999
ADAM_EPS = 1e-08
ADAM_WD = 0.01
ADAM_STEP = 10

VMEM_LIMIT = 56 * 1024 * 1024
MESH = pl.DeviceIdType.MESH


def _cparams(sem=None):
    return pltpu.CompilerParams(dimension_semantics=sem, vmem_limit_bytes=VMEM_LIMIT)


def _sds(shape, dtype):
    return jax.ShapeDtypeStruct(tuple(shape), dtype)


def _pick(n, cap):
    if n <= cap:
        return n
    best = LANES
    for t in range(LANES, cap + 1, LANES):
        if n % t == 0:
            best = t
    assert n % best == 0, (n, cap)
    return best


def _mm(name, pairs, mode, out_dtype=F32, res=None, rider=None):
    M = pairs[0][0].shape[0]
    N = pairs[0][1].shape[1] if mode == "nn" else pairs[0][1].shape[0]
    npair = len(pairs)
    tm = min(M, 1024 if npair <= 3 else 512)
    tn = _pick(N, 1536 if npair == 1 else 1024)
    tks, nks, starts = [], [], []
    s = 0
    for a, _ in pairs:
        k = a.shape[1]
        tk = _pick(k, 1536)
        tks.append(tk)
        nks.append(k // tk)
        starts.append(s)
        s += k // tk
    nk = s
    dims = (((1,), (0,)), ((), ())) if mode == "nn" else (((1,), (1,)), ((), ()))

    def body(*refs):
        ab = refs[:2 * npair]
        pos = 2 * npair
        r_ref = None
        if res is not None:
            r_ref = refs[pos]
            pos += 1
        o_ref = refs[pos]
        acc = refs[pos + 1] if nk > 1 else None
        k = pl.program_id(2)

        def finish(r):
            if r_ref is not None:
                r = r + r_ref[...].astype(F32)
            o_ref[...] = r.astype(o_ref.dtype)

        for p in range(npair):
            a_ref, b_ref = ab[2 * p], ab[2 * p + 1]
            lo, hi = starts[p], starts[p] + nks[p]

            def step(a_ref=a_ref, b_ref=b_ref, lo=lo, hi=hi):
                d = lax.dot_general(a_ref[...].astype(BF16), b_ref[...].astype(BF16), dims,
                                    preferred_element_type=F32)
                if nk == 1:
                    finish(d)
                    return
                if lo == 0:
                    @pl.when(k == 0)
                    def _():
                        acc[...] = d
                if hi == nk:
                    @pl.when(k == nk - 1)
                    def _():
                        finish(acc[...] + d)
                if max(lo, 1) < min(hi, nk - 1):
                    @pl.when((k > 0) & (k < nk - 1))
                    def _():
                        acc[...] += d

            if npair == 1:
                step()
            else:
                pl.when((k >= lo) & (k < hi))(step)

    in_specs, args = [], []
    for p, (a, b) in enumerate(pairs):
        def kk(k, p=p):
            return jnp.clip(k - starts[p], 0, nks[p] - 1)
        in_specs.append(pl.BlockSpec((tm, tks[p]), lambda i, j, k, kk=kk: (i, kk(k))))
        if mode == "nn":
            in_specs.append(pl.BlockSpec((tks[p], tn), lambda i, j, k, kk=kk: (kk(k), j)))
        else:
            in_specs.append(pl.BlockSpec((tn, tks[p]), lambda i, j, k, kk=kk: (j, kk(k))))
        args += [a, b]
    if res is not None:
        in_specs.append(pl.BlockSpec((tm, tn), lambda i, j, k: (i, j)))
        args.append(res)
    r_args, r_in, r_out, r_shape, r_scratch = _rider_parts(rider)
    grid = (M // tm, N // tn, nk)
    out = pl.pallas_call(
        _with_rider(body, len(args), 1, rider, grid), grid=grid, in_specs=in_specs + r_in,
        out_specs=[pl.BlockSpec((tm, tn), lambda i, j, k: (i, j))] + r_out,
        out_shape=[_sds((M, N), out_dtype)] + r_shape,
        scratch_shapes=([pltpu.VMEM((tm, tn), F32)] if nk > 1 else []) + r_scratch,
        compiler_params=_cparams(("parallel", "parallel", "arbitrary") if rider is None else ("arbitrary",) * 3),
        name=name,
    )(*args, *r_args)
    return out[0] if rider is None else (out[0], out[1:])


def _mm_tn(name, a, b, out_dtype=BF16):
    M, Ka = a.shape
    N = b.shape[1]
    tm = min(M, 2048)
    tka = _pick(Ka, 1024)
    tn = _pick(N, 1024)
    nm = M // tm

    def body(a_ref, b_ref, o_ref, *scratch):
        k = pl.program_id(2)
        d = lax.dot_general(a_ref[...].astype(BF16), b_ref[...].astype(BF16),
                            (((0,), (0,)), ((), ())), preferred_element_type=F32)
        if nm == 1:
            o_ref[...] = d.astype(o_ref.dtype)
            return
        acc = scratch[0]

        @pl.when(k == 0)
        def _():
            acc[...] = d

        @pl.when((k > 0) & (k < nm - 1))
        def _():
            acc[...] += d

        @pl.when(k == nm - 1)
        def _():
            o_ref[...] = (acc[...] + d).astype(o_ref.dtype)

    return pl.pallas_call(
        body, grid=(Ka // tka, N // tn, nm),
        in_specs=[pl.BlockSpec((tm, tka), lambda i, j, k: (k, i)),
                  pl.BlockSpec((tm, tn), lambda i, j, k: (k, j))],
        out_specs=pl.BlockSpec((tka, tn), lambda i, j, k: (i, j)),
        out_shape=_sds((Ka, N), out_dtype),
        scratch_shapes=[pltpu.VMEM((tka, tn), F32)] if nm > 1 else [],
        compiler_params=_cparams(("parallel", "parallel", "arbitrary")), name=name,
    )(a, b)


def _rowwise_fwd(name, fn, rows, pars, outs, tb, rider=None):
    rows = [r if isinstance(r, tuple) else (r, 0, r.shape[1]) for r in rows]
    T = rows[0][0].shape[0]
    tb = min(tb, T)
    nr, npar = len(rows), len(pars)
    r_args, r_in, r_out, r_shape, r_scratch = _rider_parts(rider)

    def body(*refs):
        rv = [r[...].astype(F32) for r in refs[:nr]]
        pv = [p[...] for p in refs[nr:nr + npar]]
        res = fn(*rv, *pv)
        for o, r in zip(refs[nr + npar:], res):
            o[...] = r.astype(o.dtype)

    in_specs = [pl.BlockSpec((tb, w), lambda i, c=c: (i, c)) for _, c, w in rows]
    in_specs += [pl.BlockSpec(p.shape, lambda i, n=p.ndim: (0,) * n) for p in pars]
    return pl.pallas_call(
        _with_rider(body, nr + npar, len(outs), rider, (T // tb,)), grid=(T // tb,), in_specs=in_specs + r_in,
        out_specs=[pl.BlockSpec((tb, w), lambda i: (i, 0)) for w, _ in outs] + r_out,
        out_shape=[_sds((T, w), dt) for w, dt in outs] + r_shape, scratch_shapes=r_scratch,
        compiler_params=_cparams(("parallel",) if rider is None else ("arbitrary",)), name=name,
    )(*[r[0] for r in rows], *pars, *r_args)


def _rowwise_bwd(name, fn, rows, pars, cot_rows, cot_fn, row_out, tb):
    rows = [r if isinstance(r, tuple) else (r, 0, r.shape[1]) for r in rows]
    cot_rows = [r if isinstance(r, tuple) else (r, 0, r.shape[1]) for r in cot_rows]
    T = rows[0][0].shape[0]
    tb = min(tb, T)
    nr, npar, nc = len(rows), len(pars), len(cot_rows)
    want, want_dt = [], []
    for k, dt in enumerate(row_out):
        for d in (() if dt is None else dt if isinstance(dt, tuple) else (dt,)):
            want.append(k)
            want_dt.append(d)

    def body(*refs):
        i = pl.program_id(0)
        rv = [r[...].astype(F32) for r in refs[:nr]]
        pv = [p[...] for p in refs[nr:nr + npar]]
        cv = [c[...].astype(F32) for c in refs[nr + npar:nr + npar + nc]]
        o_refs = refs[nr + npar + nc:]
        _, vjp = jax.vjp(fn, *rv, *pv)
        grads = vjp(tuple(cot_fn(*cv)))
        for o, k in zip(o_refs[:len(want)], want):
            o[...] = grads[k].astype(o.dtype)
        p_refs = o_refs[len(want):]

        @pl.when(i == 0)
        def _():
            for o in p_refs:
                o[...] = jnp.zeros_like(o)

        for o, g in zip(p_refs, grads[nr:]):
            o[...] += g

    in_specs = [pl.BlockSpec((tb, w), lambda i, c=c: (i, c)) for _, c, w in rows]
    in_specs += [pl.BlockSpec(p.shape, lambda i, n=p.ndim: (0,) * n) for p in pars]
    in_specs += [pl.BlockSpec((tb, w), lambda i, c=c: (i, c)) for _, c, w in cot_rows]
    out_specs = [pl.BlockSpec((tb, rows[k][2]), lambda i: (i, 0)) for k in want]
    out_specs += [pl.BlockSpec(p.shape, lambda i, n=p.ndim: (0,) * n) for p in pars]
    out_shape = [_sds((T, rows[k][2]), d) for k, d in zip(want, want_dt)] + [_sds(p.shape, F32) for p in pars]
    return pl.pallas_call(
        body, grid=(T // tb,), in_specs=in_specs, out_specs=out_specs, out_shape=out_shape,
        compiler_params=_cparams(("arbitrary",)), name=name,
    )(*[r[0] for r in rows], *pars, *[r[0] for r in cot_rows])


def _f_rms(x, g):
    r = lax.rsqrt(jnp.mean(x * x, axis=-1, keepdims=True) + EPS)
    return (x * r * g,)


def _f_lru_gates(u, wa, ba, wx, bx, lam):
    ub = u.astype(BF16)
    ra, rx = [], []
    for k in range(D_MODEL // LANES):
        uk = ub[:, LANES * k:LANES * (k + 1)]
        ra.append(jnp.dot(uk, wa[k].astype(BF16), preferred_element_type=F32))
        rx.append(jnp.dot(uk, wx[k].astype(BF16), preferred_element_type=F32))
    r = jax.nn.sigmoid(jnp.concatenate(ra, axis=1) + ba)
    i = jax.nn.sigmoid(jnp.concatenate(rx, axis=1) + bx)
    log_a = -LRU_C * r * jax.nn.softplus(-lam)
    a = jnp.exp(log_a)
    t = jnp.tanh(log_a)
    b = jnp.sqrt(-2.0 * t / (1.0 - t)) * (i * u)
    return a, b


def _f_post(hl, lgate, yssd, xpre_s, z, d_e, ng):
    ya = jax.nn.gelu(lgate) * hl
    y = (yssd + d_e * jax.nn.silu(xpre_s)) * jax.nn.silu(z)
    gw = SSD_INNER // SSD_GROUPS
    parts = []
    for g in range(SSD_GROUPS):
        yg = y[:, gw * g:gw * (g + 1)]
        parts.append(yg * lax.rsqrt(jnp.mean(yg * yg, axis=-1, keepdims=True) + EPS))
    return ya, jnp.concatenate(parts, axis=1) * ng


def _f_merge(m_a, m_b, gates, bg):
    g = jax.nn.sigmoid(gates + bg)
    return (g[:, :D_MODEL] * m_a + g[:, D_MODEL:] * m_b,)


def _f_act(gu):
    return (jax.nn.silu(gu[:, :D_FF]) * gu[:, D_FF:],)


def _ident(*c):
    return c


def _loss_head(h, target, nf):
    T = h.shape[0]
    tb = min(512, T)

    def body(h_ref, t_ref, nf_ref, dh_ref, dnf_ref, loss_ref, dhb_ref):
        i = pl.program_id(0)

        @pl.when(i == 0)
        def _():
            dnf_ref[...] = jnp.zeros_like(dnf_ref)
            loss_ref[...] = jnp.zeros_like(loss_ref)

        (y,), vjp = jax.vjp(_f_rms, h_ref[...], nf_ref[...])
        err = y - t_ref[...]
        dh, dnf = vjp((err * (1.0 / D_MODEL),))
        dh_ref[...] = dh
        dhb_ref[...] = dh.astype(BF16)
        dnf_ref[...] += dnf
        part = 0.5 * jnp.sum(jnp.mean(err * err, axis=-1, keepdims=True), axis=0, keepdims=True)
        loss_ref[...] += jnp.broadcast_to(part, loss_ref.shape)

    return pl.pallas_call(
        body, grid=(T // tb,),
        in_specs=[pl.BlockSpec((tb, D_MODEL), lambda i: (i, 0)), pl.BlockSpec((tb, D_MODEL), lambda i: (i, 0)),
                  pl.BlockSpec((1, D_MODEL), lambda i: (0, 0))],
        out_specs=[pl.BlockSpec((tb, D_MODEL), lambda i: (i, 0)), pl.BlockSpec((1, D_MODEL), lambda i: (0, 0)),
                   pl.BlockSpec((8, LANES), lambda i: (0, 0)), pl.BlockSpec((tb, D_MODEL), lambda i: (i, 0))],
        out_shape=[_sds((T, D_MODEL), F32), _sds((1, D_MODEL), F32), _sds((8, LANES), F32),
                   _sds((T, D_MODEL), BF16)],
        compiler_params=_cparams(("arbitrary",)), name="loss_head",
    )(h, target, nf)


CONV_TC = 1024
HALO = 16


def _conv_fwd(name, x, w8, b):
    T, C = x.shape
    tb = min(512, T)
    nb = tb // HALO

    def body(x_ref, xp_ref, w_ref, b_ref, y_ref, sc):
        i = pl.program_id(0)
        xv = x_ref[...].astype(F32)
        sc[pl.ds(HALO, tb), :] = xv
        sc[pl.ds(0, HALO), :] = jnp.where(i > 0, xp_ref[...].astype(F32), 0.0)
        acc = b_ref[...] + w_ref[3:4, :] * xv
        for k in range(3):
            acc = acc + w_ref[k:k + 1, :] * sc[pl.ds(HALO - 3 + k, tb), :]
        y_ref[...] = acc

    return pl.pallas_call(
        body, grid=(T // tb, C // CONV_TC),
        in_specs=[pl.BlockSpec((tb, CONV_TC), lambda i, j: (i, j)),
                  pl.BlockSpec((HALO, CONV_TC), lambda i, j: (jnp.maximum(i * nb - 1, 0), j)),
                  pl.BlockSpec((8, CONV_TC), lambda i, j: (0, j)),
                  pl.BlockSpec((1, CONV_TC), lambda i, j: (0, j))],
        out_specs=pl.BlockSpec((tb, CONV_TC), lambda i, j: (i, j)),
        out_shape=_sds((T, C), F32),
        scratch_shapes=[pltpu.VMEM((tb + HALO, CONV_TC), F32)],
        compiler_params=_cparams(("parallel", "parallel")), name=name,
    )(x, x, w8, b)


def _conv_bwd(name, x, dy, w8):
    T, C = x.shape
    tb = min(512, T)
    nb = tb // HALO
    nt = T // tb

    def body(x_ref, xp_ref, dy_ref, dyn_ref, w_ref, dx_ref, dwb_ref, scx, scd):
        i = pl.program_id(1)
        dyv = dy_ref[...]
        xv = x_ref[...].astype(F32)
        scx[pl.ds(HALO, tb), :] = xv
        scx[pl.ds(0, HALO), :] = jnp.where(i > 0, xp_ref[...].astype(F32), 0.0)
        scd[pl.ds(0, tb), :] = dyv
        scd[pl.ds(tb, HALO), :] = jnp.where(i < nt - 1, dyn_ref[...], 0.0)
        dx = w_ref[3:4, :] * dyv
        rows = []
        for k in range(3):
            dx = dx + w_ref[k:k + 1, :] * scd[pl.ds(3 - k, tb), :]
            rows.append(jnp.sum(dyv * scx[pl.ds(HALO - 3 + k, tb), :], axis=0, keepdims=True))
        rows.append(jnp.sum(dyv * xv, axis=0, keepdims=True))
        rows.append(jnp.sum(dyv, axis=0, keepdims=True))
        rows.append(jnp.zeros((3, CONV_TC), F32))
        dx_ref[...] = dx.astype(dx_ref.dtype)

        @pl.when(i == 0)
        def _():
            dwb_ref[...] = jnp.zeros_like(dwb_ref)

        dwb_ref[...] += jnp.concatenate(rows, axis=0)

    return pl.pallas_call(
        body, grid=(C // CONV_TC, nt),
        in_specs=[pl.BlockSpec((tb, CONV_TC), lambda j, i: (i, j)),
                  pl.BlockSpec((HALO, CONV_TC), lambda j, i: (jnp.maximum(i * nb - 1, 0), j)),
                  pl.BlockSpec((tb, CONV_TC), lambda j, i: (i, j)),
                  pl.BlockSpec((HALO, CONV_TC), lambda j, i: (jnp.minimum((i + 1) * nb, T // HALO - 1), j)),
                  pl.BlockSpec((8, CONV_TC), lambda j, i: (0, j))],
        out_specs=[pl.BlockSpec((tb, CONV_TC), lambda j, i: (i, j)),
                   pl.BlockSpec((8, CONV_TC), lambda j, i: (0, j))],
        out_shape=[_sds((T, C), BF16), _sds((8, C), F32)],
        scratch_shapes=[pltpu.VMEM((tb + HALO, CONV_TC), F32), pltpu.VMEM((tb + HALO, CONV_TC), F32)],
        compiler_params=_cparams(("parallel", "arbitrary")), name=name,
    )(x, x, dy, dy, w8)


SCAN_TB = 512


def _lru_scan_fwd(a, b):
    T, C = a.shape
    tb = min(SCAN_TB, T)

    def body(a_ref, b_ref, h_ref, hp_ref, carry):
        @pl.when(pl.program_id(0) == 0)
        def _():
            carry[...] = jnp.zeros_like(carry)

        def group(gi, h):
            r0 = pl.multiple_of(gi * 8, 8)
            at = a_ref[pl.ds(r0, 8), :]
            bt = b_ref[pl.ds(r0, 8), :]
            hs, hps = [], []
            for r in range(8):
                hps.append(h)
                h = at[r:r + 1, :] * h + bt[r:r + 1, :]
                hs.append(h)
            h_ref[pl.ds(r0, 8), :] = jnp.concatenate(hs, axis=0)
            hp_ref[pl.ds(r0, 8), :] = jnp.concatenate(hps, axis=0)
            return h

        carry[0:1, :] = lax.fori_loop(0, tb // 8, group, carry[0:1, :])

    spec = pl.BlockSpec((tb, C), lambda i: (i, 0))
    return pl.pallas_call(
        body, grid=(T // tb,), in_specs=[spec, spec], out_specs=[spec, spec],
        out_shape=[_sds((T, C), F32), _sds((T, C), F32)],
        scratch_shapes=[pltpu.VMEM((8, C), F32)],
        compiler_params=_cparams(("arbitrary",)), name="lru_scan_fwd",
    )(a, b)


def _lru_scan_bwd(a, dh):
    T, C = a.shape
    tb = min(SCAN_TB, T)
    nt = T // tb

    def body(a_ref, dh_ref, g_ref, carry):
        @pl.when(pl.program_id(0) == 0)
        def _():
            carry[...] = jnp.zeros_like(carry)

        def group(gi, c):
            r0 = pl.multiple_of((tb // 8 - 1 - gi) * 8, 8)
            at = a_ref[pl.ds(r0, 8), :]
            dt = dh_ref[pl.ds(r0, 8), :]
            gs = [None] * 8
            for r in range(7, -1, -1):
                g = dt[r:r + 1, :] + c
                c = at[r:r + 1, :] * g
                gs[r] = g
            g_ref[pl.ds(r0, 8), :] = jnp.concatenate(gs, axis=0)
            return c

        carry[0:1, :] = lax.fori_loop(0, tb // 8, group, carry[0:1, :])

    spec = pl.BlockSpec((tb, C), lambda i: (nt - 1 - i, 0))
    return pl.pallas_call(
        body, grid=(nt,), in_specs=[spec, spec], out_specs=spec,
        out_shape=_sds((T, C), F32),
        scratch_shapes=[pltpu.VMEM((8, C), F32)],
        compiler_params=_cparams(("arbitrary",)), name="lru_scan_bwd",
    )(a, dh)


def _ssd_chunk(xpre, dtraw, state, dtb, alog):
    xc = jax.nn.silu(xpre)
    xs = xc[:, :SSD_INNER]
    bm = xc[:, SSD_INNER:SSD_INNER + SSD_GROUPS * SSD_STATE]
    cm = xc[:, SSD_INNER + SSD_GROUPS * SSD_STATE:]
    dt = jax.nn.softplus(dtraw + dtb)
    a = dt * (-jnp.exp(alog))
    ltri = (lax.broadcasted_iota(jnp.int32, (CHUNK, CHUNK), 0)
            >= lax.broadcasted_iota(jnp.int32, (CHUNK, CHUNK), 1)).astype(F32)
    a_cs = jnp.dot(ltri, a, precision=HIGHEST, preferred_element_type=F32)
    npair = SSD_HEADS // 2
    pi = lax.broadcasted_iota(jnp.int32, (npair, LANES), 0)
    hi = lax.broadcasted_iota(jnp.int32, (npair, LANES), 1)
    sel_even = (hi == 2 * pi).astype(F32)
    sel_odd = (hi == 2 * pi + 1).astype(F32)
    top = lax.broadcasted_iota(jnp.int32, (2 * CHUNK, LANES), 0) < CHUNK

    def pair_transpose(v):
        v2 = jnp.concatenate([v, v], axis=0)
        dn = (((1,), (1,)), ((), ()))
        return (lax.dot_general(sel_even, jnp.where(top, v2, 0.0), dn, precision=HIGHEST, preferred_element_type=F32)
                + lax.dot_general(sel_odd, jnp.where(top, 0.0, v2), dn, precision=HIGHEST, preferred_element_type=F32))

    a_t2 = pair_transpose(a_cs)
    dt_t2 = pair_transpose(dt)
    a_last = a_cs[CHUNK - 1:CHUNK, :]
    dte = jnp.exp(a_last - a_cs) * dt
    cd = jnp.exp(a_last)
    lane = lax.broadcasted_iota(jnp.int32, (CHUNK, LANES), 1)
    left = lane < SSD_HEAD_DIM
    right = jnp.logical_not(left)
    left1 = left[0:1]
    tril2 = lax.broadcasted_iota(jnp.int32, (CHUNK, LANES), 0) >= (lane & (SSD_HEAD_DIM - 1))
    gw = SSD_INNER // SSD_GROUPS
    ys, news = [], []
    for g in range(SSD_GROUPS):
        bg = bm[:, SSD_STATE * g:SSD_STATE * (g + 1)].astype(BF16)
        cg = cm[:, SSD_STATE * g:SSD_STATE * (g + 1)].astype(BF16)
        bg2 = jnp.concatenate([bg, bg], axis=0)
        scores2 = lax.dot_general(cg, bg2, (((1,), (1,)), ((), ())), preferred_element_type=F32)
        coff = jnp.dot(cg, state[:, gw * g:gw * (g + 1)].astype(BF16), preferred_element_type=F32)
        for j in range(gw // LANES):
            lo = gw * g + LANES * j
            p = lo // LANES
            h0 = 2 * p
            xp = xs[:, lo:lo + LANES]
            col = jnp.where(left, a_cs[:, h0:h0 + 1], a_cs[:, h0 + 1:h0 + 2])
            dm = jnp.exp(jnp.where(tril2, col - a_t2[p:p + 1, :], -1e30)) * dt_t2[p:p + 1, :]
            sd = (scores2 * dm).astype(BF16)
            x_bd = jnp.concatenate([jnp.where(left, xp, 0.0), jnp.where(right, xp, 0.0)], axis=0).astype(BF16)
            acc = jnp.dot(sd, x_bd, preferred_element_type=F32)
            ys.append(acc + coff[:, LANES * j:LANES * (j + 1)] * jnp.exp(col))
            dtee = jnp.where(left, dte[:, h0:h0 + 1], dte[:, h0 + 1:h0 + 2])
            xw = (xp * dtee).astype(BF16)
            cde = jnp.where(left1, cd[:, h0:h0 + 1], cd[:, h0 + 1:h0 + 2])
            news.append(state[:, lo:lo + LANES] * cde
                        + lax.dot_general(bg, xw, (((0,), (0,)), ((), ())), preferred_element_type=F32))
    return jnp.concatenate(ys, axis=1), jnp.concatenate(news, axis=1)


def _ssd_fwd(name, xpre, dtraw, dtb, alog, rider=None):
    T = xpre.shape[0]
    n = T // CHUNK
    r_args, r_in, r_out, r_shape, r_scratch = _rider_parts(rider)

    def body(xp, dr, dtb_ref, al_ref, y_ref, st_ref, state):
        @pl.when(pl.program_id(0) == 0)
        def _():
            state[...] = jnp.zeros_like(state)

        st_ref[0] = state[...]
        y, new = _ssd_chunk(xp[...], dr[...], state[...], dtb_ref[...], al_ref[...])
        y_ref[...] = y
        state[...] = new

    small = pl.BlockSpec((1, LANES), lambda c: (0, 0))
    return pl.pallas_call(
        _with_rider(body, 4, 2, rider, (n,)), grid=(n,),
        in_specs=[pl.BlockSpec((CHUNK, SSD_CONV_DIM), lambda c: (c, 0)),
                  pl.BlockSpec((CHUNK, DT_PAD), lambda c: (c, 0)), small, small] + r_in,
        out_specs=[pl.BlockSpec((CHUNK, SSD_INNER), lambda c: (c, 0)),
                   pl.BlockSpec((1, SSD_STATE, SSD_INNER), lambda c: (c, 0, 0))] + r_out,
        out_shape=[_sds((T, SSD_INNER), F32), _sds((n, SSD_STATE, SSD_INNER), F32)] + r_shape,
        scratch_shapes=[pltpu.VMEM((SSD_STATE, SSD_INNER), F32)] + r_scratch,
        compiler_params=_cparams(("arbitrary",)), name=name,
    )(xpre, dtraw, dtb, alog, *r_args)


def _ssd_bwd(name, xpre, dtraw, states, dy, dxs_extra, dtb, alog, rider=None):
    T = xpre.shape[0]
    n = T // CHUNK
    r_args, r_in, r_out, r_shape, r_scratch = _rider_parts(rider)

    def body(xp, dr, st, dy_ref, dx_ref, dtb_ref, al_ref, dxp_ref, ddr_ref, ddtb_ref, dal_ref, dstate):
        @pl.when(pl.program_id(0) == 0)
        def _():
            dstate[...] = jnp.zeros_like(dstate)
            ddtb_ref[...] = jnp.zeros_like(ddtb_ref)
            dal_ref[...] = jnp.zeros_like(dal_ref)

        _, vjp = jax.vjp(_ssd_chunk, xp[...], dr[...], st[0], dtb_ref[...], al_ref[...])
        dxp, ddr, ds, db, da = vjp((dy_ref[...], dstate[...]))
        dxp_ref[:, :SSD_INNER] = dxp[:, :SSD_INNER] + dx_ref[...]
        dxp_ref[:, SSD_INNER:] = dxp[:, SSD_INNER:]
        ddr_ref[...] = ddr.astype(ddr_ref.dtype)
        dstate[...] = ds
        ddtb_ref[...] += db
        dal_ref[...] += da

    def rev(c):
        return (n - 1 - c, 0)

    small = pl.BlockSpec((1, LANES), lambda c: (0, 0))
    return pl.pallas_call(
        _with_rider(body, 7, 4, rider, (n,)), grid=(n,),
        in_specs=[pl.BlockSpec((CHUNK, SSD_CONV_DIM), rev), pl.BlockSpec((CHUNK, DT_PAD), rev),
                  pl.BlockSpec((1, SSD_STATE, SSD_INNER), lambda c: (n - 1 - c, 0, 0)),
                  pl.BlockSpec((CHUNK, SSD_INNER), rev), pl.BlockSpec((CHUNK, SSD_INNER), rev), small, small] + r_in,
        out_specs=[pl.BlockSpec((CHUNK, SSD_CONV_DIM), rev), pl.BlockSpec((CHUNK, DT_PAD), rev), small, small] + r_out,
        out_shape=[_sds((T, SSD_CONV_DIM), F32), _sds((T, DT_PAD), BF16), _sds((1, LANES), F32),
                   _sds((1, LANES), F32)] + r_shape,
        scratch_shapes=[pltpu.VMEM((SSD_STATE, SSD_INNER), F32)] + r_scratch,
        compiler_params=_cparams(("arbitrary",)), name=name,
    )(xpre, dtraw, states, dy, dxs_extra, dtb, alog, *r_args)


HBM_SPEC = pl.BlockSpec(memory_space=pltpu.HBM)
N_PEER = N_DEV - 1


def _position():
    return lax.axis_index("x"), lax.axis_index("y"), lax.axis_index("c")


def _all_gather(name, blks):
    return _exchange_call(name, _gather_phases, blks, [_sds((N_DEV,) + b.shape, b.dtype) for b in blks])


def _scatter_exchange(name, gs):
    return _exchange_call(name, _scatter_phases, gs, [_sds(g.shape, g.dtype) for g in gs])


def _exchange_scratch(na):
    return [pltpu.SemaphoreType.DMA((na * N_PEER,)), pltpu.SemaphoreType.DMA((na * N_PEER,)),
            pltpu.SemaphoreType.DMA((na,))]


def _exchange_call(name, phases, arrays, out_shape):
    na = len(arrays)

    def body(*refs):
        begin, finish = phases(refs[:na], refs[na:2 * na], *refs[2 * na:])
        begin()
        finish()

    return pl.pallas_call(
        body, out_shape=out_shape, in_specs=[HBM_SPEC] * na, out_specs=[HBM_SPEC] * na,
        scratch_shapes=_exchange_scratch(na), name=name,
    )(*arrays)


def _gather_phases(x_refs, out_refs, send_sems, recv_sems, local_sems):
    na = len(x_refs)
    x, y, c = _position()
    me, sibling = (x, y, c), (x, y, 1 - c)
    chips = [(1 - x, y), (x, 1 - y), (1 - x, 1 - y)]

    def slot(a, px, py, pc):
        return out_refs[a].at[4 * px + 2 * py + pc]

    def copy(a, k, block, to, src=None):
        return pltpu.make_async_remote_copy(
            src_ref=slot(a, *block) if src is None else src, dst_ref=slot(a, *block),
            send_sem=send_sems.at[a * N_PEER + k], recv_sem=recv_sems.at[a * N_PEER + k],
            device_id=to, device_id_type=MESH)

    mine = [pltpu.make_async_copy(x_refs[a], slot(a, *me), local_sems.at[a]) for a in range(na)]
    first = []
    for a in range(na):
        first.append(copy(a, 0, me, sibling, src=x_refs[a]))
        first += [copy(a, 1 + j, me, (*chip, c), src=x_refs[a]) for j, chip in enumerate(chips)]
    passed = [copy(a, 4 + j, (*chip, c), sibling) for j, chip in enumerate(chips) for a in range(na)]

    def begin():
        for cp in mine + first:
            cp.start()

    def finish():
        for j, chip in enumerate(chips):
            for a in range(na):
                copy(a, 1 + j, (*chip, c), me).wait_recv()
                passed[j * na + a].start()
        for a in range(na):
            copy(a, 0, sibling, me).wait_recv()
            for j, chip in enumerate(chips):
                copy(a, 4 + j, (*chip, 1 - c), me).wait_recv()
        for cp in first + passed:
            cp.wait_send()
        for cp in mine:
            cp.wait()

    return begin, finish


def _scatter_phases(g_refs, q_refs, send_sems, recv_sems, local_sems):
    na = len(g_refs)
    x, y, c = _position()
    me = 4 * x + 2 * y + c
    mine, sends, recvs = [], [], []
    for a in range(na):
        mine.append(pltpu.make_async_copy(g_refs[a].at[me], q_refs[a].at[me], local_sems.at[a]))
        for k in range(1, N_DEV):
            px, py, pc = (x + (k >> 2)) % 2, (y + ((k >> 1) & 1)) % 2, (c + (k & 1)) % 2
            peer = 4 * px + 2 * py + pc
            sem = a * N_PEER + k - 1
            sends.append(pltpu.make_async_remote_copy(
                src_ref=g_refs[a].at[peer], dst_ref=q_refs[a].at[me], send_sem=send_sems.at[sem],
                recv_sem=recv_sems.at[sem], device_id=(px, py, pc), device_id_type=MESH))
            recvs.append(pltpu.make_async_remote_copy(
                src_ref=g_refs[a].at[me], dst_ref=q_refs[a].at[peer], send_sem=send_sems.at[sem],
                recv_sem=recv_sems.at[sem], device_id=(px, py, pc), device_id_type=MESH))

    def begin():
        for cp in mine + sends:
            cp.start()

    def finish():
        for cp in recvs:
            cp.wait_recv()
        for cp in sends:
            cp.wait_send()
        for cp in mine:
            cp.wait()

    return begin, finish


def _with_rider(body, n_in, n_out, rider, grid):
    if rider is None:
        return body
    phases, arrays, _ = rider
    na = len(arrays)

    def carried(*refs):
        ins, r_in = refs[:n_in], refs[n_in:n_in + na]
        outs = refs[n_in + na:n_in + na + n_out]
        r_out = refs[n_in + na + n_out:n_in + 2 * na + n_out]
        scratch, r_scratch = refs[n_in + 2 * na + n_out:-3], refs[-3:]
        begin, finish = phases(r_in, r_out, *r_scratch)
        first = last = None
        for d, g in enumerate(grid):
            at_start, at_end = pl.program_id(d) == 0, pl.program_id(d) == g - 1
            first = at_start if first is None else first & at_start
            last = at_end if last is None else last & at_end
        pl.when(first)(begin)
        body(*ins, *outs, *scratch)
        pl.when(last)(finish)

    return carried


def _rider_parts(rider):
    if rider is None:
        return [], [], [], [], []
    _, arrays, out_shape = rider
    na = len(arrays)
    return list(arrays), [HBM_SPEC] * na, [HBM_SPEC] * na, list(out_shape), _exchange_scratch(na)


def _gather_rider(blks):
    return (_gather_phases, blks, [_sds((N_DEV,) + b.shape, b.dtype) for b in blks])


def _scatter_rider(gs):
    return (_scatter_phases, gs, [_sds(g.shape, g.dtype) for g in gs])


def _adamw_sum(name, w, m, v, recvs, rider=None):
    L, r, c = w.shape
    tr = r if r <= 512 else max(t for t in range(16, 257, 16) if r % t == 0)
    assert len(recvs) == L
    nr = r // tr

    def body(w_ref, m_ref, v_ref, *refs):
        q_refs = refs[:L]
        g_ref, d_ref, m2_ref, v2_ref = refs[L:]
        l = pl.program_id(0)
        for ll in range(L):
            @pl.when(l == ll)
            def _(q_ref=q_refs[ll]):
                gv = q_ref[0].astype(F32)
                for s in range(1, N_DEV):
                    gv = gv + q_ref[s].astype(F32)
                m2 = ADAM_B1 * m_ref[0] + (1.0 - ADAM_B1) * gv
                v2 = ADAM_B2 * v_ref[0] + (1.0 - ADAM_B2) * jnp.square(gv)
                m_hat = m2 / (1.0 - ADAM_B1 ** ADAM_STEP)
                v_hat = v2 / (1.0 - ADAM_B2 ** ADAM_STEP)
                g_ref[0] = gv
                d_ref[0] = -ADAM_LR * (m_hat / (jnp.sqrt(v_hat) + ADAM_EPS) + ADAM_WD * w_ref[0])
                m2_ref[0] = m2
                v2_ref[0] = v2

    def q_index(ll):
        return lambda l, i: (0, jnp.where(l == ll, i, jnp.where(l > ll, nr - 1, 0)), 0)

    spec = pl.BlockSpec((1, tr, c), lambda l, i: (l, i, 0))
    r_args, r_in, r_out, r_shape, r_scratch = _rider_parts(rider)
    return pl.pallas_call(
        _with_rider(body, 3 + L, 4, rider, (L, nr)), grid=(L, nr),
        in_specs=[spec] * 3 + [pl.BlockSpec((N_DEV, tr, c), q_index(ll)) for ll in range(L)] + r_in,
        out_specs=[spec] * 4 + r_out, out_shape=[_sds((L, r, c), F32)] * 4 + r_shape, scratch_shapes=r_scratch,
        compiler_params=_cparams(("arbitrary", "arbitrary")), name=name,
    )(w, m, v, *recvs, *r_args)


def _sum_blocks(name, q, tr):
    R = q.shape[0] // N_DEV
    W = q.shape[1]
    tr = min(tr, R)
    assert R % tr == 0
    nb = R // tr

    def body(*refs):
        acc = refs[0][...].astype(F32)
        for r in refs[1:N_DEV]:
            acc = acc + r[...].astype(F32)
        refs[N_DEV][...] = acc

    return pl.pallas_call(
        body, grid=(nb,),
        in_specs=[pl.BlockSpec((tr, W), lambda i, s=s: (s * nb + i, 0)) for s in range(N_DEV)],
        out_specs=pl.BlockSpec((tr, W), lambda i: (i, 0)), out_shape=_sds((R, W), F32),
        compiler_params=_cparams(("parallel",)), name=name,
    )(*([q] * N_DEV))


def _adamw(name, w, g, m, v):
    R, C = w.shape
    tr = R
    if R > 512:
        tr = max(t for t in range(8, 513, 8) if R % t == 0)

    def body(w_ref, g_ref, m_ref, v_ref, d_ref, m2_ref, v2_ref):
        gv = g_ref[...]
        m2 = ADAM_B1 * m_ref[...] + (1.0 - ADAM_B1) * gv
        v2 = ADAM_B2 * v_ref[...] + (1.0 - ADAM_B2) * jnp.square(gv)
        m_hat = m2 / (1.0 - ADAM_B1 ** ADAM_STEP)
        v_hat = v2 / (1.0 - ADAM_B2 ** ADAM_STEP)
        d_ref[...] = -ADAM_LR * (m_hat / (jnp.sqrt(v_hat) + ADAM_EPS) + ADAM_WD * w_ref[...])
        m2_ref[...] = m2
        v2_ref[...] = v2

    spec = pl.BlockSpec((tr, C), lambda i: (i, 0))
    return pl.pallas_call(
        body, grid=(R // tr,), in_specs=[spec] * 4, out_specs=[spec] * 3,
        out_shape=[_sds((R, C), F32)] * 3, compiler_params=_cparams(("parallel",)), name=name,
    )(w, g, m, v)


BIG = ("w_in", "w_branch", "w_out", "w_ffn_in", "w_ffn_out")
BIG_ROW_SHARDED = {"w_in": False, "w_branch": True, "w_out": True, "w_ffn_in": True, "w_ffn_out": True}
TRANSPOSED = ("w_ffn_in",)
SMALL = ("norm1_g", "b_gate", "lru_conv_w", "lru_conv_b", "lru_w_a", "lru_b_a", "lru_w_x", "lru_b_x", "lru_lambda",
         "ssd_conv_w", "ssd_conv_b", "ssd_dt_bias", "ssd_A_log", "ssd_D", "ssd_norm_g", "norm2_g", "norm_f")
SMALL_ROWS = 256


def _pack(arrs, dtype, row_mult):
    parts = []
    for a in arrs:
        f = a.reshape(-1).astype(dtype)
        pad = (-f.shape[0]) % LANES
        if pad:
            f = jnp.concatenate([f, jnp.zeros((pad,), dtype)])
        parts.append(f)
    f = jnp.concatenate(parts)
    pad = (-f.shape[0]) % (LANES * row_mult)
    if pad:
        f = jnp.concatenate([f, jnp.zeros((pad,), dtype)])
    return f.reshape(-1, LANES)


def _unpack(flat, shapes, lead=()):
    f = flat.reshape(lead + (-1,))
    out, off = [], 0
    for s in shapes:
        n = int(np.prod(s))
        out.append(f[..., off:off + n].reshape(lead + tuple(s)))
        off += n + (-n) % LANES
    return out


def _full_from_shards(name, st):
    if BIG_ROW_SHARDED[name]:
        return st.reshape((-1,) + st.shape[2:])
    return jnp.transpose(st, (1, 0, 2)).reshape(st.shape[1], -1)


def _shards_from_full(name, full):
    if BIG_ROW_SHARDED[name]:
        return full.reshape((N_DEV, full.shape[0] // N_DEV) + full.shape[1:])
    return jnp.transpose(full.reshape(full.shape[0], N_DEV, -1), (1, 0, 2))


def _block_diag_tiles(w):
    z = jnp.zeros((8, 64, 64), w.dtype)
    w2 = w.reshape(8, 2, 64, 64)
    top = jnp.concatenate([w2[:, 0], z], axis=2)
    bot = jnp.concatenate([z, w2[:, 1]], axis=2)
    return jnp.concatenate([top, bot], axis=1)


def _block_diag_untile(t):
    return jnp.stack([t[:, :64, :64], t[:, 64:, 64:]], axis=1).reshape(16, 64, 64)


def _pad_lanes(a, width):
    return jnp.concatenate([a, jnp.zeros(a.shape[:-1] + (width - a.shape[-1],), a.dtype)], axis=-1)


def _pad_rows8(w):
    return jnp.concatenate([w, jnp.zeros((8 - w.shape[0],) + w.shape[1:], w.dtype)], axis=0)


REST = BIG[1:]


def _in_weights(shards):
    w_in = _full_from_shards("w_in", shards)
    seg = [w_in[:, IN_OFFS[k]:IN_OFFS[k + 1]] for k in range(6)]
    return {"w_lx": seg[0], "w_lg": seg[1], "w_z": seg[2], "w_xbc": seg[3], "w_dt": _pad_lanes(seg[4], DT_PAD),
            "w_g": seg[5]}


def _rest_weights(shards):
    full = {n: _full_from_shards(n, st) for n, st in zip(REST, shards)}
    return {"w_bra": full["w_branch"][:D_MODEL], "w_brb": full["w_branch"][D_MODEL:], "w_out": full["w_out"],
            "w_ffn_in_t": full["w_ffn_in"], "w_ffn_out": full["w_ffn_out"]}


def _small_params(l, small):
    p = {}
    for n in ("norm1_g", "b_gate", "lru_conv_b", "lru_b_a", "lru_b_x", "lru_lambda", "ssd_conv_b", "ssd_norm_g", "norm2_g"):
        p[n] = small[n][l].reshape(1, -1)
    p["lru_conv_w8"] = _pad_rows8(small["lru_conv_w"][l])
    p["ssd_conv_w8"] = _pad_rows8(small["ssd_conv_w"][l])
    p["wa"] = _block_diag_tiles(small["lru_w_a"][l])
    p["wx"] = _block_diag_tiles(small["lru_w_x"][l])
    p["dtb"] = _pad_lanes(small["ssd_dt_bias"][l].reshape(1, -1), DT_PAD)
    p["alog"] = _pad_lanes(small["ssd_A_log"][l].reshape(1, -1), DT_PAD)
    p["d_e"] = jnp.repeat(small["ssd_D"][l], SSD_HEAD_DIM).reshape(1, -1)
    return p


def _layer_fwd(l, h, p, riders):
    n = f"l{l}_"
    s = {"h_in": h}
    got = {}
    (xn,) = _rowwise_fwd(n + "rms1", _f_rms, [h], [p["norm1_g"]], [(D_MODEL, BF16)], 512)
    s["xn"] = xn
    for k in ("lx", "lg", "z", "xbc", "g", "dt"):
        s[k] = _mm(n + "in_" + k, [(xn, p["w_" + k])], "nn", out_dtype=F32 if k == "dt" else BF16)
    s["u"] = _conv_fwd(n + "lru_conv", s["lx"], p["lru_conv_w8"], p["lru_conv_b"])
    lru_pars = [p["wa"], p["lru_b_a"], p["wx"], p["lru_b_x"], p["lru_lambda"]]
    s["a"], b = _rowwise_fwd(n + "lru_gates", _f_lru_gates, [s["u"]], lru_pars, [(D_MODEL, F32)] * 2, 512)
    s["hl"], s["hprev"] = _lru_scan_fwd(s["a"], b)
    s["xpre"] = _conv_fwd(n + "ssd_conv", s["xbc"], p["ssd_conv_w8"], p["ssd_conv_b"])
    s["yssd"], s["states"], *got["ssd"] = _ssd_fwd(n + "ssd", s["xpre"], s["dt"], p["dtb"], p["alog"],
                                                   riders.get("ssd"))
    if "ssd" in riders:
        p.update(_rest_weights(got["ssd"]))
    post_rows = [s["hl"], s["lg"], s["yssd"], (s["xpre"], 0, SSD_INNER), s["z"]]
    s["ya"], s["yb"], *got["post"] = _rowwise_fwd(n + "post", _f_post, post_rows, [p["d_e"], p["ssd_norm_g"]],
                                                  [(D_MODEL, BF16), (SSD_INNER, BF16)], 128, riders.get("post"))
    s["ma"] = _mm(n + "br_a", [(s["ya"], p["w_bra"])], "nn", out_dtype=BF16)
    s["mb"] = _mm(n + "br_b", [(s["yb"], p["w_brb"])], "nn", out_dtype=BF16)
    (s["merged"],) = _rowwise_fwd(n + "merge", _f_merge, [s["ma"], s["mb"], s["g"]], [p["b_gate"]],
                                  [(D_MODEL, BF16)], 512)
    s["h_mid"] = _mm(n + "out", [(s["merged"], p["w_out"])], "nn", res=h)
    (s["xn2"],) = _rowwise_fwd(n + "rms2", _f_rms, [s["h_mid"]], [p["norm2_g"]], [(D_MODEL, BF16)], 512)
    s["gu"] = _mm(n + "ffn_in", [(s["xn2"], p["w_ffn_in_t"])], "nt", out_dtype=BF16, rider=riders.get("ffn_in"))
    if "ffn_in" in riders:
        s["gu"], got["ffn_in"] = s["gu"]
    (s["act"],) = _rowwise_fwd(n + "act", _f_act, [s["gu"]], [], [(D_FF, BF16)], 512)
    h_out = _mm(n + "ffn_out", [(s["act"], p["w_ffn_out"])], "nn", res=s["h_mid"], rider=riders.get("ffn_out"))
    if "ffn_out" in riders:
        h_out, got["ffn_out"] = h_out
    return h_out, s, got


def _f_rms_res(x, g):
    return _f_rms(x, g)[0], x


def _layer_bwd(l, dh, dh_b, s, p, above, first_layer):
    n = f"l{l}_b_"
    gw, gs = {}, {}
    d_act = _mm(n + "d_act", [(dh_b, p["w_ffn_out"])], "nt", out_dtype=BF16)
    gw["w_ffn_out"] = _mm_tn(n + "dw_ffn_out", s["act"], dh_b)
    (d_gu,) = _rowwise_bwd(n + "act", _f_act, [s["gu"]], [], [d_act], _ident, [BF16], 256)
    d_xn2 = _mm(n + "d_xn2", [(d_gu, p["w_ffn_in_t"])], "nn")
    gw["w_ffn_in"] = _mm_tn(n + "dw_ffn_in", d_gu, s["xn2"])
    dh_mid, dh_mid_b, gs["norm2_g"] = _rowwise_bwd(n + "rms2", _f_rms_res, [s["h_mid"]], [p["norm2_g"]], [d_xn2, dh],
                                                   _ident, [(F32, BF16)], 512)
    d_merged = _mm(n + "d_merged", [(dh_mid_b, p["w_out"])], "nt", out_dtype=BF16)
    gw["w_out"] = _mm_tn(n + "dw_out", s["merged"], dh_mid_b)
    d_ma, d_mb, d_g, gs["b_gate"] = _rowwise_bwd(n + "merge", _f_merge, [s["ma"], s["mb"], s["g"]], [p["b_gate"]],
                                                 [d_merged], _ident, [BF16, BF16, BF16], 512)
    d_ya = _mm(n + "d_ya", [(d_ma, p["w_bra"])], "nt", out_dtype=BF16)
    d_yb = _mm(n + "d_yb", [(d_mb, p["w_brb"])], "nt", out_dtype=BF16)
    gw["w_branch"] = jnp.concatenate([_mm_tn(n + "dw_bra", s["ya"], d_ma), _mm_tn(n + "dw_brb", s["yb"], d_mb)], axis=0)
    post_rows = [s["hl"], s["lg"], s["yssd"], (s["xpre"], 0, SSD_INNER), s["z"]]
    d_hl, d_lg, d_yssd, d_xs, d_z, d_de, gs["ssd_norm_g"] = _rowwise_bwd(
        n + "post", _f_post, post_rows, [p["d_e"], p["ssd_norm_g"]], [d_ya, d_yb], _ident,
        [F32, BF16, F32, F32, BF16], 128)
    gs["ssd_D"] = d_de.reshape(SSD_HEADS, SSD_HEAD_DIM).sum(axis=1)
    contrib = [_shards_from_full(k, gw[k]) for k in REST] + ([] if above is None else [above])
    d_xpre, d_dt, d_dtb, d_alog, *carried = _ssd_bwd(n + "ssd", s["xpre"], s["dt"], s["states"], d_yssd, d_xs,
                                                     p["dtb"], p["alog"], _scatter_rider(contrib))
    arrived = dict(zip(REST, carried))
    arrived_above = None if above is None else carried[len(REST)]
    gs["ssd_dt_bias"] = d_dtb[0, :SSD_HEADS]
    gs["ssd_A_log"] = d_alog[0, :SSD_HEADS]
    d_xbc, dwb = _conv_bwd(n + "ssd_conv", s["xbc"], d_xpre, p["ssd_conv_w8"])
    gs["ssd_conv_w"], gs["ssd_conv_b"] = dwb[:4], dwb[4]
    g_scan = _lru_scan_bwd(s["a"], d_hl)
    lru_pars = [p["wa"], p["lru_b_a"], p["wx"], p["lru_b_x"], p["lru_lambda"]]
    d_u, d_wa, gs["lru_b_a"], d_wx, gs["lru_b_x"], gs["lru_lambda"] = _rowwise_bwd(
        n + "lru_gates", _f_lru_gates, [s["u"]], lru_pars, [g_scan, s["hprev"]],
        lambda g, hp: (g * hp, g), [F32], 256)
    gs["lru_w_a"], gs["lru_w_x"] = _block_diag_untile(d_wa), _block_diag_untile(d_wx)
    d_lx, dwb = _conv_bwd(n + "lru_conv", s["lx"], d_u, p["lru_conv_w8"])
    gs["lru_conv_w"], gs["lru_conv_b"] = dwb[:4], dwb[4]
    segs = [("lx", d_lx), ("lg", d_lg), ("z", d_z), ("xbc", d_xbc), ("dt", d_dt), ("g", d_g)]
    dws = [_mm_tn(n + "dw_in_" + k, s["xn"], d) for k, d in segs]
    dws[4] = dws[4][:, :IN_WIDTHS[4]]
    below = _shards_from_full("w_in", jnp.concatenate(dws, axis=1))
    pairs_a = [(d, p["w_" + k]) for k, d in segs[:3]]
    pairs_b = [(d, p["w_" + k]) for k, d in segs[3:]]
    if first_layer:
        half = below.shape[1] // 2
        d_xn, (top,) = _mm(n + "d_xn_a", pairs_a, "nt", rider=_scatter_rider([below[:, :half]]))
        d_xn, (bottom,) = _mm(n + "d_xn_b", pairs_b, "nt", res=d_xn, rider=_scatter_rider([below[:, half:]]))
        arrived["w_in"] = jnp.concatenate([top, bottom], axis=1)
        below = None
    else:
        d_xn = _mm(n + "d_xn_a", pairs_a, "nt")
        d_xn = _mm(n + "d_xn_b", pairs_b, "nt", res=d_xn)
    dh_in, dh_in_b, gs["norm1_g"] = _rowwise_bwd(n + "rms1", _f_rms_res, [s["h_in"]], [p["norm1_g"]], [d_xn, dh_mid],
                                                 _ident, [(F32, BF16)], 512)
    for k in ("norm1_g", "norm2_g", "b_gate", "ssd_norm_g", "lru_b_a", "lru_b_x", "lru_lambda"):
        gs[k] = gs[k].reshape(-1)
    return dh_in, dh_in_b, gs, arrived, arrived_above, below, gw


def _step(inp):
    x = inp["x"][0]
    target = inp["loss_target"][0]
    dev = 4 * lax.axis_index("x") + 2 * lax.axis_index("y") + lax.axis_index("c")

    def mine(l, names):
        return [(inp[n][l].T if n in TRANSPOSED else inp[n][l]).astype(BF16) for n in names]

    first = _all_gather("gather_first", mine(0, ("w_in",)) + [inp["lru_conv_w"], inp["ssd_conv_w"]])
    small = {n: inp[n] for n in SMALL}
    small["lru_conv_w"] = jnp.moveaxis(first[1], 0, 2).reshape(DEPTH, 4, -1)
    small["ssd_conv_w"] = jnp.moveaxis(first[2], 0, 2).reshape(DEPTH, 4, -1)
    shards = {"w_in": first[0]}

    h, saved, params = x, [], []
    for l in range(DEPTH):
        p = _small_params(l, small)
        p.update(_in_weights(shards["w_in"]))
        riders = {}
        if l == 0:
            riders["ssd"] = _gather_rider(mine(0, REST))
        else:
            p.update(_rest_weights([shards[n] for n in REST]))
        if l + 1 < DEPTH:
            riders["post"] = _gather_rider(mine(l + 1, ("w_ffn_in", "w_ffn_out")))
            riders["ffn_in"] = _gather_rider(mine(l + 1, ("w_in",)))
            riders["ffn_out"] = _gather_rider(mine(l + 1, ("w_branch", "w_out")))
        h, s, got = _layer_fwd(l, h, p, riders)
        if l + 1 < DEPTH:
            shards = {"w_ffn_in": got["post"][0], "w_ffn_out": got["post"][1], "w_in": got["ffn_in"][0],
                      "w_branch": got["ffn_out"][0], "w_out": got["ffn_out"][1]}
        saved.append(s)
        params.append(p)
    dh, d_nf, loss_acc, dh_b = _loss_head(h, target, small["norm_f"].reshape(1, -1))
    loss = lax.psum(loss_acc[0, 0], ("x", "y", "c"))

    gss, received = [None] * DEPTH, [None] * DEPTH
    handed_down = None
    for l in reversed(range(DEPTH)):
        dh, dh_b, gss[l], received[l], arrived_above, handed_down, _ = _layer_bwd(
            l, dh, dh_b, saved[l], params[l], handed_down, l == 0)
        if arrived_above is not None:
            received[l + 1]["w_in"] = arrived_above
    grad_x = dh[None]
    out = {"loss": loss, "grad_x": grad_x}
    small_full = {n: jnp.stack([gss[l][n] for l in range(DEPTH)]) for n in SMALL if n != "norm_f"}
    small_full["norm_f"] = d_nf.reshape(-1)
    part = _pack([small_full[n] for n in SMALL], F32, SMALL_ROWS)
    for n in BIG:
        flip = (lambda a: jnp.transpose(a, (0, 2, 1))) if n in TRANSPOSED else (lambda a: a)
        res = _adamw_sum("adamw_" + n, flip(inp[n]), flip(inp["m_" + n]), flip(inp["v_" + n]),
                         [received[l][n] for l in range(DEPTH)], _gather_rider([part]) if n == "w_in" else None)
        if n == "w_in":
            everyone = res[4]
        out["grad_" + n], out["delta_" + n], out["new_m_" + n], out["new_v_" + n] = [flip(a) for a in res[:4]]
    g_small_flat = _sum_blocks("sum_small_grads", everyone.reshape(-1, LANES), SMALL_ROWS)
    g_small = dict(zip(SMALL, _unpack(g_small_flat, [small_full[n].shape for n in SMALL])))
    for n in ("lru_conv_w", "ssd_conv_w"):
        w = inp[n].shape[-1]
        g_small[n] = lax.dynamic_slice_in_dim(g_small[n], dev * w, w, axis=2)

    shapes = [inp[n].shape for n in SMALL]
    packs = [_pack([src[pre + n] for n in SMALL], F32, SMALL_ROWS)
             for src, pre in ((inp, ""), (g_small, ""), (inp, "m_"), (inp, "v_"))]
    d, m2, v2 = _adamw("adamw_small", *packs)
    for n, dd, mm, vv in zip(SMALL, _unpack(d, shapes), _unpack(m2, shapes), _unpack(v2, shapes)):
        out["grad_" + n] = g_small[n]
        out["delta_" + n], out["new_m_" + n], out["new_v_" + n] = dd, mm, vv
    return out


WEIGHTS = ("norm1_g", "w_in", "b_gate", "lru_conv_w", "lru_conv_b", "lru_w_a", "lru_b_a", "lru_w_x", "lru_b_x",
           "lru_lambda", "ssd_conv_w", "ssd_conv_b", "ssd_dt_bias", "ssd_A_log", "ssd_D", "ssd_norm_g", "w_branch",
           "w_out", "norm2_g", "w_ffn_in", "w_ffn_out", "norm_f")


def kernel(x, norm1_g, w_in, b_gate, lru_conv_w, lru_conv_b, lru_w_a, lru_b_a, lru_w_x, lru_b_x, lru_lambda, ssd_conv_w, ssd_conv_b, ssd_dt_bias, ssd_A_log, ssd_D, ssd_norm_g, w_branch, w_out, norm2_g, w_ffn_in, w_ffn_out, norm_f, loss_target, m_norm1_g, m_w_in, m_b_gate, m_lru_conv_w, m_lru_conv_b, m_lru_w_a, m_lru_b_a, m_lru_w_x, m_lru_b_x, m_lru_lambda, m_ssd_conv_w, m_ssd_conv_b, m_ssd_dt_bias, m_ssd_A_log, m_ssd_D, m_ssd_norm_g, m_w_branch, m_w_out, m_norm2_g, m_w_ffn_in, m_w_ffn_out, m_norm_f, v_norm1_g, v_w_in, v_b_gate, v_lru_conv_w, v_lru_conv_b, v_lru_w_a, v_lru_b_a, v_lru_w_x, v_lru_b_x, v_lru_lambda, v_ssd_conv_w, v_ssd_conv_b, v_ssd_dt_bias, v_ssd_A_log, v_ssd_D, v_ssd_norm_g, v_w_branch, v_w_out, v_norm2_g, v_w_ffn_in, v_w_ffn_out, v_norm_f):
    out = _step(dict(locals()))
    res = [out["loss"], out["grad_x"]]
    for pre in ("grad_", "delta_", "new_m_", "new_v_"):
        res += [out[pre + n] for n in WEIGHTS]
    return tuple(res)
```

```python
import functools

import numpy as np
import jax
import jax.numpy as jnp
from jax import lax
from jax.experimental import pallas as pl
from jax.experimental.pallas import tpu as pltpu

F32 = jnp.float32
BF16 = jnp.bfloat16
HIGHEST = lax.Precision.HIGHEST

D_MODEL = 1024
DEPTH = 2
CHUNK = 64
LRU_C = 8.0
SSD_INNER = 2048
SSD_HEADS = 32
SSD_HEAD_DIM = 64
SSD_GROUPS = 4
SSD_STATE = 128
SSD_CONV_DIM = 3072
D_FF = 2816
EPS = 1e-6
N_DEV = 8
LANES = 128
DT_PAD = LANES
IN_WIDTHS = (1024, 1024, 2048, 3072, 32, 2048)
IN_OFFS = tuple(int(v) for v in np.cumsum((0,) + IN_WIDTHS))

ADAM_LR = 0.001
ADAM_B1 = 0.9
ADAM_B2 = 0.999
ADAM_EPS = 1e-08
ADAM_WD = 0.01
ADAM_STEP = 10

VMEM_LIMIT = 56 * 1024 * 1024
MESH = pl.DeviceIdType.MESH


def _cparams(sem=None):
    return pltpu.CompilerParams(dimension_semantics=sem, vmem_limit_bytes=VMEM_LIMIT)


def _sds(shape, dtype):
    return jax.ShapeDtypeStruct(tuple(shape), dtype)


def _pick(n, cap):
    if n <= cap:
        return n
    best = LANES
    for t in range(LANES, cap + 1, LANES):
        if n % t == 0:
            best = t
    assert n % best == 0, (n, cap)
    return best


def _mm(name, pairs, mode, out_dtype=F32, res=None, rider=None):
    M = pairs[0][0].shape[0]
    N = pairs[0][1].shape[1] if mode == "nn" else pairs[0][1].shape[0]
    npair = len(pairs)
    tm = min(M, 1024 if npair <= 3 else 512)
    tn = _pick(N, 1536 if npair == 1 else 1024)
    tks, nks, starts = [], [], []
    s = 0
    for a, _ in pairs:
        k = a.shape[1]
        tk = _pick(k, 1536)
        tks.append(tk)
        nks.append(k // tk)
        starts.append(s)
        s += k // tk
    nk = s
    dims = (((1,), (0,)), ((), ())) if mode == "nn" else (((1,), (1,)), ((), ()))

    def body(*refs):
        ab = refs[:2 * npair]
        pos = 2 * npair
        r_ref = None
        if res is not None:
            r_ref = refs[pos]
            pos += 1
        o_ref = refs[pos]
        acc = refs[pos + 1] if nk > 1 else None
        k = pl.program_id(2)

        def finish(r):
            if r_ref is not None:
                r = r + r_ref[...].astype(F32)
            o_ref[...] = r.astype(o_ref.dtype)

        for p in range(npair):
            a_ref, b_ref = ab[2 * p], ab[2 * p + 1]
            lo, hi = starts[p], starts[p] + nks[p]

            def step(a_ref=a_ref, b_ref=b_ref, lo=lo, hi=hi):
                d = lax.dot_general(a_ref[...].astype(BF16), b_ref[...].astype(BF16), dims,
                                    preferred_element_type=F32)
                if nk == 1:
                    finish(d)
                    return
                if lo == 0:
                    @pl.when(k == 0)
                    def _():
                        acc[...] = d
                if hi == nk:
                    @pl.when(k == nk - 1)
                    def _():
                        finish(acc[...] + d)
                if max(lo, 1) < min(hi, nk - 1):
                    @pl.when((k > 0) & (k < nk - 1))
                    def _():
                        acc[...] += d

            if npair == 1:
                step()
            else:
                pl.when((k >= lo) & (k < hi))(step)

    in_specs, args = [], []
    for p, (a, b) in enumerate(pairs):
        def kk(k, p=p):
            return jnp.clip(k - starts[p], 0, nks[p] - 1)
        in_specs.append(pl.BlockSpec((tm, tks[p]), lambda i, j, k, kk=kk: (i, kk(k))))
        if mode == "nn":
            in_specs.append(pl.BlockSpec((tks[p], tn), lambda i, j, k, kk=kk: (kk(k), j)))
        else:
            in_specs.append(pl.BlockSpec((tn, tks[p]), lambda i, j, k, kk=kk: (j, kk(k))))
        args += [a, b]
    if res is not None:
        in_specs.append(pl.BlockSpec((tm, tn), lambda i, j, k: (i, j)))
        args.append(res)
    r_args, r_in, r_out, r_shape, r_scratch = _rider_parts(rider)
    grid = (M // tm, N // tn, nk)
    out = pl.pallas_call(
        _with_rider(body, len(args), 1, rider, grid), grid=grid, in_specs=in_specs + r_in,
        out_specs=[pl.BlockSpec((tm, tn), lambda i, j, k: (i, j))] + r_out,
        out_shape=[_sds((M, N), out_dtype)] + r_shape,
        scratch_shapes=([pltpu.VMEM((tm, tn), F32)] if nk > 1 else []) + r_scratch,
        compiler_params=_cparams(("parallel", "parallel", "arbitrary") if rider is None else ("arbitrary",) * 3),
        name=name,
    )(*args, *r_args)
    return out[0] if rider is None else (out[0], out[1:])


def _mm_tn(name, a, b, out_dtype=BF16):
    M, Ka = a.shape
    N = b.shape[1]
    tm = min(M, 2048)
    tka = _pick(Ka, 1024)
    tn = _pick(N, 1024)
    nm = M // tm

    def body(a_ref, b_ref, o_ref, *scratch):
        k = pl.program_id(2)
        d = lax.dot_general(a_ref[...].astype(BF16), b_ref[...].astype(BF16),
                            (((0,), (0,)), ((), ())), preferred_element_type=F32)
        if nm == 1:
            o_ref[...] = d.astype(o_ref.dtype)
            return
        acc = scratch[0]

        @pl.when(k == 0)
        def _():
            acc[...] = d

        @pl.when((k > 0) & (k < nm - 1))
        def _():
            acc[...] += d

        @pl.when(k == nm - 1)
        def _():
            o_ref[...] = (acc[...] + d).astype(o_ref.dtype)

    return pl.pallas_call(
        body, grid=(Ka // tka, N // tn, nm),
        in_specs=[pl.BlockSpec((tm, tka), lambda i, j, k: (k, i)),
                  pl.BlockSpec((tm, tn), lambda i, j, k: (k, j))],
        out_specs=pl.BlockSpec((tka, tn), lambda i, j, k: (i, j)),
        out_shape=_sds((Ka, N), out_dtype),
        scratch_shapes=[pltpu.VMEM((tka, tn), F32)] if nm > 1 else [],
        compiler_params=_cparams(("parallel", "parallel", "arbitrary")), name=name,
    )(a, b)


def _rowwise_fwd(name, fn, rows, pars, outs, tb, rider=None):
    rows = [r if isinstance(r, tuple) else (r, 0, r.shape[1]) for r in rows]
    T = rows[0][0].shape[0]
    tb = min(tb, T)
    nr, npar = len(rows), len(pars)
    r_args, r_in, r_out, r_shape, r_scratch = _rider_parts(rider)

    def body(*refs):
        rv = [r[...].astype(F32) for r in refs[:nr]]
        pv = [p[...] for p in refs[nr:nr + npar]]
        res = fn(*rv, *pv)
        for o, r in zip(refs[nr + npar:], res):
            o[...] = r.astype(o.dtype)

    in_specs = [pl.BlockSpec((tb, w), lambda i, c=c: (i, c)) for _, c, w in rows]
    in_specs += [pl.BlockSpec(p.shape, lambda i, n=p.ndim: (0,) * n) for p in pars]
    return pl.pallas_call(
        _with_rider(body, nr + npar, len(outs), rider, (T // tb,)), grid=(T // tb,), in_specs=in_specs + r_in,
        out_specs=[pl.BlockSpec((tb, w), lambda i: (i, 0)) for w, _ in outs] + r_out,
        out_shape=[_sds((T, w), dt) for w, dt in outs] + r_shape, scratch_shapes=r_scratch,
        compiler_params=_cparams(("parallel",) if rider is None else ("arbitrary",)), name=name,
    )(*[r[0] for r in rows], *pars, *r_args)


def _rowwise_bwd(name, fn, rows, pars, cot_rows, cot_fn, row_out, tb):
    rows = [r if isinstance(r, tuple) else (r, 0, r.shape[1]) for r in rows]
    cot_rows = [r if isinstance(r, tuple) else (r, 0, r.shape[1]) for r in cot_rows]
    T = rows[0][0].shape[0]
    tb = min(tb, T)
    nr, npar, nc = len(rows), len(pars), len(cot_rows)
    want, want_dt = [], []
    for k, dt in enumerate(row_out):
        for d in (() if dt is None else dt if isinstance(dt, tuple) else (dt,)):
            want.append(k)
            want_dt.append(d)

    def body(*refs):
        i = pl.program_id(0)
        rv = [r[...].astype(F32) for r in refs[:nr]]
        pv = [p[...] for p in refs[nr:nr + npar]]
        cv = [c[...].astype(F32) for c in refs[nr + npar:nr + npar + nc]]
        o_refs = refs[nr + npar + nc:]
        _, vjp = jax.vjp(fn, *rv, *pv)
        grads = vjp(tuple(cot_fn(*cv)))
        for o, k in zip(o_refs[:len(want)], want):
            o[...] = grads[k].astype(o.dtype)
        p_refs = o_refs[len(want):]

        @pl.when(i == 0)
        def _():
            for o in p_refs:
                o[...] = jnp.zeros_like(o)

        for o, g in zip(p_refs, grads[nr:]):
            o[...] += g

    in_specs = [pl.BlockSpec((tb, w), lambda i, c=c: (i, c)) for _, c, w in rows]
    in_specs += [pl.BlockSpec(p.shape, lambda i, n=p.ndim: (0,) * n) for p in pars]
    in_specs += [pl.BlockSpec((tb, w), lambda i, c=c: (i, c)) for _, c, w in cot_rows]
    out_specs = [pl.BlockSpec((tb, rows[k][2]), lambda i: (i, 0)) for k in want]
    out_specs += [pl.BlockSpec(p.shape, lambda i, n=p.ndim: (0,) * n) for p in pars]
    out_shape = [_sds((T, rows[k][2]), d) for k, d in zip(want, want_dt)] + [_sds(p.shape, F32) for p in pars]
    return pl.pallas_call(
        body, grid=(T // tb,), in_specs=in_specs, out_specs=out_specs, out_shape=out_shape,
        compiler_params=_cparams(("arbitrary",)), name=name,
    )(*[r[0] for r in rows], *pars, *[r[0] for r in cot_rows])


def _f_rms(x, g):
    r = lax.rsqrt(jnp.mean(x * x, axis=-1, keepdims=True) + EPS)
    return (x * r * g,)


def _f_lru_gates(u, wa, ba, wx, bx, lam):
    ub = u.astype(BF16)
    ra, rx = [], []
    for k in range(D_MODEL // LANES):
        uk = ub[:, LANES * k:LANES * (k + 1)]
        ra.append(jnp.dot(uk, wa[k].astype(BF16), preferred_element_type=F32))
        rx.append(jnp.dot(uk, wx[k].astype(BF16), preferred_element_type=F32))
    r = jax.nn.sigmoid(jnp.concatenate(ra, axis=1) + ba)
    i = jax.nn.sigmoid(jnp.concatenate(rx, axis=1) + bx)
    log_a = -LRU_C * r * jax.nn.softplus(-lam)
    a = jnp.exp(log_a)
    t = jnp.tanh(log_a)
    b = jnp.sqrt(-2.0 * t / (1.0 - t)) * (i * u)
    return a, b


def _f_post(hl, lgate, yssd, xpre_s, z, d_e, ng):
    ya = jax.nn.gelu(lgate) * hl
    y = (yssd + d_e * jax.nn.silu(xpre_s)) * jax.nn.silu(z)
    gw = SSD_INNER // SSD_GROUPS
    parts = []
    for g in range(SSD_GROUPS):
        yg = y[:, gw * g:gw * (g + 1)]
        parts.append(yg * lax.rsqrt(jnp.mean(yg * yg, axis=-1, keepdims=True) + EPS))
    return ya, jnp.concatenate(parts, axis=1) * ng


def _f_merge(m_a, m_b, gates, bg):
    g = jax.nn.sigmoid(gates + bg)
    return (g[:, :D_MODEL] * m_a + g[:, D_MODEL:] * m_b,)


def _f_act(gu):
    return (jax.nn.silu(gu[:, :D_FF]) * gu[:, D_FF:],)


def _ident(*c):
    return c


def _loss_head(h, target, nf):
    T = h.shape[0]
    tb = min(512, T)

    def body(h_ref, t_ref, nf_ref, dh_ref, dnf_ref, loss_ref, dhb_ref):
        i = pl.program_id(0)

        @pl.when(i == 0)
        def _():
            dnf_ref[...] = jnp.zeros_like(dnf_ref)
            loss_ref[...] = jnp.zeros_like(loss_ref)

        (y,), vjp = jax.vjp(_f_rms, h_ref[...], nf_ref[...])
        err = y - t_ref[...]
        dh, dnf = vjp((err * (1.0 / D_MODEL),))
        dh_ref[...] = dh
        dhb_ref[...] = dh.astype(BF16)
        dnf_ref[...] += dnf
        part = 0.5 * jnp.sum(jnp.mean(err * err, axis=-1, keepdims=True), axis=0, keepdims=True)
        loss_ref[...] += jnp.broadcast_to(part, loss_ref.shape)

    return pl.pallas_call(
        body, grid=(T // tb,),
        in_specs=[pl.BlockSpec((tb, D_MODEL), lambda i: (i, 0)), pl.BlockSpec((tb, D_MODEL), lambda i: (i, 0)),
                  pl.BlockSpec((1, D_MODEL), lambda i: (0, 0))],
        out_specs=[pl.BlockSpec((tb, D_MODEL), lambda i: (i, 0)), pl.BlockSpec((1, D_MODEL), lambda i: (0, 0)),
                   pl.BlockSpec((8, LANES), lambda i: (0, 0)), pl.BlockSpec((tb, D_MODEL), lambda i: (i, 0))],
        out_shape=[_sds((T, D_MODEL), F32), _sds((1, D_MODEL), F32), _sds((8, LANES), F32),
                   _sds((T, D_MODEL), BF16)],
        compiler_params=_cparams(("arbitrary",)), name="loss_head",
    )(h, target, nf)


CONV_TC = 1024
HALO = 16


def _conv_fwd(name, x, w8, b):
    T, C = x.shape
    tb = min(1024, T)
    nb = tb // HALO

    def body(x_ref, xp_ref, w_ref, b_ref, y_ref, sc):
        i = pl.program_id(0)
        xv = x_ref[...].astype(F32)
        sc[pl.ds(HALO, tb), :] = xv
        sc[pl.ds(0, HALO), :] = jnp.where(i > 0, xp_ref[...].astype(F32), 0.0)
        acc = b_ref[...] + w_ref[3:4, :] * xv
        for k in range(3):
            acc = acc + w_ref[k:k + 1, :] * sc[pl.ds(HALO - 3 + k, tb), :]
        y_ref[...] = acc

    return pl.pallas_call(
        body, grid=(T // tb, C // CONV_TC),
        in_specs=[pl.BlockSpec((tb, CONV_TC), lambda i, j: (i, j)),
                  pl.BlockSpec((HALO, CONV_TC), lambda i, j: (jnp.maximum(i * nb - 1, 0), j)),
                  pl.BlockSpec((8, CONV_TC), lambda i, j: (0, j)),
                  pl.BlockSpec((1, CONV_TC), lambda i, j: (0, j))],
        out_specs=pl.BlockSpec((tb, CONV_TC), lambda i, j: (i, j)),
        out_shape=_sds((T, C), F32),
        scratch_shapes=[pltpu.VMEM((tb + HALO, CONV_TC), F32)],
        compiler_params=_cparams(("parallel", "parallel")), name=name,
    )(x, x, w8, b)


def _conv_bwd(name, x, dy, w8):
    T, C = x.shape
    tb = min(512, T)
    nb = tb // HALO
    nt = T // tb

    def body(x_ref, xp_ref, dy_ref, dyn_ref, w_ref, dx_ref, dwb_ref, scx, scd):
        i = pl.program_id(1)
        dyv = dy_ref[...]
        xv = x_ref[...].astype(F32)
        scx[pl.ds(HALO, tb), :] = xv
        scx[pl.ds(0, HALO), :] = jnp.where(i > 0, xp_ref[...].astype(F32), 0.0)
        scd[pl.ds(0, tb), :] = dyv
        scd[pl.ds(tb, HALO), :] = jnp.where(i < nt - 1, dyn_ref[...], 0.0)
        dx = w_ref[3:4, :] * dyv
        rows = []
        for k in range(3):
            dx = dx + w_ref[k:k + 1, :] * scd[pl.ds(3 - k, tb), :]
            rows.append(jnp.sum(dyv * scx[pl.ds(HALO - 3 + k, tb), :], axis=0, keepdims=True))
        rows.append(jnp.sum(dyv * xv, axis=0, keepdims=True))
        rows.append(jnp.sum(dyv, axis=0, keepdims=True))
        rows.append(jnp.zeros((3, CONV_TC), F32))
        dx_ref[...] = dx.astype(dx_ref.dtype)

        @pl.when(i == 0)
        def _():
            dwb_ref[...] = jnp.zeros_like(dwb_ref)

        dwb_ref[...] += jnp.concatenate(rows, axis=0)

    return pl.pallas_call(
        body, grid=(C // CONV_TC, nt),
        in_specs=[pl.BlockSpec((tb, CONV_TC), lambda j, i: (i, j)),
                  pl.BlockSpec((HALO, CONV_TC), lambda j, i: (jnp.maximum(i * nb - 1, 0), j)),
                  pl.BlockSpec((tb, CONV_TC), lambda j, i: (i, j)),
                  pl.BlockSpec((HALO, CONV_TC), lambda j, i: (jnp.minimum((i + 1) * nb, T // HALO - 1), j)),
                  pl.BlockSpec((8, CONV_TC), lambda j, i: (0, j))],
        out_specs=[pl.BlockSpec((tb, CONV_TC), lambda j, i: (i, j)),
                   pl.BlockSpec((8, CONV_TC), lambda j, i: (0, j))],
        out_shape=[_sds((T, C), BF16), _sds((8, C), F32)],
        scratch_shapes=[pltpu.VMEM((tb + HALO, CONV_TC), F32), pltpu.VMEM((tb + HALO, CONV_TC), F32)],
        compiler_params=_cparams(("parallel", "arbitrary")), name=name,
    )(x, x, dy, dy, w8)


SCAN_TB = 1024


def _lru_scan_fwd(a, b):
    T, C = a.shape
    tb = min(SCAN_TB, T)

    def body(a_ref, b_ref, h_ref, hp_ref, carry):
        @pl.when(pl.program_id(0) == 0)
        def _():
            carry[...] = jnp.zeros_like(carry)

        def group(gi, h):
            r0 = pl.multiple_of(gi * 8, 8)
            at = a_ref[pl.ds(r0, 8), :]
            bt = b_ref[pl.ds(r0, 8), :]
            hs, hps = [], []
            for r in range(8):
                hps.append(h)
                h = at[r:r + 1, :] * h + bt[r:r + 1, :]
                hs.append(h)
            h_ref[pl.ds(r0, 8), :] = jnp.concatenate(hs, axis=0)
            hp_ref[pl.ds(r0, 8), :] = jnp.concatenate(hps, axis=0)
            return h

        carry[0:1, :] = lax.fori_loop(0, tb // 8, group, carry[0:1, :])

    spec = pl.BlockSpec((tb, C), lambda i: (i, 0))
    return pl.pallas_call(
        body, grid=(T // tb,), in_specs=[spec, spec], out_specs=[spec, spec],
        out_shape=[_sds((T, C), F32), _sds((T, C), F32)],
        scratch_shapes=[pltpu.VMEM((8, C), F32)],
        compiler_params=_cparams(("arbitrary",)), name="lru_scan_fwd",
    )(a, b)


def _lru_scan_bwd(a, dh):
    T, C = a.shape
    tb = min(SCAN_TB, T)
    nt = T // tb

    def body(a_ref, dh_ref, g_ref, carry):
        @pl.when(pl.program_id(0) == 0)
        def _():
            carry[...] = jnp.zeros_like(carry)

        def group(gi, c):
            r0 = pl.multiple_of((tb // 8 - 1 - gi) * 8, 8)
            at = a_ref[pl.ds(r0, 8), :]
            dt = dh_ref[pl.ds(r0, 8), :]
            gs = [None] * 8
            for r in range(7, -1, -1):
                g = dt[r:r + 1, :] + c
                c = at[r:r + 1, :] * g
                gs[r] = g
            g_ref[pl.ds(r0, 8), :] = jnp.concatenate(gs, axis=0)
            return c

        carry[0:1, :] = lax.fori_loop(0, tb // 8, group, carry[0:1, :])

    spec = pl.BlockSpec((tb, C), lambda i: (nt - 1 - i, 0))
    return pl.pallas_call(
        body, grid=(nt,), in_specs=[spec, spec], out_specs=spec,
        out_shape=_sds((T, C), F32),
        scratch_shapes=[pltpu.VMEM((8, C), F32)],
        compiler_params=_cparams(("arbitrary",)), name="lru_scan_bwd",
    )(a, dh)


def _ssd_chunk(xpre, dtraw, state, dtb, alog):
    xc = jax.nn.silu(xpre)
    xs = xc[:, :SSD_INNER]
    bm = xc[:, SSD_INNER:SSD_INNER + SSD_GROUPS * SSD_STATE]
    cm = xc[:, SSD_INNER + SSD_GROUPS * SSD_STATE:]
    dt = jax.nn.softplus(dtraw + dtb)
    a = dt * (-jnp.exp(alog))
    ltri = (lax.broadcasted_iota(jnp.int32, (CHUNK, CHUNK), 0)
            >= lax.broadcasted_iota(jnp.int32, (CHUNK, CHUNK), 1)).astype(F32)
    a_cs = jnp.dot(ltri, a, precision=HIGHEST, preferred_element_type=F32)
    npair = SSD_HEADS // 2
    pi = lax.broadcasted_iota(jnp.int32, (npair, LANES), 0)
    hi = lax.broadcasted_iota(jnp.int32, (npair, LANES), 1)
    sel_even = (hi == 2 * pi).astype(F32)
    sel_odd = (hi == 2 * pi + 1).astype(F32)
    top = lax.broadcasted_iota(jnp.int32, (2 * CHUNK, LANES), 0) < CHUNK

    def pair_transpose(v):
        v2 = jnp.concatenate([v, v], axis=0)
        dn = (((1,), (1,)), ((), ()))
        return (lax.dot_general(sel_even, jnp.where(top, v2, 0.0), dn, precision=HIGHEST, preferred_element_type=F32)
                + lax.dot_general(sel_odd, jnp.where(top, 0.0, v2), dn, precision=HIGHEST, preferred_element_type=F32))

    a_t2 = pair_transpose(a_cs)
    dt_t2 = pair_transpose(dt)
    a_last = a_cs[CHUNK - 1:CHUNK, :]
    dte = jnp.exp(a_last - a_cs) * dt
    cd = jnp.exp(a_last)
    lane = lax.broadcasted_iota(jnp.int32, (CHUNK, LANES), 1)
    left = lane < SSD_HEAD_DIM
    right = jnp.logical_not(left)
    left1 = left[0:1]
    tril2 = lax.broadcasted_iota(jnp.int32, (CHUNK, LANES), 0) >= (lane & (SSD_HEAD_DIM - 1))
    gw = SSD_INNER // SSD_GROUPS
    ys, news = [], []
    for g in range(SSD_GROUPS):
        bg = bm[:, SSD_STATE * g:SSD_STATE * (g + 1)].astype(BF16)
        cg = cm[:, SSD_STATE * g:SSD_STATE * (g + 1)].astype(BF16)
        bg2 = jnp.concatenate([bg, bg], axis=0)
        scores2 = lax.dot_general(cg, bg2, (((1,), (1,)), ((), ())), preferred_element_type=F32)
        coff = jnp.dot(cg, state[:, gw * g:gw * (g + 1)].astype(BF16), preferred_element_type=F32)
        for j in range(gw // LANES):
            lo = gw * g + LANES * j
            p = lo // LANES
            h0 = 2 * p
            xp = xs[:, lo:lo + LANES]
            col = jnp.where(left, a_cs[:, h0:h0 + 1], a_cs[:, h0 + 1:h0 + 2])
            dm = jnp.exp(jnp.where(tril2, col - a_t2[p:p + 1, :], -1e30)) * dt_t2[p:p + 1, :]
            sd = (scores2 * dm).astype(BF16)
            x_bd = jnp.concatenate([jnp.where(left, xp, 0.0), jnp.where(right, xp, 0.0)], axis=0).astype(BF16)
            acc = jnp.dot(sd, x_bd, preferred_element_type=F32)
            ys.append(acc + coff[:, LANES * j:LANES * (j + 1)] * jnp.exp(col))
            dtee = jnp.where(left, dte[:, h0:h0 + 1], dte[:, h0 + 1:h0 + 2])
            xw = (xp * dtee).astype(BF16)
            cde = jnp.where(left1, cd[:, h0:h0 + 1], cd[:, h0 + 1:h0 + 2])
            news.append(state[:, lo:lo + LANES] * cde
                        + lax.dot_general(bg, xw, (((0,), (0,)), ((), ())), preferred_element_type=F32))
    return jnp.concatenate(ys, axis=1), jnp.concatenate(news, axis=1)


def _ssd_fwd(name, xpre, dtraw, dtb, alog, rider=None):
    T = xpre.shape[0]
    n = T // CHUNK
    r_args, r_in, r_out, r_shape, r_scratch = _rider_parts(rider)

    def body(xp, dr, dtb_ref, al_ref, y_ref, st_ref, state):
        @pl.when(pl.program_id(0) == 0)
        def _():
            state[...] = jnp.zeros_like(state)

        st_ref[0] = state[...]
        y, new = _ssd_chunk(xp[...], dr[...], state[...], dtb_ref[...], al_ref[...])
        y_ref[...] = y
        state[...] = new

    small = pl.BlockSpec((1, LANES), lambda c: (0, 0))
    return pl.pallas_call(
        _with_rider(body, 4, 2, rider, (n,)), grid=(n,),
        in_specs=[pl.BlockSpec((CHUNK, SSD_CONV_DIM), lambda c: (c, 0)),
                  pl.BlockSpec((CHUNK, DT_PAD), lambda c: (c, 0)), small, small] + r_in,
        out_specs=[pl.BlockSpec((CHUNK, SSD_INNER), lambda c: (c, 0)),
                   pl.BlockSpec((1, SSD_STATE, SSD_INNER), lambda c: (c, 0, 0))] + r_out,
        out_shape=[_sds((T, SSD_INNER), F32), _sds((n, SSD_STATE, SSD_INNER), F32)] + r_shape,
        scratch_shapes=[pltpu.VMEM((SSD_STATE, SSD_INNER), F32)] + r_scratch,
        compiler_params=_cparams(("arbitrary",)), name=name,
    )(xpre, dtraw, dtb, alog, *r_args)


def _ssd_bwd(name, xpre, dtraw, states, dy, dxs_extra, dtb, alog, rider=None):
    T = xpre.shape[0]
    n = T // CHUNK
    r_args, r_in, r_out, r_shape, r_scratch = _rider_parts(rider)

    def body(xp, dr, st, dy_ref, dx_ref, dtb_ref, al_ref, dxp_ref, ddr_ref, ddtb_ref, dal_ref, dstate):
        @pl.when(pl.program_id(0) == 0)
        def _():
            dstate[...] = jnp.zeros_like(dstate)
            ddtb_ref[...] = jnp.zeros_like(ddtb_ref)
            dal_ref[...] = jnp.zeros_like(dal_ref)

        _, vjp = jax.vjp(_ssd_chunk, xp[...], dr[...], st[0], dtb_ref[...], al_ref[...])
        dxp, ddr, ds, db, da = vjp((dy_ref[...], dstate[...]))
        dxp_ref[:, :SSD_INNER] = dxp[:, :SSD_INNER] + dx_ref[...]
        dxp_ref[:, SSD_INNER:] = dxp[:, SSD_INNER:]
        ddr_ref[...] = ddr.astype(ddr_ref.dtype)
        dstate[...] = ds
        ddtb_ref[...] += db
        dal_ref[...] += da

    def rev(c):
        return (n - 1 - c, 0)

    small = pl.BlockSpec((1, LANES), lambda c: (0, 0))
    return pl.pallas_call(
        _with_rider(body, 7, 4, rider, (n,)), grid=(n,),
        in_specs=[pl.BlockSpec((CHUNK, SSD_CONV_DIM), rev), pl.BlockSpec((CHUNK, DT_PAD), rev),
                  pl.BlockSpec((1, SSD_STATE, SSD_INNER), lambda c: (n - 1 - c, 0, 0)),
                  pl.BlockSpec((CHUNK, SSD_INNER), rev), pl.BlockSpec((CHUNK, SSD_INNER), rev), small, small] + r_in,
        out_specs=[pl.BlockSpec((CHUNK, SSD_CONV_DIM), rev), pl.BlockSpec((CHUNK, DT_PAD), rev), small, small] + r_out,
        out_shape=[_sds((T, SSD_CONV_DIM), F32), _sds((T, DT_PAD), BF16), _sds((1, LANES), F32),
                   _sds((1, LANES), F32)] + r_shape,
        scratch_shapes=[pltpu.VMEM((SSD_STATE, SSD_INNER), F32)] + r_scratch,
        compiler_params=_cparams(("arbitrary",)), name=name,
    )(xpre, dtraw, states, dy, dxs_extra, dtb, alog, *r_args)


HBM_SPEC = pl.BlockSpec(memory_space=pltpu.HBM)
N_PEER = N_DEV - 1


def _position():
    return lax.axis_index("x"), lax.axis_index("y"), lax.axis_index("c")


def _all_gather(name, blks):
    return _exchange_call(name, _gather_phases, blks, [_sds((N_DEV,) + b.shape, b.dtype) for b in blks])


def _scatter_exchange(name, gs):
    return _exchange_call(name, _scatter_phases, gs, [_sds(g.shape, g.dtype) for g in gs])


def _exchange_scratch(na):
    return [pltpu.SemaphoreType.DMA((na * N_PEER,)), pltpu.SemaphoreType.DMA((na * N_PEER,)),
            pltpu.SemaphoreType.DMA((na,))]


def _exchange_call(name, phases, arrays, out_shape):
    na = len(arrays)

    def body(*refs):
        begin, finish = phases(refs[:na], refs[na:2 * na], *refs[2 * na:])
        begin()
        finish()

    return pl.pallas_call(
        body, out_shape=out_shape, in_specs=[HBM_SPEC] * na, out_specs=[HBM_SPEC] * na,
        scratch_shapes=_exchange_scratch(na), name=name,
    )(*arrays)


def _gather_phases(x_refs, out_refs, send_sems, recv_sems, local_sems):
    na = len(x_refs)
    x, y, c = _position()
    me, sibling = (x, y, c), (x, y, 1 - c)
    chips = [(1 - x, y), (x, 1 - y), (1 - x, 1 - y)]

    def slot(a, px, py, pc):
        return out_refs[a].at[4 * px + 2 * py + pc]

    def copy(a, k, block, to, src=None):
        return pltpu.make_async_remote_copy(
            src_ref=slot(a, *block) if src is None else src, dst_ref=slot(a, *block),
            send_sem=send_sems.at[a * N_PEER + k], recv_sem=recv_sems.at[a * N_PEER + k],
            device_id=to, device_id_type=MESH)

    mine = [pltpu.make_async_copy(x_refs[a], slot(a, *me), local_sems.at[a]) for a in range(na)]
    first = []
    for a in range(na):
        first.append(copy(a, 0, me, sibling, src=x_refs[a]))
        first += [copy(a, 1 + j, me, (*chip, c), src=x_refs[a]) for j, chip in enumerate(chips)]
    passed = [copy(a, 4 + j, (*chip, c), sibling) for j, chip in enumerate(chips) for a in range(na)]

    def begin():
        for cp in mine + first:
            cp.start()

    def finish():
        for j, chip in enumerate(chips):
            for a in range(na):
                copy(a, 1 + j, (*chip, c), me).wait_recv()
                passed[j * na + a].start()
        for a in range(na):
            copy(a, 0, sibling, me).wait_recv()
            for j, chip in enumerate(chips):
                copy(a, 4 + j, (*chip, 1 - c), me).wait_recv()
        for cp in first + passed:
            cp.wait_send()
        for cp in mine:
            cp.wait()

    return begin, finish


def _scatter_phases(g_refs, q_refs, send_sems, recv_sems, local_sems):
    na = len(g_refs)
    x, y, c = _position()
    me = 4 * x + 2 * y + c
    mine, sends, recvs = [], [], []
    for a in range(na):
        mine.append(pltpu.make_async_copy(g_refs[a].at[me], q_refs[a].at[me], local_sems.at[a]))
        for k in range(1, N_DEV):
            px, py, pc = (x + (k >> 2)) % 2, (y + ((k >> 1) & 1)) % 2, (c + (k & 1)) % 2
            peer = 4 * px + 2 * py + pc
            sem = a * N_PEER + k - 1
            sends.append(pltpu.make_async_remote_copy(
                src_ref=g_refs[a].at[peer], dst_ref=q_refs[a].at[me], send_sem=send_sems.at[sem],
                recv_sem=recv_sems.at[sem], device_id=(px, py, pc), device_id_type=MESH))
            recvs.append(pltpu.make_async_remote_copy(
                src_ref=g_refs[a].at[me], dst_ref=q_refs[a].at[peer], send_sem=send_sems.at[sem],
                recv_sem=recv_sems.at[sem], device_id=(px, py, pc), device_id_type=MESH))

    def begin():
        for cp in mine + sends:
            cp.start()

    def finish():
        for cp in recvs:
            cp.wait_recv()
        for cp in sends:
            cp.wait_send()
        for cp in mine:
            cp.wait()

    return begin, finish


def _with_rider(body, n_in, n_out, rider, grid):
    if rider is None:
        return body
    phases, arrays, _ = rider
    na = len(arrays)

    def carried(*refs):
        ins, r_in = refs[:n_in], refs[n_in:n_in + na]
        outs = refs[n_in + na:n_in + na + n_out]
        r_out = refs[n_in + na + n_out:n_in + 2 * na + n_out]
        scratch, r_scratch = refs[n_in + 2 * na + n_out:-3], refs[-3:]
        begin, finish = phases(r_in, r_out, *r_scratch)
        first = last = None
        for d, g in enumerate(grid):
            at_start, at_end = pl.program_id(d) == 0, pl.program_id(d) == g - 1
            first = at_start if first is None else first & at_start
            last = at_end if last is None else last & at_end
        pl.when(first)(begin)
        body(*ins, *outs, *scratch)
        pl.when(last)(finish)

    return carried


def _rider_parts(rider):
    if rider is None:
        return [], [], [], [], []
    _, arrays, out_shape = rider
    na = len(arrays)
    return list(arrays), [HBM_SPEC] * na, [HBM_SPEC] * na, list(out_shape), _exchange_scratch(na)


def _gather_rider(blks):
    return (_gather_phases, blks, [_sds((N_DEV,) + b.shape, b.dtype) for b in blks])


def _scatter_rider(gs):
    return (_scatter_phases, gs, [_sds(g.shape, g.dtype) for g in gs])


def _adamw_sum(name, w, m, v, recvs, rider=None):
    L, r, c = w.shape
    tr = r if r <= 512 else max(t for t in range(16, 257, 16) if r % t == 0)
    assert len(recvs) == L
    nr = r // tr

    def body(w_ref, m_ref, v_ref, *refs):
        q_refs = refs[:L]
        g_ref, d_ref, m2_ref, v2_ref = refs[L:]
        l = pl.program_id(0)
        for ll in range(L):
            @pl.when(l == ll)
            def _(q_ref=q_refs[ll]):
                gv = q_ref[0].astype(F32)
                for s in range(1, N_DEV):
                    gv = gv + q_ref[s].astype(F32)
                m2 = ADAM_B1 * m_ref[0] + (1.0 - ADAM_B1) * gv
                v2 = ADAM_B2 * v_ref[0] + (1.0 - ADAM_B2) * jnp.square(gv)
                m_hat = m2 / (1.0 - ADAM_B1 ** ADAM_STEP)
                v_hat = v2 / (1.0 - ADAM_B2 ** ADAM_STEP)
                g_ref[0] = gv
                d_ref[0] = -ADAM_LR * (m_hat / (jnp.sqrt(v_hat) + ADAM_EPS) + ADAM_WD * w_ref[0])
                m2_ref[0] = m2
                v2_ref[0] = v2

    def q_index(ll):
        return lambda l, i: (0, jnp.where(l == ll, i, jnp.where(l > ll, nr - 1, 0)), 0)

    spec = pl.BlockSpec((1, tr, c), lambda l, i: (l, i, 0))
    r_args, r_in, r_out, r_shape, r_scratch = _rider_parts(rider)
    return pl.pallas_call(
        _with_rider(body, 3 + L, 4, rider, (L, nr)), grid=(L, nr),
        in_specs=[spec] * 3 + [pl.BlockSpec((N_DEV, tr, c), q_index(ll)) for ll in range(L)] + r_in,
        out_specs=[spec] * 4 + r_out, out_shape=[_sds((L, r, c), F32)] * 4 + r_shape, scratch_shapes=r_scratch,
        compiler_params=_cparams(("arbitrary", "arbitrary")), name=name,
    )(w, m, v, *recvs, *r_args)


def _sum_blocks(name, q, tr):
    R = q.shape[0] // N_DEV
    W = q.shape[1]
    tr = min(tr, R)
    assert R % tr == 0
    nb = R // tr

    def body(*refs):
        acc = refs[0][...].astype(F32)
        for r in refs[1:N_DEV]:
            acc = acc + r[...].astype(F32)
        refs[N_DEV][...] = acc

    return pl.pallas_call(
        body, grid=(nb,),
        in_specs=[pl.BlockSpec((tr, W), lambda i, s=s: (s * nb + i, 0)) for s in range(N_DEV)],
        out_specs=pl.BlockSpec((tr, W), lambda i: (i, 0)), out_shape=_sds((R, W), F32),
        compiler_params=_cparams(("parallel",)), name=name,
    )(*([q] * N_DEV))


def _adamw(name, w, g, m, v):
    R, C = w.shape
    tr = R
    if R > 512:
        tr = max(t for t in range(8, 513, 8) if R % t == 0)

    def body(w_ref, g_ref, m_ref, v_ref, d_ref, m2_ref, v2_ref):
        gv = g_ref[...]
        m2 = ADAM_B1 * m_ref[...] + (1.0 - ADAM_B1) * gv
        v2 = ADAM_B2 * v_ref[...] + (1.0 - ADAM_B2) * jnp.square(gv)
        m_hat = m2 / (1.0 - ADAM_B1 ** ADAM_STEP)
        v_hat = v2 / (1.0 - ADAM_B2 ** ADAM_STEP)
        d_ref[...] = -ADAM_LR * (m_hat / (jnp.sqrt(v_hat) + ADAM_EPS) + ADAM_WD * w_ref[...])
        m2_ref[...] = m2
        v2_ref[...] = v2

    spec = pl.BlockSpec((tr, C), lambda i: (i, 0))
    return pl.pallas_call(
        body, grid=(R // tr,), in_specs=[spec] * 4, out_specs=[spec] * 3,
        out_shape=[_sds((R, C), F32)] * 3, compiler_params=_cparams(("parallel",)), name=name,
    )(w, g, m, v)


BIG = ("w_in", "w_branch", "w_out", "w_ffn_in", "w_ffn_out")
BIG_ROW_SHARDED = {"w_in": False, "w_branch": True, "w_out": True, "w_ffn_in": True, "w_ffn_out": True}
TRANSPOSED = ("w_ffn_in",)
SMALL = ("norm1_g", "b_gate", "lru_conv_w", "lru_conv_b", "lru_w_a", "lru_b_a", "lru_w_x", "lru_b_x", "lru_lambda",
         "ssd_conv_w", "ssd_conv_b", "ssd_dt_bias", "ssd_A_log", "ssd_D", "ssd_norm_g", "norm2_g", "norm_f")
SMALL_ROWS = 256


def _pack(arrs, dtype, row_mult):
    parts = []
    for a in arrs:
        f = a.reshape(-1).astype(dtype)
        pad = (-f.shape[0]) % LANES
        if pad:
            f = jnp.concatenate([f, jnp.zeros((pad,), dtype)])
        parts.append(f)
    f = jnp.concatenate(parts)
    pad = (-f.shape[0]) % (LANES * row_mult)
    if pad:
        f = jnp.concatenate([f, jnp.zeros((pad,), dtype)])
    return f.reshape(-1, LANES)


def _unpack(flat, shapes, lead=()):
    f = flat.reshape(lead + (-1,))
    out, off = [], 0
    for s in shapes:
        n = int(np.prod(s))
        out.append(f[..., off:off + n].reshape(lead + tuple(s)))
        off += n + (-n) % LANES
    return out


def _full_from_shards(name, st):
    if BIG_ROW_SHARDED[name]:
        return st.reshape((-1,) + st.shape[2:])
    return jnp.transpose(st, (1, 0, 2)).reshape(st.shape[1], -1)


def _shards_from_full(name, full):
    if BIG_ROW_SHARDED[name]:
        return full.reshape((N_DEV, full.shape[0] // N_DEV) + full.shape[1:])
    return jnp.transpose(full.reshape(full.shape[0], N_DEV, -1), (1, 0, 2))


def _block_diag_tiles(w):
    z = jnp.zeros((8, 64, 64), w.dtype)
    w2 = w.reshape(8, 2, 64, 64)
    top = jnp.concatenate([w2[:, 0], z], axis=2)
    bot = jnp.concatenate([z, w2[:, 1]], axis=2)
    return jnp.concatenate([top, bot], axis=1)


def _block_diag_untile(t):
    return jnp.stack([t[:, :64, :64], t[:, 64:, 64:]], axis=1).reshape(16, 64, 64)


def _pad_lanes(a, width):
    return jnp.concatenate([a, jnp.zeros(a.shape[:-1] + (width - a.shape[-1],), a.dtype)], axis=-1)


def _pad_rows8(w):
    return jnp.concatenate([w, jnp.zeros((8 - w.shape[0],) + w.shape[1:], w.dtype)], axis=0)


REST = BIG[1:]


def _in_weights(shards):
    w_in = _full_from_shards("w_in", shards)
    seg = [w_in[:, IN_OFFS[k]:IN_OFFS[k + 1]] for k in range(6)]
    return {"w_lx": seg[0], "w_lg": seg[1], "w_z": seg[2], "w_xbc": seg[3], "w_dt": _pad_lanes(seg[4], DT_PAD),
            "w_g": seg[5]}


def _rest_weights(shards):
    full = {n: _full_from_shards(n, st) for n, st in zip(REST, shards)}
    return {"w_bra": full["w_branch"][:D_MODEL], "w_brb": full["w_branch"][D_MODEL:], "w_out": full["w_out"],
            "w_ffn_in_t": full["w_ffn_in"], "w_ffn_out": full["w_ffn_out"]}


def _small_params(l, small):
    p = {}
    for n in ("norm1_g", "b_gate", "lru_conv_b", "lru_b_a", "lru_b_x", "lru_lambda", "ssd_conv_b", "ssd_norm_g", "norm2_g"):
        p[n] = small[n][l].reshape(1, -1)
    p["lru_conv_w8"] = _pad_rows8(small["lru_conv_w"][l])
    p["ssd_conv_w8"] = _pad_rows8(small["ssd_conv_w"][l])
    p["wa"] = _block_diag_tiles(small["lru_w_a"][l])
    p["wx"] = _block_diag_tiles(small["lru_w_x"][l])
    p["dtb"] = _pad_lanes(small["ssd_dt_bias"][l].reshape(1, -1), DT_PAD)
    p["alog"] = _pad_lanes(small["ssd_A_log"][l].reshape(1, -1), DT_PAD)
    p["d_e"] = jnp.repeat(small["ssd_D"][l], SSD_HEAD_DIM).reshape(1, -1)
    return p


def _layer_fwd(l, h, p, riders):
    n = f"l{l}_"
    s = {"h_in": h}
    got = {}
    (xn,) = _rowwise_fwd(n + "rms1", _f_rms, [h], [p["norm1_g"]], [(D_MODEL, BF16)], 512)
    s["xn"] = xn
    for k in ("lx", "lg", "z", "xbc", "g", "dt"):
        s[k] = _mm(n + "in_" + k, [(xn, p["w_" + k])], "nn", out_dtype=F32 if k == "dt" else BF16)
    s["u"] = _conv_fwd(n + "lru_conv", s["lx"], p["lru_conv_w8"], p["lru_conv_b"])
    lru_pars = [p["wa"], p["lru_b_a"], p["wx"], p["lru_b_x"], p["lru_lambda"]]
    s["a"], b = _rowwise_fwd(n + "lru_gates", _f_lru_gates, [s["u"]], lru_pars, [(D_MODEL, F32)] * 2, 512)
    s["hl"], s["hprev"] = _lru_scan_fwd(s["a"], b)
    s["xpre"] = _conv_fwd(n + "ssd_conv", s["xbc"], p["ssd_conv_w8"], p["ssd_conv_b"])
    s["yssd"], s["states"], *got["ssd"] = _ssd_fwd(n + "ssd", s["xpre"], s["dt"], p["dtb"], p["alog"],
                                                   riders.get("ssd"))
    if "ssd" in riders:
        p.update(_rest_weights(got["ssd"]))
    post_rows = [s["hl"], s["lg"], s["yssd"], (s["xpre"], 0, SSD_INNER), s["z"]]
    s["ya"], s["yb"], *got["post"] = _rowwise_fwd(n + "post", _f_post, post_rows, [p["d_e"], p["ssd_norm_g"]],
                                                  [(D_MODEL, BF16), (SSD_INNER, BF16)], 128, riders.get("post"))
    s["ma"] = _mm(n + "br_a", [(s["ya"], p["w_bra"])], "nn", out_dtype=BF16)
    s["mb"] = _mm(n + "br_b", [(s["yb"], p["w_brb"])], "nn", out_dtype=BF16)
    (s["merged"],) = _rowwise_fwd(n + "merge", _f_merge, [s["ma"], s["mb"], s["g"]], [p["b_gate"]],
                                  [(D_MODEL, BF16)], 512)
    s["h_mid"] = _mm(n + "out", [(s["merged"], p["w_out"])], "nn", res=h)
    (s["xn2"],) = _rowwise_fwd(n + "rms2", _f_rms, [s["h_mid"]], [p["norm2_g"]], [(D_MODEL, BF16)], 512)
    s["gu"] = _mm(n + "ffn_in", [(s["xn2"], p["w_ffn_in_t"])], "nt", out_dtype=BF16, rider=riders.get("ffn_in"))
    if "ffn_in" in riders:
        s["gu"], got["ffn_in"] = s["gu"]
    (s["act"],) = _rowwise_fwd(n + "act", _f_act, [s["gu"]], [], [(D_FF, BF16)], 512)
    h_out = _mm(n + "ffn_out", [(s["act"], p["w_ffn_out"])], "nn", res=s["h_mid"], rider=riders.get("ffn_out"))
    if "ffn_out" in riders:
        h_out, got["ffn_out"] = h_out
    return h_out, s, got


def _f_rms_res(x, g):
    return _f_rms(x, g)[0], x


def _layer_bwd(l, dh, dh_b, s, p, above, first_layer):
    n = f"l{l}_b_"
    gw, gs = {}, {}
    d_act = _mm(n + "d_act", [(dh_b, p["w_ffn_out"])], "nt", out_dtype=BF16)
    gw["w_ffn_out"] = _mm_tn(n + "dw_ffn_out", s["act"], dh_b)
    (d_gu,) = _rowwise_bwd(n + "act", _f_act, [s["gu"]], [], [d_act], _ident, [BF16], 256)
    d_xn2 = _mm(n + "d_xn2", [(d_gu, p["w_ffn_in_t"])], "nn")
    gw["w_ffn_in"] = _mm_tn(n + "dw_ffn_in", d_gu, s["xn2"])
    dh_mid, dh_mid_b, gs["norm2_g"] = _rowwise_bwd(n + "rms2", _f_rms_res, [s["h_mid"]], [p["norm2_g"]], [d_xn2, dh],
                                                   _ident, [(F32, BF16)], 512)
    d_merged = _mm(n + "d_merged", [(dh_mid_b, p["w_out"])], "nt", out_dtype=BF16)
    gw["w_out"] = _mm_tn(n + "dw_out", s["merged"], dh_mid_b)
    d_ma, d_mb, d_g, gs["b_gate"] = _rowwise_bwd(n + "merge", _f_merge, [s["ma"], s["mb"], s["g"]], [p["b_gate"]],
                                                 [d_merged], _ident, [BF16, BF16, BF16], 512)
    d_ya = _mm(n + "d_ya", [(d_ma, p["w_bra"])], "nt", out_dtype=BF16)
    d_yb = _mm(n + "d_yb", [(d_mb, p["w_brb"])], "nt", out_dtype=BF16)
    gw["w_branch"] = jnp.concatenate([_mm_tn(n + "dw_bra", s["ya"], d_ma), _mm_tn(n + "dw_brb", s["yb"], d_mb)], axis=0)
    post_rows = [s["hl"], s["lg"], s["yssd"], (s["xpre"], 0, SSD_INNER), s["z"]]
    d_hl, d_lg, d_yssd, d_xs, d_z, d_de, gs["ssd_norm_g"] = _rowwise_bwd(
        n + "post", _f_post, post_rows, [p["d_e"], p["ssd_norm_g"]], [d_ya, d_yb], _ident,
        [F32, BF16, F32, F32, BF16], 128)
    gs["ssd_D"] = d_de.reshape(SSD_HEADS, SSD_HEAD_DIM).sum(axis=1)
    contrib = [_shards_from_full(k, gw[k]) for k in REST] + ([] if above is None else [above])
    d_xpre, d_dt, d_dtb, d_alog, *carried = _ssd_bwd(n + "ssd", s["xpre"], s["dt"], s["states"], d_yssd, d_xs,
                                                     p["dtb"], p["alog"], _scatter_rider(contrib))
    arrived = dict(zip(REST, carried))
    arrived_above = None if above is None else carried[len(REST)]
    gs["ssd_dt_bias"] = d_dtb[0, :SSD_HEADS]
    gs["ssd_A_log"] = d_alog[0, :SSD_HEADS]
    d_xbc, dwb = _conv_bwd(n + "ssd_conv", s["xbc"], d_xpre, p["ssd_conv_w8"])
    gs["ssd_conv_w"], gs["ssd_conv_b"] = dwb[:4], dwb[4]
    g_scan = _lru_scan_bwd(s["a"], d_hl)
    lru_pars = [p["wa"], p["lru_b_a"], p["wx"], p["lru_b_x"], p["lru_lambda"]]
    d_u, d_wa, gs["lru_b_a"], d_wx, gs["lru_b_x"], gs["lru_lambda"] = _rowwise_bwd(
        n + "lru_gates", _f_lru_gates, [s["u"]], lru_pars, [g_scan, s["hprev"]],
        lambda g, hp: (g * hp, g), [F32], 256)
    gs["lru_w_a"], gs["lru_w_x"] = _block_diag_untile(d_wa), _block_diag_untile(d_wx)
    d_lx, dwb = _conv_bwd(n + "lru_conv", s["lx"], d_u, p["lru_conv_w8"])
    gs["lru_conv_w"], gs["lru_conv_b"] = dwb[:4], dwb[4]
    segs = [("lx", d_lx), ("lg", d_lg), ("z", d_z), ("xbc", d_xbc), ("dt", d_dt), ("g", d_g)]
    dws = [_mm_tn(n + "dw_in_" + k, s["xn"], d) for k, d in segs]
    dws[4] = dws[4][:, :IN_WIDTHS[4]]
    below = _shards_from_full("w_in", jnp.concatenate(dws, axis=1))
    pairs_a = [(d, p["w_" + k]) for k, d in segs[:3]]
    pairs_b = [(d, p["w_" + k]) for k, d in segs[3:]]
    if first_layer:
        half = below.shape[1] // 2
        d_xn, (top,) = _mm(n + "d_xn_a", pairs_a, "nt", rider=_scatter_rider([below[:, :half]]))
        d_xn, (bottom,) = _mm(n + "d_xn_b", pairs_b, "nt", res=d_xn, rider=_scatter_rider([below[:, half:]]))
        arrived["w_in"] = jnp.concatenate([top, bottom], axis=1)
        below = None
    else:
        d_xn = _mm(n + "d_xn_a", pairs_a, "nt")
        d_xn = _mm(n + "d_xn_b", pairs_b, "nt", res=d_xn)
    dh_in, dh_in_b, gs["norm1_g"] = _rowwise_bwd(n + "rms1", _f_rms_res, [s["h_in"]], [p["norm1_g"]], [d_xn, dh_mid],
                                                 _ident, [(F32, BF16)], 512)
    for k in ("norm1_g", "norm2_g", "b_gate", "ssd_norm_g", "lru_b_a", "lru_b_x", "lru_lambda"):
        gs[k] = gs[k].reshape(-1)
    return dh_in, dh_in_b, gs, arrived, arrived_above, below, gw


def _step(inp):
    x = inp["x"][0]
    target = inp["loss_target"][0]
    dev = 4 * lax.axis_index("x") + 2 * lax.axis_index("y") + lax.axis_index("c")

    def mine(l, names):
        return [(inp[n][l].T if n in TRANSPOSED else inp[n][l]).astype(BF16) for n in names]

    first = _all_gather("gather_first", mine(0, ("w_in",)) + [inp["lru_conv_w"], inp["ssd_conv_w"]])
    small = {n: inp[n] for n in SMALL}
    small["lru_conv_w"] = jnp.moveaxis(first[1], 0, 2).reshape(DEPTH, 4, -1)
    small["ssd_conv_w"] = jnp.moveaxis(first[2], 0, 2).reshape(DEPTH, 4, -1)
    shards = {"w_in": first[0]}

    h, saved, params = x, [], []
    for l in range(DEPTH):
        p = _small_params(l, small)
        p.update(_in_weights(shards["w_in"]))
        riders = {}
        if l == 0:
            riders["ssd"] = _gather_rider(mine(0, REST))
        else:
            p.update(_rest_weights([shards[n] for n in REST]))
        if l + 1 < DEPTH:
            riders["post"] = _gather_rider(mine(l + 1, ("w_ffn_in", "w_ffn_out")))
            riders["ffn_in"] = _gather_rider(mine(l + 1, ("w_in",)))
            riders["ffn_out"] = _gather_rider(mine(l + 1, ("w_branch", "w_out")))
        h, s, got = _layer_fwd(l, h, p, riders)
        if l + 1 < DEPTH:
            shards = {"w_ffn_in": got["post"][0], "w_ffn_out": got["post"][1], "w_in": got["ffn_in"][0],
                      "w_branch": got["ffn_out"][0], "w_out": got["ffn_out"][1]}
        saved.append(s)
        params.append(p)
    dh, d_nf, loss_acc, dh_b = _loss_head(h, target, small["norm_f"].reshape(1, -1))
    loss = lax.psum(loss_acc[0, 0], ("x", "y", "c"))

    gss, received = [None] * DEPTH, [None] * DEPTH
    handed_down = None
    for l in reversed(range(DEPTH)):
        dh, dh_b, gss[l], received[l], arrived_above, handed_down, _ = _layer_bwd(
            l, dh, dh_b, saved[l], params[l], handed_down, l == 0)
        if arrived_above is not None:
            received[l + 1]["w_in"] = arrived_above
    grad_x = dh[None]
    out = {"loss": loss, "grad_x": grad_x}
    small_full = {n: jnp.stack([gss[l][n] for l in range(DEPTH)]) for n in SMALL if n != "norm_f"}
    small_full["norm_f"] = d_nf.reshape(-1)
    part = _pack([small_full[n] for n in SMALL], F32, SMALL_ROWS)
    for n in BIG:
        flip = (lambda a: jnp.transpose(a, (0, 2, 1))) if n in TRANSPOSED else (lambda a: a)
        res = _adamw_sum("adamw_" + n, flip(inp[n]), flip(inp["m_" + n]), flip(inp["v_" + n]),
                         [received[l][n] for l in range(DEPTH)], _gather_rider([part]) if n == "w_in" else None)
        if n == "w_in":
            everyone = res[4]
        out["grad_" + n], out["delta_" + n], out["new_m_" + n], out["new_v_" + n] = [flip(a) for a in res[:4]]
    g_small_flat = _sum_blocks("sum_small_grads", everyone.reshape(-1, LANES), SMALL_ROWS)
    g_small = dict(zip(SMALL, _unpack(g_small_flat, [small_full[n].shape for n in SMALL])))
    for n in ("lru_conv_w", "ssd_conv_w"):
        w = inp[n].shape[-1]
        g_small[n] = lax.dynamic_slice_in_dim(g_small[n], dev * w, w, axis=2)

    shapes = [inp[n].shape for n in SMALL]
    packs = [_pack([src[pre + n] for n in SMALL], F32, SMALL_ROWS)
             for src, pre in ((inp, ""), (g_small, ""), (inp, "m_"), (inp, "v_"))]
    d, m2, v2 = _adamw("adamw_small", *packs)
    for n, dd, mm, vv in zip(SMALL, _unpack(d, shapes), _unpack(m2, shapes), _unpack(v2, shapes)):
        out["grad_" + n] = g_small[n]
        out["delta_" + n], out["new_m_" + n], out["new_v_" + n] = dd, mm, vv
    return out


WEIGHTS = ("norm1_g", "w_in", "b_gate", "lru_conv_w", "lru_conv_b", "lru_w_a", "lru_b_a", "lru_w_x", "lru_b_x",
           "lru_lambda", "ssd_conv_w", "ssd_conv_b", "ssd_dt_bias", "ssd_A_log", "ssd_D", "ssd_norm_g", "w_branch",
           "w_out", "norm2_g", "w_ffn_in", "w_ffn_out", "norm_f")


def kernel(x, norm1_g, w_in, b_gate, lru_conv_w, lru_conv_b, lru_w_a, lru_b_a, lru_w_x, lru_b_x, lru_lambda, ssd_conv_w, ssd_conv_b, ssd_dt_bias, ssd_A_log, ssd_D, ssd_norm_g, w_branch, w_out, norm2_g, w_ffn_in, w_ffn_out, norm_f, loss_target, m_norm1_g, m_w_in, m_b_gate, m_lru_conv_w, m_lru_conv_b, m_lru_w_a, m_lru_b_a, m_lru_w_x, m_lru_b_x, m_lru_lambda, m_ssd_conv_w, m_ssd_conv_b, m_ssd_dt_bias, m_ssd_A_log, m_ssd_D, m_ssd_norm_g, m_w_branch, m_w_out, m_norm2_g, m_w_ffn_in, m_w_ffn_out, m_norm_f, v_norm1_g, v_w_in, v_b_gate, v_lru_conv_w, v_lru_conv_b, v_lru_w_a, v_lru_b_a, v_lru_w_x, v_lru_b_x, v_lru_lambda, v_ssd_conv_w, v_ssd_conv_b, v_ssd_dt_bias, v_ssd_A_log, v_ssd_D, v_ssd_norm_g, v_w_branch, v_w_out, v_norm2_g, v_w_ffn_in, v_w_ffn_out, v_norm_f):
    out = _step(dict(locals()))
    res = [out["loss"], out["grad_x"]]
    for pre in ("grad_", "delta_", "new_m_", "new_v_"):
        res += [out[pre + n] for n in WEIGHTS]
    return tuple(res)
```

```python
import functools

import numpy as np
import jax
import jax.numpy as jnp
from jax import lax
from jax.experimental import pallas as pl
from jax.experimental.pallas import tpu as pltpu

F32 = jnp.float32
BF16 = jnp.bfloat16
HIGHEST = lax.Precision.HIGHEST

D_MODEL = 1024
DEPTH = 2
CHUNK = 64
LRU_C = 8.0
SSD_INNER = 2048
SSD_HEADS = 32
SSD_HEAD_DIM = 64
SSD_GROUPS = 4
SSD_STATE = 128
SSD_CONV_DIM = 3072
D_FF = 2816
EPS = 1e-6
N_DEV = 8
LANES = 128
DT_PAD = LANES
IN_WIDTHS = (1024, 1024, 2048, 3072, 32, 2048)
IN_OFFS = tuple(int(v) for v in np.cumsum((0,) + IN_WIDTHS))

ADAM_LR = 0.001
ADAM_B1 = 0.9
ADAM_B2 = 0.999
ADAM_EPS = 1e-08
ADAM_WD = 0.01
ADAM_STEP = 10

VMEM_LIMIT = 56 * 1024 * 1024
MESH = pl.DeviceIdType.MESH


def _cparams(sem=None):
    return pltpu.CompilerParams(dimension_semantics=sem, vmem_limit_bytes=VMEM_LIMIT)


def _sds(shape, dtype):
    return jax.ShapeDtypeStruct(tuple(shape), dtype)


def _pick(n, cap):
    if n <= cap:
        return n
    best = LANES
    for t in range(LANES, cap + 1, LANES):
        if n % t == 0:
            best = t
    assert n % best == 0, (n, cap)
    return best


def _mm(name, pairs, mode, out_dtype=F32, res=None, rider=None):
    M = pairs[0][0].shape[0]
    N = pairs[0][1].shape[1] if mode == "nn" else pairs[0][1].shape[0]
    npair = len(pairs)
    tm = min(M, 1024 if npair <= 3 else 512)
    tn = _pick(N, 1536 if npair == 1 else 1024)
    tks, nks, starts = [], [], []
    s = 0
    for a, _ in pairs:
        k = a.shape[1]
        tk = _pick(k, 1536)
        tks.append(tk)
        nks.append(k // tk)
        starts.append(s)
        s += k // tk
    nk = s
    dims = (((1,), (0,)), ((), ())) if mode == "nn" else (((1,), (1,)), ((), ()))

    def body(*refs):
        ab = refs[:2 * npair]
        pos = 2 * npair
        r_ref = None
        if res is not None:
            r_ref = refs[pos]
            pos += 1
        o_ref = refs[pos]
        acc = refs[pos + 1] if nk > 1 else None
        k = pl.program_id(2)

        def finish(r):
            if r_ref is not None:
                r = r + r_ref[...].astype(F32)
            o_ref[...] = r.astype(o_ref.dtype)

        for p in range(npair):
            a_ref, b_ref = ab[2 * p], ab[2 * p + 1]
            lo, hi = starts[p], starts[p] + nks[p]

            def step(a_ref=a_ref, b_ref=b_ref, lo=lo, hi=hi):
                d = lax.dot_general(a_ref[...].astype(BF16), b_ref[...].astype(BF16), dims,
                                    preferred_element_type=F32)
                if nk == 1:
                    finish(d)
                    return
                if lo == 0:
                    @pl.when(k == 0)
                    def _():
                        acc[...] = d
                if hi == nk:
                    @pl.when(k == nk - 1)
                    def _():
                        finish(acc[...] + d)
                if max(lo, 1) < min(hi, nk - 1):
                    @pl.when((k > 0) & (k < nk - 1))
                    def _():
                        acc[...] += d

            if npair == 1:
                step()
            else:
                pl.when((k >= lo) & (k < hi))(step)

    in_specs, args = [], []
    for p, (a, b) in enumerate(pairs):
        def kk(k, p=p):
            return jnp.clip(k - starts[p], 0, nks[p] - 1)
        in_specs.append(pl.BlockSpec((tm, tks[p]), lambda i, j, k, kk=kk: (i, kk(k))))
        if mode == "nn":
            in_specs.append(pl.BlockSpec((tks[p], tn), lambda i, j, k, kk=kk: (kk(k), j)))
        else:
            in_specs.append(pl.BlockSpec((tn, tks[p]), lambda i, j, k, kk=kk: (j, kk(k))))
        args += [a, b]
    if res is not None:
        in_specs.append(pl.BlockSpec((tm, tn), lambda i, j, k: (i, j)))
        args.append(res)
    r_args, r_in, r_out, r_shape, r_scratch = _rider_parts(rider)
    grid = (M // tm, N // tn, nk)
    out = pl.pallas_call(
        _with_rider(body, len(args), 1, rider, grid), grid=grid, in_specs=in_specs + r_in,
        out_specs=[pl.BlockSpec((tm, tn), lambda i, j, k: (i, j))] + r_out,
        out_shape=[_sds((M, N), out_dtype)] + r_shape,
        scratch_shapes=([pltpu.VMEM((tm, tn), F32)] if nk > 1 else []) + r_scratch,
        compiler_params=_cparams(("parallel", "parallel", "arbitrary") if rider is None else ("arbitrary",) * 3),
        name=name,
    )(*args, *r_args)
    return out[0] if rider is None else (out[0], out[1:])


def _swiglu_out(name, gu, w, res, rider=None):
    M, F2 = gu.shape
    F = F2 // 2
    N = w.shape[1]
    tm = min(M, 512)
    tk = _pick(F, 1536)
    nk = F // tk
    assert nk > 1

    def body(g_ref, u_ref, w_ref, r_ref, o_ref, act_ref, acc):
        k = pl.program_id(2)
        g = g_ref[...].astype(F32)
        act = (jax.nn.silu(g) * u_ref[...].astype(F32)).astype(BF16)
        act_ref[...] = act
        d = jnp.dot(act, w_ref[...], preferred_element_type=F32)

        @pl.when(k == 0)
        def _():
            acc[...] = d

        @pl.when((k > 0) & (k < nk - 1))
        def _():
            acc[...] += d

        @pl.when(k == nk - 1)
        def _():
            o_ref[...] = acc[...] + d + r_ref[...]

    r_args, r_in, r_out, r_shape, r_scratch = _rider_parts(rider)
    grid = (M // tm, 1, nk)
    out = pl.pallas_call(
        _with_rider(body, 4, 2, rider, grid), grid=grid,
        in_specs=[pl.BlockSpec((tm, tk), lambda i, j, k: (i, k)), pl.BlockSpec((tm, tk), lambda i, j, k: (i, k + nk)),
                  pl.BlockSpec((tk, N), lambda i, j, k: (k, 0)), pl.BlockSpec((tm, N), lambda i, j, k: (i, 0))] + r_in,
        out_specs=[pl.BlockSpec((tm, N), lambda i, j, k: (i, 0)), pl.BlockSpec((tm, tk), lambda i, j, k: (i, k))] + r_out,
        out_shape=[_sds((M, N), F32), _sds((M, F), BF16)] + r_shape,
        scratch_shapes=[pltpu.VMEM((tm, N), F32)] + r_scratch,
        compiler_params=_cparams(("arbitrary",) * 3), name=name,
    )(gu, gu, w, res, *r_args)
    return out[0], out[1], out[2:]


def _mm_tn(name, a, b, out_dtype=BF16):
    M, Ka = a.shape
    N = b.shape[1]
    tm = min(M, 2048)
    tka = _pick(Ka, 1024)
    tn = _pick(N, 1024)
    nm = M // tm

    def body(a_ref, b_ref, o_ref, *scratch):
        k = pl.program_id(2)
        d = lax.dot_general(a_ref[...].astype(BF16), b_ref[...].astype(BF16),
                            (((0,), (0,)), ((), ())), preferred_element_type=F32)
        if nm == 1:
            o_ref[...] = d.astype(o_ref.dtype)
            return
        acc = scratch[0]

        @pl.when(k == 0)
        def _():
            acc[...] = d

        @pl.when((k > 0) & (k < nm - 1))
        def _():
            acc[...] += d

        @pl.when(k == nm - 1)
        def _():
            o_ref[...] = (acc[...] + d).astype(o_ref.dtype)

    return pl.pallas_call(
        body, grid=(Ka // tka, N // tn, nm),
        in_specs=[pl.BlockSpec((tm, tka), lambda i, j, k: (k, i)),
                  pl.BlockSpec((tm, tn), lambda i, j, k: (k, j))],
        out_specs=pl.BlockSpec((tka, tn), lambda i, j, k: (i, j)),
        out_shape=_sds((Ka, N), out_dtype),
        scratch_shapes=[pltpu.VMEM((tka, tn), F32)] if nm > 1 else [],
        compiler_params=_cparams(("parallel", "parallel", "arbitrary")), name=name,
    )(a, b)


def _rowwise_fwd(name, fn, rows, pars, outs, tb, rider=None):
    rows = [r if isinstance(r, tuple) else (r, 0, r.shape[1]) for r in rows]
    T = rows[0][0].shape[0]
    tb = min(tb, T)
    nr, npar = len(rows), len(pars)
    r_args, r_in, r_out, r_shape, r_scratch = _rider_parts(rider)

    def body(*refs):
        rv = [r[...].astype(F32) for r in refs[:nr]]
        pv = [p[...] for p in refs[nr:nr + npar]]
        res = fn(*rv, *pv)
        for o, r in zip(refs[nr + npar:], res):
            o[...] = r.astype(o.dtype)

    in_specs = [pl.BlockSpec((tb, w), lambda i, c=c: (i, c)) for _, c, w in rows]
    in_specs += [pl.BlockSpec(p.shape, lambda i, n=p.ndim: (0,) * n) for p in pars]
    return pl.pallas_call(
        _with_rider(body, nr + npar, len(outs), rider, (T // tb,)), grid=(T // tb,), in_specs=in_specs + r_in,
        out_specs=[pl.BlockSpec((tb, w), lambda i: (i, 0)) for w, _ in outs] + r_out,
        out_shape=[_sds((T, w), dt) for w, dt in outs] + r_shape, scratch_shapes=r_scratch,
        compiler_params=_cparams(("parallel",) if rider is None else ("arbitrary",)), name=name,
    )(*[r[0] for r in rows], *pars, *r_args)


def _rowwise_bwd(name, fn, rows, pars, cot_rows, cot_fn, row_out, tb):
    rows = [r if isinstance(r, tuple) else (r, 0, r.shape[1]) for r in rows]
    cot_rows = [r if isinstance(r, tuple) else (r, 0, r.shape[1]) for r in cot_rows]
    T = rows[0][0].shape[0]
    tb = min(tb, T)
    nr, npar, nc = len(rows), len(pars), len(cot_rows)
    want, want_dt = [], []
    for k, dt in enumerate(row_out):
        for d in (() if dt is None else dt if isinstance(dt, tuple) else (dt,)):
            want.append(k)
            want_dt.append(d)

    def body(*refs):
        i = pl.program_id(0)
        rv = [r[...].astype(F32) for r in refs[:nr]]
        pv = [p[...] for p in refs[nr:nr + npar]]
        cv = [c[...].astype(F32) for c in refs[nr + npar:nr + npar + nc]]
        o_refs = refs[nr + npar + nc:]
        _, vjp = jax.vjp(fn, *rv, *pv)
        grads = vjp(tuple(cot_fn(*cv)))
        for o, k in zip(o_refs[:len(want)], want):
            o[...] = grads[k].astype(o.dtype)
        p_refs = o_refs[len(want):]

        @pl.when(i == 0)
        def _():
            for o in p_refs:
                o[...] = jnp.zeros_like(o)

        for o, g in zip(p_refs, grads[nr:]):
            o[...] += g

    in_specs = [pl.BlockSpec((tb, w), lambda i, c=c: (i, c)) for _, c, w in rows]
    in_specs += [pl.BlockSpec(p.shape, lambda i, n=p.ndim: (0,) * n) for p in pars]
    in_specs += [pl.BlockSpec((tb, w), lambda i, c=c: (i, c)) for _, c, w in cot_rows]
    out_specs = [pl.BlockSpec((tb, rows[k][2]), lambda i: (i, 0)) for k in want]
    out_specs += [pl.BlockSpec(p.shape, lambda i, n=p.ndim: (0,) * n) for p in pars]
    out_shape = [_sds((T, rows[k][2]), d) for k, d in zip(want, want_dt)] + [_sds(p.shape, F32) for p in pars]
    return pl.pallas_call(
        body, grid=(T // tb,), in_specs=in_specs, out_specs=out_specs, out_shape=out_shape,
        compiler_params=_cparams(("arbitrary",)), name=name,
    )(*[r[0] for r in rows], *pars, *[r[0] for r in cot_rows])


def _f_rms(x, g):
    r = lax.rsqrt(jnp.mean(x * x, axis=-1, keepdims=True) + EPS)
    return (x * r * g,)


def _f_lru_gates(u, wa, ba, wx, bx, lam):
    ub = u.astype(BF16)
    ra, rx = [], []
    for k in range(D_MODEL // LANES):
        uk = ub[:, LANES * k:LANES * (k + 1)]
        ra.append(jnp.dot(uk, wa[k].astype(BF16), preferred_element_type=F32))
        rx.append(jnp.dot(uk, wx[k].astype(BF16), preferred_element_type=F32))
    r = jax.nn.sigmoid(jnp.concatenate(ra, axis=1) + ba)
    i = jax.nn.sigmoid(jnp.concatenate(rx, axis=1) + bx)
    log_a = -LRU_C * r * jax.nn.softplus(-lam)
    a = jnp.exp(log_a)
    t = jnp.tanh(log_a)
    b = jnp.sqrt(-2.0 * t / (1.0 - t)) * (i * u)
    return a, b


def _f_post(hl, lgate, yssd, xpre_s, z, d_e, ng):
    ya = jax.nn.gelu(lgate) * hl
    y = (yssd + d_e * jax.nn.silu(xpre_s)) * jax.nn.silu(z)
    gw = SSD_INNER // SSD_GROUPS
    parts = []
    for g in range(SSD_GROUPS):
        yg = y[:, gw * g:gw * (g + 1)]
        parts.append(yg * lax.rsqrt(jnp.mean(yg * yg, axis=-1, keepdims=True) + EPS))
    return ya, jnp.concatenate(parts, axis=1) * ng


def _f_merge(m_a, m_b, gates, bg):
    g = jax.nn.sigmoid(gates + bg)
    return (g[:, :D_MODEL] * m_a + g[:, D_MODEL:] * m_b,)


def _f_act(gu):
    return (jax.nn.silu(gu[:, :D_FF]) * gu[:, D_FF:],)


def _ident(*c):
    return c


def _loss_head(h, target, nf):
    T = h.shape[0]
    tb = min(512, T)

    def body(h_ref, t_ref, nf_ref, dh_ref, dnf_ref, loss_ref, dhb_ref):
        i = pl.program_id(0)

        @pl.when(i == 0)
        def _():
            dnf_ref[...] = jnp.zeros_like(dnf_ref)
            loss_ref[...] = jnp.zeros_like(loss_ref)

        (y,), vjp = jax.vjp(_f_rms, h_ref[...], nf_ref[...])
        err = y - t_ref[...]
        dh, dnf = vjp((err * (1.0 / D_MODEL),))
        dh_ref[...] = dh
        dhb_ref[...] = dh.astype(BF16)
        dnf_ref[...] += dnf
        part = 0.5 * jnp.sum(jnp.mean(err * err, axis=-1, keepdims=True), axis=0, keepdims=True)
        loss_ref[...] += jnp.broadcast_to(part, loss_ref.shape)

    return pl.pallas_call(
        body, grid=(T // tb,),
        in_specs=[pl.BlockSpec((tb, D_MODEL), lambda i: (i, 0)), pl.BlockSpec((tb, D_MODEL), lambda i: (i, 0)),
                  pl.BlockSpec((1, D_MODEL), lambda i: (0, 0))],
        out_specs=[pl.BlockSpec((tb, D_MODEL), lambda i: (i, 0)), pl.BlockSpec((1, D_MODEL), lambda i: (0, 0)),
                   pl.BlockSpec((8, LANES), lambda i: (0, 0)), pl.BlockSpec((tb, D_MODEL), lambda i: (i, 0))],
        out_shape=[_sds((T, D_MODEL), F32), _sds((1, D_MODEL), F32), _sds((8, LANES), F32),
                   _sds((T, D_MODEL), BF16)],
        compiler_params=_cparams(("arbitrary",)), name="loss_head",
    )(h, target, nf)


CONV_TC = 1024
HALO = 16


def _conv_fwd(name, x, w8, b):
    T, C = x.shape
    tb = min(1024, T)
    nb = tb // HALO

    def body(x_ref, xp_ref, w_ref, b_ref, y_ref, sc):
        i = pl.program_id(0)
        xv = x_ref[...].astype(F32)
        sc[pl.ds(HALO, tb), :] = xv
        sc[pl.ds(0, HALO), :] = jnp.where(i > 0, xp_ref[...].astype(F32), 0.0)
        acc = b_ref[...] + w_ref[3:4, :] * xv
        for k in range(3):
            acc = acc + w_ref[k:k + 1, :] * sc[pl.ds(HALO - 3 + k, tb), :]
        y_ref[...] = acc

    return pl.pallas_call(
        body, grid=(T // tb, C // CONV_TC),
        in_specs=[pl.BlockSpec((tb, CONV_TC), lambda i, j: (i, j)),
                  pl.BlockSpec((HALO, CONV_TC), lambda i, j: (jnp.maximum(i * nb - 1, 0), j)),
                  pl.BlockSpec((8, CONV_TC), lambda i, j: (0, j)),
                  pl.BlockSpec((1, CONV_TC), lambda i, j: (0, j))],
        out_specs=pl.BlockSpec((tb, CONV_TC), lambda i, j: (i, j)),
        out_shape=_sds((T, C), F32),
        scratch_shapes=[pltpu.VMEM((tb + HALO, CONV_TC), F32)],
        compiler_params=_cparams(("parallel", "parallel")), name=name,
    )(x, x, w8, b)


def _conv_bwd(name, x, dy, w8):
    T, C = x.shape
    tb = min(512, T)
    nb = tb // HALO
    nt = T // tb

    def body(x_ref, xp_ref, dy_ref, dyn_ref, w_ref, dx_ref, dwb_ref, scx, scd):
        i = pl.program_id(1)
        dyv = dy_ref[...]
        xv = x_ref[...].astype(F32)
        scx[pl.ds(HALO, tb), :] = xv
        scx[pl.ds(0, HALO), :] = jnp.where(i > 0, xp_ref[...].astype(F32), 0.0)
        scd[pl.ds(0, tb), :] = dyv
        scd[pl.ds(tb, HALO), :] = jnp.where(i < nt - 1, dyn_ref[...], 0.0)
        dx = w_ref[3:4, :] * dyv
        rows = []
        for k in range(3):
            dx = dx + w_ref[k:k + 1, :] * scd[pl.ds(3 - k, tb), :]
            rows.append(jnp.sum(dyv * scx[pl.ds(HALO - 3 + k, tb), :], axis=0, keepdims=True))
        rows.append(jnp.sum(dyv * xv, axis=0, keepdims=True))
        rows.append(jnp.sum(dyv, axis=0, keepdims=True))
        rows.append(jnp.zeros((3, CONV_TC), F32))
        dx_ref[...] = dx.astype(dx_ref.dtype)

        @pl.when(i == 0)
        def _():
            dwb_ref[...] = jnp.zeros_like(dwb_ref)

        dwb_ref[...] += jnp.concatenate(rows, axis=0)

    return pl.pallas_call(
        body, grid=(C // CONV_TC, nt),
        in_specs=[pl.BlockSpec((tb, CONV_TC), lambda j, i: (i, j)),
                  pl.BlockSpec((HALO, CONV_TC), lambda j, i: (jnp.maximum(i * nb - 1, 0), j)),
                  pl.BlockSpec((tb, CONV_TC), lambda j, i: (i, j)),
                  pl.BlockSpec((HALO, CONV_TC), lambda j, i: (jnp.minimum((i + 1) * nb, T // HALO - 1), j)),
                  pl.BlockSpec((8, CONV_TC), lambda j, i: (0, j))],
        out_specs=[pl.BlockSpec((tb, CONV_TC), lambda j, i: (i, j)),
                   pl.BlockSpec((8, CONV_TC), lambda j, i: (0, j))],
        out_shape=[_sds((T, C), BF16), _sds((8, C), F32)],
        scratch_shapes=[pltpu.VMEM((tb + HALO, CONV_TC), F32), pltpu.VMEM((tb + HALO, CONV_TC), F32)],
        compiler_params=_cparams(("parallel", "arbitrary")), name=name,
    )(x, x, dy, dy, w8)


SCAN_TB = 1024


def _lru_scan_fwd(a, b):
    T, C = a.shape
    tb = min(SCAN_TB, T)

    def body(a_ref, b_ref, h_ref, hp_ref, carry):
        @pl.when(pl.program_id(0) == 0)
        def _():
            carry[...] = jnp.zeros_like(carry)

        def group(gi, h):
            r0 = pl.multiple_of(gi * 8, 8)
            at = a_ref[pl.ds(r0, 8), :]
            bt = b_ref[pl.ds(r0, 8), :]
            hs, hps = [], []
            for r in range(8):
                hps.append(h)
                h = at[r:r + 1, :] * h + bt[r:r + 1, :]
                hs.append(h)
            h_ref[pl.ds(r0, 8), :] = jnp.concatenate(hs, axis=0)
            hp_ref[pl.ds(r0, 8), :] = jnp.concatenate(hps, axis=0)
            return h

        carry[0:1, :] = lax.fori_loop(0, tb // 8, group, carry[0:1, :])

    spec = pl.BlockSpec((tb, C), lambda i: (i, 0))
    return pl.pallas_call(
        body, grid=(T // tb,), in_specs=[spec, spec], out_specs=[spec, spec],
        out_shape=[_sds((T, C), F32), _sds((T, C), F32)],
        scratch_shapes=[pltpu.VMEM((8, C), F32)],
        compiler_params=_cparams(("arbitrary",)), name="lru_scan_fwd",
    )(a, b)


def _lru_scan_bwd(a, dh):
    T, C = a.shape
    tb = min(SCAN_TB, T)
    nt = T // tb

    def body(a_ref, dh_ref, g_ref, carry):
        @pl.when(pl.program_id(0) == 0)
        def _():
            carry[...] = jnp.zeros_like(carry)

        def group(gi, c):
            r0 = pl.multiple_of((tb // 8 - 1 - gi) * 8, 8)
            at = a_ref[pl.ds(r0, 8), :]
            dt = dh_ref[pl.ds(r0, 8), :]
            gs = [None] * 8
            for r in range(7, -1, -1):
                g = dt[r:r + 1, :] + c
                c = at[r:r + 1, :] * g
                gs[r] = g
            g_ref[pl.ds(r0, 8), :] = jnp.concatenate(gs, axis=0)
            return c

        carry[0:1, :] = lax.fori_loop(0, tb // 8, group, carry[0:1, :])

    spec = pl.BlockSpec((tb, C), lambda i: (nt - 1 - i, 0))
    return pl.pallas_call(
        body, grid=(nt,), in_specs=[spec, spec], out_specs=spec,
        out_shape=_sds((T, C), F32),
        scratch_shapes=[pltpu.VMEM((8, C), F32)],
        compiler_params=_cparams(("arbitrary",)), name="lru_scan_bwd",
    )(a, dh)


def _ssd_chunk(xpre, dtraw, state, dtb, alog):
    xc = jax.nn.silu(xpre)
    xs = xc[:, :SSD_INNER]
    bm = xc[:, SSD_INNER:SSD_INNER + SSD_GROUPS * SSD_STATE]
    cm = xc[:, SSD_INNER + SSD_GROUPS * SSD_STATE:]
    dt = jax.nn.softplus(dtraw + dtb)
    a = dt * (-jnp.exp(alog))
    ltri = (lax.broadcasted_iota(jnp.int32, (CHUNK, CHUNK), 0)
            >= lax.broadcasted_iota(jnp.int32, (CHUNK, CHUNK), 1)).astype(F32)
    a_cs = jnp.dot(ltri, a, precision=HIGHEST, preferred_element_type=F32)
    npair = SSD_HEADS // 2
    pi = lax.broadcasted_iota(jnp.int32, (npair, LANES), 0)
    hi = lax.broadcasted_iota(jnp.int32, (npair, LANES), 1)
    sel_even = (hi == 2 * pi).astype(F32)
    sel_odd = (hi == 2 * pi + 1).astype(F32)
    top = lax.broadcasted_iota(jnp.int32, (2 * CHUNK, LANES), 0) < CHUNK

    def pair_transpose(v):
        v2 = jnp.concatenate([v, v], axis=0)
        dn = (((1,), (1,)), ((), ()))
        return (lax.dot_general(sel_even, jnp.where(top, v2, 0.0), dn, precision=HIGHEST, preferred_element_type=F32)
                + lax.dot_general(sel_odd, jnp.where(top, 0.0, v2), dn, precision=HIGHEST, preferred_element_type=F32))

    a_t2 = pair_transpose(a_cs)
    dt_t2 = pair_transpose(dt)
    a_last = a_cs[CHUNK - 1:CHUNK, :]
    dte = jnp.exp(a_last - a_cs) * dt
    cd = jnp.exp(a_last)
    lane = lax.broadcasted_iota(jnp.int32, (CHUNK, LANES), 1)
    left = lane < SSD_HEAD_DIM
    right = jnp.logical_not(left)
    left1 = left[0:1]
    tril2 = lax.broadcasted_iota(jnp.int32, (CHUNK, LANES), 0) >= (lane & (SSD_HEAD_DIM - 1))
    gw = SSD_INNER // SSD_GROUPS
    ys, news = [], []
    for g in range(SSD_GROUPS):
        bg = bm[:, SSD_STATE * g:SSD_STATE * (g + 1)].astype(BF16)
        cg = cm[:, SSD_STATE * g:SSD_STATE * (g + 1)].astype(BF16)
        bg2 = jnp.concatenate([bg, bg], axis=0)
        scores2 = lax.dot_general(cg, bg2, (((1,), (1,)), ((), ())), preferred_element_type=F32)
        coff = jnp.dot(cg, state[:, gw * g:gw * (g + 1)].astype(BF16), preferred_element_type=F32)
        for j in range(gw // LANES):
            lo = gw * g + LANES * j
            p = lo // LANES
            h0 = 2 * p
            xp = xs[:, lo:lo + LANES]
            col = jnp.where(left, a_cs[:, h0:h0 + 1], a_cs[:, h0 + 1:h0 + 2])
            dm = jnp.exp(jnp.where(tril2, col - a_t2[p:p + 1, :], -1e30)) * dt_t2[p:p + 1, :]
            sd = (scores2 * dm).astype(BF16)
            x_bd = jnp.concatenate([jnp.where(left, xp, 0.0), jnp.where(right, xp, 0.0)], axis=0).astype(BF16)
            acc = jnp.dot(sd, x_bd, preferred_element_type=F32)
            ys.append(acc + coff[:, LANES * j:LANES * (j + 1)] * jnp.exp(col))
            dtee = jnp.where(left, dte[:, h0:h0 + 1], dte[:, h0 + 1:h0 + 2])
            xw = (xp * dtee).astype(BF16)
            cde = jnp.where(left1, cd[:, h0:h0 + 1], cd[:, h0 + 1:h0 + 2])
            news.append(state[:, lo:lo + LANES] * cde
                        + lax.dot_general(bg, xw, (((0,), (0,)), ((), ())), preferred_element_type=F32))
    return jnp.concatenate(ys, axis=1), jnp.concatenate(news, axis=1)


def _ssd_fwd(name, xpre, dtraw, dtb, alog, rider=None):
    T = xpre.shape[0]
    n = T // CHUNK
    r_args, r_in, r_out, r_shape, r_scratch = _rider_parts(rider)

    def body(xp, dr, dtb_ref, al_ref, y_ref, st_ref, state):
        @pl.when(pl.program_id(0) == 0)
        def _():
            state[...] = jnp.zeros_like(state)

        st_ref[0] = state[...]
        y, new = _ssd_chunk(xp[...], dr[...], state[...], dtb_ref[...], al_ref[...])
        y_ref[...] = y
        state[...] = new

    small = pl.BlockSpec((1, LANES), lambda c: (0, 0))
    return pl.pallas_call(
        _with_rider(body, 4, 2, rider, (n,)), grid=(n,),
        in_specs=[pl.BlockSpec((CHUNK, SSD_CONV_DIM), lambda c: (c, 0)),
                  pl.BlockSpec((CHUNK, DT_PAD), lambda c: (c, 0)), small, small] + r_in,
        out_specs=[pl.BlockSpec((CHUNK, SSD_INNER), lambda c: (c, 0)),
                   pl.BlockSpec((1, SSD_STATE, SSD_INNER), lambda c: (c, 0, 0))] + r_out,
        out_shape=[_sds((T, SSD_INNER), F32), _sds((n, SSD_STATE, SSD_INNER), F32)] + r_shape,
        scratch_shapes=[pltpu.VMEM((SSD_STATE, SSD_INNER), F32)] + r_scratch,
        compiler_params=_cparams(("arbitrary",)), name=name,
    )(xpre, dtraw, dtb, alog, *r_args)


def _ssd_bwd(name, xpre, dtraw, states, dy, dxs_extra, dtb, alog, rider=None):
    T = xpre.shape[0]
    n = T // CHUNK
    r_args, r_in, r_out, r_shape, r_scratch = _rider_parts(rider)

    def body(xp, dr, st, dy_ref, dx_ref, dtb_ref, al_ref, dxp_ref, ddr_ref, ddtb_ref, dal_ref, dstate):
        @pl.when(pl.program_id(0) == 0)
        def _():
            dstate[...] = jnp.zeros_like(dstate)
            ddtb_ref[...] = jnp.zeros_like(ddtb_ref)
            dal_ref[...] = jnp.zeros_like(dal_ref)

        _, vjp = jax.vjp(_ssd_chunk, xp[...], dr[...], st[0], dtb_ref[...], al_ref[...])
        dxp, ddr, ds, db, da = vjp((dy_ref[...], dstate[...]))
        dxp_ref[:, :SSD_INNER] = dxp[:, :SSD_INNER] + dx_ref[...]
        dxp_ref[:, SSD_INNER:] = dxp[:, SSD_INNER:]
        ddr_ref[...] = ddr.astype(ddr_ref.dtype)
        dstate[...] = ds
        ddtb_ref[...] += db
        dal_ref[...] += da

    def rev(c):
        return (n - 1 - c, 0)

    small = pl.BlockSpec((1, LANES), lambda c: (0, 0))
    return pl.pallas_call(
        _with_rider(body, 7, 4, rider, (n,)), grid=(n,),
        in_specs=[pl.BlockSpec((CHUNK, SSD_CONV_DIM), rev), pl.BlockSpec((CHUNK, DT_PAD), rev),
                  pl.BlockSpec((1, SSD_STATE, SSD_INNER), lambda c: (n - 1 - c, 0, 0)),
                  pl.BlockSpec((CHUNK, SSD_INNER), rev), pl.BlockSpec((CHUNK, SSD_INNER), rev), small, small] + r_in,
        out_specs=[pl.BlockSpec((CHUNK, SSD_CONV_DIM), rev), pl.BlockSpec((CHUNK, DT_PAD), rev), small, small] + r_out,
        out_shape=[_sds((T, SSD_CONV_DIM), F32), _sds((T, DT_PAD), BF16), _sds((1, LANES), F32),
                   _sds((1, LANES), F32)] + r_shape,
        scratch_shapes=[pltpu.VMEM((SSD_STATE, SSD_INNER), F32)] + r_scratch,
        compiler_params=_cparams(("arbitrary",)), name=name,
    )(xpre, dtraw, states, dy, dxs_extra, dtb, alog, *r_args)


HBM_SPEC = pl.BlockSpec(memory_space=pltpu.HBM)
N_PEER = N_DEV - 1


def _position():
    return lax.axis_index("x"), lax.axis_index("y"), lax.axis_index("c")


def _all_gather(name, blks):
    return _exchange_call(name, _gather_phases, blks, [_sds((N_DEV,) + b.shape, b.dtype) for b in blks])


def _scatter_exchange(name, gs):
    return _exchange_call(name, _scatter_phases, gs, [_sds(g.shape, g.dtype) for g in gs])


def _exchange_scratch(na):
    return [pltpu.SemaphoreType.DMA((na * N_PEER,)), pltpu.SemaphoreType.DMA((na * N_PEER,)),
            pltpu.SemaphoreType.DMA((na,))]


def _exchange_call(name, phases, arrays, out_shape):
    na = len(arrays)

    def body(*refs):
        begin, finish = phases(refs[:na], refs[na:2 * na], *refs[2 * na:])
        begin()
        finish()

    return pl.pallas_call(
        body, out_shape=out_shape, in_specs=[HBM_SPEC] * na, out_specs=[HBM_SPEC] * na,
        scratch_shapes=_exchange_scratch(na), name=name,
    )(*arrays)


def _gather_phases(x_refs, out_refs, send_sems, recv_sems, local_sems):
    na = len(x_refs)
    x, y, c = _position()
    me, sibling = (x, y, c), (x, y, 1 - c)
    chips = [(1 - x, y), (x, 1 - y), (1 - x, 1 - y)]

    def slot(a, px, py, pc):
        return out_refs[a].at[4 * px + 2 * py + pc]

    def copy(a, k, block, to, src=None):
        return pltpu.make_async_remote_copy(
            src_ref=slot(a, *block) if src is None else src, dst_ref=slot(a, *block),
            send_sem=send_sems.at[a * N_PEER + k], recv_sem=recv_sems.at[a * N_PEER + k],
            device_id=to, device_id_type=MESH)

    mine = [pltpu.make_async_copy(x_refs[a], slot(a, *me), local_sems.at[a]) for a in range(na)]
    first = []
    for a in range(na):
        first.append(copy(a, 0, me, sibling, src=x_refs[a]))
        first += [copy(a, 1 + j, me, (*chip, c), src=x_refs[a]) for j, chip in enumerate(chips)]
    passed = [copy(a, 4 + j, (*chip, c), sibling) for j, chip in enumerate(chips) for a in range(na)]

    def begin():
        for cp in mine + first:
            cp.start()

    def finish():
        for j, chip in enumerate(chips):
            for a in range(na):
                copy(a, 1 + j, (*chip, c), me).wait_recv()
                passed[j * na + a].start()
        for a in range(na):
            copy(a, 0, sibling, me).wait_recv()
            for j, chip in enumerate(chips):
                copy(a, 4 + j, (*chip, 1 - c), me).wait_recv()
        for cp in first + passed:
            cp.wait_send()
        for cp in mine:
            cp.wait()

    return begin, finish


def _scatter_phases(g_refs, q_refs, send_sems, recv_sems, local_sems):
    na = len(g_refs)
    x, y, c = _position()
    me = 4 * x + 2 * y + c
    mine, sends, recvs = [], [], []
    for a in range(na):
        mine.append(pltpu.make_async_copy(g_refs[a].at[me], q_refs[a].at[me], local_sems.at[a]))
        for k in range(1, N_DEV):
            px, py, pc = (x + (k >> 2)) % 2, (y + ((k >> 1) & 1)) % 2, (c + (k & 1)) % 2
            peer = 4 * px + 2 * py + pc
            sem = a * N_PEER + k - 1
            sends.append(pltpu.make_async_remote_copy(
                src_ref=g_refs[a].at[peer], dst_ref=q_refs[a].at[me], send_sem=send_sems.at[sem],
                recv_sem=recv_sems.at[sem], device_id=(px, py, pc), device_id_type=MESH))
            recvs.append(pltpu.make_async_remote_copy(
                src_ref=g_refs[a].at[me], dst_ref=q_refs[a].at[peer], send_sem=send_sems.at[sem],
                recv_sem=recv_sems.at[sem], device_id=(px, py, pc), device_id_type=MESH))

    def begin():
        for cp in mine + sends:
            cp.start()

    def finish():
        for cp in recvs:
            cp.wait_recv()
        for cp in sends:
            cp.wait_send()
        for cp in mine:
            cp.wait()

    return begin, finish


def _with_rider(body, n_in, n_out, rider, grid):
    if rider is None:
        return body
    phases, arrays, _ = rider
    na = len(arrays)

    def carried(*refs):
        ins, r_in = refs[:n_in], refs[n_in:n_in + na]
        outs = refs[n_in + na:n_in + na + n_out]
        r_out = refs[n_in + na + n_out:n_in + 2 * na + n_out]
        scratch, r_scratch = refs[n_in + 2 * na + n_out:-3], refs[-3:]
        begin, finish = phases(r_in, r_out, *r_scratch)
        first = last = None
        for d, g in enumerate(grid):
            at_start, at_end = pl.program_id(d) == 0, pl.program_id(d) == g - 1
            first = at_start if first is None else first & at_start
            last = at_end if last is None else last & at_end
        pl.when(first)(begin)
        body(*ins, *outs, *scratch)
        pl.when(last)(finish)

    return carried


def _rider_parts(rider):
    if rider is None:
        return [], [], [], [], []
    _, arrays, out_shape = rider
    na = len(arrays)
    return list(arrays), [HBM_SPEC] * na, [HBM_SPEC] * na, list(out_shape), _exchange_scratch(na)


def _gather_rider(blks):
    return (_gather_phases, blks, [_sds((N_DEV,) + b.shape, b.dtype) for b in blks])


def _scatter_rider(gs):
    return (_scatter_phases, gs, [_sds(g.shape, g.dtype) for g in gs])


def _adamw_sum(name, w, m, v, recvs, rider=None):
    L, r, c = w.shape
    tr = r if r <= 512 else max(t for t in range(16, 257, 16) if r % t == 0)
    assert len(recvs) == L
    nr = r // tr

    def body(w_ref, m_ref, v_ref, *refs):
        q_refs = refs[:L]
        g_ref, d_ref, m2_ref, v2_ref = refs[L:]
        l = pl.program_id(0)
        for ll in range(L):
            @pl.when(l == ll)
            def _(q_ref=q_refs[ll]):
                gv = q_ref[0].astype(F32)
                for s in range(1, N_DEV):
                    gv = gv + q_ref[s].astype(F32)
                m2 = ADAM_B1 * m_ref[0] + (1.0 - ADAM_B1) * gv
                v2 = ADAM_B2 * v_ref[0] + (1.0 - ADAM_B2) * jnp.square(gv)
                m_hat = m2 / (1.0 - ADAM_B1 ** ADAM_STEP)
                v_hat = v2 / (1.0 - ADAM_B2 ** ADAM_STEP)
                g_ref[0] = gv
                d_ref[0] = -ADAM_LR * (m_hat / (jnp.sqrt(v_hat) + ADAM_EPS) + ADAM_WD * w_ref[0])
                m2_ref[0] = m2
                v2_ref[0] = v2

    def q_index(ll):
        return lambda l, i: (0, jnp.where(l == ll, i, jnp.where(l > ll, nr - 1, 0)), 0)

    spec = pl.BlockSpec((1, tr, c), lambda l, i: (l, i, 0))
    r_args, r_in, r_out, r_shape, r_scratch = _rider_parts(rider)
    return pl.pallas_call(
        _with_rider(body, 3 + L, 4, rider, (L, nr)), grid=(L, nr),
        in_specs=[spec] * 3 + [pl.BlockSpec((N_DEV, tr, c), q_index(ll)) for ll in range(L)] + r_in,
        out_specs=[spec] * 4 + r_out, out_shape=[_sds((L, r, c), F32)] * 4 + r_shape, scratch_shapes=r_scratch,
        compiler_params=_cparams(("arbitrary", "arbitrary")), name=name,
    )(w, m, v, *recvs, *r_args)


def _sum_blocks(name, q, tr):
    R = q.shape[0] // N_DEV
    W = q.shape[1]
    tr = min(tr, R)
    assert R % tr == 0
    nb = R // tr

    def body(*refs):
        acc = refs[0][...].astype(F32)
        for r in refs[1:N_DEV]:
            acc = acc + r[...].astype(F32)
        refs[N_DEV][...] = acc

    return pl.pallas_call(
        body, grid=(nb,),
        in_specs=[pl.BlockSpec((tr, W), lambda i, s=s: (s * nb + i, 0)) for s in range(N_DEV)],
        out_specs=pl.BlockSpec((tr, W), lambda i: (i, 0)), out_shape=_sds((R, W), F32),
        compiler_params=_cparams(("parallel",)), name=name,
    )(*([q] * N_DEV))


def _adamw(name, w, g, m, v):
    R, C = w.shape
    tr = R
    if R > 512:
        tr = max(t for t in range(8, 513, 8) if R % t == 0)

    def body(w_ref, g_ref, m_ref, v_ref, d_ref, m2_ref, v2_ref):
        gv = g_ref[...]
        m2 = ADAM_B1 * m_ref[...] + (1.0 - ADAM_B1) * gv
        v2 = ADAM_B2 * v_ref[...] + (1.0 - ADAM_B2) * jnp.square(gv)
        m_hat = m2 / (1.0 - ADAM_B1 ** ADAM_STEP)
        v_hat = v2 / (1.0 - ADAM_B2 ** ADAM_STEP)
        d_ref[...] = -ADAM_LR * (m_hat / (jnp.sqrt(v_hat) + ADAM_EPS) + ADAM_WD * w_ref[...])
        m2_ref[...] = m2
        v2_ref[...] = v2

    spec = pl.BlockSpec((tr, C), lambda i: (i, 0))
    return pl.pallas_call(
        body, grid=(R // tr,), in_specs=[spec] * 4, out_specs=[spec] * 3,
        out_shape=[_sds((R, C), F32)] * 3, compiler_params=_cparams(("parallel",)), name=name,
    )(w, g, m, v)


BIG = ("w_in", "w_branch", "w_out", "w_ffn_in", "w_ffn_out")
BIG_ROW_SHARDED = {"w_in": False, "w_branch": True, "w_out": True, "w_ffn_in": True, "w_ffn_out": True}
TRANSPOSED = ("w_ffn_in",)
SMALL = ("norm1_g", "b_gate", "lru_conv_w", "lru_conv_b", "lru_w_a", "lru_b_a", "lru_w_x", "lru_b_x", "lru_lambda",
         "ssd_conv_w", "ssd_conv_b", "ssd_dt_bias", "ssd_A_log", "ssd_D", "ssd_norm_g", "norm2_g", "norm_f")
SMALL_ROWS = 256


def _pack(arrs, dtype, row_mult):
    parts = []
    for a in arrs:
        f = a.reshape(-1).astype(dtype)
        pad = (-f.shape[0]) % LANES
        if pad:
            f = jnp.concatenate([f, jnp.zeros((pad,), dtype)])
        parts.append(f)
    f = jnp.concatenate(parts)
    pad = (-f.shape[0]) % (LANES * row_mult)
    if pad:
        f = jnp.concatenate([f, jnp.zeros((pad,), dtype)])
    return f.reshape(-1, LANES)


def _unpack(flat, shapes, lead=()):
    f = flat.reshape(lead + (-1,))
    out, off = [], 0
    for s in shapes:
        n = int(np.prod(s))
        out.append(f[..., off:off + n].reshape(lead + tuple(s)))
        off += n + (-n) % LANES
    return out


def _full_from_shards(name, st):
    if BIG_ROW_SHARDED[name]:
        return st.reshape((-1,) + st.shape[2:])
    return jnp.transpose(st, (1, 0, 2)).reshape(st.shape[1], -1)


def _shards_from_full(name, full):
    if BIG_ROW_SHARDED[name]:
        return full.reshape((N_DEV, full.shape[0] // N_DEV) + full.shape[1:])
    return jnp.transpose(full.reshape(full.shape[0], N_DEV, -1), (1, 0, 2))


def _block_diag_tiles(w):
    z = jnp.zeros((8, 64, 64), w.dtype)
    w2 = w.reshape(8, 2, 64, 64)
    top = jnp.concatenate([w2[:, 0], z], axis=2)
    bot = jnp.concatenate([z, w2[:, 1]], axis=2)
    return jnp.concatenate([top, bot], axis=1)


def _block_diag_untile(t):
    return jnp.stack([t[:, :64, :64], t[:, 64:, 64:]], axis=1).reshape(16, 64, 64)


def _pad_lanes(a, width):
    return jnp.concatenate([a, jnp.zeros(a.shape[:-1] + (width - a.shape[-1],), a.dtype)], axis=-1)


def _pad_rows8(w):
    return jnp.concatenate([w, jnp.zeros((8 - w.shape[0],) + w.shape[1:], w.dtype)], axis=0)


REST = BIG[1:]


def _in_weights(shards):
    w_in = _full_from_shards("w_in", shards)
    seg = [w_in[:, IN_OFFS[k]:IN_OFFS[k + 1]] for k in range(6)]
    return {"w_lx": seg[0], "w_lg": seg[1], "w_z": seg[2], "w_xbc": seg[3], "w_dt": _pad_lanes(seg[4], DT_PAD),
            "w_g": seg[5]}


def _rest_weights(shards):
    full = {n: _full_from_shards(n, st) for n, st in zip(REST, shards)}
    return {"w_bra": full["w_branch"][:D_MODEL], "w_brb": full["w_branch"][D_MODEL:], "w_out": full["w_out"],
            "w_ffn_in_t": full["w_ffn_in"], "w_ffn_out": full["w_ffn_out"]}


def _small_params(l, small):
    p = {}
    for n in ("norm1_g", "b_gate", "lru_conv_b", "lru_b_a", "lru_b_x", "lru_lambda", "ssd_conv_b", "ssd_norm_g", "norm2_g"):
        p[n] = small[n][l].reshape(1, -1)
    p["lru_conv_w8"] = _pad_rows8(small["lru_conv_w"][l])
    p["ssd_conv_w8"] = _pad_rows8(small["ssd_conv_w"][l])
    p["wa"] = _block_diag_tiles(small["lru_w_a"][l])
    p["wx"] = _block_diag_tiles(small["lru_w_x"][l])
    p["dtb"] = _pad_lanes(small["ssd_dt_bias"][l].reshape(1, -1), DT_PAD)
    p["alog"] = _pad_lanes(small["ssd_A_log"][l].reshape(1, -1), DT_PAD)
    p["d_e"] = jnp.repeat(small["ssd_D"][l], SSD_HEAD_DIM).reshape(1, -1)
    return p


def _layer_fwd(l, h, p, riders):
    n = f"l{l}_"
    s = {"h_in": h}
    got = {}
    (xn,) = _rowwise_fwd(n + "rms1", _f_rms, [h], [p["norm1_g"]], [(D_MODEL, BF16)], 512)
    s["xn"] = xn
    for k in ("lx", "lg", "z", "xbc", "g", "dt"):
        s[k] = _mm(n + "in_" + k, [(xn, p["w_" + k])], "nn", out_dtype=F32 if k == "dt" else BF16)
    s["u"] = _conv_fwd(n + "lru_conv", s["lx"], p["lru_conv_w8"], p["lru_conv_b"])
    lru_pars = [p["wa"], p["lru_b_a"], p["wx"], p["lru_b_x"], p["lru_lambda"]]
    s["a"], b = _rowwise_fwd(n + "lru_gates", _f_lru_gates, [s["u"]], lru_pars, [(D_MODEL, F32)] * 2, 512)
    s["hl"], s["hprev"] = _lru_scan_fwd(s["a"], b)
    s["xpre"] = _conv_fwd(n + "ssd_conv", s["xbc"], p["ssd_conv_w8"], p["ssd_conv_b"])
    s["yssd"], s["states"], *got["ssd"] = _ssd_fwd(n + "ssd", s["xpre"], s["dt"], p["dtb"], p["alog"],
                                                   riders.get("ssd"))
    if "ssd" in riders:
        p.update(_rest_weights(got["ssd"]))
    post_rows = [s["hl"], s["lg"], s["yssd"], (s["xpre"], 0, SSD_INNER), s["z"]]
    s["ya"], s["yb"], *got["post"] = _rowwise_fwd(n + "post", _f_post, post_rows, [p["d_e"], p["ssd_norm_g"]],
                                                  [(D_MODEL, BF16), (SSD_INNER, BF16)], 128, riders.get("post"))
    s["ma"] = _mm(n + "br_a", [(s["ya"], p["w_bra"])], "nn", out_dtype=BF16)
    s["mb"] = _mm(n + "br_b", [(s["yb"], p["w_brb"])], "nn", out_dtype=BF16)
    (s["merged"],) = _rowwise_fwd(n + "merge", _f_merge, [s["ma"], s["mb"], s["g"]], [p["b_gate"]],
                                  [(D_MODEL, BF16)], 512)
    s["h_mid"] = _mm(n + "out", [(s["merged"], p["w_out"])], "nn", res=h)
    (s["xn2"],) = _rowwise_fwd(n + "rms2", _f_rms, [s["h_mid"]], [p["norm2_g"]], [(D_MODEL, BF16)], 512)
    s["gu"] = _mm(n + "ffn_in", [(s["xn2"], p["w_ffn_in_t"])], "nt", out_dtype=BF16, rider=riders.get("ffn_in"))
    if "ffn_in" in riders:
        s["gu"], got["ffn_in"] = s["gu"]
    h_out, s["act"], carried = _swiglu_out(n + "ffn_out", s["gu"], p["w_ffn_out"], s["h_mid"], riders.get("ffn_out"))
    if "ffn_out" in riders:
        got["ffn_out"] = carried
    return h_out, s, got


def _f_rms_res(x, g):
    return _f_rms(x, g)[0], x


def _layer_bwd(l, dh, dh_b, s, p, above, first_layer):
    n = f"l{l}_b_"
    gw, gs = {}, {}
    d_act = _mm(n + "d_act", [(dh_b, p["w_ffn_out"])], "nt", out_dtype=BF16)
    gw["w_ffn_out"] = _mm_tn(n + "dw_ffn_out", s["act"], dh_b)
    (d_gu,) = _rowwise_bwd(n + "act", _f_act, [s["gu"]], [], [d_act], _ident, [BF16], 256)
    d_xn2 = _mm(n + "d_xn2", [(d_gu, p["w_ffn_in_t"])], "nn")
    gw["w_ffn_in"] = _mm_tn(n + "dw_ffn_in", d_gu, s["xn2"])
    dh_mid, dh_mid_b, gs["norm2_g"] = _rowwise_bwd(n + "rms2", _f_rms_res, [s["h_mid"]], [p["norm2_g"]], [d_xn2, dh],
                                                   _ident, [(F32, BF16)], 512)
    d_merged = _mm(n + "d_merged", [(dh_mid_b, p["w_out"])], "nt", out_dtype=BF16)
    gw["w_out"] = _mm_tn(n + "dw_out", s["merged"], dh_mid_b)
    d_ma, d_mb, d_g, gs["b_gate"] = _rowwise_bwd(n + "merge", _f_merge, [s["ma"], s["mb"], s["g"]], [p["b_gate"]],
                                                 [d_merged], _ident, [BF16, BF16, BF16], 512)
    d_ya = _mm(n + "d_ya", [(d_ma, p["w_bra"])], "nt", out_dtype=BF16)
    d_yb = _mm(n + "d_yb", [(d_mb, p["w_brb"])], "nt", out_dtype=BF16)
    gw["w_branch"] = jnp.concatenate([_mm_tn(n + "dw_bra", s["ya"], d_ma), _mm_tn(n + "dw_brb", s["yb"], d_mb)], axis=0)
    post_rows = [s["hl"], s["lg"], s["yssd"], (s["xpre"], 0, SSD_INNER), s["z"]]
    d_hl, d_lg, d_yssd, d_xs, d_z, d_de, gs["ssd_norm_g"] = _rowwise_bwd(
        n + "post", _f_post, post_rows, [p["d_e"], p["ssd_norm_g"]], [d_ya, d_yb], _ident,
        [F32, BF16, F32, F32, BF16], 128)
    gs["ssd_D"] = d_de.reshape(SSD_HEADS, SSD_HEAD_DIM).sum(axis=1)
    contrib = [_shards_from_full(k, gw[k]) for k in REST] + ([] if above is None else [above])
    d_xpre, d_dt, d_dtb, d_alog, *carried = _ssd_bwd(n + "ssd", s["xpre"], s["dt"], s["states"], d_yssd, d_xs,
                                                     p["dtb"], p["alog"], _scatter_rider(contrib))
    arrived = dict(zip(REST, carried))
    arrived_above = None if above is None else carried[len(REST)]
    gs["ssd_dt_bias"] = d_dtb[0, :SSD_HEADS]
    gs["ssd_A_log"] = d_alog[0, :SSD_HEADS]
    d_xbc, dwb = _conv_bwd(n + "ssd_conv", s["xbc"], d_xpre, p["ssd_conv_w8"])
    gs["ssd_conv_w"], gs["ssd_conv_b"] = dwb[:4], dwb[4]
    g_scan = _lru_scan_bwd(s["a"], d_hl)
    lru_pars = [p["wa"], p["lru_b_a"], p["wx"], p["lru_b_x"], p["lru_lambda"]]
    d_u, d_wa, gs["lru_b_a"], d_wx, gs["lru_b_x"], gs["lru_lambda"] = _rowwise_bwd(
        n + "lru_gates", _f_lru_gates, [s["u"]], lru_pars, [g_scan, s["hprev"]],
        lambda g, hp: (g * hp, g), [F32], 256)
    gs["lru_w_a"], gs["lru_w_x"] = _block_diag_untile(d_wa), _block_diag_untile(d_wx)
    d_lx, dwb = _conv_bwd(n + "lru_conv", s["lx"], d_u, p["lru_conv_w8"])
    gs["lru_conv_w"], gs["lru_conv_b"] = dwb[:4], dwb[4]
    segs = [("lx", d_lx), ("lg", d_lg), ("z", d_z), ("xbc", d_xbc), ("dt", d_dt), ("g", d_g)]
    dws = [_mm_tn(n + "dw_in_" + k, s["xn"], d) for k, d in segs]
    dws[4] = dws[4][:, :IN_WIDTHS[4]]
    below = _shards_from_full("w_in", jnp.concatenate(dws, axis=1))
    pairs_a = [(d, p["w_" + k]) for k, d in segs[:3]]
    pairs_b = [(d, p["w_" + k]) for k, d in segs[3:]]
    if first_layer:
        half = below.shape[1] // 2
        d_xn, (top,) = _mm(n + "d_xn_a", pairs_a, "nt", rider=_scatter_rider([below[:, :half]]))
        d_xn, (bottom,) = _mm(n + "d_xn_b", pairs_b, "nt", res=d_xn, rider=_scatter_rider([below[:, half:]]))
        arrived["w_in"] = jnp.concatenate([top, bottom], axis=1)
        below = None
    else:
        d_xn = _mm(n + "d_xn_a", pairs_a, "nt")
        d_xn = _mm(n + "d_xn_b", pairs_b, "nt", res=d_xn)
    dh_in, dh_in_b, gs["norm1_g"] = _rowwise_bwd(n + "rms1", _f_rms_res, [s["h_in"]], [p["norm1_g"]], [d_xn, dh_mid],
                                                 _ident, [(F32, BF16)], 512)
    for k in ("norm1_g", "norm2_g", "b_gate", "ssd_norm_g", "lru_b_a", "lru_b_x", "lru_lambda"):
        gs[k] = gs[k].reshape(-1)
    return dh_in, dh_in_b, gs, arrived, arrived_above, below, gw


def _step(inp):
    x = inp["x"][0]
    target = inp["loss_target"][0]
    dev = 4 * lax.axis_index("x") + 2 * lax.axis_index("y") + lax.axis_index("c")

    def mine(l, names):
        return [(inp[n][l].T if n in TRANSPOSED else inp[n][l]).astype(BF16) for n in names]

    first = _all_gather("gather_first", mine(0, ("w_in",)) + [inp["lru_conv_w"], inp["ssd_conv_w"]])
    small = {n: inp[n] for n in SMALL}
    small["lru_conv_w"] = jnp.moveaxis(first[1], 0, 2).reshape(DEPTH, 4, -1)
    small["ssd_conv_w"] = jnp.moveaxis(first[2], 0, 2).reshape(DEPTH, 4, -1)
    shards = {"w_in": first[0]}

    h, saved, params = x, [], []
    for l in range(DEPTH):
        p = _small_params(l, small)
        p.update(_in_weights(shards["w_in"]))
        riders = {}
        if l == 0:
            riders["ssd"] = _gather_rider(mine(0, REST))
        else:
            p.update(_rest_weights([shards[n] for n in REST]))
        if l + 1 < DEPTH:
            riders["post"] = _gather_rider(mine(l + 1, ("w_ffn_in", "w_ffn_out")))
            riders["ffn_in"] = _gather_rider(mine(l + 1, ("w_in",)))
            riders["ffn_out"] = _gather_rider(mine(l + 1, ("w_branch", "w_out")))
        h, s, got = _layer_fwd(l, h, p, riders)
        if l + 1 < DEPTH:
            shards = {"w_ffn_in": got["post"][0], "w_ffn_out": got["post"][1], "w_in": got["ffn_in"][0],
                      "w_branch": got["ffn_out"][0], "w_out": got["ffn_out"][1]}
        saved.append(s)
        params.append(p)
    dh, d_nf, loss_acc, dh_b = _loss_head(h, target, small["norm_f"].reshape(1, -1))
    loss = lax.psum(loss_acc[0, 0], ("x", "y", "c"))

    gss, received = [None] * DEPTH, [None] * DEPTH
    handed_down = None
    for l in reversed(range(DEPTH)):
        dh, dh_b, gss[l], received[l], arrived_above, handed_down, _ = _layer_bwd(
            l, dh, dh_b, saved[l], params[l], handed_down, l == 0)
        if arrived_above is not None:
            received[l + 1]["w_in"] = arrived_above
    grad_x = dh[None]
    out = {"loss": loss, "grad_x": grad_x}
    small_full = {n: jnp.stack([gss[l][n] for l in range(DEPTH)]) for n in SMALL if n != "norm_f"}
    small_full["norm_f"] = d_nf.reshape(-1)
    part = _pack([small_full[n] for n in SMALL], F32, SMALL_ROWS)
    for n in BIG:
        flip = (lambda a: jnp.transpose(a, (0, 2, 1))) if n in TRANSPOSED else (lambda a: a)
        res = _adamw_sum("adamw_" + n, flip(inp[n]), flip(inp["m_" + n]), flip(inp["v_" + n]),
                         [received[l][n] for l in range(DEPTH)], _gather_rider([part]) if n == "w_in" else None)
        if n == "w_in":
            everyone = res[4]
        out["grad_" + n], out["delta_" + n], out["new_m_" + n], out["new_v_" + n] = [flip(a) for a in res[:4]]
    g_small_flat = _sum_blocks("sum_small_grads", everyone.reshape(-1, LANES), SMALL_ROWS)
    g_small = dict(zip(SMALL, _unpack(g_small_flat, [small_full[n].shape for n in SMALL])))
    for n in ("lru_conv_w", "ssd_conv_w"):
        w = inp[n].shape[-1]
        g_small[n] = lax.dynamic_slice_in_dim(g_small[n], dev * w, w, axis=2)

    shapes = [inp[n].shape for n in SMALL]
    packs = [_pack([src[pre + n] for n in SMALL], F32, SMALL_ROWS)
             for src, pre in ((inp, ""), (g_small, ""), (inp, "m_"), (inp, "v_"))]
    d, m2, v2 = _adamw("adamw_small", *packs)
    for n, dd, mm, vv in zip(SMALL, _unpack(d, shapes), _unpack(m2, shapes), _unpack(v2, shapes)):
        out["grad_" + n] = g_small[n]
        out["delta_" + n], out["new_m_" + n], out["new_v_" + n] = dd, mm, vv
    return out


WEIGHTS = ("norm1_g", "w_in", "b_gate", "lru_conv_w", "lru_conv_b", "lru_w_a", "lru_b_a", "lru_w_x", "lru_b_x",
           "lru_lambda", "ssd_conv_w", "ssd_conv_b", "ssd_dt_bias", "ssd_A_log", "ssd_D", "ssd_norm_g", "w_branch",
           "w_out", "norm2_g", "w_ffn_in", "w_ffn_out", "norm_f")


def kernel(x, norm1_g, w_in, b_gate, lru_conv_w, lru_conv_b, lru_w_a, lru_b_a, lru_w_x, lru_b_x, lru_lambda, ssd_conv_w, ssd_conv_b, ssd_dt_bias, ssd_A_log, ssd_D, ssd_norm_g, w_branch, w_out, norm2_g, w_ffn_in, w_ffn_out, norm_f, loss_target, m_norm1_g, m_w_in, m_b_gate, m_lru_conv_w, m_lru_conv_b, m_lru_w_a, m_lru_b_a, m_lru_w_x, m_lru_b_x, m_lru_lambda, m_ssd_conv_w, m_ssd_conv_b, m_ssd_dt_bias, m_ssd_A_log, m_ssd_D, m_ssd_norm_g, m_w_branch, m_w_out, m_norm2_g, m_w_ffn_in, m_w_ffn_out, m_norm_f, v_norm1_g, v_w_in, v_b_gate, v_lru_conv_w, v_lru_conv_b, v_lru_w_a, v_lru_b_a, v_lru_w_x, v_lru_b_x, v_lru_lambda, v_ssd_conv_w, v_ssd_conv_b, v_ssd_dt_bias, v_ssd_A_log, v_ssd_D, v_ssd_norm_g, v_w_branch, v_w_out, v_norm2_g, v_w_ffn_in, v_w_ffn_out, v_norm_f):
    out = _step(dict(locals()))
    res = [out["loss"], out["grad_x"]]
    for pre in ("grad_", "delta_", "new_m_", "new_v_"):
        res += [out[pre + n] for n in WEIGHTS]
    return tuple(res)
```

```python
import functools

import numpy as np
import jax
import jax.numpy as jnp
from jax import lax
from jax.experimental import pallas as pl
from jax.experimental.pallas import tpu as pltpu

F32 = jnp.float32
BF16 = jnp.bfloat16
HIGHEST = lax.Precision.HIGHEST

D_MODEL = 1024
DEPTH = 2
CHUNK = 64
LRU_C = 8.0
SSD_INNER = 2048
SSD_HEADS = 32
SSD_HEAD_DIM = 64
SSD_GROUPS = 4
SSD_STATE = 128
SSD_CONV_DIM = 3072
D_FF = 2816
EPS = 1e-6
N_DEV = 8
LANES = 128
DT_PAD = LANES
IN_WIDTHS = (1024, 1024, 2048, 3072, 32, 2048)
IN_OFFS = tuple(int(v) for v in np.cumsum((0,) + IN_WIDTHS))

ADAM_LR = 0.001
ADAM_B1 = 0.9
ADAM_B2 = 0.999
ADAM_EPS = 1e-08
ADAM_WD = 0.01
ADAM_STEP = 10

VMEM_LIMIT = 56 * 1024 * 1024
MESH = pl.DeviceIdType.MESH


def _cparams(sem=None):
    return pltpu.CompilerParams(dimension_semantics=sem, vmem_limit_bytes=VMEM_LIMIT)


def _sds(shape, dtype):
    return jax.ShapeDtypeStruct(tuple(shape), dtype)


def _pick(n, cap):
    if n <= cap:
        return n
    best = LANES
    for t in range(LANES, cap + 1, LANES):
        if n % t == 0:
            best = t
    assert n % best == 0, (n, cap)
    return best


def _mm(name, pairs, mode, out_dtype=F32, res=None, rider=None):
    M = pairs[0][0].shape[0]
    N = pairs[0][1].shape[1] if mode == "nn" else pairs[0][1].shape[0]
    npair = len(pairs)
    tm = min(M, 1024 if npair <= 3 else 512)
    tn = _pick(N, 1536 if npair == 1 else 1024)
    tks, nks, starts = [], [], []
    s = 0
    for a, _ in pairs:
        k = a.shape[1]
        tk = _pick(k, 1536)
        tks.append(tk)
        nks.append(k // tk)
        starts.append(s)
        s += k // tk
    nk = s
    dims = (((1,), (0,)), ((), ())) if mode == "nn" else (((1,), (1,)), ((), ()))

    def body(*refs):
        ab = refs[:2 * npair]
        pos = 2 * npair
        r_ref = None
        if res is not None:
            r_ref = refs[pos]
            pos += 1
        o_ref = refs[pos]
        acc = refs[pos + 1] if nk > 1 else None
        k = pl.program_id(2)

        def finish(r):
            if r_ref is not None:
                r = r + r_ref[...].astype(F32)
            o_ref[...] = r.astype(o_ref.dtype)

        for p in range(npair):
            a_ref, b_ref = ab[2 * p], ab[2 * p + 1]
            lo, hi = starts[p], starts[p] + nks[p]

            def step(a_ref=a_ref, b_ref=b_ref, lo=lo, hi=hi):
                d = lax.dot_general(a_ref[...].astype(BF16), b_ref[...].astype(BF16), dims,
                                    preferred_element_type=F32)
                if nk == 1:
                    finish(d)
                    return
                if lo == 0:
                    @pl.when(k == 0)
                    def _():
                        acc[...] = d
                if hi == nk:
                    @pl.when(k == nk - 1)
                    def _():
                        finish(acc[...] + d)
                if max(lo, 1) < min(hi, nk - 1):
                    @pl.when((k > 0) & (k < nk - 1))
                    def _():
                        acc[...] += d

            if npair == 1:
                step()
            else:
                pl.when((k >= lo) & (k < hi))(step)

    in_specs, args = [], []
    for p, (a, b) in enumerate(pairs):
        def kk(k, p=p):
            return jnp.clip(k - starts[p], 0, nks[p] - 1)
        in_specs.append(pl.BlockSpec((tm, tks[p]), lambda i, j, k, kk=kk: (i, kk(k))))
        if mode == "nn":
            in_specs.append(pl.BlockSpec((tks[p], tn), lambda i, j, k, kk=kk: (kk(k), j)))
        else:
            in_specs.append(pl.BlockSpec((tn, tks[p]), lambda i, j, k, kk=kk: (j, kk(k))))
        args += [a, b]
    if res is not None:
        in_specs.append(pl.BlockSpec((tm, tn), lambda i, j, k: (i, j)))
        args.append(res)
    r_args, r_in, r_out, r_shape, r_scratch = _rider_parts(rider)
    grid = (M // tm, N // tn, nk)
    out = pl.pallas_call(
        _with_rider(body, len(args), 1, rider, grid), grid=grid, in_specs=in_specs + r_in,
        out_specs=[pl.BlockSpec((tm, tn), lambda i, j, k: (i, j))] + r_out,
        out_shape=[_sds((M, N), out_dtype)] + r_shape,
        scratch_shapes=([pltpu.VMEM((tm, tn), F32)] if nk > 1 else []) + r_scratch,
        compiler_params=_cparams(("parallel", "parallel", "arbitrary") if rider is None else ("arbitrary",) * 3),
        name=name,
    )(*args, *r_args)
    return out[0] if rider is None else (out[0], out[1:])


def _swiglu_out(name, gu, w, res, rider=None):
    M, F2 = gu.shape
    F = F2 // 2
    N = w.shape[1]
    tm = min(M, 512)
    tk = _pick(F, 1536)
    nk = F // tk
    assert nk > 1

    def body(g_ref, u_ref, w_ref, r_ref, o_ref, act_ref, acc):
        k = pl.program_id(2)
        g = g_ref[...].astype(F32)
        act = (jax.nn.silu(g) * u_ref[...].astype(F32)).astype(BF16)
        act_ref[...] = act
        d = jnp.dot(act, w_ref[...], preferred_element_type=F32)

        @pl.when(k == 0)
        def _():
            acc[...] = d

        @pl.when((k > 0) & (k < nk - 1))
        def _():
            acc[...] += d

        @pl.when(k == nk - 1)
        def _():
            o_ref[...] = acc[...] + d + r_ref[...]

    r_args, r_in, r_out, r_shape, r_scratch = _rider_parts(rider)
    grid = (M // tm, 1, nk)
    out = pl.pallas_call(
        _with_rider(body, 4, 2, rider, grid), grid=grid,
        in_specs=[pl.BlockSpec((tm, tk), lambda i, j, k: (i, k)), pl.BlockSpec((tm, tk), lambda i, j, k: (i, k + nk)),
                  pl.BlockSpec((tk, N), lambda i, j, k: (k, 0)), pl.BlockSpec((tm, N), lambda i, j, k: (i, 0))] + r_in,
        out_specs=[pl.BlockSpec((tm, N), lambda i, j, k: (i, 0)), pl.BlockSpec((tm, tk), lambda i, j, k: (i, k))] + r_out,
        out_shape=[_sds((M, N), F32), _sds((M, F), BF16)] + r_shape,
        scratch_shapes=[pltpu.VMEM((tm, N), F32)] + r_scratch,
        compiler_params=_cparams(("arbitrary",) * 3), name=name,
    )(gu, gu, w, res, *r_args)
    return out[0], out[1], out[2:]


def _merge_out(name, ma, mb, gates, bg, w, res):
    M, D = ma.shape
    tm = min(M, 512)

    def body(a_ref, b_ref, ga_ref, gb_ref, bg_ref, w_ref, r_ref, o_ref, m_ref):
        g_a = jax.nn.sigmoid(ga_ref[...].astype(F32) + bg_ref[:, :D])
        g_b = jax.nn.sigmoid(gb_ref[...].astype(F32) + bg_ref[:, D:])
        merged = (g_a * a_ref[...].astype(F32) + g_b * b_ref[...].astype(F32)).astype(BF16)
        m_ref[...] = merged
        o_ref[...] = jnp.dot(merged, w_ref[...], preferred_element_type=F32) + r_ref[...]

    row = pl.BlockSpec((tm, D), lambda i: (i, 0))
    return pl.pallas_call(
        body, grid=(M // tm,),
        in_specs=[row, row, row, pl.BlockSpec((tm, D), lambda i: (i, 1)), pl.BlockSpec((1, 2 * D), lambda i: (0, 0)),
                  pl.BlockSpec((D, D), lambda i: (0, 0)), row],
        out_specs=[row, row], out_shape=[_sds((M, D), F32), _sds((M, D), BF16)],
        compiler_params=_cparams(("parallel",)), name=name,
    )(ma, mb, gates, gates, bg, w, res)


def _mm_tn(name, a, b, out_dtype=BF16):
    M, Ka = a.shape
    N = b.shape[1]
    tm = min(M, 2048)
    tka = _pick(Ka, 1024)
    tn = _pick(N, 1024)
    nm = M // tm

    def body(a_ref, b_ref, o_ref, *scratch):
        k = pl.program_id(2)
        d = lax.dot_general(a_ref[...].astype(BF16), b_ref[...].astype(BF16),
                            (((0,), (0,)), ((), ())), preferred_element_type=F32)
        if nm == 1:
            o_ref[...] = d.astype(o_ref.dtype)
            return
        acc = scratch[0]

        @pl.when(k == 0)
        def _():
            acc[...] = d

        @pl.when((k > 0) & (k < nm - 1))
        def _():
            acc[...] += d

        @pl.when(k == nm - 1)
        def _():
            o_ref[...] = (acc[...] + d).astype(o_ref.dtype)

    return pl.pallas_call(
        body, grid=(Ka // tka, N // tn, nm),
        in_specs=[pl.BlockSpec((tm, tka), lambda i, j, k: (k, i)),
                  pl.BlockSpec((tm, tn), lambda i, j, k: (k, j))],
        out_specs=pl.BlockSpec((tka, tn), lambda i, j, k: (i, j)),
        out_shape=_sds((Ka, N), out_dtype),
        scratch_shapes=[pltpu.VMEM((tka, tn), F32)] if nm > 1 else [],
        compiler_params=_cparams(("parallel", "parallel", "arbitrary")), name=name,
    )(a, b)


def _rowwise_fwd(name, fn, rows, pars, outs, tb, rider=None):
    rows = [r if isinstance(r, tuple) else (r, 0, r.shape[1]) for r in rows]
    T = rows[0][0].shape[0]
    tb = min(tb, T)
    nr, npar = len(rows), len(pars)
    r_args, r_in, r_out, r_shape, r_scratch = _rider_parts(rider)

    def body(*refs):
        rv = [r[...].astype(F32) for r in refs[:nr]]
        pv = [p[...] for p in refs[nr:nr + npar]]
        res = fn(*rv, *pv)
        for o, r in zip(refs[nr + npar:], res):
            o[...] = r.astype(o.dtype)

    in_specs = [pl.BlockSpec((tb, w), lambda i, c=c: (i, c)) for _, c, w in rows]
    in_specs += [pl.BlockSpec(p.shape, lambda i, n=p.ndim: (0,) * n) for p in pars]
    return pl.pallas_call(
        _with_rider(body, nr + npar, len(outs), rider, (T // tb,)), grid=(T // tb,), in_specs=in_specs + r_in,
        out_specs=[pl.BlockSpec((tb, w), lambda i: (i, 0)) for w, _ in outs] + r_out,
        out_shape=[_sds((T, w), dt) for w, dt in outs] + r_shape, scratch_shapes=r_scratch,
        compiler_params=_cparams(("parallel",) if rider is None else ("arbitrary",)), name=name,
    )(*[r[0] for r in rows], *pars, *r_args)


def _rowwise_bwd(name, fn, rows, pars, cot_rows, cot_fn, row_out, tb):
    rows = [r if isinstance(r, tuple) else (r, 0, r.shape[1]) for r in rows]
    cot_rows = [r if isinstance(r, tuple) else (r, 0, r.shape[1]) for r in cot_rows]
    T = rows[0][0].shape[0]
    tb = min(tb, T)
    nr, npar, nc = len(rows), len(pars), len(cot_rows)
    want, want_dt = [], []
    for k, dt in enumerate(row_out):
        for d in (() if dt is None else dt if isinstance(dt, tuple) else (dt,)):
            want.append(k)
            want_dt.append(d)

    def body(*refs):
        i = pl.program_id(0)
        rv = [r[...].astype(F32) for r in refs[:nr]]
        pv = [p[...] for p in refs[nr:nr + npar]]
        cv = [c[...].astype(F32) for c in refs[nr + npar:nr + npar + nc]]
        o_refs = refs[nr + npar + nc:]
        _, vjp = jax.vjp(fn, *rv, *pv)
        grads = vjp(tuple(cot_fn(*cv)))
        for o, k in zip(o_refs[:len(want)], want):
            o[...] = grads[k].astype(o.dtype)
        p_refs = o_refs[len(want):]

        @pl.when(i == 0)
        def _():
            for o in p_refs:
                o[...] = jnp.zeros_like(o)

        for o, g in zip(p_refs, grads[nr:]):
            o[...] += g

    in_specs = [pl.BlockSpec((tb, w), lambda i, c=c: (i, c)) for _, c, w in rows]
    in_specs += [pl.BlockSpec(p.shape, lambda i, n=p.ndim: (0,) * n) for p in pars]
    in_specs += [pl.BlockSpec((tb, w), lambda i, c=c: (i, c)) for _, c, w in cot_rows]
    out_specs = [pl.BlockSpec((tb, rows[k][2]), lambda i: (i, 0)) for k in want]
    out_specs += [pl.BlockSpec(p.shape, lambda i, n=p.ndim: (0,) * n) for p in pars]
    out_shape = [_sds((T, rows[k][2]), d) for k, d in zip(want, want_dt)] + [_sds(p.shape, F32) for p in pars]
    return pl.pallas_call(
        body, grid=(T // tb,), in_specs=in_specs, out_specs=out_specs, out_shape=out_shape,
        compiler_params=_cparams(("arbitrary",)), name=name,
    )(*[r[0] for r in rows], *pars, *[r[0] for r in cot_rows])


def _f_rms(x, g):
    r = lax.rsqrt(jnp.mean(x * x, axis=-1, keepdims=True) + EPS)
    return (x * r * g,)


def _f_lru_gates(u, wa, ba, wx, bx, lam):
    ub = u.astype(BF16)
    ra, rx = [], []
    for k in range(D_MODEL // LANES):
        uk = ub[:, LANES * k:LANES * (k + 1)]
        ra.append(jnp.dot(uk, wa[k].astype(BF16), preferred_element_type=F32))
        rx.append(jnp.dot(uk, wx[k].astype(BF16), preferred_element_type=F32))
    r = jax.nn.sigmoid(jnp.concatenate(ra, axis=1) + ba)
    i = jax.nn.sigmoid(jnp.concatenate(rx, axis=1) + bx)
    log_a = -LRU_C * r * jax.nn.softplus(-lam)
    a = jnp.exp(log_a)
    t = jnp.tanh(log_a)
    b = jnp.sqrt(-2.0 * t / (1.0 - t)) * (i * u)
    return a, b


def _f_post(hl, lgate, yssd, xpre_s, z, d_e, ng):
    ya = jax.nn.gelu(lgate) * hl
    y = (yssd + d_e * jax.nn.silu(xpre_s)) * jax.nn.silu(z)
    gw = SSD_INNER // SSD_GROUPS
    parts = []
    for g in range(SSD_GROUPS):
        yg = y[:, gw * g:gw * (g + 1)]
        parts.append(yg * lax.rsqrt(jnp.mean(yg * yg, axis=-1, keepdims=True) + EPS))
    return ya, jnp.concatenate(parts, axis=1) * ng


def _f_merge(m_a, m_b, gates, bg):
    g = jax.nn.sigmoid(gates + bg)
    return (g[:, :D_MODEL] * m_a + g[:, D_MODEL:] * m_b,)


def _f_act(gu):
    return (jax.nn.silu(gu[:, :D_FF]) * gu[:, D_FF:],)


def _ident(*c):
    return c


def _loss_head(h, target, nf):
    T = h.shape[0]
    tb = min(512, T)

    def body(h_ref, t_ref, nf_ref, dh_ref, dnf_ref, loss_ref, dhb_ref):
        i = pl.program_id(0)

        @pl.when(i == 0)
        def _():
            dnf_ref[...] = jnp.zeros_like(dnf_ref)
            loss_ref[...] = jnp.zeros_like(loss_ref)

        (y,), vjp = jax.vjp(_f_rms, h_ref[...], nf_ref[...])
        err = y - t_ref[...]
        dh, dnf = vjp((err * (1.0 / D_MODEL),))
        dh_ref[...] = dh
        dhb_ref[...] = dh.astype(BF16)
        dnf_ref[...] += dnf
        part = 0.5 * jnp.sum(jnp.mean(err * err, axis=-1, keepdims=True), axis=0, keepdims=True)
        loss_ref[...] += jnp.broadcast_to(part, loss_ref.shape)

    return pl.pallas_call(
        body, grid=(T // tb,),
        in_specs=[pl.BlockSpec((tb, D_MODEL), lambda i: (i, 0)), pl.BlockSpec((tb, D_MODEL), lambda i: (i, 0)),
                  pl.BlockSpec((1, D_MODEL), lambda i: (0, 0))],
        out_specs=[pl.BlockSpec((tb, D_MODEL), lambda i: (i, 0)), pl.BlockSpec((1, D_MODEL), lambda i: (0, 0)),
                   pl.BlockSpec((8, LANES), lambda i: (0, 0)), pl.BlockSpec((tb, D_MODEL), lambda i: (i, 0))],
        out_shape=[_sds((T, D_MODEL), F32), _sds((1, D_MODEL), F32), _sds((8, LANES), F32),
                   _sds((T, D_MODEL), BF16)],
        compiler_params=_cparams(("arbitrary",)), name="loss_head",
    )(h, target, nf)


CONV_TC = 1024
HALO = 16


def _conv_fwd(name, x, w8, b):
    T, C = x.shape
    tb = min(1024, T)
    nb = tb // HALO

    def body(x_ref, xp_ref, w_ref, b_ref, y_ref, sc):
        i = pl.program_id(0)
        xv = x_ref[...].astype(F32)
        sc[pl.ds(HALO, tb), :] = xv
        sc[pl.ds(0, HALO), :] = jnp.where(i > 0, xp_ref[...].astype(F32), 0.0)
        acc = b_ref[...] + w_ref[3:4, :] * xv
        for k in range(3):
            acc = acc + w_ref[k:k + 1, :] * sc[pl.ds(HALO - 3 + k, tb), :]
        y_ref[...] = acc

    return pl.pallas_call(
        body, grid=(T // tb, C // CONV_TC),
        in_specs=[pl.BlockSpec((tb, CONV_TC), lambda i, j: (i, j)),
                  pl.BlockSpec((HALO, CONV_TC), lambda i, j: (jnp.maximum(i * nb - 1, 0), j)),
                  pl.BlockSpec((8, CONV_TC), lambda i, j: (0, j)),
                  pl.BlockSpec((1, CONV_TC), lambda i, j: (0, j))],
        out_specs=pl.BlockSpec((tb, CONV_TC), lambda i, j: (i, j)),
        out_shape=_sds((T, C), F32),
        scratch_shapes=[pltpu.VMEM((tb + HALO, CONV_TC), F32)],
        compiler_params=_cparams(("parallel", "parallel")), name=name,
    )(x, x, w8, b)


def _conv_bwd(name, x, dy, w8):
    T, C = x.shape
    tb = min(512, T)
    nb = tb // HALO
    nt = T // tb

    def body(x_ref, xp_ref, dy_ref, dyn_ref, w_ref, dx_ref, dwb_ref, scx, scd):
        i = pl.program_id(1)
        dyv = dy_ref[...]
        xv = x_ref[...].astype(F32)
        scx[pl.ds(HALO, tb), :] = xv
        scx[pl.ds(0, HALO), :] = jnp.where(i > 0, xp_ref[...].astype(F32), 0.0)
        scd[pl.ds(0, tb), :] = dyv
        scd[pl.ds(tb, HALO), :] = jnp.where(i < nt - 1, dyn_ref[...], 0.0)
        dx = w_ref[3:4, :] * dyv
        rows = []
        for k in range(3):
            dx = dx + w_ref[k:k + 1, :] * scd[pl.ds(3 - k, tb), :]
            rows.append(jnp.sum(dyv * scx[pl.ds(HALO - 3 + k, tb), :], axis=0, keepdims=True))
        rows.append(jnp.sum(dyv * xv, axis=0, keepdims=True))
        rows.append(jnp.sum(dyv, axis=0, keepdims=True))
        rows.append(jnp.zeros((3, CONV_TC), F32))
        dx_ref[...] = dx.astype(dx_ref.dtype)

        @pl.when(i == 0)
        def _():
            dwb_ref[...] = jnp.zeros_like(dwb_ref)

        dwb_ref[...] += jnp.concatenate(rows, axis=0)

    return pl.pallas_call(
        body, grid=(C // CONV_TC, nt),
        in_specs=[pl.BlockSpec((tb, CONV_TC), lambda j, i: (i, j)),
                  pl.BlockSpec((HALO, CONV_TC), lambda j, i: (jnp.maximum(i * nb - 1, 0), j)),
                  pl.BlockSpec((tb, CONV_TC), lambda j, i: (i, j)),
                  pl.BlockSpec((HALO, CONV_TC), lambda j, i: (jnp.minimum((i + 1) * nb, T // HALO - 1), j)),
                  pl.BlockSpec((8, CONV_TC), lambda j, i: (0, j))],
        out_specs=[pl.BlockSpec((tb, CONV_TC), lambda j, i: (i, j)),
                   pl.BlockSpec((8, CONV_TC), lambda j, i: (0, j))],
        out_shape=[_sds((T, C), BF16), _sds((8, C), F32)],
        scratch_shapes=[pltpu.VMEM((tb + HALO, CONV_TC), F32), pltpu.VMEM((tb + HALO, CONV_TC), F32)],
        compiler_params=_cparams(("parallel", "arbitrary")), name=name,
    )(x, x, dy, dy, w8)


SCAN_TB = 1024


def _lru_scan_fwd(a, b):
    T, C = a.shape
    tb = min(SCAN_TB, T)

    def body(a_ref, b_ref, h_ref, hp_ref, carry):
        @pl.when(pl.program_id(0) == 0)
        def _():
            carry[...] = jnp.zeros_like(carry)

        def group(gi, h):
            r0 = pl.multiple_of(gi * 8, 8)
            at = a_ref[pl.ds(r0, 8), :]
            bt = b_ref[pl.ds(r0, 8), :]
            hs, hps = [], []
            for r in range(8):
                hps.append(h)
                h = at[r:r + 1, :] * h + bt[r:r + 1, :]
                hs.append(h)
            h_ref[pl.ds(r0, 8), :] = jnp.concatenate(hs, axis=0)
            hp_ref[pl.ds(r0, 8), :] = jnp.concatenate(hps, axis=0)
            return h

        carry[0:1, :] = lax.fori_loop(0, tb // 8, group, carry[0:1, :])

    spec = pl.BlockSpec((tb, C), lambda i: (i, 0))
    return pl.pallas_call(
        body, grid=(T // tb,), in_specs=[spec, spec], out_specs=[spec, spec],
        out_shape=[_sds((T, C), F32), _sds((T, C), F32)],
        scratch_shapes=[pltpu.VMEM((8, C), F32)],
        compiler_params=_cparams(("arbitrary",)), name="lru_scan_fwd",
    )(a, b)


def _lru_scan_bwd(a, dh):
    T, C = a.shape
    tb = min(SCAN_TB, T)
    nt = T // tb

    def body(a_ref, dh_ref, g_ref, carry):
        @pl.when(pl.program_id(0) == 0)
        def _():
            carry[...] = jnp.zeros_like(carry)

        def group(gi, c):
            r0 = pl.multiple_of((tb // 8 - 1 - gi) * 8, 8)
            at = a_ref[pl.ds(r0, 8), :]
            dt = dh_ref[pl.ds(r0, 8), :]
            gs = [None] * 8
            for r in range(7, -1, -1):
                g = dt[r:r + 1, :] + c
                c = at[r:r + 1, :] * g
                gs[r] = g
            g_ref[pl.ds(r0, 8), :] = jnp.concatenate(gs, axis=0)
            return c

        carry[0:1, :] = lax.fori_loop(0, tb // 8, group, carry[0:1, :])

    spec = pl.BlockSpec((tb, C), lambda i: (nt - 1 - i, 0))
    return pl.pallas_call(
        body, grid=(nt,), in_specs=[spec, spec], out_specs=spec,
        out_shape=_sds((T, C), F32),
        scratch_shapes=[pltpu.VMEM((8, C), F32)],
        compiler_params=_cparams(("arbitrary",)), name="lru_scan_bwd",
    )(a, dh)


def _ssd_chunk(xpre, dtraw, state, dtb, alog):
    xc = jax.nn.silu(xpre)
    xs = xc[:, :SSD_INNER]
    bm = xc[:, SSD_INNER:SSD_INNER + SSD_GROUPS * SSD_STATE]
    cm = xc[:, SSD_INNER + SSD_GROUPS * SSD_STATE:]
    dt = jax.nn.softplus(dtraw + dtb)
    a = dt * (-jnp.exp(alog))
    ltri = (lax.broadcasted_iota(jnp.int32, (CHUNK, CHUNK), 0)
            >= lax.broadcasted_iota(jnp.int32, (CHUNK, CHUNK), 1)).astype(F32)
    a_cs = jnp.dot(ltri, a, precision=HIGHEST, preferred_element_type=F32)
    npair = SSD_HEADS // 2
    pi = lax.broadcasted_iota(jnp.int32, (npair, LANES), 0)
    hi = lax.broadcasted_iota(jnp.int32, (npair, LANES), 1)
    sel_even = (hi == 2 * pi).astype(F32)
    sel_odd = (hi == 2 * pi + 1).astype(F32)
    top = lax.broadcasted_iota(jnp.int32, (2 * CHUNK, LANES), 0) < CHUNK

    def pair_transpose(v):
        v2 = jnp.concatenate([v, v], axis=0)
        dn = (((1,), (1,)), ((), ()))
        return (lax.dot_general(sel_even, jnp.where(top, v2, 0.0), dn, precision=HIGHEST, preferred_element_type=F32)
                + lax.dot_general(sel_odd, jnp.where(top, 0.0, v2), dn, precision=HIGHEST, preferred_element_type=F32))

    a_t2 = pair_transpose(a_cs)
    dt_t2 = pair_transpose(dt)
    a_last = a_cs[CHUNK - 1:CHUNK, :]
    dte = jnp.exp(a_last - a_cs) * dt
    cd = jnp.exp(a_last)
    lane = lax.broadcasted_iota(jnp.int32, (CHUNK, LANES), 1)
    left = lane < SSD_HEAD_DIM
    right = jnp.logical_not(left)
    left1 = left[0:1]
    tril2 = lax.broadcasted_iota(jnp.int32, (CHUNK, LANES), 0) >= (lane & (SSD_HEAD_DIM - 1))
    gw = SSD_INNER // SSD_GROUPS
    ys, news = [], []
    for g in range(SSD_GROUPS):
        bg = bm[:, SSD_STATE * g:SSD_STATE * (g + 1)].astype(BF16)
        cg = cm[:, SSD_STATE * g:SSD_STATE * (g + 1)].astype(BF16)
        bg2 = jnp.concatenate([bg, bg], axis=0)
        scores2 = lax.dot_general(cg, bg2, (((1,), (1,)), ((), ())), preferred_element_type=F32)
        coff = jnp.dot(cg, state[:, gw * g:gw * (g + 1)].astype(BF16), preferred_element_type=F32)
        for j in range(gw // LANES):
            lo = gw * g + LANES * j
            p = lo // LANES
            h0 = 2 * p
            xp = xs[:, lo:lo + LANES]
            col = jnp.where(left, a_cs[:, h0:h0 + 1], a_cs[:, h0 + 1:h0 + 2])
            dm = jnp.exp(jnp.where(tril2, col - a_t2[p:p + 1, :], -1e30)) * dt_t2[p:p + 1, :]
            sd = (scores2 * dm).astype(BF16)
            x_bd = jnp.concatenate([jnp.where(left, xp, 0.0), jnp.where(right, xp, 0.0)], axis=0).astype(BF16)
            acc = jnp.dot(sd, x_bd, preferred_element_type=F32)
            ys.append(acc + coff[:, LANES * j:LANES * (j + 1)] * jnp.exp(col))
            dtee = jnp.where(left, dte[:, h0:h0 + 1], dte[:, h0 + 1:h0 + 2])
            xw = (xp * dtee).astype(BF16)
            cde = jnp.where(left1, cd[:, h0:h0 + 1], cd[:, h0 + 1:h0 + 2])
            news.append(state[:, lo:lo + LANES] * cde
                        + lax.dot_general(bg, xw, (((0,), (0,)), ((), ())), preferred_element_type=F32))
    return jnp.concatenate(ys, axis=1), jnp.concatenate(news, axis=1)


def _ssd_fwd(name, xpre, dtraw, dtb, alog, rider=None):
    T = xpre.shape[0]
    n = T // CHUNK
    r_args, r_in, r_out, r_shape, r_scratch = _rider_parts(rider)

    def body(xp, dr, dtb_ref, al_ref, y_ref, st_ref, state):
        @pl.when(pl.program_id(0) == 0)
        def _():
            state[...] = jnp.zeros_like(state)

        st_ref[0] = state[...]
        y, new = _ssd_chunk(xp[...], dr[...], state[...], dtb_ref[...], al_ref[...])
        y_ref[...] = y
        state[...] = new

    small = pl.BlockSpec((1, LANES), lambda c: (0, 0))
    return pl.pallas_call(
        _with_rider(body, 4, 2, rider, (n,)), grid=(n,),
        in_specs=[pl.BlockSpec((CHUNK, SSD_CONV_DIM), lambda c: (c, 0)),
                  pl.BlockSpec((CHUNK, DT_PAD), lambda c: (c, 0)), small, small] + r_in,
        out_specs=[pl.BlockSpec((CHUNK, SSD_INNER), lambda c: (c, 0)),
                   pl.BlockSpec((1, SSD_STATE, SSD_INNER), lambda c: (c, 0, 0))] + r_out,
        out_shape=[_sds((T, SSD_INNER), F32), _sds((n, SSD_STATE, SSD_INNER), F32)] + r_shape,
        scratch_shapes=[pltpu.VMEM((SSD_STATE, SSD_INNER), F32)] + r_scratch,
        compiler_params=_cparams(("arbitrary",)), name=name,
    )(xpre, dtraw, dtb, alog, *r_args)


def _ssd_bwd(name, xpre, dtraw, states, dy, dxs_extra, dtb, alog, rider=None):
    T = xpre.shape[0]
    n = T // CHUNK
    r_args, r_in, r_out, r_shape, r_scratch = _rider_parts(rider)

    def body(xp, dr, st, dy_ref, dx_ref, dtb_ref, al_ref, dxp_ref, ddr_ref, ddtb_ref, dal_ref, dstate):
        @pl.when(pl.program_id(0) == 0)
        def _():
            dstate[...] = jnp.zeros_like(dstate)
            ddtb_ref[...] = jnp.zeros_like(ddtb_ref)
            dal_ref[...] = jnp.zeros_like(dal_ref)

        _, vjp = jax.vjp(_ssd_chunk, xp[...], dr[...], st[0], dtb_ref[...], al_ref[...])
        dxp, ddr, ds, db, da = vjp((dy_ref[...], dstate[...]))
        dxp_ref[:, :SSD_INNER] = dxp[:, :SSD_INNER] + dx_ref[...]
        dxp_ref[:, SSD_INNER:] = dxp[:, SSD_INNER:]
        ddr_ref[...] = ddr.astype(ddr_ref.dtype)
        dstate[...] = ds
        ddtb_ref[...] += db
        dal_ref[...] += da

    def rev(c):
        return (n - 1 - c, 0)

    small = pl.BlockSpec((1, LANES), lambda c: (0, 0))
    return pl.pallas_call(
        _with_rider(body, 7, 4, rider, (n,)), grid=(n,),
        in_specs=[pl.BlockSpec((CHUNK, SSD_CONV_DIM), rev), pl.BlockSpec((CHUNK, DT_PAD), rev),
                  pl.BlockSpec((1, SSD_STATE, SSD_INNER), lambda c: (n - 1 - c, 0, 0)),
                  pl.BlockSpec((CHUNK, SSD_INNER), rev), pl.BlockSpec((CHUNK, SSD_INNER), rev), small, small] + r_in,
        out_specs=[pl.BlockSpec((CHUNK, SSD_CONV_DIM), rev), pl.BlockSpec((CHUNK, DT_PAD), rev), small, small] + r_out,
        out_shape=[_sds((T, SSD_CONV_DIM), F32), _sds((T, DT_PAD), BF16), _sds((1, LANES), F32),
                   _sds((1, LANES), F32)] + r_shape,
        scratch_shapes=[pltpu.VMEM((SSD_STATE, SSD_INNER), F32)] + r_scratch,
        compiler_params=_cparams(("arbitrary",)), name=name,
    )(xpre, dtraw, states, dy, dxs_extra, dtb, alog, *r_args)


HBM_SPEC = pl.BlockSpec(memory_space=pltpu.HBM)
N_PEER = N_DEV - 1


def _position():
    return lax.axis_index("x"), lax.axis_index("y"), lax.axis_index("c")


def _all_gather(name, blks):
    return _exchange_call(name, _gather_phases, blks, [_sds((N_DEV,) + b.shape, b.dtype) for b in blks])


def _scatter_exchange(name, gs):
    return _exchange_call(name, _scatter_phases, gs, [_sds(g.shape, g.dtype) for g in gs])


def _exchange_scratch(na):
    return [pltpu.SemaphoreType.DMA((na * N_PEER,)), pltpu.SemaphoreType.DMA((na * N_PEER,)),
            pltpu.SemaphoreType.DMA((na,))]


def _exchange_call(name, phases, arrays, out_shape):
    na = len(arrays)

    def body(*refs):
        begin, finish = phases(refs[:na], refs[na:2 * na], *refs[2 * na:])
        begin()
        finish()

    return pl.pallas_call(
        body, out_shape=out_shape, in_specs=[HBM_SPEC] * na, out_specs=[HBM_SPEC] * na,
        scratch_shapes=_exchange_scratch(na), name=name,
    )(*arrays)


def _gather_phases(x_refs, out_refs, send_sems, recv_sems, local_sems):
    na = len(x_refs)
    x, y, c = _position()
    me, sibling = (x, y, c), (x, y, 1 - c)
    chips = [(1 - x, y), (x, 1 - y), (1 - x, 1 - y)]

    def slot(a, px, py, pc):
        return out_refs[a].at[4 * px + 2 * py + pc]

    def copy(a, k, block, to, src=None):
        return pltpu.make_async_remote_copy(
            src_ref=slot(a, *block) if src is None else src, dst_ref=slot(a, *block),
            send_sem=send_sems.at[a * N_PEER + k], recv_sem=recv_sems.at[a * N_PEER + k],
            device_id=to, device_id_type=MESH)

    mine = [pltpu.make_async_copy(x_refs[a], slot(a, *me), local_sems.at[a]) for a in range(na)]
    first = []
    for a in range(na):
        first.append(copy(a, 0, me, sibling, src=x_refs[a]))
        first += [copy(a, 1 + j, me, (*chip, c), src=x_refs[a]) for j, chip in enumerate(chips)]
    passed = [copy(a, 4 + j, (*chip, c), sibling) for j, chip in enumerate(chips) for a in range(na)]

    def begin():
        for cp in mine + first:
            cp.start()

    def finish():
        for j, chip in enumerate(chips):
            for a in range(na):
                copy(a, 1 + j, (*chip, c), me).wait_recv()
                passed[j * na + a].start()
        for a in range(na):
            copy(a, 0, sibling, me).wait_recv()
            for j, chip in enumerate(chips):
                copy(a, 4 + j, (*chip, 1 - c), me).wait_recv()
        for cp in first + passed:
            cp.wait_send()
        for cp in mine:
            cp.wait()

    return begin, finish


def _scatter_phases(g_refs, q_refs, send_sems, recv_sems, local_sems):
    na = len(g_refs)
    x, y, c = _position()
    me = 4 * x + 2 * y + c
    mine, sends, recvs = [], [], []
    for a in range(na):
        mine.append(pltpu.make_async_copy(g_refs[a].at[me], q_refs[a].at[me], local_sems.at[a]))
        for k in range(1, N_DEV):
            px, py, pc = (x + (k >> 2)) % 2, (y + ((k >> 1) & 1)) % 2, (c + (k & 1)) % 2
            peer = 4 * px + 2 * py + pc
            sem = a * N_PEER + k - 1
            sends.append(pltpu.make_async_remote_copy(
                src_ref=g_refs[a].at[peer], dst_ref=q_refs[a].at[me], send_sem=send_sems.at[sem],
                recv_sem=recv_sems.at[sem], device_id=(px, py, pc), device_id_type=MESH))
            recvs.append(pltpu.make_async_remote_copy(
                src_ref=g_refs[a].at[me], dst_ref=q_refs[a].at[peer], send_sem=send_sems.at[sem],
                recv_sem=recv_sems.at[sem], device_id=(px, py, pc), device_id_type=MESH))

    def begin():
        for cp in mine + sends:
            cp.start()

    def finish():
        for cp in recvs:
            cp.wait_recv()
        for cp in sends:
            cp.wait_send()
        for cp in mine:
            cp.wait()

    return begin, finish


def _with_rider(body, n_in, n_out, rider, grid):
    if rider is None:
        return body
    phases, arrays, _ = rider
    na = len(arrays)

    def carried(*refs):
        ins, r_in = refs[:n_in], refs[n_in:n_in + na]
        outs = refs[n_in + na:n_in + na + n_out]
        r_out = refs[n_in + na + n_out:n_in + 2 * na + n_out]
        scratch, r_scratch = refs[n_in + 2 * na + n_out:-3], refs[-3:]
        begin, finish = phases(r_in, r_out, *r_scratch)
        first = last = None
        for d, g in enumerate(grid):
            at_start, at_end = pl.program_id(d) == 0, pl.program_id(d) == g - 1
            first = at_start if first is None else first & at_start
            last = at_end if last is None else last & at_end
        pl.when(first)(begin)
        body(*ins, *outs, *scratch)
        pl.when(last)(finish)

    return carried


def _rider_parts(rider):
    if rider is None:
        return [], [], [], [], []
    _, arrays, out_shape = rider
    na = len(arrays)
    return list(arrays), [HBM_SPEC] * na, [HBM_SPEC] * na, list(out_shape), _exchange_scratch(na)


def _gather_rider(blks):
    return (_gather_phases, blks, [_sds((N_DEV,) + b.shape, b.dtype) for b in blks])


def _scatter_rider(gs):
    return (_scatter_phases, gs, [_sds(g.shape, g.dtype) for g in gs])


def _adamw_sum(name, w, m, v, recvs, rider=None):
    L, r, c = w.shape
    tr = r if r <= 512 else max(t for t in range(16, 257, 16) if r % t == 0)
    assert len(recvs) == L
    nr = r // tr

    def body(w_ref, m_ref, v_ref, *refs):
        q_refs = refs[:L]
        g_ref, d_ref, m2_ref, v2_ref = refs[L:]
        l = pl.program_id(0)
        for ll in range(L):
            @pl.when(l == ll)
            def _(q_ref=q_refs[ll]):
                gv = q_ref[0].astype(F32)
                for s in range(1, N_DEV):
                    gv = gv + q_ref[s].astype(F32)
                m2 = ADAM_B1 * m_ref[0] + (1.0 - ADAM_B1) * gv
                v2 = ADAM_B2 * v_ref[0] + (1.0 - ADAM_B2) * jnp.square(gv)
                m_hat = m2 / (1.0 - ADAM_B1 ** ADAM_STEP)
                v_hat = v2 / (1.0 - ADAM_B2 ** ADAM_STEP)
                g_ref[0] = gv
                d_ref[0] = -ADAM_LR * (m_hat / (jnp.sqrt(v_hat) + ADAM_EPS) + ADAM_WD * w_ref[0])
                m2_ref[0] = m2
                v2_ref[0] = v2

    def q_index(ll):
        return lambda l, i: (0, jnp.where(l == ll, i, jnp.where(l > ll, nr - 1, 0)), 0)

    spec = pl.BlockSpec((1, tr, c), lambda l, i: (l, i, 0))
    r_args, r_in, r_out, r_shape, r_scratch = _rider_parts(rider)
    return pl.pallas_call(
        _with_rider(body, 3 + L, 4, rider, (L, nr)), grid=(L, nr),
        in_specs=[spec] * 3 + [pl.BlockSpec((N_DEV, tr, c), q_index(ll)) for ll in range(L)] + r_in,
        out_specs=[spec] * 4 + r_out, out_shape=[_sds((L, r, c), F32)] * 4 + r_shape, scratch_shapes=r_scratch,
        compiler_params=_cparams(("arbitrary", "arbitrary")), name=name,
    )(w, m, v, *recvs, *r_args)


def _sum_blocks(name, q, tr):
    R = q.shape[0] // N_DEV
    W = q.shape[1]
    tr = min(tr, R)
    assert R % tr == 0
    nb = R // tr

    def body(*refs):
        acc = refs[0][...].astype(F32)
        for r in refs[1:N_DEV]:
            acc = acc + r[...].astype(F32)
        refs[N_DEV][...] = acc

    return pl.pallas_call(
        body, grid=(nb,),
        in_specs=[pl.BlockSpec((tr, W), lambda i, s=s: (s * nb + i, 0)) for s in range(N_DEV)],
        out_specs=pl.BlockSpec((tr, W), lambda i: (i, 0)), out_shape=_sds((R, W), F32),
        compiler_params=_cparams(("parallel",)), name=name,
    )(*([q] * N_DEV))


def _adamw(name, w, g, m, v):
    R, C = w.shape
    tr = R
    if R > 512:
        tr = max(t for t in range(8, 513, 8) if R % t == 0)

    def body(w_ref, g_ref, m_ref, v_ref, d_ref, m2_ref, v2_ref):
        gv = g_ref[...]
        m2 = ADAM_B1 * m_ref[...] + (1.0 - ADAM_B1) * gv
        v2 = ADAM_B2 * v_ref[...] + (1.0 - ADAM_B2) * jnp.square(gv)
        m_hat = m2 / (1.0 - ADAM_B1 ** ADAM_STEP)
        v_hat = v2 / (1.0 - ADAM_B2 ** ADAM_STEP)
        d_ref[...] = -ADAM_LR * (m_hat / (jnp.sqrt(v_hat) + ADAM_EPS) + ADAM_WD * w_ref[...])
        m2_ref[...] = m2
        v2_ref[...] = v2

    spec = pl.BlockSpec((tr, C), lambda i: (i, 0))
    return pl.pallas_call(
        body, grid=(R // tr,), in_specs=[spec] * 4, out_specs=[spec] * 3,
        out_shape=[_sds((R, C), F32)] * 3, compiler_params=_cparams(("parallel",)), name=name,
    )(w, g, m, v)


BIG = ("w_in", "w_branch", "w_out", "w_ffn_in", "w_ffn_out")
BIG_ROW_SHARDED = {"w_in": False, "w_branch": True, "w_out": True, "w_ffn_in": True, "w_ffn_out": True}
TRANSPOSED = ("w_ffn_in",)
SMALL = ("norm1_g", "b_gate", "lru_conv_w", "lru_conv_b", "lru_w_a", "lru_b_a", "lru_w_x", "lru_b_x", "lru_lambda",
         "ssd_conv_w", "ssd_conv_b", "ssd_dt_bias", "ssd_A_log", "ssd_D", "ssd_norm_g", "norm2_g", "norm_f")
SMALL_ROWS = 256


def _pack(arrs, dtype, row_mult):
    parts = []
    for a in arrs:
        f = a.reshape(-1).astype(dtype)
        pad = (-f.shape[0]) % LANES
        if pad:
            f = jnp.concatenate([f, jnp.zeros((pad,), dtype)])
        parts.append(f)
    f = jnp.concatenate(parts)
    pad = (-f.shape[0]) % (LANES * row_mult)
    if pad:
        f = jnp.concatenate([f, jnp.zeros((pad,), dtype)])
    return f.reshape(-1, LANES)


def _unpack(flat, shapes, lead=()):
    f = flat.reshape(lead + (-1,))
    out, off = [], 0
    for s in shapes:
        n = int(np.prod(s))
        out.append(f[..., off:off + n].reshape(lead + tuple(s)))
        off += n + (-n) % LANES
    return out


def _full_from_shards(name, st):
    if BIG_ROW_SHARDED[name]:
        return st.reshape((-1,) + st.shape[2:])
    return jnp.transpose(st, (1, 0, 2)).reshape(st.shape[1], -1)


def _shards_from_full(name, full):
    if BIG_ROW_SHARDED[name]:
        return full.reshape((N_DEV, full.shape[0] // N_DEV) + full.shape[1:])
    return jnp.transpose(full.reshape(full.shape[0], N_DEV, -1), (1, 0, 2))


def _block_diag_tiles(w):
    z = jnp.zeros((8, 64, 64), w.dtype)
    w2 = w.reshape(8, 2, 64, 64)
    top = jnp.concatenate([w2[:, 0], z], axis=2)
    bot = jnp.concatenate([z, w2[:, 1]], axis=2)
    return jnp.concatenate([top, bot], axis=1)


def _block_diag_untile(t):
    return jnp.stack([t[:, :64, :64], t[:, 64:, 64:]], axis=1).reshape(16, 64, 64)


def _pad_lanes(a, width):
    return jnp.concatenate([a, jnp.zeros(a.shape[:-1] + (width - a.shape[-1],), a.dtype)], axis=-1)


def _pad_rows8(w):
    return jnp.concatenate([w, jnp.zeros((8 - w.shape[0],) + w.shape[1:], w.dtype)], axis=0)


REST = BIG[1:]


def _in_weights(shards):
    w_in = _full_from_shards("w_in", shards)
    seg = [w_in[:, IN_OFFS[k]:IN_OFFS[k + 1]] for k in range(6)]
    return {"w_lx": seg[0], "w_lg": seg[1], "w_z": seg[2], "w_xbc": seg[3], "w_dt": _pad_lanes(seg[4], DT_PAD),
            "w_g": seg[5]}


def _rest_weights(shards):
    full = {n: _full_from_shards(n, st) for n, st in zip(REST, shards)}
    return {"w_bra": full["w_branch"][:D_MODEL], "w_brb": full["w_branch"][D_MODEL:], "w_out": full["w_out"],
            "w_ffn_in_t": full["w_ffn_in"], "w_ffn_out": full["w_ffn_out"]}


def _small_params(l, small):
    p = {}
    for n in ("norm1_g", "b_gate", "lru_conv_b", "lru_b_a", "lru_b_x", "lru_lambda", "ssd_conv_b", "ssd_norm_g", "norm2_g"):
        p[n] = small[n][l].reshape(1, -1)
    p["lru_conv_w8"] = _pad_rows8(small["lru_conv_w"][l])
    p["ssd_conv_w8"] = _pad_rows8(small["ssd_conv_w"][l])
    p["wa"] = _block_diag_tiles(small["lru_w_a"][l])
    p["wx"] = _block_diag_tiles(small["lru_w_x"][l])
    p["dtb"] = _pad_lanes(small["ssd_dt_bias"][l].reshape(1, -1), DT_PAD)
    p["alog"] = _pad_lanes(small["ssd_A_log"][l].reshape(1, -1), DT_PAD)
    p["d_e"] = jnp.repeat(small["ssd_D"][l], SSD_HEAD_DIM).reshape(1, -1)
    return p


def _layer_fwd(l, h, p, riders):
    n = f"l{l}_"
    s = {"h_in": h}
    got = {}
    (xn,) = _rowwise_fwd(n + "rms1", _f_rms, [h], [p["norm1_g"]], [(D_MODEL, BF16)], 512)
    s["xn"] = xn
    for k in ("lx", "lg", "z", "xbc", "g", "dt"):
        s[k] = _mm(n + "in_" + k, [(xn, p["w_" + k])], "nn", out_dtype=F32 if k == "dt" else BF16)
    s["u"] = _conv_fwd(n + "lru_conv", s["lx"], p["lru_conv_w8"], p["lru_conv_b"])
    lru_pars = [p["wa"], p["lru_b_a"], p["wx"], p["lru_b_x"], p["lru_lambda"]]
    s["a"], b = _rowwise_fwd(n + "lru_gates", _f_lru_gates, [s["u"]], lru_pars, [(D_MODEL, F32)] * 2, 512)
    s["hl"], s["hprev"] = _lru_scan_fwd(s["a"], b)
    s["xpre"] = _conv_fwd(n + "ssd_conv", s["xbc"], p["ssd_conv_w8"], p["ssd_conv_b"])
    s["yssd"], s["states"], *got["ssd"] = _ssd_fwd(n + "ssd", s["xpre"], s["dt"], p["dtb"], p["alog"],
                                                   riders.get("ssd"))
    if "ssd" in riders:
        p.update(_rest_weights(got["ssd"]))
    post_rows = [s["hl"], s["lg"], s["yssd"], (s["xpre"], 0, SSD_INNER), s["z"]]
    s["ya"], s["yb"], *got["post"] = _rowwise_fwd(n + "post", _f_post, post_rows, [p["d_e"], p["ssd_norm_g"]],
                                                  [(D_MODEL, BF16), (SSD_INNER, BF16)], 128, riders.get("post"))
    s["ma"] = _mm(n + "br_a", [(s["ya"], p["w_bra"])], "nn", out_dtype=BF16)
    s["mb"] = _mm(n + "br_b", [(s["yb"], p["w_brb"])], "nn", out_dtype=BF16)
    s["h_mid"], s["merged"] = _merge_out(n + "out", s["ma"], s["mb"], s["g"], p["b_gate"], p["w_out"], h)
    (s["xn2"],) = _rowwise_fwd(n + "rms2", _f_rms, [s["h_mid"]], [p["norm2_g"]], [(D_MODEL, BF16)], 512)
    s["gu"] = _mm(n + "ffn_in", [(s["xn2"], p["w_ffn_in_t"])], "nt", out_dtype=BF16, rider=riders.get("ffn_in"))
    if "ffn_in" in riders:
        s["gu"], got["ffn_in"] = s["gu"]
    h_out, s["act"], carried = _swiglu_out(n + "ffn_out", s["gu"], p["w_ffn_out"], s["h_mid"], riders.get("ffn_out"))
    if "ffn_out" in riders:
        got["ffn_out"] = carried
    return h_out, s, got


def _f_rms_res(x, g):
    return _f_rms(x, g)[0], x


def _layer_bwd(l, dh, dh_b, s, p, above, first_layer):
    n = f"l{l}_b_"
    gw, gs = {}, {}
    d_act = _mm(n + "d_act", [(dh_b, p["w_ffn_out"])], "nt", out_dtype=BF16)
    gw["w_ffn_out"] = _mm_tn(n + "dw_ffn_out", s["act"], dh_b)
    (d_gu,) = _rowwise_bwd(n + "act", _f_act, [s["gu"]], [], [d_act], _ident, [BF16], 256)
    d_xn2 = _mm(n + "d_xn2", [(d_gu, p["w_ffn_in_t"])], "nn")
    gw["w_ffn_in"] = _mm_tn(n + "dw_ffn_in", d_gu, s["xn2"])
    dh_mid, dh_mid_b, gs["norm2_g"] = _rowwise_bwd(n + "rms2", _f_rms_res, [s["h_mid"]], [p["norm2_g"]], [d_xn2, dh],
                                                   _ident, [(F32, BF16)], 512)
    d_merged = _mm(n + "d_merged", [(dh_mid_b, p["w_out"])], "nt", out_dtype=BF16)
    gw["w_out"] = _mm_tn(n + "dw_out", s["merged"], dh_mid_b)
    d_ma, d_mb, d_g, gs["b_gate"] = _rowwise_bwd(n + "merge", _f_merge, [s["ma"], s["mb"], s["g"]], [p["b_gate"]],
                                                 [d_merged], _ident, [BF16, BF16, BF16], 512)
    d_ya = _mm(n + "d_ya", [(d_ma, p["w_bra"])], "nt", out_dtype=BF16)
    d_yb = _mm(n + "d_yb", [(d_mb, p["w_brb"])], "nt", out_dtype=BF16)
    gw["w_branch"] = jnp.concatenate([_mm_tn(n + "dw_bra", s["ya"], d_ma), _mm_tn(n + "dw_brb", s["yb"], d_mb)], axis=0)
    post_rows = [s["hl"], s["lg"], s["yssd"], (s["xpre"], 0, SSD_INNER), s["z"]]
    d_hl, d_lg, d_yssd, d_xs, d_z, d_de, gs["ssd_norm_g"] = _rowwise_bwd(
        n + "post", _f_post, post_rows, [p["d_e"], p["ssd_norm_g"]], [d_ya, d_yb], _ident,
        [F32, BF16, F32, F32, BF16], 128)
    gs["ssd_D"] = d_de.reshape(SSD_HEADS, SSD_HEAD_DIM).sum(axis=1)
    contrib = [_shards_from_full(k, gw[k]) for k in REST] + ([] if above is None else [above])
    d_xpre, d_dt, d_dtb, d_alog, *carried = _ssd_bwd(n + "ssd", s["xpre"], s["dt"], s["states"], d_yssd, d_xs,
                                                     p["dtb"], p["alog"], _scatter_rider(contrib))
    arrived = dict(zip(REST, carried))
    arrived_above = None if above is None else carried[len(REST)]
    gs["ssd_dt_bias"] = d_dtb[0, :SSD_HEADS]
    gs["ssd_A_log"] = d_alog[0, :SSD_HEADS]
    d_xbc, dwb = _conv_bwd(n + "ssd_conv", s["xbc"], d_xpre, p["ssd_conv_w8"])
    gs["ssd_conv_w"], gs["ssd_conv_b"] = dwb[:4], dwb[4]
    g_scan = _lru_scan_bwd(s["a"], d_hl)
    lru_pars = [p["wa"], p["lru_b_a"], p["wx"], p["lru_b_x"], p["lru_lambda"]]
    d_u, d_wa, gs["lru_b_a"], d_wx, gs["lru_b_x"], gs["lru_lambda"] = _rowwise_bwd(
        n + "lru_gates", _f_lru_gates, [s["u"]], lru_pars, [g_scan, s["hprev"]],
        lambda g, hp: (g * hp, g), [F32], 256)
    gs["lru_w_a"], gs["lru_w_x"] = _block_diag_untile(d_wa), _block_diag_untile(d_wx)
    d_lx, dwb = _conv_bwd(n + "lru_conv", s["lx"], d_u, p["lru_conv_w8"])
    gs["lru_conv_w"], gs["lru_conv_b"] = dwb[:4], dwb[4]
    segs = [("lx", d_lx), ("lg", d_lg), ("z", d_z), ("xbc", d_xbc), ("dt", d_dt), ("g", d_g)]
    dws = [_mm_tn(n + "dw_in_" + k, s["xn"], d) for k, d in segs]
    dws[4] = dws[4][:, :IN_WIDTHS[4]]
    below = _shards_from_full("w_in", jnp.concatenate(dws, axis=1))
    pairs_a = [(d, p["w_" + k]) for k, d in segs[:3]]
    pairs_b = [(d, p["w_" + k]) for k, d in segs[3:]]
    if first_layer:
        half = below.shape[1] // 2
        d_xn, (top,) = _mm(n + "d_xn_a", pairs_a, "nt", rider=_scatter_rider([below[:, :half]]))
        d_xn, (bottom,) = _mm(n + "d_xn_b", pairs_b, "nt", res=d_xn, rider=_scatter_rider([below[:, half:]]))
        arrived["w_in"] = jnp.concatenate([top, bottom], axis=1)
        below = None
    else:
        d_xn = _mm(n + "d_xn_a", pairs_a, "nt")
        d_xn = _mm(n + "d_xn_b", pairs_b, "nt", res=d_xn)
    dh_in, dh_in_b, gs["norm1_g"] = _rowwise_bwd(n + "rms1", _f_rms_res, [s["h_in"]], [p["norm1_g"]], [d_xn, dh_mid],
                                                 _ident, [(F32, BF16)], 512)
    for k in ("norm1_g", "norm2_g", "b_gate", "ssd_norm_g", "lru_b_a", "lru_b_x", "lru_lambda"):
        gs[k] = gs[k].reshape(-1)
    return dh_in, dh_in_b, gs, arrived, arrived_above, below, gw


def _step(inp):
    x = inp["x"][0]
    target = inp["loss_target"][0]
    dev = 4 * lax.axis_index("x") + 2 * lax.axis_index("y") + lax.axis_index("c")

    def mine(l, names):
        return [(inp[n][l].T if n in TRANSPOSED else inp[n][l]).astype(BF16) for n in names]

    first = _all_gather("gather_first", mine(0, ("w_in",)) + [inp["lru_conv_w"], inp["ssd_conv_w"]])
    small = {n: inp[n] for n in SMALL}
    small["lru_conv_w"] = jnp.moveaxis(first[1], 0, 2).reshape(DEPTH, 4, -1)
    small["ssd_conv_w"] = jnp.moveaxis(first[2], 0, 2).reshape(DEPTH, 4, -1)
    shards = {"w_in": first[0]}

    h, saved, params = x, [], []
    for l in range(DEPTH):
        p = _small_params(l, small)
        p.update(_in_weights(shards["w_in"]))
        riders = {}
        if l == 0:
            riders["ssd"] = _gather_rider(mine(0, REST))
        else:
            p.update(_rest_weights([shards[n] for n in REST]))
        if l + 1 < DEPTH:
            riders["post"] = _gather_rider(mine(l + 1, ("w_ffn_in", "w_ffn_out")))
            riders["ffn_in"] = _gather_rider(mine(l + 1, ("w_in",)))
            riders["ffn_out"] = _gather_rider(mine(l + 1, ("w_branch", "w_out")))
        h, s, got = _layer_fwd(l, h, p, riders)
        if l + 1 < DEPTH:
            shards = {"w_ffn_in": got["post"][0], "w_ffn_out": got["post"][1], "w_in": got["ffn_in"][0],
                      "w_branch": got["ffn_out"][0], "w_out": got["ffn_out"][1]}
        saved.append(s)
        params.append(p)
    dh, d_nf, loss_acc, dh_b = _loss_head(h, target, small["norm_f"].reshape(1, -1))
    loss = lax.psum(loss_acc[0, 0], ("x", "y", "c"))

    gss, received = [None] * DEPTH, [None] * DEPTH
    handed_down = None
    for l in reversed(range(DEPTH)):
        dh, dh_b, gss[l], received[l], arrived_above, handed_down, _ = _layer_bwd(
            l, dh, dh_b, saved[l], params[l], handed_down, l == 0)
        if arrived_above is not None:
            received[l + 1]["w_in"] = arrived_above
    grad_x = dh[None]
    out = {"loss": loss, "grad_x": grad_x}
    small_full = {n: jnp.stack([gss[l][n] for l in range(DEPTH)]) for n in SMALL if n != "norm_f"}
    small_full["norm_f"] = d_nf.reshape(-1)
    part = _pack([small_full[n] for n in SMALL], F32, SMALL_ROWS)
    for n in BIG:
        flip = (lambda a: jnp.transpose(a, (0, 2, 1))) if n in TRANSPOSED else (lambda a: a)
        res = _adamw_sum("adamw_" + n, flip(inp[n]), flip(inp["m_" + n]), flip(inp["v_" + n]),
                         [received[l][n] for l in range(DEPTH)], _gather_rider([part]) if n == "w_in" else None)
        if n == "w_in":
            everyone = res[4]
        out["grad_" + n], out["delta_" + n], out["new_m_" + n], out["new_v_" + n] = [flip(a) for a in res[:4]]
    g_small_flat = _sum_blocks("sum_small_grads", everyone.reshape(-1, LANES), SMALL_ROWS)
    g_small = dict(zip(SMALL, _unpack(g_small_flat, [small_full[n].shape for n in SMALL])))
    for n in ("lru_conv_w", "ssd_conv_w"):
        w = inp[n].shape[-1]
        g_small[n] = lax.dynamic_slice_in_dim(g_small[n], dev * w, w, axis=2)

    shapes = [inp[n].shape for n in SMALL]
    packs = [_pack([src[pre + n] for n in SMALL], F32, SMALL_ROWS)
             for src, pre in ((inp, ""), (g_small, ""), (inp, "m_"), (inp, "v_"))]
    d, m2, v2 = _adamw("adamw_small", *packs)
    for n, dd, mm, vv in zip(SMALL, _unpack(d, shapes), _unpack(m2, shapes), _unpack(v2, shapes)):
        out["grad_" + n] = g_small[n]
        out["delta_" + n], out["new_m_" + n], out["new_v_" + n] = dd, mm, vv
    return out


WEIGHTS = ("norm1_g", "w_in", "b_gate", "lru_conv_w", "lru_conv_b", "lru_w_a", "lru_b_a", "lru_w_x", "lru_b_x",
           "lru_lambda", "ssd_conv_w", "ssd_conv_b", "ssd_dt_bias", "ssd_A_log", "ssd_D", "ssd_norm_g", "w_branch",
           "w_out", "norm2_g", "w_ffn_in", "w_ffn_out", "norm_f")


def kernel(x, norm1_g, w_in, b_gate, lru_conv_w, lru_conv_b, lru_w_a, lru_b_a, lru_w_x, lru_b_x, lru_lambda, ssd_conv_w, ssd_conv_b, ssd_dt_bias, ssd_A_log, ssd_D, ssd_norm_g, w_branch, w_out, norm2_g, w_ffn_in, w_ffn_out, norm_f, loss_target, m_norm1_g, m_w_in, m_b_gate, m_lru_conv_w, m_lru_conv_b, m_lru_w_a, m_lru_b_a, m_lru_w_x, m_lru_b_x, m_lru_lambda, m_ssd_conv_w, m_ssd_conv_b, m_ssd_dt_bias, m_ssd_A_log, m_ssd_D, m_ssd_norm_g, m_w_branch, m_w_out, m_norm2_g, m_w_ffn_in, m_w_ffn_out, m_norm_f, v_norm1_g, v_w_in, v_b_gate, v_lru_conv_w, v_lru_conv_b, v_lru_w_a, v_lru_b_a, v_lru_w_x, v_lru_b_x, v_lru_lambda, v_ssd_conv_w, v_ssd_conv_b, v_ssd_dt_bias, v_ssd_A_log, v_ssd_D, v_ssd_norm_g, v_w_branch, v_w_out, v_norm2_g, v_w_ffn_in, v_w_ffn_out, v_norm_f):
    out = _step(dict(locals()))
    res = [out["loss"], out["grad_x"]]
    for pre in ("grad_", "delta_", "new_m_", "new_v_"):
        res += [out[pre + n] for n in WEIGHTS]
    return tuple(res)
```
